```python
import math
import jax
import jax.numpy as jnp
from jax import lax
import numpy as np


D_MODEL = 1024
BATCH = 32
SEQ = 2048
DEPTH = 2
DEC_BATCH = 16
DEC_SEQ = 2048
PAST_LEN = 128

GROUP_WIDTH = 384
MIX_WIDTH = 4 * GROUP_WIDTH
HEAD_DIM = 64
ROPE_THETA = 500000.0
SSD_HEADS = 6
SSD_INNER = SSD_HEADS * HEAD_DIM
SSD_GROUPS = 2
SSD_STATE = 64
SSD_CONV = 5
SSD_CHUNK = 128
SSD_XBC = SSD_INNER + 2 * SSD_GROUPS * SSD_STATE
SWA_Q_HEADS = 6
SWA_KV_HEADS = 2
SWA_SIDE = 128
MLSTM_HEADS = 4
MLSTM_HEAD_DIM = 96
MLSTM_CHUNK = 128
DIL_CONFIGS = ((128, 1), (512, 4), (2048, 16))
DIL_HEADS = 6
N_EXPERTS = 32
TOP_K = 4
D_EXPERT = D_MODEL
SWIGLU_LIMIT = 7.0
SWIGLU_ALPHA = 1.702
MOE_BLOCK = 256
NORM_EPS = 1e-6

IN_SPLITS = (SSD_INNER, SSD_XBC, 2 * SSD_HEADS,
             SWA_Q_HEADS * HEAD_DIM, SWA_KV_HEADS * HEAD_DIM, SWA_KV_HEADS * HEAD_DIM,
             MLSTM_HEADS * MLSTM_HEAD_DIM, MLSTM_HEADS * MLSTM_HEAD_DIM, MLSTM_HEADS * MLSTM_HEAD_DIM,
             MLSTM_HEADS * MLSTM_HEAD_DIM, 2 * MLSTM_HEADS, 2 * MLSTM_HEADS,
             len(DIL_CONFIGS) * DIL_HEADS * HEAD_DIM, DIL_HEADS * HEAD_DIM, DIL_HEADS * HEAD_DIM)
IN_COLS = sum(IN_SPLITS)

kernel_name = 'hybrid_parallel_group_encoder'


def rmsnorm(x, g):
    xf = x.astype(jnp.float32)
    y = xf * lax.rsqrt(jnp.mean(xf * xf, axis=-1, keepdims=True) + NORM_EPS)
    return (y * g.astype(jnp.float32)).astype(x.dtype)


def partial_rotary(x):
    L, hd = x.shape[1], x.shape[-1]
    rot = hd // 4
    half = rot // 2
    inv = 1.0 / (ROPE_THETA ** (jnp.arange(half, dtype=jnp.float32) * (2.0 / rot)))
    ang = jnp.arange(L, dtype=jnp.float32)[:, None] * inv[None, :]
    cos = jnp.cos(ang)[None, :, None, :].astype(x.dtype)
    sin = jnp.sin(ang)[None, :, None, :].astype(x.dtype)
    x1, x2, xr = x[..., :half], x[..., half:rot], x[..., rot:]
    return jnp.concatenate([x1 * cos - x2 * sin, x2 * cos + x1 * sin, xr], axis=-1)


def banded_attention(q, k, v, side, sink=None):
    bt, n, hq, hd = q.shape
    hk = k.shape[2]
    g = hq // hk
    blk = side
    nb = -(-n // blk)
    n_pad = nb * blk
    qp = jnp.pad(q, ((0, 0), (0, n_pad - n), (0, 0), (0, 0))).reshape(bt, nb, blk, hk, g, hd)
    pad_kv = ((0, 0), (blk, n_pad - n + blk), (0, 0), (0, 0))
    kp = jnp.pad(k, pad_kv).reshape(bt, nb + 2, blk, hk, hd)
    vp = jnp.pad(v, pad_kv).reshape(bt, nb + 2, blk, hk, hd)
    kw = jnp.concatenate([kp[:, :nb], kp[:, 1:nb + 1], kp[:, 2:]], axis=2)
    vw = jnp.concatenate([vp[:, :nb], vp[:, 1:nb + 1], vp[:, 2:]], axis=2)
    s = jnp.einsum('bnqhgd,bnkhd->bnhgqk', qp, kw).astype(jnp.float32) * (hd ** -0.5)
    qpos = jnp.arange(nb)[:, None] * blk + jnp.arange(blk)[None, :]
    kpos = jnp.arange(nb)[:, None] * blk - blk + jnp.arange(3 * blk)[None, :]
    mask = ((jnp.abs(qpos[:, :, None] - kpos[:, None, :]) <= side)
            & (kpos[:, None, :] >= 0) & (kpos[:, None, :] < n))
    s = jnp.where(mask[None, :, None, None], s, -jnp.inf)
    m = jnp.max(s, axis=-1)
    if sink is not None:
        sk = sink.astype(jnp.float32).reshape(hk, g)[None, None, :, :, None]
        m = jnp.maximum(m, sk)
    p = jnp.exp(s - m[..., None])
    l = jnp.sum(p, axis=-1)
    if sink is not None:
        l = l + jnp.exp(sk - m)
    o = jnp.einsum('bnhgqk,bnkhd->bnqhgd', p.astype(v.dtype), vw)
    lt = jnp.transpose(l, (0, 1, 4, 2, 3))
    o = (o / lt[..., None].astype(o.dtype)).reshape(bt, n_pad, hq, hd)[:, :n]
    lse = (jnp.transpose(m, (0, 1, 4, 2, 3)) + jnp.log(lt)).reshape(bt, n_pad, hq)[:, :n]
    return o, lse


def ssd_scan(x, dt, a, bm, cm):
    b, L, H, P = x.shape
    G, N = bm.shape[2], bm.shape[3]
    R = H // G
    Q = SSD_CHUNK
    nc = L // Q
    x = x.reshape(b, nc, Q, G, R, P)
    dt = dt.reshape(b, nc, Q, G, R)
    bm = bm.reshape(b, nc, Q, G, N)
    cm = cm.reshape(b, nc, Q, G, N)
    cum = jnp.cumsum(dt * a.reshape(G, R), axis=2)
    seg = cum[:, :, :, None] - cum[:, :, None, :]
    causal = jnp.tril(jnp.ones((Q, Q), dtype=bool))[:, :, None, None]
    decay = jnp.exp(jnp.where(causal, seg, -jnp.inf))
    cb = jnp.einsum('bcign,bcjgn->bcijg', cm, bm)
    w = cb[..., None] * decay * dt[:, :, None]
    y = jnp.einsum('bcijgr,bcjgrp->bcigrp', w, x)
    end_decay = jnp.exp(cum[:, :, -1:] - cum) * dt
    states = jnp.einsum('bcjgn,bcjgrp->bcgrpn', bm, x * end_decay[..., None])
    chunk_decay = jnp.exp(cum[:, :, -1])

    def step(h, inp):
        s_c, d_c = inp
        return h * d_c[..., None, None] + s_c, h

    h0 = jnp.zeros((b, G, R, P, N), jnp.float32)
    _, prev = lax.scan(step, h0, (jnp.moveaxis(states, 1, 0), jnp.moveaxis(chunk_decay, 1, 0)))
    prev = jnp.moveaxis(prev, 0, 1)
    y = y + jnp.einsum('bcign,bcgrpn->bcigrp', cm, prev) * jnp.exp(cum)[..., None]
    return y.reshape(b, L, H, P)


def ssd_mixer(z, xbc, dt, conv_w, conv_b, a_log, dt_bias, d_skip, norm_g):
    b, L, _ = xbc.shape
    conv = lax.conv_general_dilated(xbc, conv_w[:, None, :].astype(xbc.dtype), window_strides=(1,),
                                    padding=((SSD_CONV // 2, SSD_CONV // 2),),
                                    dimension_numbers=('NWC', 'WIO', 'NWC'),
                                    feature_group_count=SSD_XBC)
    xbc = jax.nn.silu(conv + conv_b.astype(xbc.dtype))
    xs, bm, cm = jnp.split(xbc, [SSD_INNER, SSD_INNER + SSD_GROUPS * SSD_STATE], axis=-1)
    xs = xs.reshape(b, L, SSD_HEADS, HEAD_DIM).astype(jnp.float32)
    bm = bm.reshape(b, L, SSD_GROUPS, SSD_STATE).astype(jnp.float32)
    cm = cm.reshape(b, L, SSD_GROUPS, SSD_STATE).astype(jnp.float32)
    dt = dt.reshape(b, L, 2, SSD_HEADS).astype(jnp.float32)
    y = xs * d_skip.astype(jnp.float32)[:, None]
    for dirn in range(2):
        dtd = jax.nn.softplus(dt[:, :, dirn] + dt_bias[dirn].astype(jnp.float32))
        a = -jnp.exp(a_log[dirn].astype(jnp.float32))
        if dirn == 0:
            y = y + ssd_scan(xs, dtd, a, bm, cm)
        else:
            y = y + jnp.flip(ssd_scan(jnp.flip(xs, 1), jnp.flip(dtd, 1), a,
                                      jnp.flip(bm, 1), jnp.flip(cm, 1)), 1)
    y = y.reshape(b, L, SSD_INNER).astype(z.dtype)
    return rmsnorm(y * jax.nn.silu(z), norm_g)


def swa_mixer(q, k, v, sink):
    b, L, _ = q.shape
    q = partial_rotary(q.reshape(b, L, SWA_Q_HEADS, HEAD_DIM))
    k = partial_rotary(k.reshape(b, L, SWA_KV_HEADS, HEAD_DIM))
    v = v.reshape(b, L, SWA_KV_HEADS, HEAD_DIM)
    o, _ = banded_attention(q, k, v, SWA_SIDE, sink)
    return o.reshape(b, L, SWA_Q_HEADS * HEAD_DIM)


def mlstm_scan(q, k, v, i_pre, f_pre):
    b, L, H, d = q.shape
    Q = MLSTM_CHUNK
    nc = L // Q

    def chunks(t):
        return t.reshape(b, nc, Q, H, d).transpose(1, 0, 3, 2, 4)

    def gchunks(t):
        return t.reshape(b, nc, Q, H).transpose(1, 0, 3, 2)

    bcum = jnp.cumsum(jax.nn.log_sigmoid(gchunks(f_pre)), axis=-1)
    causal = jnp.tril(jnp.ones((Q, Q), dtype=bool))

    def body(carry, inp):
        c_st, n_st, m_st = carry
        qc, kc, vc, ic, bc = inp
        logd = jnp.where(causal, bc[..., :, None] - bc[..., None, :] + ic[..., None, :], -jnp.inf)
        m_inter = bc + m_st[..., None]
        m_t = jnp.maximum(jnp.max(logd, axis=-1), m_inter)
        s = jnp.einsum('bhqd,bhkd->bhqk', qc, kc) * jnp.exp(logd - m_t[..., None])
        inter = jnp.exp(m_inter - m_t)
        num = jnp.einsum('bhqk,bhkd->bhqd', s, vc) + inter[..., None] * jnp.einsum('bhvd,bhqd->bhqv', c_st, qc)
        den = jnp.sum(s, axis=-1) + inter * jnp.einsum('bhd,bhqd->bhq', n_st, qc)
        h = num / jnp.maximum(jnp.abs(den), jnp.exp(-m_t))[..., None]
        b_last = bc[..., -1]
        logw = b_last[..., None] - bc + ic
        m_new = jnp.maximum(b_last + m_st, jnp.max(logw, axis=-1))
        wk = jnp.exp(logw - m_new[..., None])
        keep = jnp.exp(b_last + m_st - m_new)
        c_st = keep[..., None, None] * c_st + jnp.einsum('bhqv,bhqd->bhvd', vc * wk[..., None], kc)
        n_st = keep[..., None] * n_st + jnp.einsum('bhq,bhqd->bhd', wk, kc)
        return (c_st, n_st, m_new), h

    init = (jnp.zeros((b, H, d, d), jnp.float32), jnp.zeros((b, H, d), jnp.float32),
            jnp.zeros((b, H), jnp.float32))
    _, hs = lax.scan(body, init, (chunks(q), chunks(k), chunks(v), gchunks(i_pre), bcum))
    return hs.transpose(1, 0, 3, 2, 4).reshape(b, L, H, d)


def mlstm_mixer(q, k, v, o, ig, fg, i_bias, f_bias, norm_g):
    b, L, _ = q.shape
    shp = (b, L, MLSTM_HEADS, MLSTM_HEAD_DIM)
    q = q.reshape(shp).astype(jnp.float32)
    k = k.reshape(shp).astype(jnp.float32) * (MLSTM_HEAD_DIM ** -0.5)
    v = v.reshape(shp).astype(jnp.float32)
    ig = ig.reshape(b, L, 2, MLSTM_HEADS).astype(jnp.float32) + i_bias.astype(jnp.float32)
    fg = fg.reshape(b, L, 2, MLSTM_HEADS).astype(jnp.float32) + f_bias.astype(jnp.float32)
    h = mlstm_scan(q, k, v, ig[:, :, 0], fg[:, :, 0])
    h = h + jnp.flip(mlstm_scan(jnp.flip(q, 1), jnp.flip(k, 1), jnp.flip(v, 1),
                                jnp.flip(ig[:, :, 1], 1), jnp.flip(fg[:, :, 1], 1)), 1)
    h = rmsnorm(h, norm_g.reshape(MLSTM_HEADS, MLSTM_HEAD_DIM)).reshape(b, L, -1)
    return (h * jax.nn.sigmoid(o.astype(jnp.float32))).astype(o.dtype)


def to_strided(t, dil):
    b, L = t.shape[0], t.shape[1]
    rest = t.shape[2:]
    return t.reshape((b, L // dil, dil) + rest).swapaxes(1, 2).reshape((b * dil, L // dil) + rest)


def from_strided(t, b, dil):
    n = t.shape[1]
    rest = t.shape[2:]
    return t.reshape((b, dil, n) + rest).swapaxes(1, 2).reshape((b, n * dil) + rest)


def dilated_mixer(q, k, v):
    b, L, _ = q.shape
    q = partial_rotary(q.reshape(b, L, len(DIL_CONFIGS) * DIL_HEADS, HEAD_DIM))
    k = partial_rotary(k.reshape(b, L, DIL_HEADS, HEAD_DIM))
    v = v.reshape(b, L, DIL_HEADS, HEAD_DIM)
    outs, lses = [], []
    for gi, (window, dil) in enumerate(DIL_CONFIGS):
        side = (window // 2) // dil
        qg = q[:, :, gi * DIL_HEADS:(gi + 1) * DIL_HEADS]
        o, lse = banded_attention(to_strided(qg, dil), to_strided(k, dil), to_strided(v, dil), side)
        outs.append(from_strided(o, b, dil).astype(jnp.float32))
        lses.append(from_strided(lse, b, dil))
    wts = jax.nn.softmax(jnp.stack(lses), axis=0)
    y = jnp.sum(jnp.stack(outs) * wts[..., None], axis=0)
    return y.reshape(b, L, DIL_HEADS * HEAD_DIM).astype(v.dtype)


def moe_ffn(h, w_r, b_r, w_gu, b_gu, w_dn, b_dn):
    n, d = h.shape
    rows = n * TOP_K
    nblk = -(-rows // MOE_BLOCK) + N_EXPERTS
    logits = (h @ w_r + b_r).astype(jnp.float32)
    top_v, top_e = lax.top_k(logits, TOP_K)
    gates = jax.nn.softmax(top_v, axis=-1).astype(h.dtype)
    e_flat = top_e.reshape(-1)
    order = jnp.argsort(e_flat)
    e_sorted = e_flat[order]
    tok_sorted = (order // TOP_K).astype(jnp.int32)
    g_sorted = gates.reshape(-1)[order]
    counts = jnp.bincount(e_flat, length=N_EXPERTS)
    padded = (counts + MOE_BLOCK - 1) // MOE_BLOCK * MOE_BLOCK
    ends = jnp.cumsum(counts)
    pends = jnp.cumsum(padded)
    dest = (pends - padded)[e_sorted] + jnp.arange(rows) - (ends - counts)[e_sorted]
    row_tok = jnp.full((nblk * MOE_BLOCK,), n, jnp.int32).at[dest].set(tok_sorted)
    row_gate = jnp.zeros((nblk * MOE_BLOCK,), h.dtype).at[dest].set(g_sorted)
    blk_e = jnp.minimum(jnp.searchsorted(pends, jnp.arange(nblk) * MOE_BLOCK, side='right'), N_EXPERTS - 1)
    xs = jnp.concatenate([h, jnp.zeros((1, d), h.dtype)], axis=0)[row_tok].reshape(nblk, MOE_BLOCK, d)

    def expert_block(args):
        xb, e = args
        gu = xb @ w_gu[e] + b_gu[e]
        gate, up = jnp.split(gu, 2, axis=-1)
        gate = jnp.minimum(gate, SWIGLU_LIMIT)
        up = jnp.clip(up, -SWIGLU_LIMIT, SWIGLU_LIMIT)
        act = (up + 1.0) * gate * jax.nn.sigmoid(SWIGLU_ALPHA * gate)
        return act @ w_dn[e] + b_dn[e]

    ys = lax.map(expert_block, (xs, blk_e)).reshape(-1, d)
    return jax.ops.segment_sum(ys * row_gate[:, None], row_tok, num_segments=n + 1)[:n]


def encoder_layer(x, c, p, l):
    b, L, D = x.shape
    mod = (jax.nn.silu(c) @ p['w_mod'][l] + p['b_mod'][l])[:, None, :]
    sh1, sc1, g1, sh2, sc2, g2 = jnp.split(mod, 6, axis=-1)
    h = rmsnorm(x, p['pre_mix_g'][l]) * (1.0 + sc1) + sh1
    u = h @ p['w_in'][l]
    offsets = np.cumsum(IN_SPLITS)[:-1].tolist()
    (ssd_z, ssd_xbc, ssd_dt, swa_q, swa_k, swa_v, ml_q, ml_k, ml_v, ml_o, ml_i, ml_f,
     dil_q, dil_k, dil_v) = jnp.split(u, offsets, axis=-1)
    ya = ssd_mixer(ssd_z, ssd_xbc, ssd_dt, p['ssd_conv_w'][l], p['ssd_conv_b'][l], p['ssd_a_log'][l],
                   p['ssd_dt_bias'][l], p['ssd_d'][l], p['ssd_norm_g'][l])
    yb = swa_mixer(swa_q, swa_k, swa_v, p['swa_sink'][l])
    yc = mlstm_mixer(ml_q, ml_k, ml_v, ml_o, ml_i, ml_f, p['mlstm_i_bias'][l], p['mlstm_f_bias'][l],
                     p['mlstm_norm_g'][l])
    yd = dilated_mixer(dil_q, dil_k, dil_v)
    y = jnp.concatenate([ya.astype(x.dtype), yb.astype(x.dtype), yc.astype(x.dtype), yd.astype(x.dtype)],
                        axis=-1) @ p['w_out'][l]
    x = x + g1 * rmsnorm(y, p['post_mix_g'][l])
    h = rmsnorm(x, p['pre_ffn_g'][l]) * (1.0 + sc2) + sh2
    f = moe_ffn(h.reshape(b * L, D), p['w_router'][l], p['b_router'][l], p['w_gate_up'][l],
                p['b_gate_up'][l], p['w_down'][l], p['b_down'][l]).reshape(b, L, D)
    return x + g2 * rmsnorm(f, p['post_ffn_g'][l])


def run_trunk(x, c, p):
    for l in range(DEPTH):
        x = encoder_layer(x, c, p, l)
    return x


def setup_inputs(seed: int = 0) -> dict:
    key = jax.random.key(seed)
    ks = jax.random.split(key, 28)
    D = D_MODEL

    def nrm(k, shape, s):
        return jax.random.normal(k, shape, jnp.float32) * s

    def gain(k, shape):
        return 1.0 + 0.01 * jax.random.normal(k, shape, jnp.float32)

    dt0 = jnp.exp(jax.random.uniform(ks[14], (DEPTH, 2, SSD_HEADS), jnp.float32,
                                     minval=math.log(1e-3), maxval=math.log(1e-1)))
    return {
        'x_prompt': nrm(ks[0], (BATCH, SEQ, D), 1.0),
        'x_sample': nrm(ks[1], (DEC_BATCH, DEC_SEQ, D), 1.0),
        'c_prompt': nrm(ks[2], (BATCH, D), 1.0),
        'c_sample': nrm(ks[3], (DEC_BATCH, D), 1.0),
        'w_mod': nrm(ks[4], (DEPTH, D, 6 * D), 0.5 * D ** -0.5),
        'b_mod': nrm(ks[5], (DEPTH, 6 * D), 0.01),
        'pre_mix_g': gain(ks[6], (DEPTH, D)),
        'post_mix_g': gain(ks[7], (DEPTH, D)),
        'pre_ffn_g': gain(ks[8], (DEPTH, D)),
        'post_ffn_g': gain(ks[9], (DEPTH, D)),
        'w_in': nrm(ks[10], (DEPTH, D, IN_COLS), D ** -0.5),
        'w_out': nrm(ks[11], (DEPTH, MIX_WIDTH, D), MIX_WIDTH ** -0.5),
        'ssd_conv_w': nrm(ks[12], (DEPTH, SSD_CONV, SSD_XBC), SSD_CONV ** -0.5),
        'ssd_conv_b': nrm(ks[13], (DEPTH, SSD_XBC), 0.01),
        'ssd_a_log': jnp.log(jax.random.uniform(ks[15], (DEPTH, 2, SSD_HEADS), jnp.float32, minval=1.0, maxval=16.0)),
        'ssd_dt_bias': dt0 + jnp.log(-jnp.expm1(-dt0)),
        'ssd_d': gain(ks[16], (DEPTH, SSD_HEADS)),
        'ssd_norm_g': gain(ks[17], (DEPTH, SSD_INNER)),
        'swa_sink': nrm(ks[18], (DEPTH, SWA_Q_HEADS), 0.5),
        'mlstm_i_bias': nrm(ks[19], (DEPTH, 2, MLSTM_HEADS), 0.1),
        'mlstm_f_bias': jax.random.uniform(ks[20], (DEPTH, 2, MLSTM_HEADS), jnp.float32, minval=3.0, maxval=6.0),
        'mlstm_norm_g': gain(ks[21], (DEPTH, MLSTM_HEADS * MLSTM_HEAD_DIM)),
        'w_router': nrm(ks[22], (DEPTH, D, N_EXPERTS), D ** -0.5),
        'b_router': nrm(ks[23], (DEPTH, N_EXPERTS), 0.01),
        'w_gate_up': nrm(ks[24], (DEPTH, N_EXPERTS, D, 2 * D_EXPERT), D ** -0.5),
        'b_gate_up': nrm(ks[25], (DEPTH, N_EXPERTS, 2 * D_EXPERT), 0.01),
        'w_down': nrm(ks[26], (DEPTH, N_EXPERTS, D_EXPERT, D), D_EXPERT ** -0.5),
        'b_down': nrm(ks[27], (DEPTH, N_EXPERTS, D), 0.01),
    }


def reference(x_prompt, x_sample, c_prompt, c_sample, w_mod, b_mod, pre_mix_g, post_mix_g, pre_ffn_g,
              post_ffn_g, w_in, w_out, ssd_conv_w, ssd_conv_b, ssd_a_log, ssd_dt_bias, ssd_d, ssd_norm_g,
              swa_sink, mlstm_i_bias, mlstm_f_bias, mlstm_norm_g, w_router, b_router, w_gate_up, b_gate_up,
              w_down, b_down):
    p = dict(w_mod=w_mod, b_mod=b_mod, pre_mix_g=pre_mix_g, post_mix_g=post_mix_g, pre_ffn_g=pre_ffn_g,
             post_ffn_g=post_ffn_g, w_in=w_in, w_out=w_out, ssd_conv_w=ssd_conv_w, ssd_conv_b=ssd_conv_b,
             ssd_a_log=ssd_a_log, ssd_dt_bias=ssd_dt_bias, ssd_d=ssd_d, ssd_norm_g=ssd_norm_g,
             swa_sink=swa_sink, mlstm_i_bias=mlstm_i_bias, mlstm_f_bias=mlstm_f_bias,
             mlstm_norm_g=mlstm_norm_g, w_router=w_router, b_router=b_router, w_gate_up=w_gate_up,
             b_gate_up=b_gate_up, w_down=w_down, b_down=b_down)
    y_prompt = run_trunk(x_prompt, c_prompt, p)
    y_sample = run_trunk(x_sample, c_sample, p)
    return (y_prompt, y_sample)
```

```python
import functools
import math

import jax
import jax.numpy as jnp
import numpy as np
from jax import lax
from jax.experimental import pallas as pl
from jax.experimental.pallas import tpu as pltpu

F32 = jnp.float32
BF16 = jnp.bfloat16

D_MODEL = 1024
HEAD_DIM = 64
ROPE_THETA = 500000.0
NORM_EPS = 1e-6
CHUNK = 128

SSD_HEADS = 6
SSD_INNER = 384
SSD_STATE = 64
SSD_CONV = 5
SSD_XBC = 640
SWA_Q_HEADS = 6
SWA_SIDE = 128
ML_HEADS = 4
ML_DIM = 96
ML_PAD = 128
DIL_CONFIGS = ((128, 1), (512, 4), (2048, 16))
DIL_HEADS = 6
DIL_SIDE = 64
N_EXPERTS = 32
TOP_K = 4
D_EXPERT = 1024
SWIGLU_LIMIT = 7.0
SWIGLU_ALPHA = 1.702

IN_SPLITS = (384, 640, 12, 384, 128, 128, 384, 384, 384, 384, 8, 8, 1152, 384, 384)

COL_A = 0
COL_G = 1024
COL_B = 1152
COL_C = 1792
COL_D = 3840
COL_END = 5760
LANE_IG = 16
LANE_FG = 24

NEG = -1e30
VMEM_LIMIT = 56 * 1024 * 1024
MOE_ROWS = 128
MOE_TILE = 2048


def _cparams(sem):
    return pltpu.CompilerParams(dimension_semantics=sem, vmem_limit_bytes=VMEM_LIMIT)


def _dot(a, b):
    return jnp.dot(a, b, preferred_element_type=F32)


def _dot_nt(a, b):
    return lax.dot_general(a, b, (((1,), (1,)), ((), ())), preferred_element_type=F32)


def _dot_tn(a, b):
    return lax.dot_general(a, b, (((0,), (0,)), ((), ())), preferred_element_type=F32)


def _sigmoid(x):
    return 1.0 / (1.0 + jnp.exp(-x))


def _softplus(x):
    return jnp.maximum(x, 0.0) + jnp.log(1.0 + jnp.exp(-jnp.abs(x)))


def _prefix_rows(x):
    row = lax.broadcasted_iota(jnp.int32, x.shape, 0)
    s = 1
    while s < x.shape[0]:
        x = x + jnp.where(row >= s, pltpu.roll(x, s, 0), 0.0)
        s *= 2
    return x


def _mod_kernel(c_ref, w_ref, b_ref, o_ref):
    c = c_ref[...]
    s = c * _sigmoid(c)
    o_ref[0] = jnp.dot(s, w_ref[0], preferred_element_type=F32,
                       precision=lax.Precision.HIGHEST) + b_ref[0]


def _mod_call(c, w_mod, b_mod):
    nb = c.shape[0]
    depth, d, cols = w_mod.shape
    tn = 1536
    return pl.pallas_call(
        _mod_kernel,
        grid=(depth, cols // tn),
        in_specs=[pl.BlockSpec((nb, d), lambda l, j: (0, 0)),
                  pl.BlockSpec((1, d, tn), lambda l, j: (l, 0, j)),
                  pl.BlockSpec((1, 1, tn), lambda l, j: (l, 0, j))],
        out_specs=pl.BlockSpec((1, nb, tn), lambda l, j: (l, 0, j)),
        out_shape=jax.ShapeDtypeStruct((depth, nb, cols), F32),
        compiler_params=_cparams(("parallel", "parallel")),
        name="mod",
    )(c, w_mod, b_mod.reshape(depth, 1, cols))


def _rope(a, rc, rs1, rs2):
    return a * rc + pltpu.roll(a, 8, 1) * rs1 + pltpu.roll(a, 120, 1) * rs2


def _in_proj_kernel(x_ref, sc_ref, sh_ref, g_ref, w_ref, rc_ref, rs1_ref, rs2_ref,
                    oa_ref, og_ref, ob_ref, oc_ref, od_ref):
    x = x_ref[...]
    ms = jnp.mean(x * x, axis=-1, keepdims=True)
    h = x * lax.rsqrt(ms + NORM_EPS) * g_ref[...]
    h = h * (1.0 + sc_ref[0]) + sh_ref[0]
    hb = h.astype(BF16)
    rc, rs1, rs2 = rc_ref[...], rs1_ref[...], rs2_ref[...]

    def mm(c0, width):
        return _dot(hb, w_ref[:, c0:c0 + width])

    def plain(o_ref, col0, dst0, width, scale=None, step=512):
        for c in range(0, width, step):
            wd = min(step, width - c)
            a = mm(col0 + c, wd)
            if scale is not None:
                a = a * scale
            o_ref[:, dst0 + c:dst0 + c + wd] = a.astype(o_ref.dtype)

    def roped(o_ref, col0, dst0, width, scale):
        for c in range(0, width, 128):
            a = _rope(mm(col0 + c, 128), rc, rs1, rs2)
            if scale is not None:
                a = a * scale
            o_ref[:, dst0 + c:dst0 + c + 128] = a.astype(o_ref.dtype)

    qscale = HEAD_DIM ** -0.5
    plain(oa_ref, COL_A, 0, 1024)
    plain(og_ref, COL_G, 0, 128)
    roped(ob_ref, COL_B, 0, 384, qscale)
    roped(ob_ref, COL_B + 384, 384, 128, None)
    plain(ob_ref, COL_B + 512, 512, 128)
    plain(oc_ref, COL_C, 0, 512)
    plain(oc_ref, COL_C + 512, 512, 512, scale=ML_DIM ** -0.5)
    plain(oc_ref, COL_C + 1024, 1024, 1024)
    roped(od_ref, COL_D, 0, 1152, qscale)
    roped(od_ref, COL_D + 1152, 1152, 384, None)
    plain(od_ref, COL_D + 1536, 1536, 384, step=384)


def _in_proj_call(x, sc, sh, g, w, rc, rs1, rs2, seq_len, tm=512):
    n, d = x.shape
    per_seq = seq_len // tm
    row = lambda i: (i, 0)
    seq = lambda i: (i // per_seq, 0, 0)
    pos = lambda i: (i % per_seq, 0)
    const = lambda i: (0, 0)
    widths = (1024, 128, 640, 2048, 1920)
    dtypes = (BF16, F32, BF16, BF16, BF16)
    return pl.pallas_call(
        _in_proj_kernel,
        grid=(n // tm,),
        in_specs=[pl.BlockSpec((tm, d), row),
                  pl.BlockSpec((1, 1, d), seq),
                  pl.BlockSpec((1, 1, d), seq),
                  pl.BlockSpec((1, d), const),
                  pl.BlockSpec((d, COL_END), const, pipeline_mode=pl.Buffered(1)),
                  pl.BlockSpec((tm, 128), pos),
                  pl.BlockSpec((tm, 128), pos),
                  pl.BlockSpec((tm, 128), pos)],
        out_specs=[pl.BlockSpec((tm, wd), row) for wd in widths],
        out_shape=[jax.ShapeDtypeStruct((n, wd), dt) for wd, dt in zip(widths, dtypes)],
        compiler_params=_cparams(("parallel",)),
        name="in_proj",
    )(x, sc, sh, g, w, rc, rs1, rs2)


def _ssd_kernel(ua_ref, ug_ref, cw_ref, cb_ref, alog_ref, dtb_ref, dsk_ref, ng_ref, o_ref,
                xp_ref, xa_ref, cum_ref, dtd_ref, tr_ref, y_ref, y2_ref, sf_ref, sb_ref,
                *, seq_len):
    nc = seq_len // CHUNK
    q = CHUNK
    xp_ref[0:8, :] = jnp.zeros((8, SSD_XBC), F32)
    xp_ref[seq_len + 8:seq_len + 16, :] = jnp.zeros((8, SSD_XBC), F32)
    xp_ref[8:seq_len + 8, :] = ua_ref[:, SSD_INNER:SSD_INNER + SSD_XBC].astype(F32)
    sf_ref[...] = jnp.zeros_like(sf_ref)
    sb_ref[...] = jnp.zeros_like(sb_ref)
    a_row = -jnp.exp(alog_ref[...])
    lane = lax.broadcasted_iota(jnp.int32, (q, 128), 1)

    def prep(c, carry):
        r0 = pl.multiple_of(c * q, q)
        win = xp_ref[pl.ds(r0, q + 16), :]
        conv = cb_ref[...] + cw_ref[0:1, :] * win[6:6 + q]
        for k in range(1, SSD_CONV):
            conv = conv + cw_ref[k:k + 1, :] * win[6 + k:6 + k + q]
        xa_ref[pl.ds(r0, q), :] = conv * _sigmoid(conv)
        dt = _softplus(ug_ref[pl.ds(r0, q), :] + dtb_ref[...])
        dta = dt * a_row
        p = _prefix_rows(dta)
        s = p[q - 1:q, :] - p + dta
        cum = jnp.where(lane < SSD_HEADS, p, s)
        cum_ref[pl.ds(r0, q), :] = cum
        dtd_ref[pl.ds(r0, q), :] = dt
        packed = jnp.where(lane < 16, cum, pltpu.roll(dt, 16, 1))
        tr_ref[c] = packed.T
        return carry

    lax.fori_loop(0, nc, prep, 0)

    ri = lax.broadcasted_iota(jnp.int32, (q, q), 0)
    ci = lax.broadcasted_iota(jnp.int32, (q, q), 1)
    lower_incl = ci <= ri
    lower = ci < ri
    upper = ci > ri

    def main(c, carry):
        r0 = pl.multiple_of(c * q, q)
        xa = xa_ref[pl.ds(r0, q), :]
        cum = cum_ref[pl.ds(r0, q), :]
        dtd = dtd_ref[pl.ds(r0, q), :]
        tr = tr_ref[c]
        ys = []
        for g in range(2):
            bg = xa[:, SSD_INNER + g * 64:SSD_INNER + (g + 1) * 64].astype(BF16)
            cg = xa[:, SSD_INNER + 128 + g * 64:SSD_INNER + 128 + (g + 1) * 64].astype(BF16)
            cb = _dot_nt(cg, bg)
            for r in range(3):
                h = 3 * g + r
                cf_col, rb_col = cum[:, h:h + 1], cum[:, 6 + h:7 + h]
                cf_row, rb_row = tr[h:h + 1, :], tr[6 + h:7 + h, :]
                dtf_row, dtb_row = tr[16 + h:17 + h, :], tr[22 + h:23 + h, :]
                arg = jnp.where(lower_incl, cf_col - cf_row, rb_col - rb_row)
                fac = jnp.where(lower, dtf_row, jnp.where(upper, dtb_row, dtf_row + dtb_row))
                w = (cb * jnp.exp(arg) * fac).astype(BF16)
                xh = xa[:, h * 64:(h + 1) * 64]
                yh = _dot(w, xh.astype(BF16))
                sf = sf_ref[h]
                yh = yh + _dot(cg, sf.astype(BF16)) * jnp.exp(cf_col)
                tf = cum[q - 1:q, h:h + 1]
                wcol = jnp.exp(tf - cf_col) * dtd[:, h:h + 1]
                sf_ref[h] = sf * jnp.exp(tf) + _dot_tn(bg, (xh * wcol).astype(BF16))
                ys.append(yh)
        y_ref[pl.ds(r0, q), :] = jnp.concatenate(ys, axis=1)

        cbk = nc - 1 - c
        r1 = pl.multiple_of(cbk * q, q)
        xa = xa_ref[pl.ds(r1, q), :]
        cum = cum_ref[pl.ds(r1, q), :]
        dtd = dtd_ref[pl.ds(r1, q), :]
        ys = []
        for g in range(2):
            bg = xa[:, SSD_INNER + g * 64:SSD_INNER + (g + 1) * 64].astype(BF16)
            cg = xa[:, SSD_INNER + 128 + g * 64:SSD_INNER + 128 + (g + 1) * 64].astype(BF16)
            for r in range(3):
                h = 3 * g + r
                rb_col = cum[:, 6 + h:7 + h]
                sb = sb_ref[h]
                ys.append(_dot(cg, sb.astype(BF16)) * jnp.exp(rb_col))
                tb = cum[0:1, 6 + h:7 + h]
                wcol = jnp.exp(tb - rb_col) * dtd[:, 6 + h:7 + h]
                xh = xa[:, h * 64:(h + 1) * 64]
                sb_ref[h] = sb * jnp.exp(tb) + _dot_tn(bg, (xh * wcol).astype(BF16))
        y2_ref[pl.ds(r1, q), :] = jnp.concatenate(ys, axis=1)
        return carry

    lax.fori_loop(0, nc, main, 0)

    def fin(c, carry):
        r0 = pl.multiple_of(c * q, q)
        xs = xa_ref[pl.ds(r0, q), 0:SSD_INNER]
        y = y_ref[pl.ds(r0, q), :] + y2_ref[pl.ds(r0, q), :] + xs * dsk_ref[...]
        z = ua_ref[pl.ds(r0, q), 0:SSD_INNER].astype(F32)
        v = y * (z * _sigmoid(z))
        ms = jnp.mean(v * v, axis=-1, keepdims=True)
        o_ref[pl.ds(r0, q), :] = (v * lax.rsqrt(ms + NORM_EPS) * ng_ref[...]).astype(o_ref.dtype)
        return carry

    lax.fori_loop(0, nc, fin, 0)


def _ssd_call(ua, ug, cw, cb, alog, dtb, dsk, ng, seq_len):
    n = ua.shape[0]
    nseq = n // seq_len
    nc = seq_len // CHUNK
    seq = lambda b: (b, 0)
    const = lambda b: (0, 0)
    return pl.pallas_call(
        functools.partial(_ssd_kernel, seq_len=seq_len),
        grid=(nseq,),
        in_specs=[pl.BlockSpec((seq_len, 1024), seq),
                  pl.BlockSpec((seq_len, 128), seq),
                  pl.BlockSpec((8, SSD_XBC), const),
                  pl.BlockSpec((1, SSD_XBC), const),
                  pl.BlockSpec((1, 128), const),
                  pl.BlockSpec((1, 128), const),
                  pl.BlockSpec((1, SSD_INNER), const),
                  pl.BlockSpec((1, SSD_INNER), const)],
        out_specs=pl.BlockSpec((seq_len, SSD_INNER), seq),
        out_shape=jax.ShapeDtypeStruct((n, SSD_INNER), BF16),
        scratch_shapes=[pltpu.VMEM((seq_len + 16, SSD_XBC), F32),
                        pltpu.VMEM((seq_len, SSD_XBC), F32),
                        pltpu.VMEM((seq_len, 128), F32),
                        pltpu.VMEM((seq_len, 128), F32),
                        pltpu.VMEM((nc, 128, CHUNK), F32),
                        pltpu.VMEM((seq_len, SSD_INNER), F32),
                        pltpu.VMEM((seq_len, SSD_INNER), F32),
                        pltpu.VMEM((SSD_HEADS, SSD_STATE, HEAD_DIM), F32),
                        pltpu.VMEM((SSD_HEADS, SSD_STATE, HEAD_DIM), F32)],
        compiler_params=_cparams(("parallel",)),
        name="ssd",
    )(ua, ug, cw, cb, alog, dtb, dsk, ng)


def _swa_kernel(sink_ref, u_ref, o_ref, kp_ref, vp_ref, *, seq_len):
    q = CHUNK
    nb = seq_len // q
    zeros = jnp.zeros((q, 128), BF16)
    kp_ref[0:q, :] = zeros
    vp_ref[0:q, :] = zeros
    kp_ref[seq_len + q:seq_len + 2 * q, :] = zeros
    vp_ref[seq_len + q:seq_len + 2 * q, :] = zeros
    kp_ref[q:seq_len + q, :] = u_ref[:, 384:512]
    vp_ref[q:seq_len + q, :] = u_ref[:, 512:640]
    row = lax.broadcasted_iota(jnp.int32, (3 * q, 3 * q), 0) & (q - 1)
    col = lax.broadcasted_iota(jnp.int32, (3 * q, 3 * q), 1)
    band = jnp.abs(col - q - row) <= SWA_SIDE
    rgrp = lax.broadcasted_iota(jnp.int32, (3 * q, 1), 0) // q

    def body(qb, carry):
        q0 = pl.multiple_of(qb * q, q)
        qblk = u_ref[pl.ds(q0, q), 0:384]
        valid = band & (col >= q - q0) & (col < seq_len + q - q0)
        outs = []
        for hk in range(2):
            qs = jnp.concatenate(
                [qblk[:, (3 * hk + j) * 64:(3 * hk + j + 1) * 64] for j in range(3)], axis=0)
            kw = kp_ref[pl.ds(q0, 3 * q), hk * 64:(hk + 1) * 64]
            vw = vp_ref[pl.ds(q0, 3 * q), hk * 64:(hk + 1) * 64]
            s = jnp.where(valid, _dot_nt(qs, kw), NEG)
            sk = jnp.where(rgrp == 0, sink_ref[3 * hk],
                           jnp.where(rgrp == 1, sink_ref[3 * hk + 1], sink_ref[3 * hk + 2]))
            m = jnp.maximum(jnp.max(s, axis=-1, keepdims=True), sk)
            p = jnp.exp(s - m)
            l = jnp.sum(p, axis=-1, keepdims=True) + jnp.exp(sk - m)
            o = _dot(p.astype(BF16), vw) / l
            for j in range(3):
                outs.append(o[j * q:(j + 1) * q, :])
        o_ref[pl.ds(q0, q), :] = jnp.concatenate(outs, axis=1).astype(o_ref.dtype)
        return carry

    lax.fori_loop(0, nb, body, 0)


def _swa_call(ub, sink, seq_len):
    n = ub.shape[0]
    nseq = n // seq_len
    return pl.pallas_call(
        functools.partial(_swa_kernel, seq_len=seq_len),
        grid=(nseq,),
        in_specs=[pl.BlockSpec(memory_space=pltpu.SMEM),
                  pl.BlockSpec((seq_len, 640), lambda b: (b, 0))],
        out_specs=pl.BlockSpec((seq_len, 384), lambda b: (b, 0)),
        out_shape=jax.ShapeDtypeStruct((n, 384), BF16),
        scratch_shapes=[pltpu.VMEM((seq_len + 2 * CHUNK, 128), BF16),
                        pltpu.VMEM((seq_len + 2 * CHUNK, 128), BF16)],
        compiler_params=_cparams(("parallel",)),
        name="swa",
    )(sink, ub)


def _mlstm_kernel(uc_ref, ug_ref, gb_ref, ng_ref, o_ref,
                  gcol_ref, gtr_ref, hacc_ref, ct_ref, n_ref, m_ref, *, seq_len):
    q = CHUNK
    nc = seq_len // q
    lane = lax.broadcasted_iota(jnp.int32, (q, 128), 1)
    ct_ref[...] = jnp.zeros_like(ct_ref)
    n_ref[...] = jnp.zeros_like(n_ref)
    m_ref[...] = jnp.zeros_like(m_ref)
    hacc_ref[...] = jnp.zeros_like(hacc_ref)

    def prep(c, carry):
        r0 = pl.multiple_of(c * q, q)
        raw = ug_ref[pl.ds(r0, q), :] + gb_ref[...]
        lf = jnp.minimum(raw, 0.0) - jnp.log(1.0 + jnp.exp(-jnp.abs(raw)))
        p = _prefix_rows(lf)
        s = p[q - 1:q, :] - p + lf
        g = jnp.where(lane < LANE_FG, raw, jnp.where(lane < LANE_FG + ML_HEADS, p, s))
        gcol_ref[pl.ds(r0, q), :] = g
        gtr_ref[c] = g.T
        return carry

    lax.fori_loop(0, nc, prep, 0)

    ri = lax.broadcasted_iota(jnp.int32, (q, q), 0)
    ci = lax.broadcasted_iota(jnp.int32, (q, q), 1)
    masks = (ci <= ri, ci >= ri)

    def main(c, carry):
        for dirn in range(2):
            ck = c if dirn == 0 else nc - 1 - c
            r0 = pl.multiple_of(ck * q, q)
            g = gcol_ref[pl.ds(r0, q), :]
            gt = gtr_ref[ck]
            for h in range(ML_HEADS):
                s_idx = dirn * ML_HEADS + h
                li = LANE_IG + s_idx
                lb = LANE_FG + s_idx
                b_col, b_row = g[:, lb:lb + 1], gt[lb:lb + 1, :]
                i_col, i_row = g[:, li:li + 1], gt[li:li + 1, :]
                logd = jnp.where(masks[dirn], b_col - b_row + i_row, NEG)
                m_st = m_ref[s_idx][0:1, 0:1]
                m_inter = b_col + m_st
                m_t = jnp.maximum(jnp.max(logd, axis=-1, keepdims=True), m_inter)
                qh = uc_ref[pl.ds(r0, q), h * ML_PAD:(h + 1) * ML_PAD]
                kh = uc_ref[pl.ds(r0, q), 512 + h * ML_PAD:512 + (h + 1) * ML_PAD]
                vh = uc_ref[pl.ds(r0, q), 1024 + h * ML_PAD:1024 + (h + 1) * ML_PAD]
                sm = _dot_nt(qh, kh) * jnp.exp(logd - m_t)
                inter = jnp.exp(m_inter - m_t)
                ct = ct_ref[s_idx]
                nrow = n_ref[s_idx][0:1, :]
                num = _dot(sm.astype(BF16), vh) + inter * _dot(qh, ct.astype(BF16))
                den = (jnp.sum(sm, axis=-1, keepdims=True)
                       + inter * jnp.sum(qh.astype(F32) * nrow, axis=-1, keepdims=True))
                hh = num / jnp.maximum(jnp.abs(den), jnp.exp(-m_t))
                hacc_ref[pl.ds(r0, q), h * ML_PAD:(h + 1) * ML_PAD] += hh
                edge = q - 1 if dirn == 0 else 0
                tot = g[edge:edge + 1, lb:lb + 1]
                m_new = jnp.maximum(tot + m_st,
                                    jnp.max(tot - b_row + i_row, axis=-1, keepdims=True))
                wk_col = jnp.exp(tot - b_col + i_col - m_new)
                keep = jnp.exp(tot + m_st - m_new)
                kf = kh.astype(F32)
                vw = (vh.astype(F32) * wk_col).astype(BF16)
                ct_ref[s_idx] = keep * ct + _dot_tn(kh, vw)
                n_new = keep * nrow + jnp.sum(kf * wk_col, axis=0, keepdims=True)
                n_ref[s_idx] = jnp.broadcast_to(n_new, (8, ML_PAD))
                m_ref[s_idx] = jnp.broadcast_to(m_new, (8, 128))
        return carry

    lax.fori_loop(0, nc, main, 0)

    def fin(c, carry):
        r0 = pl.multiple_of(c * q, q)
        outs = []
        for h in range(ML_HEADS):
            hs = hacc_ref[pl.ds(r0, q), h * ML_PAD:(h + 1) * ML_PAD]
            ms = jnp.sum(hs * hs, axis=-1, keepdims=True) * (1.0 / ML_DIM)
            hn = hs * lax.rsqrt(ms + NORM_EPS) * ng_ref[:, h * ML_PAD:(h + 1) * ML_PAD]
            og = uc_ref[pl.ds(r0, q), 1536 + h * ML_PAD:1536 + (h + 1) * ML_PAD].astype(F32)
            outs.append(hn * _sigmoid(og))
        o_ref[pl.ds(r0, q), :] = jnp.concatenate(outs, axis=1).astype(o_ref.dtype)
        return carry

    lax.fori_loop(0, nc, fin, 0)


def _mlstm_call(uc, ug, gb, ng, seq_len):
    n = uc.shape[0]
    nseq = n // seq_len
    nc = seq_len // CHUNK
    seq = lambda b: (b, 0)
    const = lambda b: (0, 0)
    width = ML_HEADS * ML_PAD
    return pl.pallas_call(
        functools.partial(_mlstm_kernel, seq_len=seq_len),
        grid=(nseq,),
        in_specs=[pl.BlockSpec((seq_len, 4 * width), seq),
                  pl.BlockSpec((seq_len, 128), seq),
                  pl.BlockSpec((1, 128), const),
                  pl.BlockSpec((1, width), const)],
        out_specs=pl.BlockSpec((seq_len, width), seq),
        out_shape=jax.ShapeDtypeStruct((n, width), BF16),
        scratch_shapes=[pltpu.VMEM((seq_len, 128), F32),
                        pltpu.VMEM((nc, 128, CHUNK), F32),
                        pltpu.VMEM((seq_len, width), F32),
                        pltpu.VMEM((2 * ML_HEADS, ML_PAD, ML_PAD), F32),
                        pltpu.VMEM((2 * ML_HEADS, 8, ML_PAD), F32),
                        pltpu.VMEM((2 * ML_HEADS, 8, 128), F32)],
        compiler_params=_cparams(("parallel",)),
        name="mlstm",
    )(uc, ug, gb, ng)


def _dil_kernel(q0_ref, q1_ref, q2_ref, k_ref, v_ref, o_ref,
                qf_ref, kf_ref, vf_ref, kc_ref, vc_ref, og_ref, lg_ref, *, seq_len):
    qb_rows = CHUNK
    pad = DIL_SIDE
    for g, qr in enumerate((q0_ref, q1_ref, q2_ref)):
        qf_ref[g] = qr[...].astype(F32)
    kf_ref[...] = k_ref[...].astype(F32)
    vf_ref[...] = v_ref[...].astype(F32)
    zeros = jnp.zeros((pad, 128), BF16)
    kc_ref[0:pad, :] = zeros
    vc_ref[0:pad, :] = zeros
    row = lax.broadcasted_iota(jnp.int32, (qb_rows, 2 * qb_rows), 0)
    col = lax.broadcasted_iota(jnp.int32, (qb_rows, 2 * qb_rows), 1)
    band = jnp.abs(col - pad - row) <= DIL_SIDE

    for g, (_, dil) in enumerate(DIL_CONFIGS):
        n = seq_len // dil
        nblk = n // qb_rows

        def per_class(r, carry, g=g, dil=dil, n=n, nblk=nblk):
            kc_ref[pad:pad + n, :] = kf_ref[pl.ds(r, n, stride=dil), :].astype(BF16)
            vc_ref[pad:pad + n, :] = vf_ref[pl.ds(r, n, stride=dil), :].astype(BF16)
            kc_ref[pad + n:2 * pad + n, :] = zeros
            vc_ref[pad + n:2 * pad + n, :] = zeros

            def per_block(b, carry2):
                u0 = pl.multiple_of(b * qb_rows, qb_rows)
                rows = pl.ds(r + u0 * dil, qb_rows, stride=dil)
                qc = qf_ref[g, rows, :].astype(BF16)
                kw = kc_ref[pl.ds(u0, 2 * qb_rows), :]
                vw = vc_ref[pl.ds(u0, 2 * qb_rows), :]
                valid = band & (col >= pad - u0) & (col < n + pad - u0)
                outs, lses = [], []
                for hh in range(2):
                    sl = slice(hh * 64, (hh + 1) * 64)
                    s = jnp.where(valid, _dot_nt(qc[:, sl], kw[:, sl]), NEG)
                    m = jnp.max(s, axis=-1, keepdims=True)
                    p = jnp.exp(s - m)
                    l = jnp.sum(p, axis=-1, keepdims=True)
                    outs.append(_dot(p.astype(BF16), vw[:, sl]) / l)
                    lses.append(jnp.broadcast_to(m + jnp.log(l), (qb_rows, 64)))
                og_ref[g, rows, :] = jnp.concatenate(outs, axis=1)
                lg_ref[g, rows, :] = jnp.concatenate(lses, axis=1)
                return carry2

            lax.fori_loop(0, nblk, per_block, 0)
            return carry

        lax.fori_loop(0, dil, per_class, 0)

    def fin(c, carry):
        r0 = pl.multiple_of(c * qb_rows, qb_rows)
        rows = pl.ds(r0, qb_rows)
        l0, l1, l2 = lg_ref[0, rows, :], lg_ref[1, rows, :], lg_ref[2, rows, :]
        m = jnp.maximum(jnp.maximum(l0, l1), l2)
        w0, w1, w2 = jnp.exp(l0 - m), jnp.exp(l1 - m), jnp.exp(l2 - m)
        y = (og_ref[0, rows, :] * w0 + og_ref[1, rows, :] * w1 + og_ref[2, rows, :] * w2)
        o_ref[rows, :] = (y / (w0 + w1 + w2)).astype(o_ref.dtype)
        return carry

    lax.fori_loop(0, seq_len // qb_rows, fin, 0)


def _dil_call(ud, seq_len):
    n = ud.shape[0]
    nseq = n // seq_len
    blk = (seq_len, 128)

    def col(off):
        return lambda b, p: (b, off + p)

    return pl.pallas_call(
        functools.partial(_dil_kernel, seq_len=seq_len),
        grid=(nseq, 3),
        in_specs=[pl.BlockSpec(blk, col(0)), pl.BlockSpec(blk, col(3)), pl.BlockSpec(blk, col(6)),
                  pl.BlockSpec(blk, col(9)), pl.BlockSpec(blk, col(12))],
        out_specs=pl.BlockSpec(blk, col(0)),
        out_shape=jax.ShapeDtypeStruct((n, 384), BF16),
        scratch_shapes=[pltpu.VMEM((3, seq_len, 128), F32),
                        pltpu.VMEM((seq_len, 128), F32),
                        pltpu.VMEM((seq_len, 128), F32),
                        pltpu.VMEM((seq_len + 2 * DIL_SIDE, 128), BF16),
                        pltpu.VMEM((seq_len + 2 * DIL_SIDE, 128), BF16),
                        pltpu.VMEM((3, seq_len, 128), F32),
                        pltpu.VMEM((3, seq_len, 128), F32)],
        compiler_params=_cparams(("parallel", "parallel")),
        name="dil",
    )(ud, ud, ud, ud, ud)


def _out_proj_kernel(ya_ref, yb_ref, yc_ref, yd_ref, x_ref, g1_ref, sc_ref, sh_ref,
                     pg_ref, fg_ref, wa_ref, wb_ref, wc_ref, wd_ref, wr_ref, br_ref,
                     x1_ref, h2_ref, rt_ref):
    y = (_dot(ya_ref[...], wa_ref[...]) + _dot(yb_ref[...], wb_ref[...])
         + _dot(yc_ref[...], wc_ref[...]) + _dot(yd_ref[...], wd_ref[...]))
    ms = jnp.mean(y * y, axis=-1, keepdims=True)
    yn = y * lax.rsqrt(ms + NORM_EPS) * pg_ref[...]
    x1 = x_ref[...] + g1_ref[0] * yn
    x1_ref[...] = x1
    ms2 = jnp.mean(x1 * x1, axis=-1, keepdims=True)
    h2 = x1 * lax.rsqrt(ms2 + NORM_EPS) * fg_ref[...]
    h2 = h2 * (1.0 + sc_ref[0]) + sh_ref[0]
    h2_ref[...] = h2
    logits = jnp.dot(h2, wr_ref[...], preferred_element_type=F32,
                     precision=lax.Precision.HIGHEST) + br_ref[...]
    lane = lax.broadcasted_iota(jnp.int32, logits.shape, 1)
    vals, idxs = [], []
    for _ in range(TOP_K):
        m = jnp.max(logits, axis=-1, keepdims=True)
        idx = jnp.min(jnp.where(logits == m, lane, 128), axis=-1, keepdims=True)
        vals.append(m)
        idxs.append(idx)
        logits = jnp.where(lane == idx, -3e38, logits)
    es = [jnp.exp(v - vals[0]) for v in vals]
    tot = es[0] + es[1] + es[2] + es[3]
    rt = jnp.zeros(lane.shape, F32)
    for k in range(TOP_K):
        rt = jnp.where(lane == k, idxs[k].astype(F32), rt)
        rt = jnp.where(lane == TOP_K + k, es[k] / tot, rt)
    rt_ref[...] = rt[:, 0:2 * TOP_K]


def _out_proj_call(ya, yb, yc, yd, x, g1, sc, sh, pg, fg, wa, wb, wc, wd, wr, br, seq_len, tm=512):
    n, d = x.shape
    per_seq = seq_len // tm
    row = lambda i: (i, 0)
    seq = lambda i: (i // per_seq, 0, 0)
    const = lambda i: (0, 0)

    def full(a):
        return pl.BlockSpec(a.shape, const)

    return pl.pallas_call(
        _out_proj_kernel,
        grid=(n // tm,),
        in_specs=[pl.BlockSpec((tm, ya.shape[1]), row), pl.BlockSpec((tm, yb.shape[1]), row),
                  pl.BlockSpec((tm, yc.shape[1]), row), pl.BlockSpec((tm, yd.shape[1]), row),
                  pl.BlockSpec((tm, d), row),
                  pl.BlockSpec((1, 1, d), seq), pl.BlockSpec((1, 1, d), seq),
                  pl.BlockSpec((1, 1, d), seq),
                  full(pg), full(fg), full(wa), full(wb), full(wc), full(wd), full(wr), full(br)],
        out_specs=[pl.BlockSpec((tm, d), row), pl.BlockSpec((tm, d), row),
                   pl.BlockSpec((tm, 2 * TOP_K), row)],
        out_shape=[jax.ShapeDtypeStruct((n, d), F32), jax.ShapeDtypeStruct((n, d), F32),
                   jax.ShapeDtypeStruct((n, 2 * TOP_K), F32)],
        compiler_params=_cparams(("parallel",)),
        name="out_proj",
    )(ya, yb, yc, yd, x, g1, sc, sh, pg, fg, wa, wb, wc, wd, wr, br)


def _moe_kernel(off_ref, idx_ref, gate_ref, h_ref, wgu_ref, bgu_ref, wdn_ref, bdn_ref, o_ref,
                xs_ref, ys_ref):
    t = pl.program_id(0)
    e = pl.program_id(1)
    rows = MOE_ROWS

    @pl.when(e == 0)
    def _():
        o_ref[...] = jnp.zeros_like(o_ref)

    start = off_ref[t * (N_EXPERTS + 1) + e]
    cnt = off_ref[t * (N_EXPERTS + 1) + e + 1] - start

    def block(b, carry):
        base = start + b * rows

        def gather(r, c2):
            i = idx_ref[0, 0, base + r]
            xs_ref[pl.ds(r, 1), :] = h_ref[pl.ds(i, 1), :]
            return c2

        lax.fori_loop(0, rows, gather, 0, unroll=8)
        gu = _dot(xs_ref[...].astype(BF16), wgu_ref[0]) + bgu_ref[0]
        gate = jnp.minimum(gu[:, :D_EXPERT], SWIGLU_LIMIT)
        up = jnp.clip(gu[:, D_EXPERT:], -SWIGLU_LIMIT, SWIGLU_LIMIT)
        act = (up + 1.0) * gate * _sigmoid(SWIGLU_ALPHA * gate)
        ys_ref[...] = _dot(act.astype(BF16), wdn_ref[0]) + bdn_ref[0]

        def scatter(r, c2):
            i = idx_ref[0, 0, base + r]
            gt = jnp.where(b * rows + r < cnt, gate_ref[0, 0, base + r], 0.0)
            o_ref[pl.ds(i, 1), :] = o_ref[pl.ds(i, 1), :] + gt * ys_ref[pl.ds(r, 1), :]
            return c2

        lax.fori_loop(0, rows, scatter, 0, unroll=8)
        return carry

    lax.fori_loop(0, (cnt + rows - 1) // rows, block, 0)


def _moe_call(off, idx, gates, h2, wgu, bgu, wdn, bdn, tile):
    n, d = h2.shape
    ntile = n // tile
    plen = idx.shape[-1]
    grid_spec = pltpu.PrefetchScalarGridSpec(
        num_scalar_prefetch=1,
        grid=(ntile, N_EXPERTS),
        in_specs=[pl.BlockSpec((1, 1, plen), lambda t, e, off: (t, 0, 0), memory_space=pltpu.SMEM),
                  pl.BlockSpec((1, 1, plen), lambda t, e, off: (t, 0, 0), memory_space=pltpu.SMEM),
                  pl.BlockSpec((tile, d), lambda t, e, off: (t, 0)),
                  pl.BlockSpec((1, d, 2 * D_EXPERT), lambda t, e, off: (e, 0, 0)),
                  pl.BlockSpec((1, 1, 2 * D_EXPERT), lambda t, e, off: (e, 0, 0)),
                  pl.BlockSpec((1, D_EXPERT, d), lambda t, e, off: (e, 0, 0)),
                  pl.BlockSpec((1, 1, d), lambda t, e, off: (e, 0, 0))],
        out_specs=pl.BlockSpec((tile, d), lambda t, e, off: (t, 0)),
        scratch_shapes=[pltpu.VMEM((MOE_ROWS, d), F32), pltpu.VMEM((MOE_ROWS, d), F32)],
    )
    return pl.pallas_call(
        _moe_kernel,
        grid_spec=grid_spec,
        out_shape=jax.ShapeDtypeStruct((n, d), F32),
        compiler_params=_cparams(("parallel", "arbitrary")),
        name="moe",
    )(off, idx, gates, h2, wgu, bgu, wdn, bdn)


def _route(rt, tile):
    n = rt.shape[0]
    ntile = n // tile
    e = rt[:, :TOP_K].astype(jnp.int32).reshape(ntile, tile * TOP_K)
    g = rt[:, TOP_K:].reshape(ntile, tile * TOP_K)
    order = jnp.argsort(e, axis=-1, stable=True)
    tok = (order // TOP_K).astype(jnp.int32)
    gs = jnp.take_along_axis(g, order, axis=-1)
    counts = jnp.sum(e[:, :, None] == jnp.arange(N_EXPERTS, dtype=jnp.int32), axis=1, dtype=jnp.int32)
    off = jnp.concatenate([jnp.zeros((ntile, 1), jnp.int32), jnp.cumsum(counts, axis=-1)], axis=-1)
    padw = ((0, 0), (0, MOE_ROWS))
    idx = jnp.pad(tok, padw).reshape(ntile, 1, -1)
    gs = jnp.pad(gs, padw).reshape(ntile, 1, -1)
    return off.reshape(-1).astype(jnp.int32), idx, gs


def _ffn_res_kernel(x_ref, f_ref, g2_ref, pg_ref, o_ref):
    f = f_ref[...]
    ms = jnp.mean(f * f, axis=-1, keepdims=True)
    o_ref[...] = x_ref[...] + g2_ref[0] * (f * lax.rsqrt(ms + NORM_EPS) * pg_ref[...])


def _ffn_res_call(x1, f, g2, pg, seq_len, tm=1024):
    n, d = x1.shape
    per_seq = seq_len // tm
    row = lambda i: (i, 0)
    return pl.pallas_call(
        _ffn_res_kernel,
        grid=(n // tm,),
        in_specs=[pl.BlockSpec((tm, d), row), pl.BlockSpec((tm, d), row),
                  pl.BlockSpec((1, 1, d), lambda i: (i // per_seq, 0, 0)),
                  pl.BlockSpec((1, d), lambda i: (0, 0))],
        out_specs=pl.BlockSpec((tm, d), row),
        out_shape=jax.ShapeDtypeStruct((n, d), F32),
        compiler_params=_cparams(("parallel",)),
        name="ffn_res",
    )(x1, f, g2, pg)


def _rope_tables(seq_len):
    half = 8
    inv = 1.0 / (ROPE_THETA ** (jnp.arange(half, dtype=F32) * (2.0 / 16)))
    ang = jnp.arange(seq_len, dtype=F32)[:, None] * inv[None, :]
    cos, sin = jnp.cos(ang), jnp.sin(ang)
    ones = jnp.ones((seq_len, 48), F32)
    zeros8 = jnp.zeros((seq_len, 8), F32)
    zeros48 = jnp.zeros((seq_len, 48), F32)
    rc = jnp.concatenate([cos, cos, ones], axis=1)
    rs1 = jnp.concatenate([zeros8, sin, zeros48], axis=1)
    rs2 = jnp.concatenate([-sin, zeros8, zeros48], axis=1)
    tile2 = lambda a: jnp.concatenate([a, a], axis=1)
    return tile2(rc), tile2(rs1), tile2(rs2)


def _pad_heads(w):
    lead = w.shape[:-1]
    w = w.reshape(lead + (ML_HEADS, ML_DIM))
    w = jnp.pad(w, [(0, 0)] * len(lead) + [(0, 0), (0, ML_PAD - ML_DIM)])
    return w.reshape(lead + (ML_HEADS * ML_PAD,))


def _layout_w_in(w_in):
    d = w_in.shape[0]
    offs = np.cumsum((0,) + IN_SPLITS)
    parts = [w_in[:, offs[i]:offs[i + 1]] for i in range(len(IN_SPLITS))]
    (z, xbc, dt, sq, sk, sv, mq, mk, mv, mo, mi, mf, dq, dk, dv) = parts
    zc = lambda k: jnp.zeros((d, k), w_in.dtype)
    gates = jnp.concatenate([dt, zc(4), mi, mf, zc(128 - 32)], axis=1)
    w = jnp.concatenate([z, xbc, gates, sq, sk, sv,
                         _pad_heads(mq), _pad_heads(mk), _pad_heads(mv), _pad_heads(mo),
                         dq, dk, dv], axis=1)
    assert w.shape[1] == COL_END
    return w.astype(BF16)


def _lane_row(parts, width=128):
    row = jnp.zeros((width,), F32)
    for off, v in parts:
        row = row.at[off:off + v.shape[0]].set(v.astype(F32))
    return row.reshape(1, width)


def _layer(x, mod_l, p, l, tabs, seq_len):
    d = D_MODEL
    nseq = x.shape[0] // seq_len
    sh1, sc1, g1, sh2, sc2, g2 = [mod_l[:, i * d:(i + 1) * d].reshape(nseq, 1, d) for i in range(6)]
    row = lambda v: v.reshape(1, -1).astype(F32)

    ua, ug, ub, uc, ud = _in_proj_call(x, sc1, sh1, row(p['pre_mix_g'][l]), _layout_w_in(p['w_in'][l]),
                                       *tabs, seq_len)
    cw = jnp.pad(p['ssd_conv_w'][l], ((0, 8 - SSD_CONV), (0, 0)))
    ya = _ssd_call(ua, ug, cw, row(p['ssd_conv_b'][l]),
                   _lane_row([(0, p['ssd_a_log'][l].reshape(-1))]),
                   _lane_row([(0, p['ssd_dt_bias'][l].reshape(-1))]),
                   row(jnp.repeat(p['ssd_d'][l], HEAD_DIM)), row(p['ssd_norm_g'][l]), seq_len)
    yb = _swa_call(ub, p['swa_sink'][l].astype(F32), seq_len)
    gb = _lane_row([(LANE_IG, p['mlstm_i_bias'][l].reshape(-1)),
                    (LANE_FG, p['mlstm_f_bias'][l].reshape(-1))])
    yc = _mlstm_call(uc, ug, gb, row(_pad_heads(p['mlstm_norm_g'][l])), seq_len)
    yd = _dil_call(ud, seq_len)

    w_out = p['w_out'][l]
    wa, wb = w_out[0:384].astype(BF16), w_out[384:768].astype(BF16)
    wc = jnp.pad(w_out[768:1152].reshape(ML_HEADS, ML_DIM, d),
                 ((0, 0), (0, ML_PAD - ML_DIM), (0, 0))).reshape(ML_HEADS * ML_PAD, d).astype(BF16)
    wd = w_out[1152:1536].astype(BF16)
    wr = jnp.pad(p['w_router'][l], ((0, 0), (0, 128 - N_EXPERTS)))
    br = jnp.concatenate([p['b_router'][l].astype(F32), jnp.full((128 - N_EXPERTS,), NEG, F32)]).reshape(1, 128)
    x1, h2, rt = _out_proj_call(ya, yb, yc, yd, x, g1, sc2, sh2, row(p['post_mix_g'][l]),
                                row(p['pre_ffn_g'][l]), wa, wb, wc, wd, wr, br, seq_len)

    tile = min(MOE_TILE, x.shape[0])
    off, idx, gs = _route(rt, tile)
    f = _moe_call(off, idx, gs, h2, p['w_gate_up'][l].astype(BF16),
                  p['b_gate_up'][l].reshape(N_EXPERTS, 1, -1), p['w_down'][l].astype(BF16),
                  p['b_down'][l].reshape(N_EXPERTS, 1, -1), tile)
    return _ffn_res_call(x1, f, g2, row(p['post_ffn_g'][l]), seq_len)


def _trunk(x, c, p):
    nseq, seq_len, d = x.shape
    depth = p['w_in'].shape[0]
    mod = _mod_call(c, p['w_mod'], p['b_mod'])
    tabs = _rope_tables(seq_len)
    x = x.reshape(nseq * seq_len, d)
    for l in range(depth):
        x = _layer(x, mod[l], p, l, tabs, seq_len)
    return x.reshape(nseq, seq_len, d)


def kernel(x_prompt, x_sample, c_prompt, c_sample, w_mod, b_mod, pre_mix_g, post_mix_g, pre_ffn_g, post_ffn_g, w_in, w_out, ssd_conv_w, ssd_conv_b, ssd_a_log, ssd_dt_bias, ssd_d, ssd_norm_g, swa_sink, mlstm_i_bias, mlstm_f_bias, mlstm_norm_g, w_router, b_router, w_gate_up, b_gate_up, w_down, b_down):
    p = dict(w_mod=w_mod, b_mod=b_mod, pre_mix_g=pre_mix_g, post_mix_g=post_mix_g, pre_ffn_g=pre_ffn_g,
             post_ffn_g=post_ffn_g, w_in=w_in, w_out=w_out, ssd_conv_w=ssd_conv_w, ssd_conv_b=ssd_conv_b,
             ssd_a_log=ssd_a_log, ssd_dt_bias=ssd_dt_bias, ssd_d=ssd_d, ssd_norm_g=ssd_norm_g,
             swa_sink=swa_sink, mlstm_i_bias=mlstm_i_bias, mlstm_f_bias=mlstm_f_bias,
             mlstm_norm_g=mlstm_norm_g, w_router=w_router, b_router=b_router, w_gate_up=w_gate_up,
             b_gate_up=b_gate_up, w_down=w_down, b_down=b_down)
    nb = x_prompt.shape[0]
    x = jnp.concatenate([x_prompt, x_sample], axis=0)
    c = jnp.concatenate([c_prompt, c_sample], axis=0)
    y = _trunk(x, c, p)
    return (y[:nb], y[nb:])
```

```python
import functools
import math

import jax
import jax.numpy as jnp
import numpy as np
from jax import lax
from jax.experimental import pallas as pl
from jax.experimental.pallas import tpu as pltpu

F32 = jnp.float32
BF16 = jnp.bfloat16

D_MODEL = 1024
HEAD_DIM = 64
ROPE_THETA = 500000.0
NORM_EPS = 1e-6
CHUNK = 128

SSD_HEADS = 6
SSD_INNER = 384
SSD_STATE = 64
SSD_CONV = 5
SSD_XBC = 640
SWA_Q_HEADS = 6
SWA_SIDE = 128
SWA_HEAD_ORDER = (0, 3, 1, 4, 2, 5)
ML_HEADS = 4
ML_DIM = 96
ML_PAD = 128
DIL_CONFIGS = ((128, 1), (512, 4), (2048, 16))
DIL_HEADS = 6
DIL_SIDE = 64
DIL_BATCH = 4
N_EXPERTS = 32
TOP_K = 4
D_EXPERT = 1024
SWIGLU_LIMIT = 7.0
SWIGLU_ALPHA = 1.702

IN_SPLITS = (384, 640, 12, 384, 128, 128, 384, 384, 384, 384, 8, 8, 1152, 384, 384)

COL_A = 0
COL_G = 1024
COL_B = 1152
COL_C = 1792
COL_D = 3840
COL_END = 5760
LANE_IG = 16
LANE_FG = 24

NEG = -1e30
VMEM_LIMIT = 56 * 1024 * 1024
MOE_ROWS = 128
MOE_GROUP = 4
MOE_TILE = 2048


def _cparams(sem):
    return pltpu.CompilerParams(dimension_semantics=sem, vmem_limit_bytes=VMEM_LIMIT)


def _dot(a, b):
    return jnp.dot(a, b, preferred_element_type=F32)


def _dot_nt(a, b):
    return lax.dot_general(a, b, (((1,), (1,)), ((), ())), preferred_element_type=F32)


def _dot_tn(a, b):
    return lax.dot_general(a, b, (((0,), (0,)), ((), ())), preferred_element_type=F32)


def _sigmoid(x):
    return 1.0 / (1.0 + jnp.exp(-x))


def _softplus(x):
    return jnp.maximum(x, 0.0) + jnp.log(1.0 + jnp.exp(-jnp.abs(x)))


def _prefix_rows(x):
    row = lax.broadcasted_iota(jnp.int32, x.shape, 0)
    s = 1
    while s < x.shape[0]:
        x = x + jnp.where(row >= s, pltpu.roll(x, s, 0), 0.0)
        s *= 2
    return x


def _mod_kernel(c_ref, w_ref, b_ref, o_ref):
    c = c_ref[...]
    s = c * _sigmoid(c)
    o_ref[0] = jnp.dot(s, w_ref[0], preferred_element_type=F32,
                       precision=lax.Precision.HIGHEST) + b_ref[0]


def _mod_call(c, w_mod, b_mod):
    nb = c.shape[0]
    depth, d, cols = w_mod.shape
    tn = 1536
    return pl.pallas_call(
        _mod_kernel,
        grid=(depth, cols // tn),
        in_specs=[pl.BlockSpec((nb, d), lambda l, j: (0, 0)),
                  pl.BlockSpec((1, d, tn), lambda l, j: (l, 0, j)),
                  pl.BlockSpec((1, 1, tn), lambda l, j: (l, 0, j))],
        out_specs=pl.BlockSpec((1, nb, tn), lambda l, j: (l, 0, j)),
        out_shape=jax.ShapeDtypeStruct((depth, nb, cols), F32),
        compiler_params=_cparams(("parallel", "parallel")),
        name="mod",
    )(c, w_mod, b_mod.reshape(depth, 1, cols))


def _rope(a, rc, rs1, rs2):
    return a * rc + pltpu.roll(a, 8, 1) * rs1 + pltpu.roll(a, 120, 1) * rs2


def _in_proj_kernel(x_ref, sc_ref, sh_ref, g_ref, w_ref, rc_ref, rs1_ref, rs2_ref,
                    oa_ref, og_ref, ob_ref, oc_ref, od_ref):
    x = x_ref[...]
    ms = jnp.mean(x * x, axis=-1, keepdims=True)
    h = x * lax.rsqrt(ms + NORM_EPS) * g_ref[...]
    h = h * (1.0 + sc_ref[0]) + sh_ref[0]
    hb = h.astype(BF16)
    rc, rs1, rs2 = rc_ref[...], rs1_ref[...], rs2_ref[...]

    def mm(c0, width):
        return _dot(hb, w_ref[:, c0:c0 + width])

    def plain(o_ref, col0, dst0, width, scale=None, step=512):
        for c in range(0, width, step):
            wd = min(step, width - c)
            a = mm(col0 + c, wd)
            if scale is not None:
                a = a * scale
            o_ref[:, dst0 + c:dst0 + c + wd] = a.astype(o_ref.dtype)

    def roped(o_ref, col0, dst0, width, scale):
        for c in range(0, width, 128):
            a = _rope(mm(col0 + c, 128), rc, rs1, rs2)
            if scale is not None:
                a = a * scale
            o_ref[:, dst0 + c:dst0 + c + 128] = a.astype(o_ref.dtype)

    qscale = HEAD_DIM ** -0.5
    plain(oa_ref, COL_A, 0, 1024)
    plain(og_ref, COL_G, 0, 128)
    roped(ob_ref, COL_B, 0, 384, qscale)
    roped(ob_ref, COL_B + 384, 384, 128, None)
    plain(ob_ref, COL_B + 512, 512, 128)
    plain(oc_ref, COL_C, 0, 512)
    plain(oc_ref, COL_C + 512, 512, 512, scale=ML_DIM ** -0.5)
    plain(oc_ref, COL_C + 1024, 1024, 1024)
    roped(od_ref, COL_D, 0, 1152, qscale)
    roped(od_ref, COL_D + 1152, 1152, 384, None)
    plain(od_ref, COL_D + 1536, 1536, 384, step=384)


def _in_proj_call(x, sc, sh, g, w, rc, rs1, rs2, seq_len, tm=512):
    n, d = x.shape
    per_seq = seq_len // tm
    row = lambda i: (i, 0)
    seq = lambda i: (i // per_seq, 0, 0)
    pos = lambda i: (i % per_seq, 0)
    const = lambda i: (0, 0)
    widths = (1024, 128, 640, 2048, 1920)
    dtypes = (BF16, F32, BF16, BF16, BF16)
    return pl.pallas_call(
        _in_proj_kernel,
        grid=(n // tm,),
        in_specs=[pl.BlockSpec((tm, d), row),
                  pl.BlockSpec((1, 1, d), seq),
                  pl.BlockSpec((1, 1, d), seq),
                  pl.BlockSpec((1, d), const),
                  pl.BlockSpec((d, COL_END), const, pipeline_mode=pl.Buffered(1)),
                  pl.BlockSpec((tm, 128), pos),
                  pl.BlockSpec((tm, 128), pos),
                  pl.BlockSpec((tm, 128), pos)],
        out_specs=[pl.BlockSpec((tm, wd), row) for wd in widths],
        out_shape=[jax.ShapeDtypeStruct((n, wd), dt) for wd, dt in zip(widths, dtypes)],
        compiler_params=_cparams(("parallel",)),
        name="in_proj",
    )(x, sc, sh, g, w, rc, rs1, rs2)


def _ssd_kernel(ua_ref, ug_ref, cw_ref, cb_ref, alog_ref, dtb_ref, dsk_ref, ng_ref, o_ref,
                xp_ref, xa_ref, cum_ref, dtd_ref, tr_ref, y_ref, y2_ref, sf_ref, sb_ref,
                *, seq_len):
    nc = seq_len // CHUNK
    q = CHUNK
    xp_ref[0:8, :] = jnp.zeros((8, SSD_XBC), F32)
    xp_ref[seq_len + 8:seq_len + 16, :] = jnp.zeros((8, SSD_XBC), F32)
    xp_ref[8:seq_len + 8, :] = ua_ref[:, SSD_INNER:SSD_INNER + SSD_XBC].astype(F32)
    sf_ref[...] = jnp.zeros_like(sf_ref)
    sb_ref[...] = jnp.zeros_like(sb_ref)
    a_row = -jnp.exp(alog_ref[...])
    lane = lax.broadcasted_iota(jnp.int32, (q, 128), 1)

    def prep(c, carry):
        r0 = pl.multiple_of(c * q, q)
        win = xp_ref[pl.ds(r0, q + 16), :]
        conv = cb_ref[...] + cw_ref[0:1, :] * win[6:6 + q]
        for k in range(1, SSD_CONV):
            conv = conv + cw_ref[k:k + 1, :] * win[6 + k:6 + k + q]
        xa_ref[pl.ds(r0, q), :] = conv * _sigmoid(conv)
        dt = _softplus(ug_ref[pl.ds(r0, q), :] + dtb_ref[...])
        dta = dt * a_row
        p = _prefix_rows(dta)
        s = p[q - 1:q, :] - p + dta
        cum = jnp.where(lane < SSD_HEADS, p, s)
        cum_ref[pl.ds(r0, q), :] = cum
        dtd_ref[pl.ds(r0, q), :] = dt
        packed = jnp.where(lane < 16, cum, pltpu.roll(dt, 16, 1))
        tr_ref[c] = packed.T
        return carry

    lax.fori_loop(0, nc, prep, 0)

    ri = lax.broadcasted_iota(jnp.int32, (q, q), 0)
    ci = lax.broadcasted_iota(jnp.int32, (q, q), 1)
    lower_incl = ci <= ri
    lower = ci < ri
    upper = ci > ri

    def main(c, carry):
        r0 = pl.multiple_of(c * q, q)
        xa = xa_ref[pl.ds(r0, q), :]
        cum = cum_ref[pl.ds(r0, q), :]
        dtd = dtd_ref[pl.ds(r0, q), :]
        tr = tr_ref[c]
        ys = []
        for g in range(2):
            bg = xa[:, SSD_INNER + g * 64:SSD_INNER + (g + 1) * 64].astype(BF16)
            cg = xa[:, SSD_INNER + 128 + g * 64:SSD_INNER + 128 + (g + 1) * 64].astype(BF16)
            cb = _dot_nt(cg, bg)
            for r in range(3):
                h = 3 * g + r
                cf_col, rb_col = cum[:, h:h + 1], cum[:, 6 + h:7 + h]
                cf_row, rb_row = tr[h:h + 1, :], tr[6 + h:7 + h, :]
                dtf_row, dtb_row = tr[16 + h:17 + h, :], tr[22 + h:23 + h, :]
                arg = jnp.where(lower_incl, cf_col - cf_row, rb_col - rb_row)
                fac = jnp.where(lower, dtf_row, jnp.where(upper, dtb_row, dtf_row + dtb_row))
                w = (cb * jnp.exp(arg) * fac).astype(BF16)
                xh = xa[:, h * 64:(h + 1) * 64]
                yh = _dot(w, xh.astype(BF16))
                sf = sf_ref[h]
                yh = yh + _dot(cg, sf.astype(BF16)) * jnp.exp(cf_col)
                tf = cum[q - 1:q, h:h + 1]
                wcol = jnp.exp(tf - cf_col) * dtd[:, h:h + 1]
                sf_ref[h] = sf * jnp.exp(tf) + _dot_tn(bg, (xh * wcol).astype(BF16))
                ys.append(yh)
        y_ref[pl.ds(r0, q), :] = jnp.concatenate(ys, axis=1)

        cbk = nc - 1 - c
        r1 = pl.multiple_of(cbk * q, q)
        xa = xa_ref[pl.ds(r1, q), :]
        cum = cum_ref[pl.ds(r1, q), :]
        dtd = dtd_ref[pl.ds(r1, q), :]
        ys = []
        for g in range(2):
            bg = xa[:, SSD_INNER + g * 64:SSD_INNER + (g + 1) * 64].astype(BF16)
            cg = xa[:, SSD_INNER + 128 + g * 64:SSD_INNER + 128 + (g + 1) * 64].astype(BF16)
            for r in range(3):
                h = 3 * g + r
                rb_col = cum[:, 6 + h:7 + h]
                sb = sb_ref[h]
                ys.append(_dot(cg, sb.astype(BF16)) * jnp.exp(rb_col))
                tb = cum[0:1, 6 + h:7 + h]
                wcol = jnp.exp(tb - rb_col) * dtd[:, 6 + h:7 + h]
                xh = xa[:, h * 64:(h + 1) * 64]
                sb_ref[h] = sb * jnp.exp(tb) + _dot_tn(bg, (xh * wcol).astype(BF16))
        y2_ref[pl.ds(r1, q), :] = jnp.concatenate(ys, axis=1)
        return carry

    lax.fori_loop(0, nc, main, 0)

    def fin(c, carry):
        r0 = pl.multiple_of(c * q, q)
        xs = xa_ref[pl.ds(r0, q), 0:SSD_INNER]
        y = y_ref[pl.ds(r0, q), :] + y2_ref[pl.ds(r0, q), :] + xs * dsk_ref[...]
        z = ua_ref[pl.ds(r0, q), 0:SSD_INNER].astype(F32)
        v = y * (z * _sigmoid(z))
        ms = jnp.mean(v * v, axis=-1, keepdims=True)
        o_ref[pl.ds(r0, q), :] = (v * lax.rsqrt(ms + NORM_EPS) * ng_ref[...]).astype(o_ref.dtype)
        return carry

    lax.fori_loop(0, nc, fin, 0)


def _ssd_call(ua, ug, cw, cb, alog, dtb, dsk, ng, seq_len):
    n = ua.shape[0]
    nseq = n // seq_len
    nc = seq_len // CHUNK
    seq = lambda b: (b, 0)
    const = lambda b: (0, 0)
    return pl.pallas_call(
        functools.partial(_ssd_kernel, seq_len=seq_len),
        grid=(nseq,),
        in_specs=[pl.BlockSpec((seq_len, 1024), seq),
                  pl.BlockSpec((seq_len, 128), seq),
                  pl.BlockSpec((8, SSD_XBC), const),
                  pl.BlockSpec((1, SSD_XBC), const),
                  pl.BlockSpec((1, 128), const),
                  pl.BlockSpec((1, 128), const),
                  pl.BlockSpec((1, SSD_INNER), const),
                  pl.BlockSpec((1, SSD_INNER), const)],
        out_specs=pl.BlockSpec((seq_len, SSD_INNER), seq),
        out_shape=jax.ShapeDtypeStruct((n, SSD_INNER), BF16),
        scratch_shapes=[pltpu.VMEM((seq_len + 16, SSD_XBC), F32),
                        pltpu.VMEM((seq_len, SSD_XBC), F32),
                        pltpu.VMEM((seq_len, 128), F32),
                        pltpu.VMEM((seq_len, 128), F32),
                        pltpu.VMEM((nc, 128, CHUNK), F32),
                        pltpu.VMEM((seq_len, SSD_INNER), F32),
                        pltpu.VMEM((seq_len, SSD_INNER), F32),
                        pltpu.VMEM((SSD_HEADS, SSD_STATE, HEAD_DIM), F32),
                        pltpu.VMEM((SSD_HEADS, SSD_STATE, HEAD_DIM), F32)],
        compiler_params=_cparams(("parallel",)),
        name="ssd",
    )(ua, ug, cw, cb, alog, dtb, dsk, ng)


def _swa_kernel(sink_ref, u_ref, o_ref, kp_ref, vp_ref, bias_ref, *, seq_len):
    q = CHUNK
    nb = seq_len // q
    zeros = jnp.zeros((q, 128), BF16)
    kp_ref[0:q, :] = zeros
    vp_ref[0:q, :] = zeros
    kp_ref[seq_len + q:seq_len + 2 * q, :] = zeros
    vp_ref[seq_len + q:seq_len + 2 * q, :] = zeros
    kp_ref[q:seq_len + q, :] = u_ref[:, 384:512]
    vp_ref[q:seq_len + q, :] = u_ref[:, 512:640]
    row = lax.broadcasted_iota(jnp.int32, (3 * q, 3 * q), 0) & (q - 1)
    col = lax.broadcasted_iota(jnp.int32, (3 * q, 3 * q), 1)
    bias_ref[...] = jnp.where(jnp.abs(col - q - row) <= SWA_SIDE, 0.0, NEG)
    rgrp = lax.broadcasted_iota(jnp.int32, (3 * q, 1), 0) // q
    col1 = lax.broadcasted_iota(jnp.int32, (1, 3 * q), 1)
    lane1 = lax.broadcasted_iota(jnp.int32, (1, 128), 1)
    lo_f = jnp.where(lane1 < HEAD_DIM, 1.0, 0.0)
    hi_f = 1.0 - lo_f
    halves = ((lo_f.astype(BF16), lo_f), (hi_f.astype(BF16), hi_f))

    def body(qb, carry):
        q0 = pl.multiple_of(qb * q, q)
        qs = jnp.concatenate([u_ref[pl.ds(q0, q), j * 128:(j + 1) * 128] for j in range(3)], axis=0)
        kw = kp_ref[pl.ds(q0, 3 * q), :]
        vw = vp_ref[pl.ds(q0, 3 * q), :]
        inside = (col1 >= q - q0) & (col1 < seq_len + q - q0)
        bias = bias_ref[...] + jnp.where(inside, 0.0, NEG)
        o = None
        for hk, (sel_b, sel_f) in enumerate(halves):
            s = _dot_nt(qs * sel_b, kw) + bias
            sk = jnp.where(rgrp == 0, sink_ref[3 * hk],
                           jnp.where(rgrp == 1, sink_ref[3 * hk + 1], sink_ref[3 * hk + 2]))
            m = jnp.maximum(jnp.max(s, axis=-1, keepdims=True), sk)
            p = jnp.exp(s - m)
            l = jnp.sum(p, axis=-1, keepdims=True) + jnp.exp(sk - m)
            oh = (_dot(p.astype(BF16), vw) / l) * sel_f
            o = oh if o is None else o + oh
        o_ref[pl.ds(q0, q), :] = jnp.concatenate(
            [o[j * q:(j + 1) * q, :] for j in range(3)], axis=1).astype(o_ref.dtype)
        return carry

    lax.fori_loop(0, nb, body, 0, unroll=2)


def _swa_call(ub, sink, seq_len):
    n = ub.shape[0]
    nseq = n // seq_len
    return pl.pallas_call(
        functools.partial(_swa_kernel, seq_len=seq_len),
        grid=(nseq,),
        in_specs=[pl.BlockSpec(memory_space=pltpu.SMEM),
                  pl.BlockSpec((seq_len, 640), lambda b: (b, 0))],
        out_specs=pl.BlockSpec((seq_len, 384), lambda b: (b, 0)),
        out_shape=jax.ShapeDtypeStruct((n, 384), BF16),
        scratch_shapes=[pltpu.VMEM((seq_len + 2 * CHUNK, 128), BF16),
                        pltpu.VMEM((seq_len + 2 * CHUNK, 128), BF16),
                        pltpu.VMEM((3 * CHUNK, 3 * CHUNK), F32)],
        compiler_params=_cparams(("parallel",)),
        name="swa",
    )(sink, ub)


def _mlstm_kernel(uc_ref, ug_ref, gb_ref, ng_ref, o_ref,
                  gcol_ref, gtr_ref, hacc_ref, ct_ref, n_ref, m_ref, *, seq_len):
    q = CHUNK
    nc = seq_len // q
    lane = lax.broadcasted_iota(jnp.int32, (q, 128), 1)
    ct_ref[...] = jnp.zeros_like(ct_ref)
    n_ref[...] = jnp.zeros_like(n_ref)
    m_ref[...] = jnp.zeros_like(m_ref)
    hacc_ref[...] = jnp.zeros_like(hacc_ref)

    def prep(c, carry):
        r0 = pl.multiple_of(c * q, q)
        raw = ug_ref[pl.ds(r0, q), :] + gb_ref[...]
        lf = jnp.minimum(raw, 0.0) - jnp.log(1.0 + jnp.exp(-jnp.abs(raw)))
        p = _prefix_rows(lf)
        s = p[q - 1:q, :] - p + lf
        g = jnp.where(lane < LANE_FG, raw, jnp.where(lane < LANE_FG + ML_HEADS, p, s))
        gcol_ref[pl.ds(r0, q), :] = g
        gtr_ref[c] = g.T
        return carry

    lax.fori_loop(0, nc, prep, 0)

    ri = lax.broadcasted_iota(jnp.int32, (q, q), 0)
    ci = lax.broadcasted_iota(jnp.int32, (q, q), 1)
    masks = (ci <= ri, ci >= ri)

    def main(c, carry):
        for dirn in range(2):
            ck = c if dirn == 0 else nc - 1 - c
            r0 = pl.multiple_of(ck * q, q)
            g = gcol_ref[pl.ds(r0, q), :]
            gt = gtr_ref[ck]
            for h in range(ML_HEADS):
                s_idx = dirn * ML_HEADS + h
                li = LANE_IG + s_idx
                lb = LANE_FG + s_idx
                b_col, b_row = g[:, lb:lb + 1], gt[lb:lb + 1, :]
                i_col, i_row = g[:, li:li + 1], gt[li:li + 1, :]
                logd = jnp.where(masks[dirn], b_col - b_row + i_row, NEG)
                m_st = m_ref[s_idx][0:1, 0:1]
                m_inter = b_col + m_st
                m_t = jnp.maximum(jnp.max(logd, axis=-1, keepdims=True), m_inter)
                qh = uc_ref[pl.ds(r0, q), h * ML_PAD:(h + 1) * ML_PAD]
                kh = uc_ref[pl.ds(r0, q), 512 + h * ML_PAD:512 + (h + 1) * ML_PAD]
                vh = uc_ref[pl.ds(r0, q), 1024 + h * ML_PAD:1024 + (h + 1) * ML_PAD]
                sm = _dot_nt(qh, kh) * jnp.exp(logd - m_t)
                inter = jnp.exp(m_inter - m_t)
                ct = ct_ref[s_idx]
                nrow = n_ref[s_idx][0:1, :]
                num = _dot(sm.astype(BF16), vh) + inter * _dot(qh, ct.astype(BF16))
                den = (jnp.sum(sm, axis=-1, keepdims=True)
                       + inter * jnp.sum(qh.astype(F32) * nrow, axis=-1, keepdims=True))
                hh = num / jnp.maximum(jnp.abs(den), jnp.exp(-m_t))
                hacc_ref[pl.ds(r0, q), h * ML_PAD:(h + 1) * ML_PAD] += hh
                edge = q - 1 if dirn == 0 else 0
                tot = g[edge:edge + 1, lb:lb + 1]
                m_new = jnp.maximum(tot + m_st,
                                    jnp.max(tot - b_row + i_row, axis=-1, keepdims=True))
                wk_col = jnp.exp(tot - b_col + i_col - m_new)
                keep = jnp.exp(tot + m_st - m_new)
                kf = kh.astype(F32)
                vw = (vh.astype(F32) * wk_col).astype(BF16)
                ct_ref[s_idx] = keep * ct + _dot_tn(kh, vw)
                n_new = keep * nrow + jnp.sum(kf * wk_col, axis=0, keepdims=True)
                n_ref[s_idx] = jnp.broadcast_to(n_new, (8, ML_PAD))
                m_ref[s_idx] = jnp.broadcast_to(m_new, (8, 128))
        return carry

    lax.fori_loop(0, nc, main, 0)

    def fin(c, carry):
        r0 = pl.multiple_of(c * q, q)
        outs = []
        for h in range(ML_HEADS):
            hs = hacc_ref[pl.ds(r0, q), h * ML_PAD:(h + 1) * ML_PAD]
            ms = jnp.sum(hs * hs, axis=-1, keepdims=True) * (1.0 / ML_DIM)
            hn = hs * lax.rsqrt(ms + NORM_EPS) * ng_ref[:, h * ML_PAD:(h + 1) * ML_PAD]
            og = uc_ref[pl.ds(r0, q), 1536 + h * ML_PAD:1536 + (h + 1) * ML_PAD].astype(F32)
            outs.append(hn * _sigmoid(og))
        o_ref[pl.ds(r0, q), :] = jnp.concatenate(outs, axis=1).astype(o_ref.dtype)
        return carry

    lax.fori_loop(0, nc, fin, 0)


def _mlstm_call(uc, ug, gb, ng, seq_len):
    n = uc.shape[0]
    nseq = n // seq_len
    nc = seq_len // CHUNK
    seq = lambda b: (b, 0)
    const = lambda b: (0, 0)
    width = ML_HEADS * ML_PAD
    return pl.pallas_call(
        functools.partial(_mlstm_kernel, seq_len=seq_len),
        grid=(nseq,),
        in_specs=[pl.BlockSpec((seq_len, 4 * width), seq),
                  pl.BlockSpec((seq_len, 128), seq),
                  pl.BlockSpec((1, 128), const),
                  pl.BlockSpec((1, width), const)],
        out_specs=pl.BlockSpec((seq_len, width), seq),
        out_shape=jax.ShapeDtypeStruct((n, width), BF16),
        scratch_shapes=[pltpu.VMEM((seq_len, 128), F32),
                        pltpu.VMEM((nc, 128, CHUNK), F32),
                        pltpu.VMEM((seq_len, width), F32),
                        pltpu.VMEM((2 * ML_HEADS, ML_PAD, ML_PAD), F32),
                        pltpu.VMEM((2 * ML_HEADS, 8, ML_PAD), F32),
                        pltpu.VMEM((2 * ML_HEADS, 8, 128), F32)],
        compiler_params=_cparams(("parallel",)),
        name="mlstm",
    )(uc, ug, gb, ng)


def _attend_units(units, lo_b, hi_b, lo_f, hi_f):
    n = units[0][0].shape[0]
    ss = []
    for q, kw, _, bias in units:
        s2 = _dot_nt(jnp.concatenate([q * lo_b, q * hi_b], axis=0), kw)
        ss.append(jnp.concatenate([s2[0:n] + bias, s2[n:2 * n] + bias], axis=0))
    ms = [jnp.max(s, axis=-1, keepdims=True) for s in ss]
    ps = [jnp.exp(s - m) for s, m in zip(ss, ms)]
    ls = [jnp.sum(p, axis=-1, keepdims=True) for p in ps]
    os_ = [_dot(p.astype(BF16), u[2]) / l for p, u, l in zip(ps, units, ls)]
    res = []
    for o2, m, l in zip(os_, ms, ls):
        lse2 = m + jnp.log(l)
        res.append((o2[0:n] * lo_f + o2[n:2 * n] * hi_f, lse2[0:n] * lo_f + lse2[n:2 * n] * hi_f))
    return res


def _dil_kernel(q0_ref, q1_ref, q2_ref, k_ref, v_ref, o_ref,
                qf_ref, kf_ref, vf_ref, kc0_ref, vc0_ref, kc1_ref, vc1_ref, og_ref, lg_ref,
                qc2_ref, kc2_ref, vc2_ref, og2_ref, lg2_ref, band_ref, band2_ref, *, seq_len):
    qr = CHUNK
    pad = DIL_SIDE
    d1, d2 = DIL_CONFIGS[1][1], DIL_CONFIGS[2][1]
    n1, n2 = seq_len // d1, seq_len // d2
    assert n2 == qr and n1 % qr == 0
    qf_ref[0] = q1_ref[...].astype(F32)
    qf_ref[1] = q2_ref[...].astype(F32)
    kf_ref[...] = k_ref[...].astype(F32)
    vf_ref[...] = v_ref[...].astype(F32)
    zeros = jnp.zeros((pad, 128), BF16)
    kc0_ref[0:pad, :] = zeros
    vc0_ref[0:pad, :] = zeros
    kc0_ref[pad + seq_len:2 * pad + seq_len, :] = zeros
    vc0_ref[pad + seq_len:2 * pad + seq_len, :] = zeros
    kc0_ref[pad:pad + seq_len, :] = k_ref[...]
    vc0_ref[pad:pad + seq_len, :] = v_ref[...]
    for r in range(d1):
        kc1_ref[r, 0:pad, :] = zeros
        vc1_ref[r, 0:pad, :] = zeros
        kc1_ref[r, pad + n1:2 * pad + n1, :] = zeros
        vc1_ref[r, pad + n1:2 * pad + n1, :] = zeros
        kc1_ref[r, pad:pad + n1, :] = kf_ref[pl.ds(r, n1, stride=d1), :].astype(BF16)
        vc1_ref[r, pad:pad + n1, :] = vf_ref[pl.ds(r, n1, stride=d1), :].astype(BF16)

    row = lax.broadcasted_iota(jnp.int32, (qr, 2 * qr), 0)
    col = lax.broadcasted_iota(jnp.int32, (qr, 2 * qr), 1)
    band_ref[...] = jnp.where(jnp.abs(col - pad - row) <= DIL_SIDE, 0.0, NEG)
    row2 = lax.broadcasted_iota(jnp.int32, (qr, qr), 0)
    col2 = lax.broadcasted_iota(jnp.int32, (qr, qr), 1)
    band2_ref[...] = jnp.where(jnp.abs(col2 - row2) <= DIL_SIDE, 0.0, NEG)
    lane1 = lax.broadcasted_iota(jnp.int32, (1, 128), 1)
    col1 = lax.broadcasted_iota(jnp.int32, (1, 2 * qr), 1)
    lo_f = jnp.where(lane1 < HEAD_DIM, 1.0, 0.0)
    hi_f = 1.0 - lo_f
    sel = (lo_f.astype(BF16), hi_f.astype(BF16), lo_f, hi_f)

    def window_bias(u0, n):
        inside = (col1 >= pad - u0) & (col1 < n + pad - u0)
        return band_ref[...] + jnp.where(inside, 0.0, NEG)

    def batch0(b4, carry):
        units, starts = [], []
        for i in range(DIL_BATCH):
            u0 = pl.multiple_of((b4 * DIL_BATCH + i) * qr, qr)
            starts.append(u0)
            units.append((q0_ref[pl.ds(u0, qr), :], kc0_ref[pl.ds(u0, 2 * qr), :],
                          vc0_ref[pl.ds(u0, 2 * qr), :], window_bias(u0, seq_len)))
        for u0, (o, lse) in zip(starts, _attend_units(units, *sel)):
            og_ref[0, pl.ds(u0, qr), :] = o
            lg_ref[0, pl.ds(u0, qr), :] = lse
        return carry

    lax.fori_loop(0, seq_len // qr // DIL_BATCH, batch0, 0)

    nb1 = n1 // qr
    assert nb1 == DIL_BATCH

    def batch1(r, carry):
        units, rowsl = [], []
        for i in range(DIL_BATCH):
            u0 = i * qr
            rows = pl.ds(r + u0 * d1, qr, stride=d1)
            rowsl.append(rows)
            units.append((qf_ref[0, rows, :].astype(BF16), kc1_ref[r, u0:u0 + 2 * qr, :],
                          vc1_ref[r, u0:u0 + 2 * qr, :], window_bias(u0, n1)))
        for rows, (o, lse) in zip(rowsl, _attend_units(units, *sel)):
            og_ref[1, rows, :] = o
            lg_ref[1, rows, :] = lse
        return carry

    lax.fori_loop(0, d1, batch1, 0)

    for r in range(d2):
        rows = pl.ds(r, qr, stride=d2)
        qc2_ref[r] = qf_ref[1, rows, :].astype(BF16)
        kc2_ref[r] = kf_ref[rows, :].astype(BF16)
        vc2_ref[r] = vf_ref[rows, :].astype(BF16)

    def batch2(b4, carry):
        rs = [b4 * DIL_BATCH + i for i in range(DIL_BATCH)]
        units = [(qc2_ref[r], kc2_ref[r], vc2_ref[r], band2_ref[...]) for r in rs]
        for r, (o, lse) in zip(rs, _attend_units(units, *sel)):
            og2_ref[r] = o
            lg2_ref[r] = lse
        return carry

    lax.fori_loop(0, d2 // DIL_BATCH, batch2, 0)
    for r in range(d2):
        rows = pl.ds(r, qr, stride=d2)
        og_ref[2, rows, :] = og2_ref[r]
        lg_ref[2, rows, :] = lg2_ref[r]

    def fin(c, carry):
        r0 = pl.multiple_of(c * qr, qr)
        rows = pl.ds(r0, qr)
        l0, l1, l2 = lg_ref[0, rows, :], lg_ref[1, rows, :], lg_ref[2, rows, :]
        m = jnp.maximum(jnp.maximum(l0, l1), l2)
        w0, w1, w2 = jnp.exp(l0 - m), jnp.exp(l1 - m), jnp.exp(l2 - m)
        y = (og_ref[0, rows, :] * w0 + og_ref[1, rows, :] * w1 + og_ref[2, rows, :] * w2)
        o_ref[rows, :] = (y / (w0 + w1 + w2)).astype(o_ref.dtype)
        return carry

    lax.fori_loop(0, seq_len // qr, fin, 0, unroll=2)


def _dil_call(ud, seq_len):
    n = ud.shape[0]
    nseq = n // seq_len
    blk = (seq_len, 128)
    d1, d2 = DIL_CONFIGS[1][1], DIL_CONFIGS[2][1]
    n1 = seq_len // d1

    def col(off):
        return lambda b, p: (b, off + p)

    return pl.pallas_call(
        functools.partial(_dil_kernel, seq_len=seq_len),
        grid=(nseq, 3),
        in_specs=[pl.BlockSpec(blk, col(0)), pl.BlockSpec(blk, col(3)), pl.BlockSpec(blk, col(6)),
                  pl.BlockSpec(blk, col(9)), pl.BlockSpec(blk, col(12))],
        out_specs=pl.BlockSpec(blk, col(0)),
        out_shape=jax.ShapeDtypeStruct((n, 384), BF16),
        scratch_shapes=[pltpu.VMEM((2, seq_len, 128), F32),
                        pltpu.VMEM((seq_len, 128), F32),
                        pltpu.VMEM((seq_len, 128), F32),
                        pltpu.VMEM((seq_len + 2 * DIL_SIDE, 128), BF16),
                        pltpu.VMEM((seq_len + 2 * DIL_SIDE, 128), BF16),
                        pltpu.VMEM((d1, n1 + 2 * DIL_SIDE, 128), BF16),
                        pltpu.VMEM((d1, n1 + 2 * DIL_SIDE, 128), BF16),
                        pltpu.VMEM((3, seq_len, 128), F32),
                        pltpu.VMEM((3, seq_len, 128), F32)]
                       + [pltpu.VMEM((d2, CHUNK, 128), BF16)] * 3
                       + [pltpu.VMEM((d2, CHUNK, 128), F32)] * 2
                       + [pltpu.VMEM((CHUNK, 2 * CHUNK), F32), pltpu.VMEM((CHUNK, CHUNK), F32)],
        compiler_params=_cparams(("parallel", "parallel")),
        name="dil",
    )(ud, ud, ud, ud, ud)


def _out_proj_kernel(ya_ref, yb_ref, yc_ref, yd_ref, x_ref, g1_ref, sc_ref, sh_ref,
                     pg_ref, fg_ref, wa_ref, wb_ref, wc_ref, wd_ref, wr_ref, br_ref,
                     x1_ref, h2_ref, rt_ref):
    y = (_dot(ya_ref[...], wa_ref[...]) + _dot(yb_ref[...], wb_ref[...])
         + _dot(yc_ref[...], wc_ref[...]) + _dot(yd_ref[...], wd_ref[...]))
    ms = jnp.mean(y * y, axis=-1, keepdims=True)
    yn = y * lax.rsqrt(ms + NORM_EPS) * pg_ref[...]
    x1 = x_ref[...] + g1_ref[0] * yn
    x1_ref[...] = x1
    ms2 = jnp.mean(x1 * x1, axis=-1, keepdims=True)
    h2 = x1 * lax.rsqrt(ms2 + NORM_EPS) * fg_ref[...]
    h2 = h2 * (1.0 + sc_ref[0]) + sh_ref[0]
    h2_ref[...] = h2
    logits = jnp.dot(h2, wr_ref[...], preferred_element_type=F32,
                     precision=lax.Precision.HIGHEST) + br_ref[...]
    lane = lax.broadcasted_iota(jnp.int32, logits.shape, 1)
    vals, idxs = [], []
    for _ in range(TOP_K):
        m = jnp.max(logits, axis=-1, keepdims=True)
        idx = jnp.min(jnp.where(logits == m, lane, 128), axis=-1, keepdims=True)
        vals.append(m)
        idxs.append(idx)
        logits = jnp.where(lane == idx, -3e38, logits)
    es = [jnp.exp(v - vals[0]) for v in vals]
    tot = es[0] + es[1] + es[2] + es[3]
    rt = jnp.zeros(lane.shape, F32)
    for k in range(TOP_K):
        rt = jnp.where(lane == k, idxs[k].astype(F32), rt)
        rt = jnp.where(lane == TOP_K + k, es[k] / tot, rt)
    rt_ref[...] = rt[:, 0:2 * TOP_K]


def _out_proj_call(ya, yb, yc, yd, x, g1, sc, sh, pg, fg, wa, wb, wc, wd, wr, br, seq_len, tm=512):
    n, d = x.shape
    per_seq = seq_len // tm
    row = lambda i: (i, 0)
    seq = lambda i: (i // per_seq, 0, 0)
    const = lambda i: (0, 0)

    def full(a):
        return pl.BlockSpec(a.shape, const)

    return pl.pallas_call(
        _out_proj_kernel,
        grid=(n // tm,),
        in_specs=[pl.BlockSpec((tm, ya.shape[1]), row), pl.BlockSpec((tm, yb.shape[1]), row),
                  pl.BlockSpec((tm, yc.shape[1]), row), pl.BlockSpec((tm, yd.shape[1]), row),
                  pl.BlockSpec((tm, d), row),
                  pl.BlockSpec((1, 1, d), seq), pl.BlockSpec((1, 1, d), seq),
                  pl.BlockSpec((1, 1, d), seq),
                  full(pg), full(fg), full(wa), full(wb), full(wc), full(wd), full(wr), full(br)],
        out_specs=[pl.BlockSpec((tm, d), row), pl.BlockSpec((tm, d), row),
                   pl.BlockSpec((tm, 2 * TOP_K), row)],
        out_shape=[jax.ShapeDtypeStruct((n, d), F32), jax.ShapeDtypeStruct((n, d), F32),
                   jax.ShapeDtypeStruct((n, 2 * TOP_K), F32)],
        compiler_params=_cparams(("parallel",)),
        name="out_proj",
    )(ya, yb, yc, yd, x, g1, sc, sh, pg, fg, wa, wb, wc, wd, wr, br)


def _moe_kernel(nb_ref, be_ref, idx_ref, gate_ref, h_ref, wgu_ref, bgu_ref, wdn_ref, bdn_ref, o_ref,
                xs0_ref, xs1_ref, ys0_ref, ys1_ref, acc_ref):
    t = pl.program_id(0)
    j = pl.program_id(1)
    nb = nb_ref[t]
    rows = MOE_ROWS
    tile = h_ref.shape[0]

    def gather(dst_ref, blk):
        base = (blk + 1) * rows
        for r in range(rows):
            i = jnp.minimum(idx_ref[0, 0, base + r], tile - 1)
            dst_ref[r:r + 1, :] = h_ref[pl.ds(i, 1), :]

    def scatter(src_ref, blk):
        base = (blk + 1) * rows
        for r0 in range(0, rows, MOE_GROUP):
            ii = [idx_ref[0, 0, base + r0 + k] for k in range(MOE_GROUP)]
            gg = [gate_ref[0, 0, base + r0 + k] for k in range(MOE_GROUP)]
            cur = [acc_ref[pl.ds(ii[k], 1), :] for k in range(MOE_GROUP)]
            for k in range(MOE_GROUP):
                acc_ref[pl.ds(ii[k], 1), :] = cur[k] + gg[k] * src_ref[r0 + k:r0 + k + 1, :]

    def expert(xs_ref, ys_ref):
        gu = _dot(xs_ref[...].astype(BF16), wgu_ref[0]) + bgu_ref[0]
        gate = jnp.minimum(gu[:, :D_EXPERT], SWIGLU_LIMIT)
        up = jnp.clip(gu[:, D_EXPERT:], -SWIGLU_LIMIT, SWIGLU_LIMIT)
        act = (up + 1.0) * gate * _sigmoid(SWIGLU_ALPHA * gate)
        ys_ref[...] = _dot(act.astype(BF16), wdn_ref[0]) + bdn_ref[0]

    @pl.when(j == 0)
    def _():
        acc_ref[...] = jnp.zeros_like(acc_ref)
        ys1_ref[...] = jnp.zeros_like(ys1_ref)
        gather(xs0_ref, 0)

    bufs = ((xs0_ref, xs1_ref, ys0_ref, ys1_ref), (xs1_ref, xs0_ref, ys1_ref, ys0_ref))
    for par, (xs_cur, xs_nxt, ys_cur, ys_prv) in enumerate(bufs):
        @pl.when((j < nb) & (j % 2 == par))
        def _(xs_cur=xs_cur, xs_nxt=xs_nxt, ys_cur=ys_cur, ys_prv=ys_prv):
            gather(xs_nxt, j + 1)
            expert(xs_cur, ys_cur)
            scatter(ys_prv, j - 1)

        @pl.when((j == nb) & (j % 2 == par))
        def _(ys_prv=ys_prv):
            scatter(ys_prv, j - 1)
            o_ref[...] = acc_ref[0:tile, :]


def _moe_call(nb, be, idx, gates, h2, wgu, bgu, wdn, bdn, tile):
    n, d = h2.shape
    ntile = n // tile
    plen = idx.shape[-1]
    nslot = be.shape[0] // ntile

    def wmap(t, j, nb_ref, be_ref):
        return (be_ref[t * nslot + j], 0, 0)

    tmap = lambda t, j, nb_ref, be_ref: (t, 0, 0)
    grid_spec = pltpu.PrefetchScalarGridSpec(
        num_scalar_prefetch=2,
        grid=(ntile, nslot),
        in_specs=[pl.BlockSpec((1, 1, plen), tmap, memory_space=pltpu.SMEM),
                  pl.BlockSpec((1, 1, plen), tmap, memory_space=pltpu.SMEM),
                  pl.BlockSpec((tile, d), lambda t, j, nb_ref, be_ref: (t, 0)),
                  pl.BlockSpec((1, d, 2 * D_EXPERT), wmap),
                  pl.BlockSpec((1, 1, 2 * D_EXPERT), wmap),
                  pl.BlockSpec((1, D_EXPERT, d), wmap),
                  pl.BlockSpec((1, 1, d), wmap)],
        out_specs=pl.BlockSpec((tile, d), lambda t, j, nb_ref, be_ref: (t, 0),
                               pipeline_mode=pl.Buffered(1)),
        scratch_shapes=[pltpu.VMEM((MOE_ROWS, d), F32)] * 4 + [pltpu.VMEM((tile + 8, d), F32)],
    )
    return pl.pallas_call(
        _moe_kernel,
        grid_spec=grid_spec,
        out_shape=jax.ShapeDtypeStruct((n, d), F32),
        compiler_params=_cparams(("parallel", "arbitrary")),
        name="moe",
    )(nb, be, idx, gates, h2, wgu, bgu, wdn, bdn)


def _route(rt, tile):
    n = rt.shape[0]
    ntile = n // tile
    npair = tile * TOP_K
    nblk = npair // MOE_ROWS + N_EXPERTS
    e = rt[:, :TOP_K].astype(jnp.int32).reshape(ntile, npair)
    g = rt[:, TOP_K:].reshape(ntile, npair)
    order = jnp.argsort(e, axis=-1, stable=True)
    es = jnp.take_along_axis(e, order, axis=-1)
    tok = (order // TOP_K).astype(jnp.int32)
    gs = jnp.take_along_axis(g, order, axis=-1)
    counts = jnp.sum(e[:, :, None] == jnp.arange(N_EXPERTS, dtype=jnp.int32), axis=1, dtype=jnp.int32)
    padded = (counts + MOE_ROWS - 1) // MOE_ROWS * MOE_ROWS
    pends = jnp.cumsum(padded, axis=-1)
    starts = jnp.cumsum(counts, axis=-1) - counts
    dest = (jnp.take_along_axis(pends - padded, es, axis=-1) + jnp.arange(npair, dtype=jnp.int32)[None, :]
            - jnp.take_along_axis(starts, es, axis=-1)) + MOE_ROWS
    plen = (nblk + 2) * MOE_ROWS
    tix = jnp.arange(ntile, dtype=jnp.int32)[:, None]
    idx = jnp.full((ntile, plen), tile, jnp.int32).at[tix, dest].set(tok)
    gl = jnp.zeros((ntile, plen), F32).at[tix, dest].set(gs)
    nb = pends[:, -1] // MOE_ROWS
    slot_row = jnp.arange(nblk + 1, dtype=jnp.int32)[None, :] * MOE_ROWS
    slot_row = jnp.minimum(slot_row, pends[:, -1:] - MOE_ROWS)
    be = jnp.sum(slot_row[:, :, None] >= pends[:, None, :], axis=-1, dtype=jnp.int32)
    be = jnp.minimum(be, N_EXPERTS - 1)
    return (nb.astype(jnp.int32), be.reshape(-1), idx.reshape(ntile, 1, plen), gl.reshape(ntile, 1, plen))


def _ffn_res_kernel(x_ref, f_ref, g2_ref, pg_ref, o_ref):
    f = f_ref[...]
    ms = jnp.mean(f * f, axis=-1, keepdims=True)
    o_ref[...] = x_ref[...] + g2_ref[0] * (f * lax.rsqrt(ms + NORM_EPS) * pg_ref[...])


def _ffn_res_call(x1, f, g2, pg, seq_len, tm=1024):
    n, d = x1.shape
    per_seq = seq_len // tm
    row = lambda i: (i, 0)
    return pl.pallas_call(
        _ffn_res_kernel,
        grid=(n // tm,),
        in_specs=[pl.BlockSpec((tm, d), row), pl.BlockSpec((tm, d), row),
                  pl.BlockSpec((1, 1, d), lambda i: (i // per_seq, 0, 0)),
                  pl.BlockSpec((1, d), lambda i: (0, 0))],
        out_specs=pl.BlockSpec((tm, d), row),
        out_shape=jax.ShapeDtypeStruct((n, d), F32),
        compiler_params=_cparams(("parallel",)),
        name="ffn_res",
    )(x1, f, g2, pg)


def _rope_tables(seq_len):
    half = 8
    inv = 1.0 / (ROPE_THETA ** (jnp.arange(half, dtype=F32) * (2.0 / 16)))
    ang = jnp.arange(seq_len, dtype=F32)[:, None] * inv[None, :]
    cos, sin = jnp.cos(ang), jnp.sin(ang)
    ones = jnp.ones((seq_len, 48), F32)
    zeros8 = jnp.zeros((seq_len, 8), F32)
    zeros48 = jnp.zeros((seq_len, 48), F32)
    rc = jnp.concatenate([cos, cos, ones], axis=1)
    rs1 = jnp.concatenate([zeros8, sin, zeros48], axis=1)
    rs2 = jnp.concatenate([-sin, zeros8, zeros48], axis=1)
    tile2 = lambda a: jnp.concatenate([a, a], axis=1)
    return tile2(rc), tile2(rs1), tile2(rs2)


def _pad_heads(w):
    lead = w.shape[:-1]
    w = w.reshape(lead + (ML_HEADS, ML_DIM))
    w = jnp.pad(w, [(0, 0)] * len(lead) + [(0, 0), (0, ML_PAD - ML_DIM)])
    return w.reshape(lead + (ML_HEADS * ML_PAD,))


def _layout_w_in(w_in):
    d = w_in.shape[0]
    offs = np.cumsum((0,) + IN_SPLITS)
    parts = [w_in[:, offs[i]:offs[i + 1]] for i in range(len(IN_SPLITS))]
    (z, xbc, dt, sq, sk, sv, mq, mk, mv, mo, mi, mf, dq, dk, dv) = parts
    zc = lambda k: jnp.zeros((d, k), w_in.dtype)
    gates = jnp.concatenate([dt, zc(4), mi, mf, zc(128 - 32)], axis=1)
    sq = sq.reshape(d, SWA_Q_HEADS, HEAD_DIM)[:, SWA_HEAD_ORDER, :].reshape(d, -1)
    w = jnp.concatenate([z, xbc, gates, sq, sk, sv,
                         _pad_heads(mq), _pad_heads(mk), _pad_heads(mv), _pad_heads(mo),
                         dq, dk, dv], axis=1)
    assert w.shape[1] == COL_END
    return w.astype(BF16)


def _lane_row(parts, width=128):
    row = jnp.zeros((width,), F32)
    for off, v in parts:
        row = row.at[off:off + v.shape[0]].set(v.astype(F32))
    return row.reshape(1, width)


def _layer(x, mod_l, p, l, tabs, seq_len):
    d = D_MODEL
    nseq = x.shape[0] // seq_len
    sh1, sc1, g1, sh2, sc2, g2 = [mod_l[:, i * d:(i + 1) * d].reshape(nseq, 1, d) for i in range(6)]
    row = lambda v: v.reshape(1, -1).astype(F32)

    ua, ug, ub, uc, ud = _in_proj_call(x, sc1, sh1, row(p['pre_mix_g'][l]), _layout_w_in(p['w_in'][l]),
                                       *tabs, seq_len)
    cw = jnp.pad(p['ssd_conv_w'][l], ((0, 8 - SSD_CONV), (0, 0)))
    ya = _ssd_call(ua, ug, cw, row(p['ssd_conv_b'][l]),
                   _lane_row([(0, p['ssd_a_log'][l].reshape(-1))]),
                   _lane_row([(0, p['ssd_dt_bias'][l].reshape(-1))]),
                   row(jnp.repeat(p['ssd_d'][l], HEAD_DIM)), row(p['ssd_norm_g'][l]), seq_len)
    yb = _swa_call(ub, p['swa_sink'][l].astype(F32), seq_len)
    gb = _lane_row([(LANE_IG, p['mlstm_i_bias'][l].reshape(-1)),
                    (LANE_FG, p['mlstm_f_bias'][l].reshape(-1))])
    yc = _mlstm_call(uc, ug, gb, row(_pad_heads(p['mlstm_norm_g'][l])), seq_len)
    yd = _dil_call(ud, seq_len)

    w_out = p['w_out'][l]
    wa = w_out[0:384].astype(BF16)
    wb = w_out[384:768].reshape(SWA_Q_HEADS, HEAD_DIM, d)[SWA_HEAD_ORDER, :, :].reshape(384, d).astype(BF16)
    wc = jnp.pad(w_out[768:1152].reshape(ML_HEADS, ML_DIM, d),
                 ((0, 0), (0, ML_PAD - ML_DIM), (0, 0))).reshape(ML_HEADS * ML_PAD, d).astype(BF16)
    wd = w_out[1152:1536].astype(BF16)
    wr = jnp.pad(p['w_router'][l], ((0, 0), (0, 128 - N_EXPERTS)))
    br = jnp.concatenate([p['b_router'][l].astype(F32), jnp.full((128 - N_EXPERTS,), NEG, F32)]).reshape(1, 128)
    x1, h2, rt = _out_proj_call(ya, yb, yc, yd, x, g1, sc2, sh2, row(p['post_mix_g'][l]),
                                row(p['pre_ffn_g'][l]), wa, wb, wc, wd, wr, br, seq_len)

    tile = min(MOE_TILE, x.shape[0])
    nb, be, idx, gs = _route(rt, tile)
    f = _moe_call(nb, be, idx, gs, h2, p['w_gate_up'][l].astype(BF16),
                  p['b_gate_up'][l].reshape(N_EXPERTS, 1, -1), p['w_down'][l].astype(BF16),
                  p['b_down'][l].reshape(N_EXPERTS, 1, -1), tile)
    return _ffn_res_call(x1, f, g2, row(p['post_ffn_g'][l]), seq_len)


def _trunk(x, c, p):
    nseq, seq_len, d = x.shape
    depth = p['w_in'].shape[0]
    mod = _mod_call(c, p['w_mod'], p['b_mod'])
    tabs = _rope_tables(seq_len)
    x = x.reshape(nseq * seq_len, d)
    for l in range(depth):
        x = _layer(x, mod[l], p, l, tabs, seq_len)
    return x.reshape(nseq, seq_len, d)


def kernel(x_prompt, x_sample, c_prompt, c_sample, w_mod, b_mod, pre_mix_g, post_mix_g, pre_ffn_g, post_ffn_g, w_in, w_out, ssd_conv_w, ssd_conv_b, ssd_a_log, ssd_dt_bias, ssd_d, ssd_norm_g, swa_sink, mlstm_i_bias, mlstm_f_bias, mlstm_norm_g, w_router, b_router, w_gate_up, b_gate_up, w_down, b_down):
    p = dict(w_mod=w_mod, b_mod=b_mod, pre_mix_g=pre_mix_g, post_mix_g=post_mix_g, pre_ffn_g=pre_ffn_g,
             post_ffn_g=post_ffn_g, w_in=w_in, w_out=w_out, ssd_conv_w=ssd_conv_w, ssd_conv_b=ssd_conv_b,
             ssd_a_log=ssd_a_log, ssd_dt_bias=ssd_dt_bias, ssd_d=ssd_d, ssd_norm_g=ssd_norm_g,
             swa_sink=swa_sink, mlstm_i_bias=mlstm_i_bias, mlstm_f_bias=mlstm_f_bias,
             mlstm_norm_g=mlstm_norm_g, w_router=w_router, b_router=b_router, w_gate_up=w_gate_up,
             b_gate_up=b_gate_up, w_down=w_down, b_down=b_down)
    nb = x_prompt.shape[0]
    x = jnp.concatenate([x_prompt, x_sample], axis=0)
    c = jnp.concatenate([c_prompt, c_sample], axis=0)
    y = _trunk(x, c, p)
    return (y[:nb], y[nb:])
```

```python
import functools
import math

import jax
import jax.numpy as jnp
import numpy as np
from jax import lax
from jax.experimental import pallas as pl
from jax.experimental.pallas import tpu as pltpu

F32 = jnp.float32
BF16 = jnp.bfloat16

D_MODEL = 1024
HEAD_DIM = 64
ROPE_THETA = 500000.0
NORM_EPS = 1e-6
CHUNK = 128

SSD_HEADS = 6
SSD_INNER = 384
SSD_STATE = 64
SSD_CONV = 5
SSD_XBC = 640
SWA_Q_HEADS = 6
SWA_SIDE = 128
SWA_HEAD_ORDER = (0, 3, 1, 4, 2, 5)
ML_HEADS = 4
ML_DIM = 96
ML_PAD = 128
DIL_CONFIGS = ((128, 1), (512, 4), (2048, 16))
DIL_HEADS = 6
DIL_SIDE = 64
DIL_BATCH = 4
N_EXPERTS = 32
TOP_K = 4
D_EXPERT = 1024
SWIGLU_LIMIT = 7.0
SWIGLU_ALPHA = 1.702

IN_SPLITS = (384, 640, 12, 384, 128, 128, 384, 384, 384, 384, 8, 8, 1152, 384, 384)

COL_A = 0
COL_G = 1024
COL_B = 1152
COL_C = 1792
COL_D = 3840
COL_END = 5760
LANE_IG = 16
LANE_FG = 24

NEG = -1e30
VMEM_LIMIT = 56 * 1024 * 1024
OUT_SUB = 256
MOE_ROWS = 128
MOE_GROUP = 4
MOE_TILE = 4096


def _cparams(sem):
    return pltpu.CompilerParams(dimension_semantics=sem, vmem_limit_bytes=VMEM_LIMIT)


def _dot(a, b):
    return jnp.dot(a, b, preferred_element_type=F32)


def _dot_nt(a, b):
    return lax.dot_general(a, b, (((1,), (1,)), ((), ())), preferred_element_type=F32)


def _dot_tn(a, b):
    return lax.dot_general(a, b, (((0,), (0,)), ((), ())), preferred_element_type=F32)


def _sigmoid(x):
    return 1.0 / (1.0 + jnp.exp(-x))


def _softplus(x):
    return jnp.maximum(x, 0.0) + jnp.log(1.0 + jnp.exp(-jnp.abs(x)))


def _prefix_rows(x):
    row = lax.broadcasted_iota(jnp.int32, x.shape, 0)
    s = 1
    while s < x.shape[0]:
        x = x + jnp.where(row >= s, pltpu.roll(x, s, 0), 0.0)
        s *= 2
    return x


def _mod_kernel(c_ref, w_ref, b_ref, o_ref):
    c = c_ref[...]
    s = c * _sigmoid(c)
    o_ref[0] = jnp.dot(s, w_ref[0], preferred_element_type=F32,
                       precision=lax.Precision.HIGHEST) + b_ref[0]


def _mod_call(c, w_mod, b_mod):
    nb = c.shape[0]
    depth, d, cols = w_mod.shape
    tn = 1536
    return pl.pallas_call(
        _mod_kernel,
        grid=(depth, cols // tn),
        in_specs=[pl.BlockSpec((nb, d), lambda l, j: (0, 0)),
                  pl.BlockSpec((1, d, tn), lambda l, j: (l, 0, j)),
                  pl.BlockSpec((1, 1, tn), lambda l, j: (l, 0, j))],
        out_specs=pl.BlockSpec((1, nb, tn), lambda l, j: (l, 0, j)),
        out_shape=jax.ShapeDtypeStruct((depth, nb, cols), F32),
        compiler_params=_cparams(("parallel", "parallel")),
        name="mod",
    )(c, w_mod, b_mod.reshape(depth, 1, cols))


def _rope(a, rc, rs1, rs2):
    return a * rc + pltpu.roll(a, 8, 1) * rs1 + pltpu.roll(a, 120, 1) * rs2


def _in_proj_kernel(x_ref, sc_ref, sh_ref, g_ref, w_ref, rc_ref, rs1_ref, rs2_ref,
                    oa_ref, og_ref, ob_ref, oc_ref, od_ref):
    x = x_ref[...]
    ms = jnp.mean(x * x, axis=-1, keepdims=True)
    h = x * lax.rsqrt(ms + NORM_EPS) * g_ref[...]
    h = h * (1.0 + sc_ref[0]) + sh_ref[0]
    hb = h.astype(BF16)
    rc, rs1, rs2 = rc_ref[...], rs1_ref[...], rs2_ref[...]

    def mm(c0, width):
        return _dot(hb, w_ref[:, c0:c0 + width])

    def plain(o_ref, col0, dst0, width, scale=None, step=512):
        for c in range(0, width, step):
            wd = min(step, width - c)
            a = mm(col0 + c, wd)
            if scale is not None:
                a = a * scale
            o_ref[:, dst0 + c:dst0 + c + wd] = a.astype(o_ref.dtype)

    def roped(o_ref, col0, dst0, width, scale):
        for c in range(0, width, 128):
            a = _rope(mm(col0 + c, 128), rc, rs1, rs2)
            if scale is not None:
                a = a * scale
            o_ref[:, dst0 + c:dst0 + c + 128] = a.astype(o_ref.dtype)

    qscale = HEAD_DIM ** -0.5
    plain(oa_ref, COL_A, 0, 1024)
    plain(og_ref, COL_G, 0, 128)
    roped(ob_ref, COL_B, 0, 384, qscale)
    roped(ob_ref, COL_B + 384, 384, 128, None)
    plain(ob_ref, COL_B + 512, 512, 128)
    plain(oc_ref, COL_C, 0, 512)
    plain(oc_ref, COL_C + 512, 512, 512, scale=ML_DIM ** -0.5)
    plain(oc_ref, COL_C + 1024, 1024, 1024)
    roped(od_ref, COL_D, 0, 1152, qscale)
    roped(od_ref, COL_D + 1152, 1152, 384, None)
    plain(od_ref, COL_D + 1536, 1536, 384, step=384)


def _in_proj_call(x, sc, sh, g, w, rc, rs1, rs2, seq_len, tm=512):
    n, d = x.shape
    per_seq = seq_len // tm
    row = lambda i: (i, 0)
    seq = lambda i: (i // per_seq, 0, 0)
    pos = lambda i: (i % per_seq, 0)
    const = lambda i: (0, 0)
    widths = (1024, 128, 640, 2048, 1920)
    dtypes = (BF16, F32, BF16, BF16, BF16)
    return pl.pallas_call(
        _in_proj_kernel,
        grid=(n // tm,),
        in_specs=[pl.BlockSpec((tm, d), row),
                  pl.BlockSpec((1, 1, d), seq),
                  pl.BlockSpec((1, 1, d), seq),
                  pl.BlockSpec((1, d), const),
                  pl.BlockSpec((d, COL_END), const, pipeline_mode=pl.Buffered(1)),
                  pl.BlockSpec((tm, 128), pos),
                  pl.BlockSpec((tm, 128), pos),
                  pl.BlockSpec((tm, 128), pos)],
        out_specs=[pl.BlockSpec((tm, wd), row) for wd in widths],
        out_shape=[jax.ShapeDtypeStruct((n, wd), dt) for wd, dt in zip(widths, dtypes)],
        compiler_params=_cparams(("parallel",)),
        name="in_proj",
    )(x, sc, sh, g, w, rc, rs1, rs2)


def _ssd_kernel(ua_ref, ug_ref, cw_ref, cb_ref, alog_ref, dtb_ref, dsk_ref, ng_ref, o_ref,
                xp_ref, xa_ref, cum_ref, dtd_ref, tr_ref, y_ref, y2_ref, sf_ref, sb_ref,
                *, seq_len):
    nc = seq_len // CHUNK
    q = CHUNK
    xp_ref[0:8, :] = jnp.zeros((8, SSD_XBC), F32)
    xp_ref[seq_len + 8:seq_len + 16, :] = jnp.zeros((8, SSD_XBC), F32)
    xp_ref[8:seq_len + 8, :] = ua_ref[:, SSD_INNER:SSD_INNER + SSD_XBC].astype(F32)
    sf_ref[...] = jnp.zeros_like(sf_ref)
    sb_ref[...] = jnp.zeros_like(sb_ref)
    a_row = -jnp.exp(alog_ref[...])
    lane = lax.broadcasted_iota(jnp.int32, (q, 128), 1)

    def prep(c, carry):
        r0 = pl.multiple_of(c * q, q)
        win = xp_ref[pl.ds(r0, q + 16), :]
        conv = cb_ref[...] + cw_ref[0:1, :] * win[6:6 + q]
        for k in range(1, SSD_CONV):
            conv = conv + cw_ref[k:k + 1, :] * win[6 + k:6 + k + q]
        xa_ref[pl.ds(r0, q), :] = conv * _sigmoid(conv)
        dt = _softplus(ug_ref[pl.ds(r0, q), :] + dtb_ref[...])
        dta = dt * a_row
        p = _prefix_rows(dta)
        s = p[q - 1:q, :] - p + dta
        cum = jnp.where(lane < SSD_HEADS, p, s)
        cum_ref[pl.ds(r0, q), :] = cum
        dtd_ref[pl.ds(r0, q), :] = dt
        packed = jnp.where(lane < 16, cum, pltpu.roll(dt, 16, 1))
        tr_ref[c] = packed.T
        return carry

    lax.fori_loop(0, nc, prep, 0)

    ri = lax.broadcasted_iota(jnp.int32, (q, q), 0)
    ci = lax.broadcasted_iota(jnp.int32, (q, q), 1)
    lower_incl = ci <= ri
    lower = ci < ri
    upper = ci > ri

    def main(c, carry):
        r0 = pl.multiple_of(c * q, q)
        xa = xa_ref[pl.ds(r0, q), :]
        cum = cum_ref[pl.ds(r0, q), :]
        dtd = dtd_ref[pl.ds(r0, q), :]
        tr = tr_ref[c]
        ys = []
        for g in range(2):
            bg = xa[:, SSD_INNER + g * 64:SSD_INNER + (g + 1) * 64].astype(BF16)
            cg = xa[:, SSD_INNER + 128 + g * 64:SSD_INNER + 128 + (g + 1) * 64].astype(BF16)
            cb = _dot_nt(cg, bg)
            for r in range(3):
                h = 3 * g + r
                cf_col, rb_col = cum[:, h:h + 1], cum[:, 6 + h:7 + h]
                cf_row, rb_row = tr[h:h + 1, :], tr[6 + h:7 + h, :]
                dtf_row, dtb_row = tr[16 + h:17 + h, :], tr[22 + h:23 + h, :]
                arg = jnp.where(lower_incl, cf_col - cf_row, rb_col - rb_row)
                fac = jnp.where(lower, dtf_row, jnp.where(upper, dtb_row, dtf_row + dtb_row))
                w = (cb * jnp.exp(arg) * fac).astype(BF16)
                xh = xa[:, h * 64:(h + 1) * 64]
                yh = _dot(w, xh.astype(BF16))
                sf = sf_ref[h]
                yh = yh + _dot(cg, sf.astype(BF16)) * jnp.exp(cf_col)
                tf = cum[q - 1:q, h:h + 1]
                wcol = jnp.exp(tf - cf_col) * dtd[:, h:h + 1]
                sf_ref[h] = sf * jnp.exp(tf) + _dot_tn(bg, (xh * wcol).astype(BF16))
                ys.append(yh)
        y_ref[pl.ds(r0, q), :] = jnp.concatenate(ys, axis=1)

        cbk = nc - 1 - c
        r1 = pl.multiple_of(cbk * q, q)
        xa = xa_ref[pl.ds(r1, q), :]
        cum = cum_ref[pl.ds(r1, q), :]
        dtd = dtd_ref[pl.ds(r1, q), :]
        ys = []
        for g in range(2):
            bg = xa[:, SSD_INNER + g * 64:SSD_INNER + (g + 1) * 64].astype(BF16)
            cg = xa[:, SSD_INNER + 128 + g * 64:SSD_INNER + 128 + (g + 1) * 64].astype(BF16)
            for r in range(3):
                h = 3 * g + r
                rb_col = cum[:, 6 + h:7 + h]
                sb = sb_ref[h]
                ys.append(_dot(cg, sb.astype(BF16)) * jnp.exp(rb_col))
                tb = cum[0:1, 6 + h:7 + h]
                wcol = jnp.exp(tb - rb_col) * dtd[:, 6 + h:7 + h]
                xh = xa[:, h * 64:(h + 1) * 64]
                sb_ref[h] = sb * jnp.exp(tb) + _dot_tn(bg, (xh * wcol).astype(BF16))
        y2_ref[pl.ds(r1, q), :] = jnp.concatenate(ys, axis=1)
        return carry

    lax.fori_loop(0, nc, main, 0)

    def fin(c, carry):
        r0 = pl.multiple_of(c * q, q)
        xs = xa_ref[pl.ds(r0, q), 0:SSD_INNER]
        y = y_ref[pl.ds(r0, q), :] + y2_ref[pl.ds(r0, q), :] + xs * dsk_ref[...]
        z = ua_ref[pl.ds(r0, q), 0:SSD_INNER].astype(F32)
        v = y * (z * _sigmoid(z))
        ms = jnp.mean(v * v, axis=-1, keepdims=True)
        o_ref[pl.ds(r0, q), :] = (v * lax.rsqrt(ms + NORM_EPS) * ng_ref[...]).astype(o_ref.dtype)
        return carry

    lax.fori_loop(0, nc, fin, 0)


def _ssd_call(ua, ug, cw, cb, alog, dtb, dsk, ng, seq_len):
    n = ua.shape[0]
    nseq = n // seq_len
    nc = seq_len // CHUNK
    seq = lambda b: (b, 0)
    const = lambda b: (0, 0)
    return pl.pallas_call(
        functools.partial(_ssd_kernel, seq_len=seq_len),
        grid=(nseq,),
        in_specs=[pl.BlockSpec((seq_len, 1024), seq),
                  pl.BlockSpec((seq_len, 128), seq),
                  pl.BlockSpec((8, SSD_XBC), const),
                  pl.BlockSpec((1, SSD_XBC), const),
                  pl.BlockSpec((1, 128), const),
                  pl.BlockSpec((1, 128), const),
                  pl.BlockSpec((1, SSD_INNER), const),
                  pl.BlockSpec((1, SSD_INNER), const)],
        out_specs=pl.BlockSpec((seq_len, SSD_INNER), seq),
        out_shape=jax.ShapeDtypeStruct((n, SSD_INNER), BF16),
        scratch_shapes=[pltpu.VMEM((seq_len + 16, SSD_XBC), F32),
                        pltpu.VMEM((seq_len, SSD_XBC), F32),
                        pltpu.VMEM((seq_len, 128), F32),
                        pltpu.VMEM((seq_len, 128), F32),
                        pltpu.VMEM((nc, 128, CHUNK), F32),
                        pltpu.VMEM((seq_len, SSD_INNER), F32),
                        pltpu.VMEM((seq_len, SSD_INNER), F32),
                        pltpu.VMEM((SSD_HEADS, SSD_STATE, HEAD_DIM), F32),
                        pltpu.VMEM((SSD_HEADS, SSD_STATE, HEAD_DIM), F32)],
        compiler_params=_cparams(("parallel",)),
        name="ssd",
    )(ua, ug, cw, cb, alog, dtb, dsk, ng)


def _swa_kernel(sink_ref, u_ref, o_ref, kp_ref, vp_ref, bias_ref, *, seq_len):
    q = CHUNK
    nb = seq_len // q
    zeros = jnp.zeros((q, 128), BF16)
    kp_ref[0:q, :] = zeros
    vp_ref[0:q, :] = zeros
    kp_ref[seq_len + q:seq_len + 2 * q, :] = zeros
    vp_ref[seq_len + q:seq_len + 2 * q, :] = zeros
    kp_ref[q:seq_len + q, :] = u_ref[:, 384:512]
    vp_ref[q:seq_len + q, :] = u_ref[:, 512:640]
    row = lax.broadcasted_iota(jnp.int32, (3 * q, 3 * q), 0) & (q - 1)
    col = lax.broadcasted_iota(jnp.int32, (3 * q, 3 * q), 1)
    bias_ref[...] = jnp.where(jnp.abs(col - q - row) <= SWA_SIDE, 0.0, NEG)
    rgrp = lax.broadcasted_iota(jnp.int32, (3 * q, 1), 0) // q
    col1 = lax.broadcasted_iota(jnp.int32, (1, 3 * q), 1)
    lane1 = lax.broadcasted_iota(jnp.int32, (1, 128), 1)
    lo_f = jnp.where(lane1 < HEAD_DIM, 1.0, 0.0)
    hi_f = 1.0 - lo_f
    halves = ((lo_f.astype(BF16), lo_f), (hi_f.astype(BF16), hi_f))

    def body(qb, carry):
        q0 = pl.multiple_of(qb * q, q)
        qs = jnp.concatenate([u_ref[pl.ds(q0, q), j * 128:(j + 1) * 128] for j in range(3)], axis=0)
        kw = kp_ref[pl.ds(q0, 3 * q), :]
        vw = vp_ref[pl.ds(q0, 3 * q), :]
        inside = (col1 >= q - q0) & (col1 < seq_len + q - q0)
        bias = bias_ref[...] + jnp.where(inside, 0.0, NEG)
        o = None
        for hk, (sel_b, sel_f) in enumerate(halves):
            s = _dot_nt(qs * sel_b, kw) + bias
            sk = jnp.where(rgrp == 0, sink_ref[3 * hk],
                           jnp.where(rgrp == 1, sink_ref[3 * hk + 1], sink_ref[3 * hk + 2]))
            m = jnp.maximum(jnp.max(s, axis=-1, keepdims=True), sk)
            p = jnp.exp(s - m)
            l = jnp.sum(p, axis=-1, keepdims=True) + jnp.exp(sk - m)
            oh = (_dot(p.astype(BF16), vw) / l) * sel_f
            o = oh if o is None else o + oh
        o_ref[pl.ds(q0, q), :] = jnp.concatenate(
            [o[j * q:(j + 1) * q, :] for j in range(3)], axis=1).astype(o_ref.dtype)
        return carry

    lax.fori_loop(0, nb, body, 0, unroll=2)


def _swa_call(ub, sink, seq_len):
    n = ub.shape[0]
    nseq = n // seq_len
    return pl.pallas_call(
        functools.partial(_swa_kernel, seq_len=seq_len),
        grid=(nseq,),
        in_specs=[pl.BlockSpec(memory_space=pltpu.SMEM),
                  pl.BlockSpec((seq_len, 640), lambda b: (b, 0))],
        out_specs=pl.BlockSpec((seq_len, 384), lambda b: (b, 0)),
        out_shape=jax.ShapeDtypeStruct((n, 384), BF16),
        scratch_shapes=[pltpu.VMEM((seq_len + 2 * CHUNK, 128), BF16),
                        pltpu.VMEM((seq_len + 2 * CHUNK, 128), BF16),
                        pltpu.VMEM((3 * CHUNK, 3 * CHUNK), F32)],
        compiler_params=_cparams(("parallel",)),
        name="swa",
    )(sink, ub)


def _mlstm_kernel(uc_ref, ug_ref, gb_ref, ng_ref, o_ref,
                  gcol_ref, gtr_ref, hacc_ref, ct_ref, n_ref, m_ref, *, seq_len):
    q = CHUNK
    nc = seq_len // q
    lane = lax.broadcasted_iota(jnp.int32, (q, 128), 1)
    ct_ref[...] = jnp.zeros_like(ct_ref)
    n_ref[...] = jnp.zeros_like(n_ref)
    m_ref[...] = jnp.zeros_like(m_ref)
    hacc_ref[...] = jnp.zeros_like(hacc_ref)

    def prep(c, carry):
        r0 = pl.multiple_of(c * q, q)
        raw = ug_ref[pl.ds(r0, q), :] + gb_ref[...]
        lf = jnp.minimum(raw, 0.0) - jnp.log(1.0 + jnp.exp(-jnp.abs(raw)))
        p = _prefix_rows(lf)
        s = p[q - 1:q, :] - p + lf
        g = jnp.where(lane < LANE_FG, raw, jnp.where(lane < LANE_FG + ML_HEADS, p, s))
        gcol_ref[pl.ds(r0, q), :] = g
        gtr_ref[c] = g.T
        return carry

    lax.fori_loop(0, nc, prep, 0)

    ri = lax.broadcasted_iota(jnp.int32, (q, q), 0)
    ci = lax.broadcasted_iota(jnp.int32, (q, q), 1)
    masks = (ci <= ri, ci >= ri)

    def main(c, carry):
        for dirn in range(2):
            ck = c if dirn == 0 else nc - 1 - c
            r0 = pl.multiple_of(ck * q, q)
            g = gcol_ref[pl.ds(r0, q), :]
            gt = gtr_ref[ck]
            for h in range(ML_HEADS):
                s_idx = dirn * ML_HEADS + h
                li = LANE_IG + s_idx
                lb = LANE_FG + s_idx
                b_col, b_row = g[:, lb:lb + 1], gt[lb:lb + 1, :]
                i_col, i_row = g[:, li:li + 1], gt[li:li + 1, :]
                logd = jnp.where(masks[dirn], b_col - b_row + i_row, NEG)
                m_st = m_ref[s_idx][0:1, 0:1]
                m_inter = b_col + m_st
                m_t = jnp.maximum(jnp.max(logd, axis=-1, keepdims=True), m_inter)
                qh = uc_ref[pl.ds(r0, q), h * ML_PAD:(h + 1) * ML_PAD]
                kh = uc_ref[pl.ds(r0, q), 512 + h * ML_PAD:512 + (h + 1) * ML_PAD]
                vh = uc_ref[pl.ds(r0, q), 1024 + h * ML_PAD:1024 + (h + 1) * ML_PAD]
                sm = _dot_nt(qh, kh) * jnp.exp(logd - m_t)
                inter = jnp.exp(m_inter - m_t)
                ct = ct_ref[s_idx]
                nrow = n_ref[s_idx][0:1, :]
                num = _dot(sm.astype(BF16), vh) + inter * _dot(qh, ct.astype(BF16))
                den = (jnp.sum(sm, axis=-1, keepdims=True)
                       + inter * jnp.sum(qh.astype(F32) * nrow, axis=-1, keepdims=True))
                hh = num / jnp.maximum(jnp.abs(den), jnp.exp(-m_t))
                hacc_ref[pl.ds(r0, q), h * ML_PAD:(h + 1) * ML_PAD] += hh
                edge = q - 1 if dirn == 0 else 0
                tot = g[edge:edge + 1, lb:lb + 1]
                m_new = jnp.maximum(tot + m_st,
                                    jnp.max(tot - b_row + i_row, axis=-1, keepdims=True))
                wk_col = jnp.exp(tot - b_col + i_col - m_new)
                keep = jnp.exp(tot + m_st - m_new)
                kf = kh.astype(F32)
                vw = (vh.astype(F32) * wk_col).astype(BF16)
                ct_ref[s_idx] = keep * ct + _dot_tn(kh, vw)
                n_new = keep * nrow + jnp.sum(kf * wk_col, axis=0, keepdims=True)
                n_ref[s_idx] = jnp.broadcast_to(n_new, (8, ML_PAD))
                m_ref[s_idx] = jnp.broadcast_to(m_new, (8, 128))
        return carry

    lax.fori_loop(0, nc, main, 0)

    def fin(c, carry):
        r0 = pl.multiple_of(c * q, q)
        outs = []
        for h in range(ML_HEADS):
            hs = hacc_ref[pl.ds(r0, q), h * ML_PAD:(h + 1) * ML_PAD]
            ms = jnp.sum(hs * hs, axis=-1, keepdims=True) * (1.0 / ML_DIM)
            hn = hs * lax.rsqrt(ms + NORM_EPS) * ng_ref[:, h * ML_PAD:(h + 1) * ML_PAD]
            og = uc_ref[pl.ds(r0, q), 1536 + h * ML_PAD:1536 + (h + 1) * ML_PAD].astype(F32)
            outs.append(hn * _sigmoid(og))
        o_ref[pl.ds(r0, q), :] = jnp.concatenate(outs, axis=1).astype(o_ref.dtype)
        return carry

    lax.fori_loop(0, nc, fin, 0)


def _mlstm_call(uc, ug, gb, ng, seq_len):
    n = uc.shape[0]
    nseq = n // seq_len
    nc = seq_len // CHUNK
    seq = lambda b: (b, 0)
    const = lambda b: (0, 0)
    width = ML_HEADS * ML_PAD
    return pl.pallas_call(
        functools.partial(_mlstm_kernel, seq_len=seq_len),
        grid=(nseq,),
        in_specs=[pl.BlockSpec((seq_len, 4 * width), seq),
                  pl.BlockSpec((seq_len, 128), seq),
                  pl.BlockSpec((1, 128), const),
                  pl.BlockSpec((1, width), const)],
        out_specs=pl.BlockSpec((seq_len, width), seq),
        out_shape=jax.ShapeDtypeStruct((n, width), BF16),
        scratch_shapes=[pltpu.VMEM((seq_len, 128), F32),
                        pltpu.VMEM((nc, 128, CHUNK), F32),
                        pltpu.VMEM((seq_len, width), F32),
                        pltpu.VMEM((2 * ML_HEADS, ML_PAD, ML_PAD), F32),
                        pltpu.VMEM((2 * ML_HEADS, 8, ML_PAD), F32),
                        pltpu.VMEM((2 * ML_HEADS, 8, 128), F32)],
        compiler_params=_cparams(("parallel",)),
        name="mlstm",
    )(uc, ug, gb, ng)


def _attend_units(units, lo_b, hi_b, lo_f, hi_f):
    n = units[0][0].shape[0]
    ss = []
    for q, kw, _, bias in units:
        s2 = _dot_nt(jnp.concatenate([q * lo_b, q * hi_b], axis=0), kw)
        ss.append(jnp.concatenate([s2[0:n] + bias, s2[n:2 * n] + bias], axis=0))
    ms = [jnp.max(s, axis=-1, keepdims=True) for s in ss]
    ps = [jnp.exp(s - m) for s, m in zip(ss, ms)]
    ls = [jnp.sum(p, axis=-1, keepdims=True) for p in ps]
    os_ = [_dot(p.astype(BF16), u[2]) / l for p, u, l in zip(ps, units, ls)]
    res = []
    for o2, m, l in zip(os_, ms, ls):
        lse2 = m + jnp.log(l)
        res.append((o2[0:n] * lo_f + o2[n:2 * n] * hi_f, lse2[0:n] * lo_f + lse2[n:2 * n] * hi_f))
    return res


def _dil_kernel(q0_ref, q1_ref, q2_ref, k_ref, v_ref, o_ref,
                qf_ref, kf_ref, vf_ref, kc0_ref, vc0_ref, kc1_ref, vc1_ref, og_ref, lg_ref,
                qc2_ref, kc2_ref, vc2_ref, og2_ref, lg2_ref, band_ref, band2_ref, *, seq_len):
    qr = CHUNK
    pad = DIL_SIDE
    d1, d2 = DIL_CONFIGS[1][1], DIL_CONFIGS[2][1]
    n1, n2 = seq_len // d1, seq_len // d2
    assert n2 == qr and n1 % qr == 0
    qf_ref[0] = q1_ref[...].astype(F32)
    qf_ref[1] = q2_ref[...].astype(F32)
    kf_ref[...] = k_ref[...].astype(F32)
    vf_ref[...] = v_ref[...].astype(F32)
    zeros = jnp.zeros((pad, 128), BF16)
    kc0_ref[0:pad, :] = zeros
    vc0_ref[0:pad, :] = zeros
    kc0_ref[pad + seq_len:2 * pad + seq_len, :] = zeros
    vc0_ref[pad + seq_len:2 * pad + seq_len, :] = zeros
    kc0_ref[pad:pad + seq_len, :] = k_ref[...]
    vc0_ref[pad:pad + seq_len, :] = v_ref[...]
    for r in range(d1):
        kc1_ref[r, 0:pad, :] = zeros
        vc1_ref[r, 0:pad, :] = zeros
        kc1_ref[r, pad + n1:2 * pad + n1, :] = zeros
        vc1_ref[r, pad + n1:2 * pad + n1, :] = zeros
        kc1_ref[r, pad:pad + n1, :] = kf_ref[pl.ds(r, n1, stride=d1), :].astype(BF16)
        vc1_ref[r, pad:pad + n1, :] = vf_ref[pl.ds(r, n1, stride=d1), :].astype(BF16)

    row = lax.broadcasted_iota(jnp.int32, (qr, 2 * qr), 0)
    col = lax.broadcasted_iota(jnp.int32, (qr, 2 * qr), 1)
    band_ref[...] = jnp.where(jnp.abs(col - pad - row) <= DIL_SIDE, 0.0, NEG)
    row2 = lax.broadcasted_iota(jnp.int32, (qr, qr), 0)
    col2 = lax.broadcasted_iota(jnp.int32, (qr, qr), 1)
    band2_ref[...] = jnp.where(jnp.abs(col2 - row2) <= DIL_SIDE, 0.0, NEG)
    lane1 = lax.broadcasted_iota(jnp.int32, (1, 128), 1)
    col1 = lax.broadcasted_iota(jnp.int32, (1, 2 * qr), 1)
    lo_f = jnp.where(lane1 < HEAD_DIM, 1.0, 0.0)
    hi_f = 1.0 - lo_f
    sel = (lo_f.astype(BF16), hi_f.astype(BF16), lo_f, hi_f)

    def window_bias(u0, n):
        inside = (col1 >= pad - u0) & (col1 < n + pad - u0)
        return band_ref[...] + jnp.where(inside, 0.0, NEG)

    def batch0(b4, carry):
        units, starts = [], []
        for i in range(DIL_BATCH):
            u0 = pl.multiple_of((b4 * DIL_BATCH + i) * qr, qr)
            starts.append(u0)
            units.append((q0_ref[pl.ds(u0, qr), :], kc0_ref[pl.ds(u0, 2 * qr), :],
                          vc0_ref[pl.ds(u0, 2 * qr), :], window_bias(u0, seq_len)))
        for u0, (o, lse) in zip(starts, _attend_units(units, *sel)):
            og_ref[0, pl.ds(u0, qr), :] = o
            lg_ref[0, pl.ds(u0, qr), :] = lse
        return carry

    lax.fori_loop(0, seq_len // qr // DIL_BATCH, batch0, 0)

    nb1 = n1 // qr
    assert nb1 == DIL_BATCH

    def batch1(r, carry):
        units, rowsl = [], []
        for i in range(DIL_BATCH):
            u0 = i * qr
            rows = pl.ds(r + u0 * d1, qr, stride=d1)
            rowsl.append(rows)
            units.append((qf_ref[0, rows, :].astype(BF16), kc1_ref[r, u0:u0 + 2 * qr, :],
                          vc1_ref[r, u0:u0 + 2 * qr, :], window_bias(u0, n1)))
        for rows, (o, lse) in zip(rowsl, _attend_units(units, *sel)):
            og_ref[1, rows, :] = o
            lg_ref[1, rows, :] = lse
        return carry

    lax.fori_loop(0, d1, batch1, 0)

    for r in range(d2):
        rows = pl.ds(r, qr, stride=d2)
        qc2_ref[r] = qf_ref[1, rows, :].astype(BF16)
        kc2_ref[r] = kf_ref[rows, :].astype(BF16)
        vc2_ref[r] = vf_ref[rows, :].astype(BF16)

    def batch2(b4, carry):
        rs = [b4 * DIL_BATCH + i for i in range(DIL_BATCH)]
        units = [(qc2_ref[r], kc2_ref[r], vc2_ref[r], band2_ref[...]) for r in rs]
        for r, (o, lse) in zip(rs, _attend_units(units, *sel)):
            og2_ref[r] = o
            lg2_ref[r] = lse
        return carry

    lax.fori_loop(0, d2 // DIL_BATCH, batch2, 0)
    for r in range(d2):
        rows = pl.ds(r, qr, stride=d2)
        og_ref[2, rows, :] = og2_ref[r]
        lg_ref[2, rows, :] = lg2_ref[r]

    def fin(c, carry):
        r0 = pl.multiple_of(c * qr, qr)
        rows = pl.ds(r0, qr)
        l0, l1, l2 = lg_ref[0, rows, :], lg_ref[1, rows, :], lg_ref[2, rows, :]
        m = jnp.maximum(jnp.maximum(l0, l1), l2)
        w0, w1, w2 = jnp.exp(l0 - m), jnp.exp(l1 - m), jnp.exp(l2 - m)
        y = (og_ref[0, rows, :] * w0 + og_ref[1, rows, :] * w1 + og_ref[2, rows, :] * w2)
        o_ref[rows, :] = (y / (w0 + w1 + w2)).astype(o_ref.dtype)
        return carry

    lax.fori_loop(0, seq_len // qr, fin, 0, unroll=2)


def _dil_call(ud, seq_len):
    n = ud.shape[0]
    nseq = n // seq_len
    blk = (seq_len, 128)
    d1, d2 = DIL_CONFIGS[1][1], DIL_CONFIGS[2][1]
    n1 = seq_len // d1

    def col(off):
        return lambda b, p: (b, off + p)

    return pl.pallas_call(
        functools.partial(_dil_kernel, seq_len=seq_len),
        grid=(nseq, 3),
        in_specs=[pl.BlockSpec(blk, col(0)), pl.BlockSpec(blk, col(3)), pl.BlockSpec(blk, col(6)),
                  pl.BlockSpec(blk, col(9)), pl.BlockSpec(blk, col(12))],
        out_specs=pl.BlockSpec(blk, col(0)),
        out_shape=jax.ShapeDtypeStruct((n, 384), BF16),
        scratch_shapes=[pltpu.VMEM((2, seq_len, 128), F32),
                        pltpu.VMEM((seq_len, 128), F32),
                        pltpu.VMEM((seq_len, 128), F32),
                        pltpu.VMEM((seq_len + 2 * DIL_SIDE, 128), BF16),
                        pltpu.VMEM((seq_len + 2 * DIL_SIDE, 128), BF16),
                        pltpu.VMEM((d1, n1 + 2 * DIL_SIDE, 128), BF16),
                        pltpu.VMEM((d1, n1 + 2 * DIL_SIDE, 128), BF16),
                        pltpu.VMEM((3, seq_len, 128), F32),
                        pltpu.VMEM((3, seq_len, 128), F32)]
                       + [pltpu.VMEM((d2, CHUNK, 128), BF16)] * 3
                       + [pltpu.VMEM((d2, CHUNK, 128), F32)] * 2
                       + [pltpu.VMEM((CHUNK, 2 * CHUNK), F32), pltpu.VMEM((CHUNK, CHUNK), F32)],
        compiler_params=_cparams(("parallel", "parallel")),
        name="dil",
    )(ud, ud, ud, ud, ud)


def _out_proj_kernel(ya_ref, yb_ref, yc_ref, yd_ref, x_ref, g1_ref, sc_ref, sh_ref,
                     pg_ref, fg_ref, wa_ref, wb_ref, wc_ref, wd_ref, wrh_ref, wrl_ref, br_ref,
                     x1_ref, h2_ref, rt_ref):
    for r0 in range(0, x_ref.shape[0], OUT_SUB):
        rs = slice(r0, r0 + OUT_SUB)
        y = (_dot(ya_ref[rs, :], wa_ref[...]) + _dot(yb_ref[rs, :], wb_ref[...])
             + _dot(yc_ref[rs, :], wc_ref[...]) + _dot(yd_ref[rs, :], wd_ref[...]))
        ms = jnp.mean(y * y, axis=-1, keepdims=True)
        yn = y * lax.rsqrt(ms + NORM_EPS) * pg_ref[...]
        x1 = x_ref[rs, :] + g1_ref[0] * yn
        x1_ref[rs, :] = x1
        ms2 = jnp.mean(x1 * x1, axis=-1, keepdims=True)
        h2 = x1 * lax.rsqrt(ms2 + NORM_EPS) * fg_ref[...]
        h2 = h2 * (1.0 + sc_ref[0]) + sh_ref[0]
        h2_ref[rs, :] = h2
        h_hi = h2.astype(BF16)
        h_lo = (h2 - h_hi.astype(F32)).astype(BF16)
        logits = (_dot(h_hi, wrh_ref[...]) + _dot(h_lo, wrh_ref[...]) + _dot(h_hi, wrl_ref[...])
                  + br_ref[...])
        lane = lax.broadcasted_iota(jnp.int32, logits.shape, 1)
        vals, idxs = [], []
        for _ in range(TOP_K):
            m = jnp.max(logits, axis=-1, keepdims=True)
            idx = jnp.min(jnp.where(logits == m, lane, 128), axis=-1, keepdims=True)
            vals.append(m)
            idxs.append(idx)
            logits = jnp.where(lane == idx, -3e38, logits)
        es = [jnp.exp(v - vals[0]) for v in vals]
        tot = es[0] + es[1] + es[2] + es[3]
        rt = jnp.zeros(lane.shape, F32)
        for k in range(TOP_K):
            rt = jnp.where(lane == k, idxs[k].astype(F32), rt)
            rt = jnp.where(lane == TOP_K + k, es[k] / tot, rt)
        rt_ref[rs, :] = rt[:, 0:2 * TOP_K]


def _out_proj_call(ya, yb, yc, yd, x, g1, sc, sh, pg, fg, wa, wb, wc, wd, wrh, wrl, br, seq_len, tm=512):
    n, d = x.shape
    per_seq = seq_len // tm
    row = lambda i: (i, 0)
    seq = lambda i: (i // per_seq, 0, 0)
    const = lambda i: (0, 0)

    def full(a):
        return pl.BlockSpec(a.shape, const)

    return pl.pallas_call(
        _out_proj_kernel,
        grid=(n // tm,),
        in_specs=[pl.BlockSpec((tm, ya.shape[1]), row), pl.BlockSpec((tm, yb.shape[1]), row),
                  pl.BlockSpec((tm, yc.shape[1]), row), pl.BlockSpec((tm, yd.shape[1]), row),
                  pl.BlockSpec((tm, d), row),
                  pl.BlockSpec((1, 1, d), seq), pl.BlockSpec((1, 1, d), seq),
                  pl.BlockSpec((1, 1, d), seq),
                  full(pg), full(fg), full(wa), full(wb), full(wc), full(wd), full(wrh), full(wrl), full(br)],
        out_specs=[pl.BlockSpec((tm, d), row), pl.BlockSpec((tm, d), row),
                   pl.BlockSpec((tm, 2 * TOP_K), row)],
        out_shape=[jax.ShapeDtypeStruct((n, d), F32), jax.ShapeDtypeStruct((n, d), F32),
                   jax.ShapeDtypeStruct((n, 2 * TOP_K), F32)],
        compiler_params=_cparams(("parallel",)),
        name="out_proj",
    )(ya, yb, yc, yd, x, g1, sc, sh, pg, fg, wa, wb, wc, wd, wrh, wrl, br)


def _moe_kernel(nb_ref, be_ref, bs_ref, bv_ref, idx_ref, gate_ref, h_ref, wgu_ref, bgu_ref, wdn_ref,
                bdn_ref, o_ref, xs0_ref, xs1_ref, ys0_ref, ys1_ref):
    t = pl.program_id(0)
    j = pl.program_id(1)
    nb = nb_ref[t]
    rows = MOE_ROWS
    nslot = pl.num_programs(1) + 1

    def gather(dst_ref, blk):
        base = bs_ref[t * nslot + blk]
        for r in range(rows):
            i = idx_ref[0, 0, base + r]
            dst_ref[r:r + 1, :] = h_ref[pl.ds(i, 1), :]

    def scatter(src_ref, blk, nvalid):
        base = bs_ref[t * nslot + blk]
        i0 = idx_ref[0, 0, base]
        for r0 in range(0, rows, MOE_GROUP):
            ks = range(MOE_GROUP)
            ok = [r0 + k < nvalid for k in ks]
            ii = [jnp.where(ok[k], idx_ref[0, 0, base + r0 + k], i0) for k in ks]
            gg = [jnp.where(ok[k], gate_ref[0, 0, base + r0 + k], 0.0) for k in ks]
            cur = [o_ref[pl.ds(ii[k], 1), :] for k in ks]
            for k in reversed(ks):
                o_ref[pl.ds(ii[k], 1), :] = cur[k] + gg[k] * src_ref[r0 + k:r0 + k + 1, :]

    def expert(xs_ref, ys_ref):
        gu = _dot(xs_ref[...].astype(BF16), wgu_ref[0]) + bgu_ref[0]
        gate = jnp.minimum(gu[:, :D_EXPERT], SWIGLU_LIMIT)
        up = jnp.clip(gu[:, D_EXPERT:], -SWIGLU_LIMIT, SWIGLU_LIMIT)
        act = (up + 1.0) * gate * _sigmoid(SWIGLU_ALPHA * gate)
        ys_ref[...] = _dot(act.astype(BF16), wdn_ref[0]) + bdn_ref[0]

    @pl.when(j == 0)
    def _():
        o_ref[...] = jnp.zeros_like(o_ref)
        ys1_ref[...] = jnp.zeros_like(ys1_ref)
        gather(xs0_ref, 0)

    prev = jnp.maximum(j - 1, 0)
    nv_prev = jnp.where(j > 0, bv_ref[t * nslot + prev], 0)
    bufs = ((xs0_ref, xs1_ref, ys0_ref, ys1_ref), (xs1_ref, xs0_ref, ys1_ref, ys0_ref))
    for par, (xs_cur, xs_nxt, ys_cur, ys_prv) in enumerate(bufs):
        @pl.when((j < nb) & (j % 2 == par))
        def _(xs_cur=xs_cur, xs_nxt=xs_nxt, ys_cur=ys_cur, ys_prv=ys_prv):
            gather(xs_nxt, j + 1)
            expert(xs_cur, ys_cur)
            scatter(ys_prv, prev, nv_prev)

        @pl.when((j == nb) & (j % 2 == par))
        def _(ys_prv=ys_prv):
            scatter(ys_prv, prev, nv_prev)


def _moe_call(nb, be, bs, bv, idx, gates, h2, wgu, bgu, wdn, bdn, tile):
    n, d = h2.shape
    ntile = n // tile
    plen = idx.shape[-1]
    nslot = be.shape[0] // ntile

    def wmap(t, j, nb_ref, be_ref, bs_ref, bv_ref):
        return (be_ref[t * nslot + j], 0, 0)

    tmap3 = lambda t, j, *_: (t, 0, 0)
    tmap2 = lambda t, j, *_: (t, 0)
    grid_spec = pltpu.PrefetchScalarGridSpec(
        num_scalar_prefetch=4,
        grid=(ntile, nslot - 1),
        in_specs=[pl.BlockSpec((1, 1, plen), tmap3, memory_space=pltpu.SMEM),
                  pl.BlockSpec((1, 1, plen), tmap3, memory_space=pltpu.SMEM),
                  pl.BlockSpec((tile, d), tmap2, pipeline_mode=pl.Buffered(1)),
                  pl.BlockSpec((1, d, 2 * D_EXPERT), wmap),
                  pl.BlockSpec((1, 1, 2 * D_EXPERT), wmap),
                  pl.BlockSpec((1, D_EXPERT, d), wmap),
                  pl.BlockSpec((1, 1, d), wmap)],
        out_specs=pl.BlockSpec((tile, d), tmap2, pipeline_mode=pl.Buffered(1)),
        scratch_shapes=[pltpu.VMEM((MOE_ROWS, d), F32)] * 4,
    )
    return pl.pallas_call(
        _moe_kernel,
        grid_spec=grid_spec,
        out_shape=jax.ShapeDtypeStruct((n, d), F32),
        compiler_params=_cparams(("parallel", "arbitrary")),
        name="moe",
    )(nb, be, bs, bv, idx, gates, h2, wgu, bgu, wdn, bdn)


def _route(rt, tile):
    n = rt.shape[0]
    ntile = n // tile
    npair = tile * TOP_K
    nslot = npair // MOE_ROWS + N_EXPERTS + 2
    e = rt[:, :TOP_K].astype(jnp.int32).reshape(ntile, npair)
    g = rt[:, TOP_K:].reshape(ntile, npair)
    tok = jnp.broadcast_to((jnp.arange(npair, dtype=jnp.int32) // TOP_K)[None, :], (ntile, npair))
    _, tok_s, g_s = lax.sort((e, tok, g), dimension=1, is_stable=True, num_keys=1)
    experts = jnp.arange(N_EXPERTS, dtype=jnp.int32)
    counts = jnp.sum(e[:, :, None] == experts, axis=1, dtype=jnp.int32)
    nblocks = (counts + MOE_ROWS - 1) // MOE_ROWS
    bends = jnp.cumsum(nblocks, axis=-1)
    starts = jnp.cumsum(counts, axis=-1) - counts
    nb = bends[:, -1]
    slot = jnp.minimum(jnp.arange(nslot, dtype=jnp.int32)[None, :], nb[:, None] - 1)
    be = jnp.sum(slot[:, :, None] >= bends[:, None, :], axis=-1, dtype=jnp.int32)
    onehot = (be[:, :, None] == experts).astype(jnp.int32)
    pick = lambda v: jnp.sum(onehot * v[:, None, :], axis=-1)
    within = slot - pick(bends - nblocks)
    bs = pick(starts) + within * MOE_ROWS
    bv = jnp.clip(pick(counts) - within * MOE_ROWS, 0, MOE_ROWS)
    padw = ((0, 0), (0, MOE_ROWS))
    idx = jnp.pad(tok_s, padw).reshape(ntile, 1, -1)
    gl = jnp.pad(g_s, padw).reshape(ntile, 1, -1)
    flat = lambda v: v.reshape(-1).astype(jnp.int32)
    return flat(nb), flat(be), flat(bs), flat(bv), idx, gl


def _ffn_res_kernel(x_ref, f_ref, g2_ref, pg_ref, o_ref):
    f = f_ref[...]
    ms = jnp.mean(f * f, axis=-1, keepdims=True)
    o_ref[...] = x_ref[...] + g2_ref[0] * (f * lax.rsqrt(ms + NORM_EPS) * pg_ref[...])


def _ffn_res_call(x1, f, g2, pg, seq_len, tm=1024):
    n, d = x1.shape
    per_seq = seq_len // tm
    row = lambda i: (i, 0)
    return pl.pallas_call(
        _ffn_res_kernel,
        grid=(n // tm,),
        in_specs=[pl.BlockSpec((tm, d), row), pl.BlockSpec((tm, d), row),
                  pl.BlockSpec((1, 1, d), lambda i: (i // per_seq, 0, 0)),
                  pl.BlockSpec((1, d), lambda i: (0, 0))],
        out_specs=pl.BlockSpec((tm, d), row),
        out_shape=jax.ShapeDtypeStruct((n, d), F32),
        compiler_params=_cparams(("parallel",)),
        name="ffn_res",
    )(x1, f, g2, pg)


def _rope_tables(seq_len):
    half = 8
    inv = 1.0 / (ROPE_THETA ** (jnp.arange(half, dtype=F32) * (2.0 / 16)))
    ang = jnp.arange(seq_len, dtype=F32)[:, None] * inv[None, :]
    cos, sin = jnp.cos(ang), jnp.sin(ang)
    ones = jnp.ones((seq_len, 48), F32)
    zeros8 = jnp.zeros((seq_len, 8), F32)
    zeros48 = jnp.zeros((seq_len, 48), F32)
    rc = jnp.concatenate([cos, cos, ones], axis=1)
    rs1 = jnp.concatenate([zeros8, sin, zeros48], axis=1)
    rs2 = jnp.concatenate([-sin, zeros8, zeros48], axis=1)
    tile2 = lambda a: jnp.concatenate([a, a], axis=1)
    return tile2(rc), tile2(rs1), tile2(rs2)


def _pad_heads(w):
    lead = w.shape[:-1]
    w = w.reshape(lead + (ML_HEADS, ML_DIM))
    w = jnp.pad(w, [(0, 0)] * len(lead) + [(0, 0), (0, ML_PAD - ML_DIM)])
    return w.reshape(lead + (ML_HEADS * ML_PAD,))


def _layout_w_in(w_in):
    d = w_in.shape[0]
    offs = np.cumsum((0,) + IN_SPLITS)
    parts = [w_in[:, offs[i]:offs[i + 1]] for i in range(len(IN_SPLITS))]
    (z, xbc, dt, sq, sk, sv, mq, mk, mv, mo, mi, mf, dq, dk, dv) = parts
    zc = lambda k: jnp.zeros((d, k), w_in.dtype)
    gates = jnp.concatenate([dt, zc(4), mi, mf, zc(128 - 32)], axis=1)
    sq = sq.reshape(d, SWA_Q_HEADS, HEAD_DIM)[:, SWA_HEAD_ORDER, :].reshape(d, -1)
    w = jnp.concatenate([z, xbc, gates, sq, sk, sv,
                         _pad_heads(mq), _pad_heads(mk), _pad_heads(mv), _pad_heads(mo),
                         dq, dk, dv], axis=1)
    assert w.shape[1] == COL_END
    return w.astype(BF16)


def _lane_row(parts, width=128):
    row = jnp.zeros((width,), F32)
    for off, v in parts:
        row = row.at[off:off + v.shape[0]].set(v.astype(F32))
    return row.reshape(1, width)


def _layer(x, mod_l, p, l, tabs, seq_len):
    d = D_MODEL
    nseq = x.shape[0] // seq_len
    sh1, sc1, g1, sh2, sc2, g2 = [mod_l[:, i * d:(i + 1) * d].reshape(nseq, 1, d) for i in range(6)]
    row = lambda v: v.reshape(1, -1).astype(F32)

    ua, ug, ub, uc, ud = _in_proj_call(x, sc1, sh1, row(p['pre_mix_g'][l]), _layout_w_in(p['w_in'][l]),
                                       *tabs, seq_len)
    cw = jnp.pad(p['ssd_conv_w'][l], ((0, 8 - SSD_CONV), (0, 0)))
    ya = _ssd_call(ua, ug, cw, row(p['ssd_conv_b'][l]),
                   _lane_row([(0, p['ssd_a_log'][l].reshape(-1))]),
                   _lane_row([(0, p['ssd_dt_bias'][l].reshape(-1))]),
                   row(jnp.repeat(p['ssd_d'][l], HEAD_DIM)), row(p['ssd_norm_g'][l]), seq_len)
    yb = _swa_call(ub, p['swa_sink'][l].astype(F32), seq_len)
    gb = _lane_row([(LANE_IG, p['mlstm_i_bias'][l].reshape(-1)),
                    (LANE_FG, p['mlstm_f_bias'][l].reshape(-1))])
    yc = _mlstm_call(uc, ug, gb, row(_pad_heads(p['mlstm_norm_g'][l])), seq_len)
    yd = _dil_call(ud, seq_len)

    w_out = p['w_out'][l]
    wa = w_out[0:384].astype(BF16)
    wb = w_out[384:768].reshape(SWA_Q_HEADS, HEAD_DIM, d)[SWA_HEAD_ORDER, :, :].reshape(384, d).astype(BF16)
    wc = jnp.pad(w_out[768:1152].reshape(ML_HEADS, ML_DIM, d),
                 ((0, 0), (0, ML_PAD - ML_DIM), (0, 0))).reshape(ML_HEADS * ML_PAD, d).astype(BF16)
    wd = w_out[1152:1536].astype(BF16)
    wr = jnp.pad(p['w_router'][l], ((0, 0), (0, 128 - N_EXPERTS))).astype(F32)
    wrh = wr.astype(BF16)
    wrl = (wr - wrh.astype(F32)).astype(BF16)
    br = jnp.concatenate([p['b_router'][l].astype(F32), jnp.full((128 - N_EXPERTS,), NEG, F32)]).reshape(1, 128)
    x1, h2, rt = _out_proj_call(ya, yb, yc, yd, x, g1, sc2, sh2, row(p['post_mix_g'][l]),
                                row(p['pre_ffn_g'][l]), wa, wb, wc, wd, wrh, wrl, br, seq_len)

    tile = min(MOE_TILE, x.shape[0])
    nb, be, bs, bv, idx, gs = _route(rt, tile)
    f = _moe_call(nb, be, bs, bv, idx, gs, h2, p['w_gate_up'][l].astype(BF16),
                  p['b_gate_up'][l].reshape(N_EXPERTS, 1, -1), p['w_down'][l].astype(BF16),
                  p['b_down'][l].reshape(N_EXPERTS, 1, -1), tile)
    return _ffn_res_call(x1, f, g2, row(p['post_ffn_g'][l]), seq_len)


def _trunk(x, c, p):
    nseq, seq_len, d = x.shape
    depth = p['w_in'].shape[0]
    mod = _mod_call(c, p['w_mod'], p['b_mod'])
    tabs = _rope_tables(seq_len)
    x = x.reshape(nseq * seq_len, d)
    for l in range(depth):
        x = _layer(x, mod[l], p, l, tabs, seq_len)
    return x.reshape(nseq, seq_len, d)


def kernel(x_prompt, x_sample, c_prompt, c_sample, w_mod, b_mod, pre_mix_g, post_mix_g, pre_ffn_g, post_ffn_g, w_in, w_out, ssd_conv_w, ssd_conv_b, ssd_a_log, ssd_dt_bias, ssd_d, ssd_norm_g, swa_sink, mlstm_i_bias, mlstm_f_bias, mlstm_norm_g, w_router, b_router, w_gate_up, b_gate_up, w_down, b_down):
    p = dict(w_mod=w_mod, b_mod=b_mod, pre_mix_g=pre_mix_g, post_mix_g=post_mix_g, pre_ffn_g=pre_ffn_g,
             post_ffn_g=post_ffn_g, w_in=w_in, w_out=w_out, ssd_conv_w=ssd_conv_w, ssd_conv_b=ssd_conv_b,
             ssd_a_log=ssd_a_log, ssd_dt_bias=ssd_dt_bias, ssd_d=ssd_d, ssd_norm_g=ssd_norm_g,
             swa_sink=swa_sink, mlstm_i_bias=mlstm_i_bias, mlstm_f_bias=mlstm_f_bias,
             mlstm_norm_g=mlstm_norm_g, w_router=w_router, b_router=b_router, w_gate_up=w_gate_up,
             b_gate_up=b_gate_up, w_down=w_down, b_down=b_down)
    nb = x_prompt.shape[0]
    x = jnp.concatenate([x_prompt, x_sample], axis=0)
    c = jnp.concatenate([c_prompt, c_sample], axis=0)
    y = _trunk(x, c, p)
    return (y[:nb], y[nb:])
```

```python
import functools
import math

import jax
import jax.numpy as jnp
import numpy as np
from jax import lax
from jax.experimental import pallas as pl
from jax.experimental.pallas import tpu as pltpu

F32 = jnp.float32
BF16 = jnp.bfloat16

D_MODEL = 1024
HEAD_DIM = 64
ROPE_THETA = 500000.0
NORM_EPS = 1e-6
CHUNK = 128

SSD_HEADS = 6
SSD_INNER = 384
SSD_STATE = 64
SSD_CONV = 5
SSD_XBC = 640
SWA_Q_HEADS = 6
SWA_SIDE = 128
SWA_HEAD_ORDER = (0, 3, 1, 4, 2, 5)
ML_HEADS = 4
ML_DIM = 96
ML_PAD = 128
DIL_CONFIGS = ((128, 1), (512, 4), (2048, 16))
DIL_HEADS = 6
DIL_SIDE = 64
DIL_BATCH = 4
N_EXPERTS = 32
TOP_K = 4
D_EXPERT = 1024
SWIGLU_LIMIT = 7.0
SWIGLU_ALPHA = 1.702

IN_SPLITS = (384, 640, 12, 384, 128, 128, 384, 384, 384, 384, 8, 8, 1152, 384, 384)

COL_A = 0
COL_G = 1024
COL_B = 1152
COL_C = 1792
COL_D = 3840
COL_END = 5760
LANE_IG = 16
LANE_FG = 24

NEG = -1e30
VMEM_LIMIT = 56 * 1024 * 1024
OUT_SUB = 256
MOE_ROWS = 128
MOE_GROUP = 4
MOE_TILE = 4096


def _cparams(sem):
    return pltpu.CompilerParams(dimension_semantics=sem, vmem_limit_bytes=VMEM_LIMIT)


def _dot(a, b):
    return jnp.dot(a, b, preferred_element_type=F32)


def _dot_nt(a, b):
    return lax.dot_general(a, b, (((1,), (1,)), ((), ())), preferred_element_type=F32)


def _dot_tn(a, b):
    return lax.dot_general(a, b, (((0,), (0,)), ((), ())), preferred_element_type=F32)


def _sigmoid(x):
    return 1.0 / (1.0 + jnp.exp(-x))


def _softplus(x):
    return jnp.maximum(x, 0.0) + jnp.log(1.0 + jnp.exp(-jnp.abs(x)))


def _prefix_rows(x):
    row = lax.broadcasted_iota(jnp.int32, x.shape, 0)
    s = 1
    while s < x.shape[0]:
        x = x + jnp.where(row >= s, pltpu.roll(x, s, 0), 0.0)
        s *= 2
    return x


def _mod_kernel(c_ref, w_ref, b_ref, o_ref):
    c = c_ref[...]
    s = c * _sigmoid(c)
    o_ref[0] = jnp.dot(s, w_ref[0], preferred_element_type=F32,
                       precision=lax.Precision.HIGHEST) + b_ref[0]


def _mod_call(c, w_mod, b_mod):
    nb = c.shape[0]
    depth, d, cols = w_mod.shape
    tn = 1536
    return pl.pallas_call(
        _mod_kernel,
        grid=(depth, cols // tn),
        in_specs=[pl.BlockSpec((nb, d), lambda l, j: (0, 0)),
                  pl.BlockSpec((1, d, tn), lambda l, j: (l, 0, j)),
                  pl.BlockSpec((1, 1, tn), lambda l, j: (l, 0, j))],
        out_specs=pl.BlockSpec((1, nb, tn), lambda l, j: (l, 0, j)),
        out_shape=jax.ShapeDtypeStruct((depth, nb, cols), F32),
        compiler_params=_cparams(("parallel", "parallel")),
        name="mod",
    )(c, w_mod, b_mod.reshape(depth, 1, cols))


def _rope(a, rc, rs1, rs2):
    return a * rc + pltpu.roll(a, 8, 1) * rs1 + pltpu.roll(a, 120, 1) * rs2


def _in_proj_kernel(x_ref, sc_ref, sh_ref, g_ref, w_ref, rc_ref, rs1_ref, rs2_ref,
                    oa_ref, og_ref, ob_ref, oc_ref, od_ref):
    x = x_ref[...]
    ms = jnp.mean(x * x, axis=-1, keepdims=True)
    h = x * lax.rsqrt(ms + NORM_EPS) * g_ref[...]
    h = h * (1.0 + sc_ref[0]) + sh_ref[0]
    hb = h.astype(BF16)
    rc, rs1, rs2 = rc_ref[...], rs1_ref[...], rs2_ref[...]

    def mm(c0, width):
        return _dot(hb, w_ref[:, c0:c0 + width])

    def plain(o_ref, col0, dst0, width, scale=None, step=512, add=None):
        for c in range(0, width, step):
            wd = min(step, width - c)
            a = mm(col0 + c, wd)
            if scale is not None:
                a = a * scale
            if add is not None:
                a = a + add
            o_ref[:, dst0 + c:dst0 + c + wd] = a.astype(o_ref.dtype)

    def roped(o_ref, col0, dst0, width, scale):
        for c in range(0, width, 128):
            a = _rope(mm(col0 + c, 128), rc, rs1, rs2)
            if scale is not None:
                a = a * scale
            o_ref[:, dst0 + c:dst0 + c + 128] = a.astype(o_ref.dtype)

    qscale = HEAD_DIM ** -0.5
    plain(oa_ref, COL_A, 0, 1024)
    plain(og_ref, COL_G, 0, 128)
    roped(ob_ref, COL_B, 0, 384, qscale)
    roped(ob_ref, COL_B + 384, 384, 128, None)
    plain(ob_ref, COL_B + 512, 512, 128)
    plain(oc_ref, COL_C, 0, 512)
    plain(oc_ref, COL_C + 512, 512, 512, scale=ML_DIM ** -0.5)
    lane_in_head = lax.broadcasted_iota(jnp.int32, (1, 512), 1) % ML_PAD
    plain(oc_ref, COL_C + 1024, 1024, 512, add=jnp.where(lane_in_head == ML_DIM, 1.0, 0.0))
    plain(oc_ref, COL_C + 1536, 1536, 512)
    roped(od_ref, COL_D, 0, 1152, qscale)
    roped(od_ref, COL_D + 1152, 1152, 384, None)
    plain(od_ref, COL_D + 1536, 1536, 384, step=384)


def _in_proj_call(x, sc, sh, g, w, rc, rs1, rs2, seq_len, tm=512):
    n, d = x.shape
    per_seq = seq_len // tm
    row = lambda i: (i, 0)
    seq = lambda i: (i // per_seq, 0, 0)
    pos = lambda i: (i % per_seq, 0)
    const = lambda i: (0, 0)
    widths = (1024, 128, 640, 2048, 1920)
    dtypes = (BF16, F32, BF16, BF16, BF16)
    return pl.pallas_call(
        _in_proj_kernel,
        grid=(n // tm,),
        in_specs=[pl.BlockSpec((tm, d), row),
                  pl.BlockSpec((1, 1, d), seq),
                  pl.BlockSpec((1, 1, d), seq),
                  pl.BlockSpec((1, d), const),
                  pl.BlockSpec((d, COL_END), const, pipeline_mode=pl.Buffered(1)),
                  pl.BlockSpec((tm, 128), pos),
                  pl.BlockSpec((tm, 128), pos),
                  pl.BlockSpec((tm, 128), pos)],
        out_specs=[pl.BlockSpec((tm, wd), row) for wd in widths],
        out_shape=[jax.ShapeDtypeStruct((n, wd), dt) for wd, dt in zip(widths, dtypes)],
        compiler_params=_cparams(("parallel",)),
        name="in_proj",
    )(x, sc, sh, g, w, rc, rs1, rs2)


def _ssd_kernel(ua_ref, ug_ref, cw_ref, cb_ref, alog_ref, dtb_ref, dsk_ref, ng_ref, o_ref,
                xp_ref, xa_ref, cum_ref, dtd_ref, tr_ref, y_ref, y2_ref, sf_ref, sb_ref,
                *, seq_len):
    nc = seq_len // CHUNK
    q = CHUNK
    xp_ref[0:8, :] = jnp.zeros((8, SSD_XBC), F32)
    xp_ref[seq_len + 8:seq_len + 16, :] = jnp.zeros((8, SSD_XBC), F32)
    xp_ref[8:seq_len + 8, :] = ua_ref[:, SSD_INNER:SSD_INNER + SSD_XBC].astype(F32)
    sf_ref[...] = jnp.zeros_like(sf_ref)
    sb_ref[...] = jnp.zeros_like(sb_ref)
    a_row = -jnp.exp(alog_ref[...])
    lane = lax.broadcasted_iota(jnp.int32, (q, 128), 1)

    def prep(c, carry):
        r0 = pl.multiple_of(c * q, q)
        win = xp_ref[pl.ds(r0, q + 16), :]
        conv = cb_ref[...] + cw_ref[0:1, :] * win[6:6 + q]
        for k in range(1, SSD_CONV):
            conv = conv + cw_ref[k:k + 1, :] * win[6 + k:6 + k + q]
        xa_ref[pl.ds(r0, q), :] = conv * _sigmoid(conv)
        dt = _softplus(ug_ref[pl.ds(r0, q), :] + dtb_ref[...])
        dta = dt * a_row
        p = _prefix_rows(dta)
        s = p[q - 1:q, :] - p + dta
        cum = jnp.where(lane < SSD_HEADS, p, s)
        cum_ref[pl.ds(r0, q), :] = cum
        dtd_ref[pl.ds(r0, q), :] = dt
        packed = jnp.where(lane < 16, cum, pltpu.roll(dt, 16, 1))
        tr_ref[c] = packed.T
        return carry

    lax.fori_loop(0, nc, prep, 0)

    ri = lax.broadcasted_iota(jnp.int32, (q, q), 0)
    ci = lax.broadcasted_iota(jnp.int32, (q, q), 1)
    lower_incl = ci <= ri
    lower = ci < ri
    upper = ci > ri

    def main(c, carry):
        r0 = pl.multiple_of(c * q, q)
        cbk = nc - 1 - c
        r1 = pl.multiple_of(cbk * q, q)
        tr = tr_ref[c]
        visits = []
        for rr in (r0, r1):
            xa = xa_ref[pl.ds(rr, q), :]
            bt = xa[:, SSD_INNER:SSD_INNER + 128].T.astype(BF16)
            visits.append(dict(
                xa=xa, cum=cum_ref[pl.ds(rr, q), :], dtd=dtd_ref[pl.ds(rr, q), :],
                bgt=[bt[g * 64:(g + 1) * 64, :] for g in range(2)],
                bg=[xa[:, SSD_INNER + g * 64:SSD_INNER + (g + 1) * 64].astype(BF16) for g in range(2)],
                cg=[xa[:, SSD_INNER + 128 + g * 64:SSD_INNER + 128 + (g + 1) * 64].astype(BF16)
                    for g in range(2)]))
        fw, bw = visits
        heads = range(SSD_HEADS)
        cbs = [_dot_nt(fw['cg'][g], fw['bg'][g]) for g in range(2)]
        ws = []
        for h in heads:
            cum = fw['cum']
            cf_col, rb_col = cum[:, h:h + 1], cum[:, 6 + h:7 + h]
            cf_row, rb_row = tr[h:h + 1, :], tr[6 + h:7 + h, :]
            dtf_row, dtb_row = tr[16 + h:17 + h, :], tr[22 + h:23 + h, :]
            arg = jnp.where(lower_incl, cf_col - cf_row, rb_col - rb_row)
            fac = jnp.where(lower, dtf_row, jnp.where(upper, dtb_row, dtf_row + dtb_row))
            ws.append((cbs[h // 3] * jnp.exp(arg) * fac).astype(BF16))
        xf = [fw['xa'][:, h * 64:(h + 1) * 64] for h in heads]
        xb = [bw['xa'][:, h * 64:(h + 1) * 64] for h in heads]
        sfs = [sf_ref[h] for h in heads]
        sbs = [sb_ref[h] for h in heads]
        y_intra = [_dot(ws[h], xf[h].astype(BF16)) for h in heads]
        y_carry_f = [_dot(fw['cg'][h // 3], sfs[h].astype(BF16)) for h in heads]
        y_carry_b = [_dot(bw['cg'][h // 3], sbs[h].astype(BF16)) for h in heads]
        y_ref[pl.ds(r0, q), :] = jnp.concatenate(
            [y_intra[h] + y_carry_f[h] * jnp.exp(fw['cum'][:, h:h + 1]) for h in heads], axis=1)
        y2_ref[pl.ds(r1, q), :] = jnp.concatenate(
            [y_carry_b[h] * jnp.exp(bw['cum'][:, 6 + h:7 + h]) for h in heads], axis=1)
        xws, decays = [], []
        for h in heads:
            tf = fw['cum'][q - 1:q, h:h + 1]
            wcol = jnp.exp(tf - fw['cum'][:, h:h + 1]) * fw['dtd'][:, h:h + 1]
            xws.append((xf[h] * wcol).astype(BF16))
            decays.append(jnp.exp(tf))
        for h in heads:
            tb = bw['cum'][0:1, 6 + h:7 + h]
            wcol = jnp.exp(tb - bw['cum'][:, 6 + h:7 + h]) * bw['dtd'][:, 6 + h:7 + h]
            xws.append((xb[h] * wcol).astype(BF16))
            decays.append(jnp.exp(tb))
        for h in heads:
            sf_ref[h] = sfs[h] * decays[h] + _dot(fw['bgt'][h // 3], xws[h])
        for h in heads:
            sb_ref[h] = sbs[h] * decays[6 + h] + _dot(bw['bgt'][h // 3], xws[6 + h])
        return carry

    lax.fori_loop(0, nc, main, 0)

    def fin(c, carry):
        r0 = pl.multiple_of(c * q, q)
        xs = xa_ref[pl.ds(r0, q), 0:SSD_INNER]
        y = y_ref[pl.ds(r0, q), :] + y2_ref[pl.ds(r0, q), :] + xs * dsk_ref[...]
        z = ua_ref[pl.ds(r0, q), 0:SSD_INNER].astype(F32)
        v = y * (z * _sigmoid(z))
        ms = jnp.mean(v * v, axis=-1, keepdims=True)
        o_ref[pl.ds(r0, q), :] = (v * lax.rsqrt(ms + NORM_EPS) * ng_ref[...]).astype(o_ref.dtype)
        return carry

    lax.fori_loop(0, nc, fin, 0)


def _ssd_call(ua, ug, cw, cb, alog, dtb, dsk, ng, seq_len):
    n = ua.shape[0]
    nseq = n // seq_len
    nc = seq_len // CHUNK
    seq = lambda b: (b, 0)
    const = lambda b: (0, 0)
    return pl.pallas_call(
        functools.partial(_ssd_kernel, seq_len=seq_len),
        grid=(nseq,),
        in_specs=[pl.BlockSpec((seq_len, 1024), seq),
                  pl.BlockSpec((seq_len, 128), seq),
                  pl.BlockSpec((8, SSD_XBC), const),
                  pl.BlockSpec((1, SSD_XBC), const),
                  pl.BlockSpec((1, 128), const),
                  pl.BlockSpec((1, 128), const),
                  pl.BlockSpec((1, SSD_INNER), const),
                  pl.BlockSpec((1, SSD_INNER), const)],
        out_specs=pl.BlockSpec((seq_len, SSD_INNER), seq),
        out_shape=jax.ShapeDtypeStruct((n, SSD_INNER), BF16),
        scratch_shapes=[pltpu.VMEM((seq_len + 16, SSD_XBC), F32),
                        pltpu.VMEM((seq_len, SSD_XBC), F32),
                        pltpu.VMEM((seq_len, 128), F32),
                        pltpu.VMEM((seq_len, 128), F32),
                        pltpu.VMEM((nc, 128, CHUNK), F32),
                        pltpu.VMEM((seq_len, SSD_INNER), F32),
                        pltpu.VMEM((seq_len, SSD_INNER), F32),
                        pltpu.VMEM((SSD_HEADS, SSD_STATE, HEAD_DIM), F32),
                        pltpu.VMEM((SSD_HEADS, SSD_STATE, HEAD_DIM), F32)],
        compiler_params=_cparams(("parallel",)),
        name="ssd",
    )(ua, ug, cw, cb, alog, dtb, dsk, ng)


def _swa_kernel(sink_ref, u_ref, o_ref, kp_ref, vp_ref, bias_ref, *, seq_len):
    q = CHUNK
    nb = seq_len // q
    zeros = jnp.zeros((q, 128), BF16)
    kp_ref[0:q, :] = zeros
    vp_ref[0:q, :] = zeros
    kp_ref[seq_len + q:seq_len + 2 * q, :] = zeros
    vp_ref[seq_len + q:seq_len + 2 * q, :] = zeros
    kp_ref[q:seq_len + q, :] = u_ref[:, 384:512]
    vp_ref[q:seq_len + q, :] = u_ref[:, 512:640]
    row = lax.broadcasted_iota(jnp.int32, (3 * q, 3 * q), 0) & (q - 1)
    col = lax.broadcasted_iota(jnp.int32, (3 * q, 3 * q), 1)
    bias_ref[...] = jnp.where(jnp.abs(col - q - row) <= SWA_SIDE, 0.0, NEG)
    rgrp = lax.broadcasted_iota(jnp.int32, (3 * q, 1), 0) // q
    col1 = lax.broadcasted_iota(jnp.int32, (1, 3 * q), 1)
    lane1 = lax.broadcasted_iota(jnp.int32, (1, 128), 1)
    lo_f = jnp.where(lane1 < HEAD_DIM, 1.0, 0.0)
    hi_f = 1.0 - lo_f
    halves = ((lo_f.astype(BF16), lo_f), (hi_f.astype(BF16), hi_f))

    def body(qb, carry):
        q0 = pl.multiple_of(qb * q, q)
        qs = jnp.concatenate([u_ref[pl.ds(q0, q), j * 128:(j + 1) * 128] for j in range(3)], axis=0)
        kw = kp_ref[pl.ds(q0, 3 * q), :]
        vw = vp_ref[pl.ds(q0, 3 * q), :]
        inside = (col1 >= q - q0) & (col1 < seq_len + q - q0)
        bias = bias_ref[...] + jnp.where(inside, 0.0, NEG)
        o = None
        for hk, (sel_b, sel_f) in enumerate(halves):
            s = _dot_nt(qs * sel_b, kw) + bias
            sk = jnp.where(rgrp == 0, sink_ref[3 * hk],
                           jnp.where(rgrp == 1, sink_ref[3 * hk + 1], sink_ref[3 * hk + 2]))
            m = jnp.maximum(jnp.max(s, axis=-1, keepdims=True), sk)
            p = jnp.exp(s - m)
            l = jnp.sum(p, axis=-1, keepdims=True) + jnp.exp(sk - m)
            oh = (_dot(p.astype(BF16), vw) / l) * sel_f
            o = oh if o is None else o + oh
        o_ref[pl.ds(q0, q), :] = jnp.concatenate(
            [o[j * q:(j + 1) * q, :] for j in range(3)], axis=1).astype(o_ref.dtype)
        return carry

    lax.fori_loop(0, nb, body, 0, unroll=2)


def _swa_call(ub, sink, seq_len):
    n = ub.shape[0]
    nseq = n // seq_len
    return pl.pallas_call(
        functools.partial(_swa_kernel, seq_len=seq_len),
        grid=(nseq,),
        in_specs=[pl.BlockSpec(memory_space=pltpu.SMEM),
                  pl.BlockSpec((seq_len, 640), lambda b: (b, 0))],
        out_specs=pl.BlockSpec((seq_len, 384), lambda b: (b, 0)),
        out_shape=jax.ShapeDtypeStruct((n, 384), BF16),
        scratch_shapes=[pltpu.VMEM((seq_len + 2 * CHUNK, 128), BF16),
                        pltpu.VMEM((seq_len + 2 * CHUNK, 128), BF16),
                        pltpu.VMEM((3 * CHUNK, 3 * CHUNK), F32)],
        compiler_params=_cparams(("parallel",)),
        name="swa",
    )(sink, ub)


def _mlstm_kernel(uc_ref, ug_ref, gb_ref, ng_ref, o_ref,
                  gcol_ref, gtr_ref, hacc_ref, ct_ref, m_ref, mask_ref, *, seq_len):
    q = CHUNK
    nc = seq_len // q
    lane = lax.broadcasted_iota(jnp.int32, (q, 128), 1)
    ct_ref[...] = jnp.zeros_like(ct_ref)
    m_ref[...] = jnp.zeros_like(m_ref)
    hacc_ref[...] = jnp.zeros_like(hacc_ref)

    def prep(c, carry):
        r0 = pl.multiple_of(c * q, q)
        raw = ug_ref[pl.ds(r0, q), :] + gb_ref[...]
        lf = jnp.minimum(raw, 0.0) - jnp.log(1.0 + jnp.exp(-jnp.abs(raw)))
        p = _prefix_rows(lf)
        s = p[q - 1:q, :] - p + lf
        g = jnp.where(lane < LANE_FG, raw, jnp.where(lane < LANE_FG + ML_HEADS, p, s))
        gcol_ref[pl.ds(r0, q), :] = g
        gtr_ref[c] = g.T
        return carry

    lax.fori_loop(0, nc, prep, 0)

    ri = lax.broadcasted_iota(jnp.int32, (q, q), 0)
    ci = lax.broadcasted_iota(jnp.int32, (q, q), 1)
    mask_ref[0] = jnp.where(ci <= ri, 0.0, NEG)
    mask_ref[1] = jnp.where(ci >= ri, 0.0, NEG)
    keep_lanes = jnp.where(lax.broadcasted_iota(jnp.int32, (1, ML_PAD), 1) < ML_DIM, 1.0, 0.0)

    def main(c, carry):
        units = []
        for dirn in range(2):
            ck = c if dirn == 0 else nc - 1 - c
            r0 = pl.multiple_of(ck * q, q)
            g = gcol_ref[pl.ds(r0, q), :]
            gt = gtr_ref[ck]
            for h in range(ML_HEADS):
                s_idx = dirn * ML_HEADS + h
                li, lb = LANE_IG + s_idx, LANE_FG + s_idx
                edge = q - 1 if dirn == 0 else 0
                units.append(dict(
                    dirn=dirn, h=h, r0=r0, s_idx=s_idx,
                    b_col=g[:, lb:lb + 1], b_row=gt[lb:lb + 1, :], i_col=g[:, li:li + 1],
                    i_row=gt[li:li + 1, :], tot=g[edge:edge + 1, lb:lb + 1],
                    m_st=m_ref[s_idx][0:1, 0:1], ct=ct_ref[s_idx],
                    qh=uc_ref[pl.ds(r0, q), h * ML_PAD:(h + 1) * ML_PAD],
                    kh=uc_ref[pl.ds(r0, q), 512 + h * ML_PAD:512 + (h + 1) * ML_PAD],
                    vh=uc_ref[pl.ds(r0, q), 1024 + h * ML_PAD:1024 + (h + 1) * ML_PAD]))
        for u in units:
            u['logd'] = u['b_col'] - u['b_row'] + u['i_row'] + mask_ref[u['dirn']]
            u['m_inter'] = u['b_col'] + u['m_st']
        for u in units:
            u['m_t'] = jnp.maximum(jnp.max(u['logd'], axis=-1, keepdims=True), u['m_inter'])
        for u in units:
            u['qk'] = _dot_nt(u['qh'], u['kh'])
            u['qc'] = _dot(u['qh'], u['ct'].astype(BF16))
        for u in units:
            u['sm'] = (u['qk'] * jnp.exp(u['logd'] - u['m_t'])).astype(BF16)
            u['inter'] = jnp.exp(u['m_inter'] - u['m_t'])
        for u in units:
            u['num'] = _dot(u['sm'], u['vh']) + u['inter'] * u['qc']
        for u in units:
            den = u['num'][:, ML_DIM:ML_DIM + 1]
            hh = u['num'] / jnp.maximum(jnp.abs(den), jnp.exp(-u['m_t'])) * keep_lanes
            hacc_ref[pl.ds(u['r0'], q), u['h'] * ML_PAD:(u['h'] + 1) * ML_PAD] += hh
        for u in units:
            tot, m_st = u['tot'], u['m_st']
            m_new = jnp.maximum(tot + m_st,
                                jnp.max(tot - u['b_row'] + u['i_row'], axis=-1, keepdims=True))
            wk_col = jnp.exp(tot - u['b_col'] + u['i_col'] - m_new)
            u['keep'] = jnp.exp(tot + m_st - m_new)
            u['vw'] = (u['vh'].astype(F32) * wk_col).astype(BF16)
            m_ref[u['s_idx']] = jnp.broadcast_to(m_new, (8, 128))
        for u in units:
            ct_ref[u['s_idx']] = u['keep'] * u['ct'] + _dot_tn(u['kh'], u['vw'])
        return carry

    lax.fori_loop(0, nc, main, 0)

    def fin(c, carry):
        r0 = pl.multiple_of(c * q, q)
        outs = []
        for h in range(ML_HEADS):
            hs = hacc_ref[pl.ds(r0, q), h * ML_PAD:(h + 1) * ML_PAD]
            ms = jnp.sum(hs * hs, axis=-1, keepdims=True) * (1.0 / ML_DIM)
            hn = hs * lax.rsqrt(ms + NORM_EPS) * ng_ref[:, h * ML_PAD:(h + 1) * ML_PAD]
            og = uc_ref[pl.ds(r0, q), 1536 + h * ML_PAD:1536 + (h + 1) * ML_PAD].astype(F32)
            outs.append(hn * _sigmoid(og))
        o_ref[pl.ds(r0, q), :] = jnp.concatenate(outs, axis=1).astype(o_ref.dtype)
        return carry

    lax.fori_loop(0, nc, fin, 0)


def _mlstm_call(uc, ug, gb, ng, seq_len):
    n = uc.shape[0]
    nseq = n // seq_len
    nc = seq_len // CHUNK
    seq = lambda b: (b, 0)
    const = lambda b: (0, 0)
    width = ML_HEADS * ML_PAD
    return pl.pallas_call(
        functools.partial(_mlstm_kernel, seq_len=seq_len),
        grid=(nseq,),
        in_specs=[pl.BlockSpec((seq_len, 4 * width), seq),
                  pl.BlockSpec((seq_len, 128), seq),
                  pl.BlockSpec((1, 128), const),
                  pl.BlockSpec((1, width), const)],
        out_specs=pl.BlockSpec((seq_len, width), seq),
        out_shape=jax.ShapeDtypeStruct((n, width), BF16),
        scratch_shapes=[pltpu.VMEM((seq_len, 128), F32),
                        pltpu.VMEM((nc, 128, CHUNK), F32),
                        pltpu.VMEM((seq_len, width), F32),
                        pltpu.VMEM((2 * ML_HEADS, ML_PAD, ML_PAD), F32),
                        pltpu.VMEM((2 * ML_HEADS, 8, 128), F32),
                        pltpu.VMEM((2, CHUNK, CHUNK), F32)],
        compiler_params=_cparams(("parallel",)),
        name="mlstm",
    )(uc, ug, gb, ng)


def _attend_units(units, lo_b, hi_b, lo_f, hi_f):
    n = units[0][0].shape[0]
    ss = []
    for q, kw, _, bias in units:
        s2 = _dot_nt(jnp.concatenate([q * lo_b, q * hi_b], axis=0), kw)
        ss.append(jnp.concatenate([s2[0:n] + bias, s2[n:2 * n] + bias], axis=0))
    ms = [jnp.max(s, axis=-1, keepdims=True) for s in ss]
    ps = [jnp.exp(s - m) for s, m in zip(ss, ms)]
    ls = [jnp.sum(p, axis=-1, keepdims=True) for p in ps]
    os_ = [_dot(p.astype(BF16), u[2]) / l for p, u, l in zip(ps, units, ls)]
    res = []
    for o2, m, l in zip(os_, ms, ls):
        lse2 = m + jnp.log(l)
        res.append((o2[0:n] * lo_f + o2[n:2 * n] * hi_f, lse2[0:n] * lo_f + lse2[n:2 * n] * hi_f))
    return res


def _dil_kernel(q0_ref, q1_ref, q2_ref, k_ref, v_ref, o_ref,
                qf_ref, kf_ref, vf_ref, kc0_ref, vc0_ref, kc1_ref, vc1_ref, og_ref, lg_ref,
                qc2_ref, kc2_ref, vc2_ref, og2_ref, lg2_ref, band_ref, band2_ref, *, seq_len):
    qr = CHUNK
    pad = DIL_SIDE
    d1, d2 = DIL_CONFIGS[1][1], DIL_CONFIGS[2][1]
    n1, n2 = seq_len // d1, seq_len // d2
    assert n2 == qr and n1 % qr == 0
    qf_ref[0] = q1_ref[...].astype(F32)
    qf_ref[1] = q2_ref[...].astype(F32)
    kf_ref[...] = k_ref[...].astype(F32)
    vf_ref[...] = v_ref[...].astype(F32)
    zeros = jnp.zeros((pad, 128), BF16)
    kc0_ref[0:pad, :] = zeros
    vc0_ref[0:pad, :] = zeros
    kc0_ref[pad + seq_len:2 * pad + seq_len, :] = zeros
    vc0_ref[pad + seq_len:2 * pad + seq_len, :] = zeros
    kc0_ref[pad:pad + seq_len, :] = k_ref[...]
    vc0_ref[pad:pad + seq_len, :] = v_ref[...]
    for r in range(d1):
        kc1_ref[r, 0:pad, :] = zeros
        vc1_ref[r, 0:pad, :] = zeros
        kc1_ref[r, pad + n1:2 * pad + n1, :] = zeros
        vc1_ref[r, pad + n1:2 * pad + n1, :] = zeros
        kc1_ref[r, pad:pad + n1, :] = kf_ref[pl.ds(r, n1, stride=d1), :].astype(BF16)
        vc1_ref[r, pad:pad + n1, :] = vf_ref[pl.ds(r, n1, stride=d1), :].astype(BF16)

    row = lax.broadcasted_iota(jnp.int32, (qr, 2 * qr), 0)
    col = lax.broadcasted_iota(jnp.int32, (qr, 2 * qr), 1)
    band_ref[...] = jnp.where(jnp.abs(col - pad - row) <= DIL_SIDE, 0.0, NEG)
    row2 = lax.broadcasted_iota(jnp.int32, (qr, qr), 0)
    col2 = lax.broadcasted_iota(jnp.int32, (qr, qr), 1)
    band2_ref[...] = jnp.where(jnp.abs(col2 - row2) <= DIL_SIDE, 0.0, NEG)
    lane1 = lax.broadcasted_iota(jnp.int32, (1, 128), 1)
    col1 = lax.broadcasted_iota(jnp.int32, (1, 2 * qr), 1)
    lo_f = jnp.where(lane1 < HEAD_DIM, 1.0, 0.0)
    hi_f = 1.0 - lo_f
    sel = (lo_f.astype(BF16), hi_f.astype(BF16), lo_f, hi_f)

    def window_bias(u0, n):
        inside = (col1 >= pad - u0) & (col1 < n + pad - u0)
        return band_ref[...] + jnp.where(inside, 0.0, NEG)

    def batch0(b4, carry):
        units, starts = [], []
        for i in range(DIL_BATCH):
            u0 = pl.multiple_of((b4 * DIL_BATCH + i) * qr, qr)
            starts.append(u0)
            units.append((q0_ref[pl.ds(u0, qr), :], kc0_ref[pl.ds(u0, 2 * qr), :],
                          vc0_ref[pl.ds(u0, 2 * qr), :], window_bias(u0, seq_len)))
        for u0, (o, lse) in zip(starts, _attend_units(units, *sel)):
            og_ref[0, pl.ds(u0, qr), :] = o
            lg_ref[0, pl.ds(u0, qr), :] = lse
        return carry

    lax.fori_loop(0, seq_len // qr // DIL_BATCH, batch0, 0)

    nb1 = n1 // qr
    assert nb1 == DIL_BATCH

    def batch1(r, carry):
        units, rowsl = [], []
        for i in range(DIL_BATCH):
            u0 = i * qr
            rows = pl.ds(r + u0 * d1, qr, stride=d1)
            rowsl.append(rows)
            units.append((qf_ref[0, rows, :].astype(BF16), kc1_ref[r, u0:u0 + 2 * qr, :],
                          vc1_ref[r, u0:u0 + 2 * qr, :], window_bias(u0, n1)))
        for rows, (o, lse) in zip(rowsl, _attend_units(units, *sel)):
            og_ref[1, rows, :] = o
            lg_ref[1, rows, :] = lse
        return carry

    lax.fori_loop(0, d1, batch1, 0)

    for r in range(d2):
        rows = pl.ds(r, qr, stride=d2)
        qc2_ref[r] = qf_ref[1, rows, :].astype(BF16)
        kc2_ref[r] = kf_ref[rows, :].astype(BF16)
        vc2_ref[r] = vf_ref[rows, :].astype(BF16)

    def batch2(b4, carry):
        rs = [b4 * DIL_BATCH + i for i in range(DIL_BATCH)]
        units = [(qc2_ref[r], kc2_ref[r], vc2_ref[r], band2_ref[...]) for r in rs]
        for r, (o, lse) in zip(rs, _attend_units(units, *sel)):
            og2_ref[r] = o
            lg2_ref[r] = lse
        return carry

    lax.fori_loop(0, d2 // DIL_BATCH, batch2, 0)
    for r in range(d2):
        rows = pl.ds(r, qr, stride=d2)
        og_ref[2, rows, :] = og2_ref[r]
        lg_ref[2, rows, :] = lg2_ref[r]

    def fin(c, carry):
        r0 = pl.multiple_of(c * qr, qr)
        rows = pl.ds(r0, qr)
        l0, l1, l2 = lg_ref[0, rows, :], lg_ref[1, rows, :], lg_ref[2, rows, :]
        m = jnp.maximum(jnp.maximum(l0, l1), l2)
        w0, w1, w2 = jnp.exp(l0 - m), jnp.exp(l1 - m), jnp.exp(l2 - m)
        y = (og_ref[0, rows, :] * w0 + og_ref[1, rows, :] * w1 + og_ref[2, rows, :] * w2)
        o_ref[rows, :] = (y / (w0 + w1 + w2)).astype(o_ref.dtype)
        return carry

    lax.fori_loop(0, seq_len // qr, fin, 0, unroll=2)


def _dil_call(ud, seq_len):
    n = ud.shape[0]
    nseq = n // seq_len
    blk = (seq_len, 128)
    d1, d2 = DIL_CONFIGS[1][1], DIL_CONFIGS[2][1]
    n1 = seq_len // d1

    def col(off):
        return lambda b, p: (b, off + p)

    return pl.pallas_call(
        functools.partial(_dil_kernel, seq_len=seq_len),
        grid=(nseq, 3),
        in_specs=[pl.BlockSpec(blk, col(0)), pl.BlockSpec(blk, col(3)), pl.BlockSpec(blk, col(6)),
                  pl.BlockSpec(blk, col(9)), pl.BlockSpec(blk, col(12))],
        out_specs=pl.BlockSpec(blk, col(0)),
        out_shape=jax.ShapeDtypeStruct((n, 384), BF16),
        scratch_shapes=[pltpu.VMEM((2, seq_len, 128), F32),
                        pltpu.VMEM((seq_len, 128), F32),
                        pltpu.VMEM((seq_len, 128), F32),
                        pltpu.VMEM((seq_len + 2 * DIL_SIDE, 128), BF16),
                        pltpu.VMEM((seq_len + 2 * DIL_SIDE, 128), BF16),
                        pltpu.VMEM((d1, n1 + 2 * DIL_SIDE, 128), BF16),
                        pltpu.VMEM((d1, n1 + 2 * DIL_SIDE, 128), BF16),
                        pltpu.VMEM((3, seq_len, 128), F32),
                        pltpu.VMEM((3, seq_len, 128), F32)]
                       + [pltpu.VMEM((d2, CHUNK, 128), BF16)] * 3
                       + [pltpu.VMEM((d2, CHUNK, 128), F32)] * 2
                       + [pltpu.VMEM((CHUNK, 2 * CHUNK), F32), pltpu.VMEM((CHUNK, CHUNK), F32)],
        compiler_params=_cparams(("parallel", "parallel")),
        name="dil",
    )(ud, ud, ud, ud, ud)


def _out_proj_kernel(ya_ref, yb_ref, yc_ref, yd_ref, x_ref, g1_ref, sc_ref, sh_ref,
                     pg_ref, fg_ref, wa_ref, wb_ref, wc_ref, wd_ref, wrh_ref, wrl_ref, br_ref,
                     x1_ref, h2_ref, rt_ref):
    for r0 in range(0, x_ref.shape[0], OUT_SUB):
        rs = slice(r0, r0 + OUT_SUB)
        y = (_dot(ya_ref[rs, :], wa_ref[...]) + _dot(yb_ref[rs, :], wb_ref[...])
             + _dot(yc_ref[rs, :], wc_ref[...]) + _dot(yd_ref[rs, :], wd_ref[...]))
        ms = jnp.mean(y * y, axis=-1, keepdims=True)
        yn = y * lax.rsqrt(ms + NORM_EPS) * pg_ref[...]
        x1 = x_ref[rs, :] + g1_ref[0] * yn
        x1_ref[rs, :] = x1
        ms2 = jnp.mean(x1 * x1, axis=-1, keepdims=True)
        h2 = x1 * lax.rsqrt(ms2 + NORM_EPS) * fg_ref[...]
        h2 = h2 * (1.0 + sc_ref[0]) + sh_ref[0]
        h2_ref[rs, :] = h2
        h_hi = h2.astype(BF16)
        h_lo = (h2 - h_hi.astype(F32)).astype(BF16)
        logits = (_dot(h_hi, wrh_ref[...]) + _dot(h_lo, wrh_ref[...]) + _dot(h_hi, wrl_ref[...])
                  + br_ref[...])
        lane = lax.broadcasted_iota(jnp.int32, logits.shape, 1)
        vals, idxs = [], []
        for _ in range(TOP_K):
            m = jnp.max(logits, axis=-1, keepdims=True)
            idx = jnp.min(jnp.where(logits == m, lane, 128), axis=-1, keepdims=True)
            vals.append(m)
            idxs.append(idx)
            logits = jnp.where(lane == idx, -3e38, logits)
        es = [jnp.exp(v - vals[0]) for v in vals]
        tot = es[0] + es[1] + es[2] + es[3]
        rt = jnp.zeros(lane.shape, F32)
        for k in range(TOP_K):
            rt = jnp.where(lane == k, idxs[k].astype(F32), rt)
            rt = jnp.where(lane == TOP_K + k, es[k] / tot, rt)
        rt_ref[rs, :] = rt[:, 0:2 * TOP_K]


def _out_proj_call(ya, yb, yc, yd, x, g1, sc, sh, pg, fg, wa, wb, wc, wd, wrh, wrl, br, seq_len, tm=512):
    n, d = x.shape
    per_seq = seq_len // tm
    row = lambda i: (i, 0)
    seq = lambda i: (i // per_seq, 0, 0)
    const = lambda i: (0, 0)

    def full(a):
        return pl.BlockSpec(a.shape, const)

    return pl.pallas_call(
        _out_proj_kernel,
        grid=(n // tm,),
        in_specs=[pl.BlockSpec((tm, ya.shape[1]), row), pl.BlockSpec((tm, yb.shape[1]), row),
                  pl.BlockSpec((tm, yc.shape[1]), row), pl.BlockSpec((tm, yd.shape[1]), row),
                  pl.BlockSpec((tm, d), row),
                  pl.BlockSpec((1, 1, d), seq), pl.BlockSpec((1, 1, d), seq),
                  pl.BlockSpec((1, 1, d), seq),
                  full(pg), full(fg), full(wa), full(wb), full(wc), full(wd), full(wrh), full(wrl), full(br)],
        out_specs=[pl.BlockSpec((tm, d), row), pl.BlockSpec((tm, d), row),
                   pl.BlockSpec((tm, 2 * TOP_K), row)],
        out_shape=[jax.ShapeDtypeStruct((n, d), F32), jax.ShapeDtypeStruct((n, d), F32),
                   jax.ShapeDtypeStruct((n, 2 * TOP_K), F32)],
        compiler_params=_cparams(("parallel",)),
        name="out_proj",
    )(ya, yb, yc, yd, x, g1, sc, sh, pg, fg, wa, wb, wc, wd, wrh, wrl, br)


def _moe_kernel(nb_ref, be_ref, bs_ref, bv_ref, idx_ref, gate_ref, h_ref, wgu_ref, bgu_ref, wdn_ref,
                bdn_ref, o_ref, xs0_ref, xs1_ref, ys0_ref, ys1_ref):
    t = pl.program_id(0)
    j = pl.program_id(1)
    nb = nb_ref[t]
    rows = MOE_ROWS
    nslot = pl.num_programs(1) + 1

    def gather(dst_ref, blk):
        base = bs_ref[t * nslot + blk]
        for r in range(rows):
            i = idx_ref[0, 0, base + r]
            dst_ref[r:r + 1, :] = h_ref[pl.ds(i, 1), :]

    def scatter(src_ref, blk):
        base = bs_ref[t * nslot + blk]
        for r0 in range(0, rows, MOE_GROUP):
            ks = range(MOE_GROUP)
            ii = [idx_ref[0, 0, base + r0 + k] for k in ks]
            gg = [gate_ref[0, 0, base + r0 + k] for k in ks]
            cur = [o_ref[pl.ds(ii[k], 1), :] for k in ks]
            for k in reversed(ks):
                o_ref[pl.ds(ii[k], 1), :] = cur[k] + gg[k] * src_ref[r0 + k:r0 + k + 1, :]

    def expert(xs_ref, ys_ref):
        gu = _dot(xs_ref[...].astype(BF16), wgu_ref[0]) + bgu_ref[0]
        gate = jnp.minimum(gu[:, :D_EXPERT], SWIGLU_LIMIT)
        up = jnp.clip(gu[:, D_EXPERT:], -SWIGLU_LIMIT, SWIGLU_LIMIT)
        act = (up + 1.0) * gate * _sigmoid(SWIGLU_ALPHA * gate)
        own = lax.broadcasted_iota(jnp.int32, (rows, 1), 0) < bv_ref[t * nslot + j]
        ys_ref[...] = jnp.where(own, _dot(act.astype(BF16), wdn_ref[0]) + bdn_ref[0], 0.0)

    @pl.when(j == 0)
    def _():
        o_ref[...] = jnp.zeros_like(o_ref)
        ys1_ref[...] = jnp.zeros_like(ys1_ref)
        gather(xs0_ref, 0)

    prev = jnp.maximum(j - 1, 0)
    bufs = ((xs0_ref, xs1_ref, ys0_ref, ys1_ref), (xs1_ref, xs0_ref, ys1_ref, ys0_ref))
    for par, (xs_cur, xs_nxt, ys_cur, ys_prv) in enumerate(bufs):
        @pl.when((j < nb) & (j % 2 == par))
        def _(xs_cur=xs_cur, xs_nxt=xs_nxt, ys_cur=ys_cur, ys_prv=ys_prv):
            gather(xs_nxt, j + 1)
            expert(xs_cur, ys_cur)
            scatter(ys_prv, prev)

        @pl.when((j == nb) & (j % 2 == par))
        def _(ys_prv=ys_prv):
            scatter(ys_prv, prev)


def _moe_call(nb, be, bs, bv, idx, gates, h2, wgu, bgu, wdn, bdn, tile):
    n, d = h2.shape
    ntile = n // tile
    plen = idx.shape[-1]
    nslot = be.shape[0] // ntile

    def wmap(t, j, nb_ref, be_ref, bs_ref, bv_ref):
        return (be_ref[t * nslot + j], 0, 0)

    tmap3 = lambda t, j, *_: (t, 0, 0)
    tmap2 = lambda t, j, *_: (t, 0)
    grid_spec = pltpu.PrefetchScalarGridSpec(
        num_scalar_prefetch=4,
        grid=(ntile, nslot - 1),
        in_specs=[pl.BlockSpec((1, 1, plen), tmap3, memory_space=pltpu.SMEM),
                  pl.BlockSpec((1, 1, plen), tmap3, memory_space=pltpu.SMEM),
                  pl.BlockSpec((tile, d), tmap2, pipeline_mode=pl.Buffered(1)),
                  pl.BlockSpec((1, d, 2 * D_EXPERT), wmap),
                  pl.BlockSpec((1, 1, 2 * D_EXPERT), wmap),
                  pl.BlockSpec((1, D_EXPERT, d), wmap),
                  pl.BlockSpec((1, 1, d), wmap)],
        out_specs=pl.BlockSpec((tile, d), tmap2, pipeline_mode=pl.Buffered(1)),
        scratch_shapes=[pltpu.VMEM((MOE_ROWS, d), F32)] * 4,
    )
    return pl.pallas_call(
        _moe_kernel,
        grid_spec=grid_spec,
        out_shape=jax.ShapeDtypeStruct((n, d), F32),
        compiler_params=_cparams(("parallel", "arbitrary")),
        name="moe",
    )(nb, be, bs, bv, idx, gates, h2, wgu, bgu, wdn, bdn)


def _route(rt, tile):
    n = rt.shape[0]
    ntile = n // tile
    npair = tile * TOP_K
    nslot = npair // MOE_ROWS + N_EXPERTS + 2
    e = rt[:, :TOP_K].astype(jnp.int32).reshape(ntile, npair)
    g = rt[:, TOP_K:].reshape(ntile, npair)
    tok = jnp.broadcast_to((jnp.arange(npair, dtype=jnp.int32) // TOP_K)[None, :], (ntile, npair))
    _, tok_s, g_s = lax.sort((e, tok, g), dimension=1, is_stable=True, num_keys=1)
    experts = jnp.arange(N_EXPERTS, dtype=jnp.int32)
    counts = jnp.sum(e[:, :, None] == experts, axis=1, dtype=jnp.int32)
    nblocks = (counts + MOE_ROWS - 1) // MOE_ROWS
    bends = jnp.cumsum(nblocks, axis=-1)
    starts = jnp.cumsum(counts, axis=-1) - counts
    nb = bends[:, -1]
    slot = jnp.minimum(jnp.arange(nslot, dtype=jnp.int32)[None, :], nb[:, None] - 1)
    be = jnp.sum(slot[:, :, None] >= bends[:, None, :], axis=-1, dtype=jnp.int32)
    onehot = (be[:, :, None] == experts).astype(jnp.int32)
    pick = lambda v: jnp.sum(onehot * v[:, None, :], axis=-1)
    within = slot - pick(bends - nblocks)
    bs = pick(starts) + within * MOE_ROWS
    bv = jnp.clip(pick(counts) - within * MOE_ROWS, 0, MOE_ROWS)
    padw = ((0, 0), (0, MOE_ROWS))
    idx = jnp.pad(tok_s, padw).reshape(ntile, 1, -1)
    gl = jnp.pad(g_s, padw).reshape(ntile, 1, -1)
    flat = lambda v: v.reshape(-1).astype(jnp.int32)
    return flat(nb), flat(be), flat(bs), flat(bv), idx, gl


def _ffn_res_kernel(x_ref, f_ref, g2_ref, pg_ref, o_ref):
    f = f_ref[...]
    ms = jnp.mean(f * f, axis=-1, keepdims=True)
    o_ref[...] = x_ref[...] + g2_ref[0] * (f * lax.rsqrt(ms + NORM_EPS) * pg_ref[...])


def _ffn_res_call(x1, f, g2, pg, seq_len, tm=1024):
    n, d = x1.shape
    per_seq = seq_len // tm
    row = lambda i: (i, 0)
    return pl.pallas_call(
        _ffn_res_kernel,
        grid=(n // tm,),
        in_specs=[pl.BlockSpec((tm, d), row), pl.BlockSpec((tm, d), row),
                  pl.BlockSpec((1, 1, d), lambda i: (i // per_seq, 0, 0)),
                  pl.BlockSpec((1, d), lambda i: (0, 0))],
        out_specs=pl.BlockSpec((tm, d), row),
        out_shape=jax.ShapeDtypeStruct((n, d), F32),
        compiler_params=_cparams(("parallel",)),
        name="ffn_res",
    )(x1, f, g2, pg)


def _rope_tables(seq_len):
    half = 8
    inv = 1.0 / (ROPE_THETA ** (jnp.arange(half, dtype=F32) * (2.0 / 16)))
    ang = jnp.arange(seq_len, dtype=F32)[:, None] * inv[None, :]
    cos, sin = jnp.cos(ang), jnp.sin(ang)
    ones = jnp.ones((seq_len, 48), F32)
    zeros8 = jnp.zeros((seq_len, 8), F32)
    zeros48 = jnp.zeros((seq_len, 48), F32)
    rc = jnp.concatenate([cos, cos, ones], axis=1)
    rs1 = jnp.concatenate([zeros8, sin, zeros48], axis=1)
    rs2 = jnp.concatenate([-sin, zeros8, zeros48], axis=1)
    tile2 = lambda a: jnp.concatenate([a, a], axis=1)
    return tile2(rc), tile2(rs1), tile2(rs2)


def _pad_heads(w):
    lead = w.shape[:-1]
    w = w.reshape(lead + (ML_HEADS, ML_DIM))
    w = jnp.pad(w, [(0, 0)] * len(lead) + [(0, 0), (0, ML_PAD - ML_DIM)])
    return w.reshape(lead + (ML_HEADS * ML_PAD,))


def _layout_w_in(w_in):
    d = w_in.shape[0]
    offs = np.cumsum((0,) + IN_SPLITS)
    parts = [w_in[:, offs[i]:offs[i + 1]] for i in range(len(IN_SPLITS))]
    (z, xbc, dt, sq, sk, sv, mq, mk, mv, mo, mi, mf, dq, dk, dv) = parts
    zc = lambda k: jnp.zeros((d, k), w_in.dtype)
    gates = jnp.concatenate([dt, zc(4), mi, mf, zc(128 - 32)], axis=1)
    sq = sq.reshape(d, SWA_Q_HEADS, HEAD_DIM)[:, SWA_HEAD_ORDER, :].reshape(d, -1)
    w = jnp.concatenate([z, xbc, gates, sq, sk, sv,
                         _pad_heads(mq), _pad_heads(mk), _pad_heads(mv), _pad_heads(mo),
                         dq, dk, dv], axis=1)
    assert w.shape[1] == COL_END
    return w.astype(BF16)


def _lane_row(parts, width=128):
    row = jnp.zeros((width,), F32)
    for off, v in parts:
        row = row.at[off:off + v.shape[0]].set(v.astype(F32))
    return row.reshape(1, width)


def _layer(x, mod_l, p, l, tabs, seq_len):
    d = D_MODEL
    nseq = x.shape[0] // seq_len
    sh1, sc1, g1, sh2, sc2, g2 = [mod_l[:, i * d:(i + 1) * d].reshape(nseq, 1, d) for i in range(6)]
    row = lambda v: v.reshape(1, -1).astype(F32)

    ua, ug, ub, uc, ud = _in_proj_call(x, sc1, sh1, row(p['pre_mix_g'][l]), _layout_w_in(p['w_in'][l]),
                                       *tabs, seq_len)
    cw = jnp.pad(p['ssd_conv_w'][l], ((0, 8 - SSD_CONV), (0, 0)))
    ya = _ssd_call(ua, ug, cw, row(p['ssd_conv_b'][l]),
                   _lane_row([(0, p['ssd_a_log'][l].reshape(-1))]),
                   _lane_row([(0, p['ssd_dt_bias'][l].reshape(-1))]),
                   row(jnp.repeat(p['ssd_d'][l], HEAD_DIM)), row(p['ssd_norm_g'][l]), seq_len)
    yb = _swa_call(ub, p['swa_sink'][l].astype(F32), seq_len)
    gb = _lane_row([(LANE_IG, p['mlstm_i_bias'][l].reshape(-1)),
                    (LANE_FG, p['mlstm_f_bias'][l].reshape(-1))])
    yc = _mlstm_call(uc, ug, gb, row(_pad_heads(p['mlstm_norm_g'][l])), seq_len)
    yd = _dil_call(ud, seq_len)

    w_out = p['w_out'][l]
    wa = w_out[0:384].astype(BF16)
    wb = w_out[384:768].reshape(SWA_Q_HEADS, HEAD_DIM, d)[SWA_HEAD_ORDER, :, :].reshape(384, d).astype(BF16)
    wc = jnp.pad(w_out[768:1152].reshape(ML_HEADS, ML_DIM, d),
                 ((0, 0), (0, ML_PAD - ML_DIM), (0, 0))).reshape(ML_HEADS * ML_PAD, d).astype(BF16)
    wd = w_out[1152:1536].astype(BF16)
    wr = jnp.pad(p['w_router'][l], ((0, 0), (0, 128 - N_EXPERTS))).astype(F32)
    wrh = wr.astype(BF16)
    wrl = (wr - wrh.astype(F32)).astype(BF16)
    br = jnp.concatenate([p['b_router'][l].astype(F32), jnp.full((128 - N_EXPERTS,), NEG, F32)]).reshape(1, 128)
    x1, h2, rt = _out_proj_call(ya, yb, yc, yd, x, g1, sc2, sh2, row(p['post_mix_g'][l]),
                                row(p['pre_ffn_g'][l]), wa, wb, wc, wd, wrh, wrl, br, seq_len)

    tile = min(MOE_TILE, x.shape[0])
    nb, be, bs, bv, idx, gs = _route(rt, tile)
    f = _moe_call(nb, be, bs, bv, idx, gs, h2, p['w_gate_up'][l].astype(BF16),
                  p['b_gate_up'][l].reshape(N_EXPERTS, 1, -1), p['w_down'][l].astype(BF16),
                  p['b_down'][l].reshape(N_EXPERTS, 1, -1), tile)
    return _ffn_res_call(x1, f, g2, row(p['post_ffn_g'][l]), seq_len)


def _trunk(x, c, p):
    nseq, seq_len, d = x.shape
    depth = p['w_in'].shape[0]
    mod = _mod_call(c, p['w_mod'], p['b_mod'])
    tabs = _rope_tables(seq_len)
    x = x.reshape(nseq * seq_len, d)
    for l in range(depth):
        x = _layer(x, mod[l], p, l, tabs, seq_len)
    return x.reshape(nseq, seq_len, d)


def kernel(x_prompt, x_sample, c_prompt, c_sample, w_mod, b_mod, pre_mix_g, post_mix_g, pre_ffn_g, post_ffn_g, w_in, w_out, ssd_conv_w, ssd_conv_b, ssd_a_log, ssd_dt_bias, ssd_d, ssd_norm_g, swa_sink, mlstm_i_bias, mlstm_f_bias, mlstm_norm_g, w_router, b_router, w_gate_up, b_gate_up, w_down, b_down):
    p = dict(w_mod=w_mod, b_mod=b_mod, pre_mix_g=pre_mix_g, post_mix_g=post_mix_g, pre_ffn_g=pre_ffn_g,
             post_ffn_g=post_ffn_g, w_in=w_in, w_out=w_out, ssd_conv_w=ssd_conv_w, ssd_conv_b=ssd_conv_b,
             ssd_a_log=ssd_a_log, ssd_dt_bias=ssd_dt_bias, ssd_d=ssd_d, ssd_norm_g=ssd_norm_g,
             swa_sink=swa_sink, mlstm_i_bias=mlstm_i_bias, mlstm_f_bias=mlstm_f_bias,
             mlstm_norm_g=mlstm_norm_g, w_router=w_router, b_router=b_router, w_gate_up=w_gate_up,
             b_gate_up=b_gate_up, w_down=w_down, b_down=b_down)
    nb = x_prompt.shape[0]
    x = jnp.concatenate([x_prompt, x_sample], axis=0)
    c = jnp.concatenate([c_prompt, c_sample], axis=0)
    y = _trunk(x, c, p)
    return (y[:nb], y[nb:])
```

```python
import functools
import math

import jax
import jax.numpy as jnp
import numpy as np
from jax import lax
from jax.experimental import pallas as pl
from jax.experimental.pallas import tpu as pltpu

F32 = jnp.float32
BF16 = jnp.bfloat16

D_MODEL = 1024
HEAD_DIM = 64
ROPE_THETA = 500000.0
NORM_EPS = 1e-6
CHUNK = 128

SSD_HEADS = 6
SSD_INNER = 384
SSD_STATE = 64
SSD_CONV = 5
SSD_XBC = 640
SWA_Q_HEADS = 6
SWA_SIDE = 128
SWA_HEAD_ORDER = (0, 3, 1, 4, 2, 5)
ML_HEADS = 4
ML_DIM = 96
ML_PAD = 128
DIL_CONFIGS = ((128, 1), (512, 4), (2048, 16))
DIL_HEADS = 6
DIL_SIDE = 64
DIL_BATCH = 4
N_EXPERTS = 32
TOP_K = 4
D_EXPERT = 1024
SWIGLU_LIMIT = 7.0
SWIGLU_ALPHA = 1.702

IN_SPLITS = (384, 640, 12, 384, 128, 128, 384, 384, 384, 384, 8, 8, 1152, 384, 384)

COL_A = 0
COL_G = 1024
COL_B = 1152
COL_C = 1792
COL_D = 3840
COL_END = 5760
LANE_IG = 16
LANE_FG = 24

NEG = -1e30
VMEM_LIMIT = 56 * 1024 * 1024
OUT_SUB = 256
MOE_ROWS = 128
MOE_GROUP = 4
MOE_TILE = 4096


def _cparams(sem):
    return pltpu.CompilerParams(dimension_semantics=sem, vmem_limit_bytes=VMEM_LIMIT)


def _dot(a, b):
    return jnp.dot(a, b, preferred_element_type=F32)


def _dot_nt(a, b):
    return lax.dot_general(a, b, (((1,), (1,)), ((), ())), preferred_element_type=F32)


def _dot_tn(a, b):
    return lax.dot_general(a, b, (((0,), (0,)), ((), ())), preferred_element_type=F32)


def _sigmoid(x):
    return 1.0 / (1.0 + jnp.exp(-x))


def _softplus(x):
    return jnp.maximum(x, 0.0) + jnp.log(1.0 + jnp.exp(-jnp.abs(x)))


def _prefix_rows(x):
    row = lax.broadcasted_iota(jnp.int32, x.shape, 0)
    s = 1
    while s < x.shape[0]:
        x = x + jnp.where(row >= s, pltpu.roll(x, s, 0), 0.0)
        s *= 2
    return x


def _mod_kernel(c_ref, w_ref, b_ref, o_ref):
    c = c_ref[...]
    s = c * _sigmoid(c)
    o_ref[0] = jnp.dot(s, w_ref[0], preferred_element_type=F32,
                       precision=lax.Precision.HIGHEST) + b_ref[0]


def _mod_call(c, w_mod, b_mod):
    nb = c.shape[0]
    depth, d, cols = w_mod.shape
    tn = 1536
    return pl.pallas_call(
        _mod_kernel,
        grid=(depth, cols // tn),
        in_specs=[pl.BlockSpec((nb, d), lambda l, j: (0, 0)),
                  pl.BlockSpec((1, d, tn), lambda l, j: (l, 0, j)),
                  pl.BlockSpec((1, 1, tn), lambda l, j: (l, 0, j))],
        out_specs=pl.BlockSpec((1, nb, tn), lambda l, j: (l, 0, j)),
        out_shape=jax.ShapeDtypeStruct((depth, nb, cols), F32),
        compiler_params=_cparams(("parallel", "parallel")),
        name="mod",
    )(c, w_mod, b_mod.reshape(depth, 1, cols))


def _rope(a, rc, rs1, rs2):
    return a * rc + pltpu.roll(a, 8, 1) * rs1 + pltpu.roll(a, 120, 1) * rs2


def _in_proj_kernel(x_ref, sc_ref, sh_ref, g_ref, w_ref, rc_ref, rs1_ref, rs2_ref,
                    oa_ref, og_ref, ob_ref, oc_ref, od_ref):
    x = x_ref[...]
    ms = jnp.mean(x * x, axis=-1, keepdims=True)
    h = x * lax.rsqrt(ms + NORM_EPS) * g_ref[...]
    h = h * (1.0 + sc_ref[0]) + sh_ref[0]
    hb = h.astype(BF16)
    rc, rs1, rs2 = rc_ref[...], rs1_ref[...], rs2_ref[...]

    def mm(c0, width):
        return _dot(hb, w_ref[:, c0:c0 + width])

    def plain(o_ref, col0, dst0, width, scale=None, step=512, add=None):
        for c in range(0, width, step):
            wd = min(step, width - c)
            a = mm(col0 + c, wd)
            if scale is not None:
                a = a * scale
            if add is not None:
                a = a + add
            o_ref[:, dst0 + c:dst0 + c + wd] = a.astype(o_ref.dtype)

    def roped(o_ref, col0, dst0, width, scale):
        for c in range(0, width, 128):
            a = _rope(mm(col0 + c, 128), rc, rs1, rs2)
            if scale is not None:
                a = a * scale
            o_ref[:, dst0 + c:dst0 + c + 128] = a.astype(o_ref.dtype)

    qscale = HEAD_DIM ** -0.5
    plain(oa_ref, COL_A, 0, 1024)
    plain(og_ref, COL_G, 0, 128)
    roped(ob_ref, COL_B, 0, 384, qscale)
    roped(ob_ref, COL_B + 384, 384, 128, None)
    plain(ob_ref, COL_B + 512, 512, 128)
    plain(oc_ref, COL_C, 0, 512)
    plain(oc_ref, COL_C + 512, 512, 512, scale=ML_DIM ** -0.5)
    lane_in_head = lax.broadcasted_iota(jnp.int32, (1, 512), 1) % ML_PAD
    plain(oc_ref, COL_C + 1024, 1024, 512, add=jnp.where(lane_in_head == ML_DIM, 1.0, 0.0))
    plain(oc_ref, COL_C + 1536, 1536, 512)
    roped(od_ref, COL_D, 0, 1152, qscale)
    roped(od_ref, COL_D + 1152, 1152, 384, None)
    plain(od_ref, COL_D + 1536, 1536, 384, step=384)


def _in_proj_call(x, sc, sh, g, w, rc, rs1, rs2, seq_len, tm=512):
    n, d = x.shape
    per_seq = seq_len // tm
    row = lambda i: (i, 0)
    seq = lambda i: (i // per_seq, 0, 0)
    pos = lambda i: (i % per_seq, 0)
    const = lambda i: (0, 0)
    widths = (1024, 128, 640, 2048, 1920)
    dtypes = (BF16, F32, BF16, BF16, BF16)
    return pl.pallas_call(
        _in_proj_kernel,
        grid=(n // tm,),
        in_specs=[pl.BlockSpec((tm, d), row),
                  pl.BlockSpec((1, 1, d), seq),
                  pl.BlockSpec((1, 1, d), seq),
                  pl.BlockSpec((1, d), const),
                  pl.BlockSpec((d, COL_END), const, pipeline_mode=pl.Buffered(1)),
                  pl.BlockSpec((tm, 128), pos),
                  pl.BlockSpec((tm, 128), pos),
                  pl.BlockSpec((tm, 128), pos)],
        out_specs=[pl.BlockSpec((tm, wd), row) for wd in widths],
        out_shape=[jax.ShapeDtypeStruct((n, wd), dt) for wd, dt in zip(widths, dtypes)],
        compiler_params=_cparams(("parallel",)),
        name="in_proj",
    )(x, sc, sh, g, w, rc, rs1, rs2)


def _ssd_kernel(ua_ref, ug_ref, cw_ref, cb_ref, alog_ref, dtb_ref, dsk_ref, ng_ref, o_ref,
                xp_ref, xa_ref, cum_ref, dtd_ref, tr_ref, y_ref, y2_ref, sf_ref, sb_ref,
                *, seq_len):
    nc = seq_len // CHUNK
    q = CHUNK
    xp_ref[0:8, :] = jnp.zeros((8, SSD_XBC), F32)
    xp_ref[seq_len + 8:seq_len + 16, :] = jnp.zeros((8, SSD_XBC), F32)
    xp_ref[8:seq_len + 8, :] = ua_ref[:, SSD_INNER:SSD_INNER + SSD_XBC].astype(F32)
    sf_ref[...] = jnp.zeros_like(sf_ref)
    sb_ref[...] = jnp.zeros_like(sb_ref)
    a_row = -jnp.exp(alog_ref[...])
    lane = lax.broadcasted_iota(jnp.int32, (q, 128), 1)

    def prep(c, carry):
        r0 = pl.multiple_of(c * q, q)
        win = xp_ref[pl.ds(r0, q + 16), :]
        conv = cb_ref[...] + cw_ref[0:1, :] * win[6:6 + q]
        for k in range(1, SSD_CONV):
            conv = conv + cw_ref[k:k + 1, :] * win[6 + k:6 + k + q]
        xa_ref[pl.ds(r0, q), :] = conv * _sigmoid(conv)
        dt = _softplus(ug_ref[pl.ds(r0, q), :] + dtb_ref[...])
        dta = dt * a_row
        p = _prefix_rows(dta)
        s = p[q - 1:q, :] - p + dta
        cum = jnp.where(lane < SSD_HEADS, p, s)
        cum_ref[pl.ds(r0, q), :] = cum
        dtd_ref[pl.ds(r0, q), :] = dt
        packed = jnp.where(lane < 16, cum, pltpu.roll(dt, 16, 1))
        tr_ref[c] = packed.T
        return carry

    lax.fori_loop(0, nc, prep, 0)

    ri = lax.broadcasted_iota(jnp.int32, (q, q), 0)
    ci = lax.broadcasted_iota(jnp.int32, (q, q), 1)
    lower_incl = ci <= ri
    lower = ci < ri
    upper = ci > ri

    def main(c, carry):
        r0 = pl.multiple_of(c * q, q)
        cbk = nc - 1 - c
        r1 = pl.multiple_of(cbk * q, q)
        tr = tr_ref[c]
        visits = []
        for rr in (r0, r1):
            xa = xa_ref[pl.ds(rr, q), :]
            bt = xa[:, SSD_INNER:SSD_INNER + 128].T.astype(BF16)
            visits.append(dict(
                xa=xa, cum=cum_ref[pl.ds(rr, q), :], dtd=dtd_ref[pl.ds(rr, q), :],
                bgt=[bt[g * 64:(g + 1) * 64, :] for g in range(2)],
                bg=[xa[:, SSD_INNER + g * 64:SSD_INNER + (g + 1) * 64].astype(BF16) for g in range(2)],
                cg=[xa[:, SSD_INNER + 128 + g * 64:SSD_INNER + 128 + (g + 1) * 64].astype(BF16)
                    for g in range(2)]))
        fw, bw = visits
        heads = range(SSD_HEADS)
        cbs = [_dot_nt(fw['cg'][g], fw['bg'][g]) for g in range(2)]
        ws = []
        for h in heads:
            cum = fw['cum']
            cf_col, rb_col = cum[:, h:h + 1], cum[:, 6 + h:7 + h]
            cf_row, rb_row = tr[h:h + 1, :], tr[6 + h:7 + h, :]
            dtf_row, dtb_row = tr[16 + h:17 + h, :], tr[22 + h:23 + h, :]
            arg = jnp.where(lower_incl, cf_col - cf_row, rb_col - rb_row)
            fac = jnp.where(lower, dtf_row, jnp.where(upper, dtb_row, dtf_row + dtb_row))
            ws.append((cbs[h // 3] * jnp.exp(arg) * fac).astype(BF16))
        xf = [fw['xa'][:, h * 64:(h + 1) * 64] for h in heads]
        xb = [bw['xa'][:, h * 64:(h + 1) * 64] for h in heads]
        sfs = [sf_ref[h] for h in heads]
        sbs = [sb_ref[h] for h in heads]
        y_intra = [_dot(ws[h], xf[h].astype(BF16)) for h in heads]
        y_carry_f = [_dot(fw['cg'][h // 3], sfs[h].astype(BF16)) for h in heads]
        y_carry_b = [_dot(bw['cg'][h // 3], sbs[h].astype(BF16)) for h in heads]
        y_ref[pl.ds(r0, q), :] = jnp.concatenate(
            [y_intra[h] + y_carry_f[h] * jnp.exp(fw['cum'][:, h:h + 1]) for h in heads], axis=1)
        y2_ref[pl.ds(r1, q), :] = jnp.concatenate(
            [y_carry_b[h] * jnp.exp(bw['cum'][:, 6 + h:7 + h]) for h in heads], axis=1)
        xws, decays = [], []
        for h in heads:
            tf = fw['cum'][q - 1:q, h:h + 1]
            wcol = jnp.exp(tf - fw['cum'][:, h:h + 1]) * fw['dtd'][:, h:h + 1]
            xws.append((xf[h] * wcol).astype(BF16))
            decays.append(jnp.exp(tf))
        for h in heads:
            tb = bw['cum'][0:1, 6 + h:7 + h]
            wcol = jnp.exp(tb - bw['cum'][:, 6 + h:7 + h]) * bw['dtd'][:, 6 + h:7 + h]
            xws.append((xb[h] * wcol).astype(BF16))
            decays.append(jnp.exp(tb))
        for h in heads:
            sf_ref[h] = sfs[h] * decays[h] + _dot(fw['bgt'][h // 3], xws[h])
        for h in heads:
            sb_ref[h] = sbs[h] * decays[6 + h] + _dot(bw['bgt'][h // 3], xws[6 + h])
        return carry

    lax.fori_loop(0, nc, main, 0)

    def fin(c, carry):
        r0 = pl.multiple_of(c * q, q)
        xs = xa_ref[pl.ds(r0, q), 0:SSD_INNER]
        y = y_ref[pl.ds(r0, q), :] + y2_ref[pl.ds(r0, q), :] + xs * dsk_ref[...]
        z = ua_ref[pl.ds(r0, q), 0:SSD_INNER].astype(F32)
        v = y * (z * _sigmoid(z))
        ms = jnp.mean(v * v, axis=-1, keepdims=True)
        o_ref[pl.ds(r0, q), :] = (v * lax.rsqrt(ms + NORM_EPS) * ng_ref[...]).astype(o_ref.dtype)
        return carry

    lax.fori_loop(0, nc, fin, 0)


def _ssd_call(ua, ug, cw, cb, alog, dtb, dsk, ng, seq_len):
    n = ua.shape[0]
    nseq = n // seq_len
    nc = seq_len // CHUNK
    seq = lambda b: (b, 0)
    const = lambda b: (0, 0)
    return pl.pallas_call(
        functools.partial(_ssd_kernel, seq_len=seq_len),
        grid=(nseq,),
        in_specs=[pl.BlockSpec((seq_len, 1024), seq),
                  pl.BlockSpec((seq_len, 128), seq),
                  pl.BlockSpec((8, SSD_XBC), const),
                  pl.BlockSpec((1, SSD_XBC), const),
                  pl.BlockSpec((1, 128), const),
                  pl.BlockSpec((1, 128), const),
                  pl.BlockSpec((1, SSD_INNER), const),
                  pl.BlockSpec((1, SSD_INNER), const)],
        out_specs=pl.BlockSpec((seq_len, SSD_INNER), seq),
        out_shape=jax.ShapeDtypeStruct((n, SSD_INNER), BF16),
        scratch_shapes=[pltpu.VMEM((seq_len + 16, SSD_XBC), F32),
                        pltpu.VMEM((seq_len, SSD_XBC), F32),
                        pltpu.VMEM((seq_len, 128), F32),
                        pltpu.VMEM((seq_len, 128), F32),
                        pltpu.VMEM((nc, 128, CHUNK), F32),
                        pltpu.VMEM((seq_len, SSD_INNER), F32),
                        pltpu.VMEM((seq_len, SSD_INNER), F32),
                        pltpu.VMEM((SSD_HEADS, SSD_STATE, HEAD_DIM), F32),
                        pltpu.VMEM((SSD_HEADS, SSD_STATE, HEAD_DIM), F32)],
        compiler_params=_cparams(("parallel",)),
        name="ssd",
    )(ua, ug, cw, cb, alog, dtb, dsk, ng)


def _swa_kernel(sink_ref, u_ref, o_ref, kp_ref, vp_ref, bias_ref, *, seq_len):
    q = CHUNK
    nb = seq_len // q
    zeros = jnp.zeros((q, 128), BF16)
    kp_ref[0:q, :] = zeros
    vp_ref[0:q, :] = zeros
    kp_ref[seq_len + q:seq_len + 2 * q, :] = zeros
    vp_ref[seq_len + q:seq_len + 2 * q, :] = zeros
    kp_ref[q:seq_len + q, :] = u_ref[:, 384:512]
    vp_ref[q:seq_len + q, :] = u_ref[:, 512:640]
    row = lax.broadcasted_iota(jnp.int32, (3 * q, 3 * q), 0) & (q - 1)
    col = lax.broadcasted_iota(jnp.int32, (3 * q, 3 * q), 1)
    bias_ref[...] = jnp.where(jnp.abs(col - q - row) <= SWA_SIDE, 0.0, NEG)
    rgrp = lax.broadcasted_iota(jnp.int32, (3 * q, 1), 0) // q
    col1 = lax.broadcasted_iota(jnp.int32, (1, 3 * q), 1)
    lane1 = lax.broadcasted_iota(jnp.int32, (1, 128), 1)
    lo_f = jnp.where(lane1 < HEAD_DIM, 1.0, 0.0)
    hi_f = 1.0 - lo_f
    halves = ((lo_f.astype(BF16), lo_f), (hi_f.astype(BF16), hi_f))

    def body(qb, carry):
        q0 = pl.multiple_of(qb * q, q)
        qs = jnp.concatenate([u_ref[pl.ds(q0, q), j * 128:(j + 1) * 128] for j in range(3)], axis=0)
        kw = kp_ref[pl.ds(q0, 3 * q), :]
        vw = vp_ref[pl.ds(q0, 3 * q), :]
        inside = (col1 >= q - q0) & (col1 < seq_len + q - q0)
        bias = bias_ref[...] + jnp.where(inside, 0.0, NEG)
        o = None
        for hk, (sel_b, sel_f) in enumerate(halves):
            s = _dot_nt(qs * sel_b, kw) + bias
            sk = jnp.where(rgrp == 0, sink_ref[3 * hk],
                           jnp.where(rgrp == 1, sink_ref[3 * hk + 1], sink_ref[3 * hk + 2]))
            m = jnp.maximum(jnp.max(s, axis=-1, keepdims=True), sk)
            p = jnp.exp(s - m)
            l = jnp.sum(p, axis=-1, keepdims=True) + jnp.exp(sk - m)
            oh = (_dot(p.astype(BF16), vw) / l) * sel_f
            o = oh if o is None else o + oh
        o_ref[pl.ds(q0, q), :] = jnp.concatenate(
            [o[j * q:(j + 1) * q, :] for j in range(3)], axis=1).astype(o_ref.dtype)
        return carry

    lax.fori_loop(0, nb, body, 0, unroll=2)


def _swa_call(ub, sink, seq_len):
    n = ub.shape[0]
    nseq = n // seq_len
    return pl.pallas_call(
        functools.partial(_swa_kernel, seq_len=seq_len),
        grid=(nseq,),
        in_specs=[pl.BlockSpec(memory_space=pltpu.SMEM),
                  pl.BlockSpec((seq_len, 640), lambda b: (b, 0))],
        out_specs=pl.BlockSpec((seq_len, 384), lambda b: (b, 0)),
        out_shape=jax.ShapeDtypeStruct((n, 384), BF16),
        scratch_shapes=[pltpu.VMEM((seq_len + 2 * CHUNK, 128), BF16),
                        pltpu.VMEM((seq_len + 2 * CHUNK, 128), BF16),
                        pltpu.VMEM((3 * CHUNK, 3 * CHUNK), F32)],
        compiler_params=_cparams(("parallel",)),
        name="swa",
    )(sink, ub)


def _mlstm_kernel(uc_ref, ug_ref, gb_ref, ng_ref, o_ref,
                  gcol_ref, gtr_ref, hacc_ref, ct_ref, m_ref, mask_ref, *, seq_len):
    q = CHUNK
    nc = seq_len // q
    lane = lax.broadcasted_iota(jnp.int32, (q, 128), 1)
    ct_ref[...] = jnp.zeros_like(ct_ref)
    m_ref[...] = jnp.zeros_like(m_ref)
    hacc_ref[...] = jnp.zeros_like(hacc_ref)

    def prep(c, carry):
        r0 = pl.multiple_of(c * q, q)
        raw = ug_ref[pl.ds(r0, q), :] + gb_ref[...]
        lf = jnp.minimum(raw, 0.0) - jnp.log(1.0 + jnp.exp(-jnp.abs(raw)))
        p = _prefix_rows(lf)
        s = p[q - 1:q, :] - p + lf
        g = jnp.where(lane < LANE_FG, raw, jnp.where(lane < LANE_FG + ML_HEADS, p, s))
        gcol_ref[pl.ds(r0, q), :] = g
        gtr_ref[c] = g.T
        return carry

    lax.fori_loop(0, nc, prep, 0)

    ri = lax.broadcasted_iota(jnp.int32, (q, q), 0)
    ci = lax.broadcasted_iota(jnp.int32, (q, q), 1)
    mask_ref[0] = jnp.where(ci <= ri, 0.0, NEG)
    mask_ref[1] = jnp.where(ci >= ri, 0.0, NEG)
    keep_lanes = jnp.where(lax.broadcasted_iota(jnp.int32, (1, ML_PAD), 1) < ML_DIM, 1.0, 0.0)

    def main(c, carry):
        units = []
        for dirn in range(2):
            ck = c if dirn == 0 else nc - 1 - c
            r0 = pl.multiple_of(ck * q, q)
            g = gcol_ref[pl.ds(r0, q), :]
            gt = gtr_ref[ck]
            for h in range(ML_HEADS):
                s_idx = dirn * ML_HEADS + h
                li, lb = LANE_IG + s_idx, LANE_FG + s_idx
                edge = q - 1 if dirn == 0 else 0
                units.append(dict(
                    dirn=dirn, h=h, r0=r0, s_idx=s_idx,
                    b_col=g[:, lb:lb + 1], b_row=gt[lb:lb + 1, :], i_col=g[:, li:li + 1],
                    i_row=gt[li:li + 1, :], tot=g[edge:edge + 1, lb:lb + 1],
                    m_st=m_ref[s_idx][0:1, 0:1], ct=ct_ref[s_idx],
                    qh=uc_ref[pl.ds(r0, q), h * ML_PAD:(h + 1) * ML_PAD],
                    kh=uc_ref[pl.ds(r0, q), 512 + h * ML_PAD:512 + (h + 1) * ML_PAD],
                    vh=uc_ref[pl.ds(r0, q), 1024 + h * ML_PAD:1024 + (h + 1) * ML_PAD]))
        for u in units:
            u['logd'] = u['b_col'] - u['b_row'] + u['i_row'] + mask_ref[u['dirn']]
            u['m_inter'] = u['b_col'] + u['m_st']
        for u in units:
            u['m_t'] = jnp.maximum(jnp.max(u['logd'], axis=-1, keepdims=True), u['m_inter'])
        for u in units:
            u['qk'] = _dot_nt(u['qh'], u['kh'])
            u['qc'] = _dot(u['qh'], u['ct'].astype(BF16))
        for u in units:
            u['sm'] = (u['qk'] * jnp.exp(u['logd'] - u['m_t'])).astype(BF16)
            u['inter'] = jnp.exp(u['m_inter'] - u['m_t'])
        for u in units:
            u['num'] = _dot(u['sm'], u['vh']) + u['inter'] * u['qc']
        for u in units:
            den = u['num'][:, ML_DIM:ML_DIM + 1]
            hh = u['num'] / jnp.maximum(jnp.abs(den), jnp.exp(-u['m_t'])) * keep_lanes
            hacc_ref[pl.ds(u['r0'], q), u['h'] * ML_PAD:(u['h'] + 1) * ML_PAD] += hh
        for u in units:
            tot, m_st = u['tot'], u['m_st']
            m_new = jnp.maximum(tot + m_st,
                                jnp.max(tot - u['b_row'] + u['i_row'], axis=-1, keepdims=True))
            wk_col = jnp.exp(tot - u['b_col'] + u['i_col'] - m_new)
            u['keep'] = jnp.exp(tot + m_st - m_new)
            u['vw'] = (u['vh'].astype(F32) * wk_col).astype(BF16)
            m_ref[u['s_idx']] = jnp.broadcast_to(m_new, (8, 128))
        for u in units:
            ct_ref[u['s_idx']] = u['keep'] * u['ct'] + _dot_tn(u['kh'], u['vw'])
        return carry

    lax.fori_loop(0, nc, main, 0)

    def fin(c, carry):
        r0 = pl.multiple_of(c * q, q)
        outs = []
        for h in range(ML_HEADS):
            hs = hacc_ref[pl.ds(r0, q), h * ML_PAD:(h + 1) * ML_PAD]
            ms = jnp.sum(hs * hs, axis=-1, keepdims=True) * (1.0 / ML_DIM)
            hn = hs * lax.rsqrt(ms + NORM_EPS) * ng_ref[:, h * ML_PAD:(h + 1) * ML_PAD]
            og = uc_ref[pl.ds(r0, q), 1536 + h * ML_PAD:1536 + (h + 1) * ML_PAD].astype(F32)
            outs.append(hn * _sigmoid(og))
        o_ref[pl.ds(r0, q), :] = jnp.concatenate(outs, axis=1).astype(o_ref.dtype)
        return carry

    lax.fori_loop(0, nc, fin, 0)


def _mlstm_call(uc, ug, gb, ng, seq_len):
    n = uc.shape[0]
    nseq = n // seq_len
    nc = seq_len // CHUNK
    seq = lambda b: (b, 0)
    const = lambda b: (0, 0)
    width = ML_HEADS * ML_PAD
    return pl.pallas_call(
        functools.partial(_mlstm_kernel, seq_len=seq_len),
        grid=(nseq,),
        in_specs=[pl.BlockSpec((seq_len, 4 * width), seq),
                  pl.BlockSpec((seq_len, 128), seq),
                  pl.BlockSpec((1, 128), const),
                  pl.BlockSpec((1, width), const)],
        out_specs=pl.BlockSpec((seq_len, width), seq),
        out_shape=jax.ShapeDtypeStruct((n, width), BF16),
        scratch_shapes=[pltpu.VMEM((seq_len, 128), F32),
                        pltpu.VMEM((nc, 128, CHUNK), F32),
                        pltpu.VMEM((seq_len, width), F32),
                        pltpu.VMEM((2 * ML_HEADS, ML_PAD, ML_PAD), F32),
                        pltpu.VMEM((2 * ML_HEADS, 8, 128), F32),
                        pltpu.VMEM((2, CHUNK, CHUNK), F32)],
        compiler_params=_cparams(("parallel",)),
        name="mlstm",
    )(uc, ug, gb, ng)


def _attend_units(units, lo_b, hi_b, lo_f, hi_f):
    n = units[0][0].shape[0]
    ss = []
    for q, kw, _, bias in units:
        s2 = _dot_nt(jnp.concatenate([q * lo_b, q * hi_b], axis=0), kw)
        ss.append(jnp.concatenate([s2[0:n] + bias, s2[n:2 * n] + bias], axis=0))
    ms = [jnp.max(s, axis=-1, keepdims=True) for s in ss]
    ps = [jnp.exp(s - m) for s, m in zip(ss, ms)]
    ls = [jnp.sum(p, axis=-1, keepdims=True) for p in ps]
    os_ = [_dot(p.astype(BF16), u[2]) / l for p, u, l in zip(ps, units, ls)]
    res = []
    for o2, m, l in zip(os_, ms, ls):
        lse2 = m + jnp.log(l)
        res.append((o2[0:n] * lo_f + o2[n:2 * n] * hi_f, lse2[0:n] * lo_f + lse2[n:2 * n] * hi_f))
    return res


def _dil_kernel(q0_ref, q1_ref, q2_ref, k_ref, v_ref, o_ref,
                qf_ref, kf_ref, vf_ref, kc0_ref, vc0_ref, kc1_ref, vc1_ref, og_ref, lg_ref,
                qc2_ref, kc2_ref, vc2_ref, og2_ref, lg2_ref, band_ref, band2_ref, *, seq_len):
    qr = CHUNK
    pad = DIL_SIDE
    d1, d2 = DIL_CONFIGS[1][1], DIL_CONFIGS[2][1]
    n1, n2 = seq_len // d1, seq_len // d2
    assert n2 == qr and n1 % qr == 0
    qf_ref[0] = q1_ref[...].astype(F32)
    qf_ref[1] = q2_ref[...].astype(F32)
    kf_ref[...] = k_ref[...].astype(F32)
    vf_ref[...] = v_ref[...].astype(F32)
    zeros = jnp.zeros((pad, 128), BF16)
    kc0_ref[0:pad, :] = zeros
    vc0_ref[0:pad, :] = zeros
    kc0_ref[pad + seq_len:2 * pad + seq_len, :] = zeros
    vc0_ref[pad + seq_len:2 * pad + seq_len, :] = zeros
    kc0_ref[pad:pad + seq_len, :] = k_ref[...]
    vc0_ref[pad:pad + seq_len, :] = v_ref[...]
    for r in range(d1):
        kc1_ref[r, 0:pad, :] = zeros
        vc1_ref[r, 0:pad, :] = zeros
        kc1_ref[r, pad + n1:2 * pad + n1, :] = zeros
        vc1_ref[r, pad + n1:2 * pad + n1, :] = zeros
        kc1_ref[r, pad:pad + n1, :] = kf_ref[pl.ds(r, n1, stride=d1), :].astype(BF16)
        vc1_ref[r, pad:pad + n1, :] = vf_ref[pl.ds(r, n1, stride=d1), :].astype(BF16)

    row = lax.broadcasted_iota(jnp.int32, (qr, 2 * qr), 0)
    col = lax.broadcasted_iota(jnp.int32, (qr, 2 * qr), 1)
    band_ref[...] = jnp.where(jnp.abs(col - pad - row) <= DIL_SIDE, 0.0, NEG)
    row2 = lax.broadcasted_iota(jnp.int32, (qr, qr), 0)
    col2 = lax.broadcasted_iota(jnp.int32, (qr, qr), 1)
    band2_ref[...] = jnp.where(jnp.abs(col2 - row2) <= DIL_SIDE, 0.0, NEG)
    lane1 = lax.broadcasted_iota(jnp.int32, (1, 128), 1)
    col1 = lax.broadcasted_iota(jnp.int32, (1, 2 * qr), 1)
    lo_f = jnp.where(lane1 < HEAD_DIM, 1.0, 0.0)
    hi_f = 1.0 - lo_f
    sel = (lo_f.astype(BF16), hi_f.astype(BF16), lo_f, hi_f)

    def window_bias(u0, n):
        inside = (col1 >= pad - u0) & (col1 < n + pad - u0)
        return band_ref[...] + jnp.where(inside, 0.0, NEG)

    def batch0(b4, carry):
        units, starts = [], []
        for i in range(DIL_BATCH):
            u0 = pl.multiple_of((b4 * DIL_BATCH + i) * qr, qr)
            starts.append(u0)
            units.append((q0_ref[pl.ds(u0, qr), :], kc0_ref[pl.ds(u0, 2 * qr), :],
                          vc0_ref[pl.ds(u0, 2 * qr), :], window_bias(u0, seq_len)))
        for u0, (o, lse) in zip(starts, _attend_units(units, *sel)):
            og_ref[0, pl.ds(u0, qr), :] = o
            lg_ref[0, pl.ds(u0, qr), :] = lse
        return carry

    lax.fori_loop(0, seq_len // qr // DIL_BATCH, batch0, 0)

    nb1 = n1 // qr
    assert nb1 == DIL_BATCH

    def batch1(r, carry):
        units, rowsl = [], []
        for i in range(DIL_BATCH):
            u0 = i * qr
            rows = pl.ds(r + u0 * d1, qr, stride=d1)
            rowsl.append(rows)
            units.append((qf_ref[0, rows, :].astype(BF16), kc1_ref[r, u0:u0 + 2 * qr, :],
                          vc1_ref[r, u0:u0 + 2 * qr, :], window_bias(u0, n1)))
        for rows, (o, lse) in zip(rowsl, _attend_units(units, *sel)):
            og_ref[1, rows, :] = o
            lg_ref[1, rows, :] = lse
        return carry

    lax.fori_loop(0, d1, batch1, 0)

    for r in range(d2):
        rows = pl.ds(r, qr, stride=d2)
        qc2_ref[r] = qf_ref[1, rows, :].astype(BF16)
        kc2_ref[r] = kf_ref[rows, :].astype(BF16)
        vc2_ref[r] = vf_ref[rows, :].astype(BF16)

    def batch2(b4, carry):
        rs = [b4 * DIL_BATCH + i for i in range(DIL_BATCH)]
        units = [(qc2_ref[r], kc2_ref[r], vc2_ref[r], band2_ref[...]) for r in rs]
        for r, (o, lse) in zip(rs, _attend_units(units, *sel)):
            og2_ref[r] = o
            lg2_ref[r] = lse
        return carry

    lax.fori_loop(0, d2 // DIL_BATCH, batch2, 0)
    for r in range(d2):
        rows = pl.ds(r, qr, stride=d2)
        og_ref[2, rows, :] = og2_ref[r]
        lg_ref[2, rows, :] = lg2_ref[r]

    def fin(c, carry):
        r0 = pl.multiple_of(c * qr, qr)
        rows = pl.ds(r0, qr)
        l0, l1, l2 = lg_ref[0, rows, :], lg_ref[1, rows, :], lg_ref[2, rows, :]
        m = jnp.maximum(jnp.maximum(l0, l1), l2)
        w0, w1, w2 = jnp.exp(l0 - m), jnp.exp(l1 - m), jnp.exp(l2 - m)
        y = (og_ref[0, rows, :] * w0 + og_ref[1, rows, :] * w1 + og_ref[2, rows, :] * w2)
        o_ref[rows, :] = (y / (w0 + w1 + w2)).astype(o_ref.dtype)
        return carry

    lax.fori_loop(0, seq_len // qr, fin, 0, unroll=2)


def _dil_call(ud, seq_len):
    n = ud.shape[0]
    nseq = n // seq_len
    blk = (seq_len, 128)
    d1, d2 = DIL_CONFIGS[1][1], DIL_CONFIGS[2][1]
    n1 = seq_len // d1

    def col(off):
        return lambda b, p: (b, off + p)

    return pl.pallas_call(
        functools.partial(_dil_kernel, seq_len=seq_len),
        grid=(nseq, 3),
        in_specs=[pl.BlockSpec(blk, col(0)), pl.BlockSpec(blk, col(3)), pl.BlockSpec(blk, col(6)),
                  pl.BlockSpec(blk, col(9)), pl.BlockSpec(blk, col(12))],
        out_specs=pl.BlockSpec(blk, col(0)),
        out_shape=jax.ShapeDtypeStruct((n, 384), BF16),
        scratch_shapes=[pltpu.VMEM((2, seq_len, 128), F32),
                        pltpu.VMEM((seq_len, 128), F32),
                        pltpu.VMEM((seq_len, 128), F32),
                        pltpu.VMEM((seq_len + 2 * DIL_SIDE, 128), BF16),
                        pltpu.VMEM((seq_len + 2 * DIL_SIDE, 128), BF16),
                        pltpu.VMEM((d1, n1 + 2 * DIL_SIDE, 128), BF16),
                        pltpu.VMEM((d1, n1 + 2 * DIL_SIDE, 128), BF16),
                        pltpu.VMEM((3, seq_len, 128), F32),
                        pltpu.VMEM((3, seq_len, 128), F32)]
                       + [pltpu.VMEM((d2, CHUNK, 128), BF16)] * 3
                       + [pltpu.VMEM((d2, CHUNK, 128), F32)] * 2
                       + [pltpu.VMEM((CHUNK, 2 * CHUNK), F32), pltpu.VMEM((CHUNK, CHUNK), F32)],
        compiler_params=_cparams(("parallel", "parallel")),
        name="dil",
    )(ud, ud, ud, ud, ud)


def _out_proj_kernel(ya_ref, yb_ref, yc_ref, yd_ref, x_ref, g1_ref, sc_ref, sh_ref,
                     pg_ref, fg_ref, wa_ref, wb_ref, wc_ref, wd_ref, wrh_ref, wrl_ref, br_ref,
                     x1_ref, h2_ref, rt_ref):
    for r0 in range(0, x_ref.shape[0], OUT_SUB):
        rs = slice(r0, r0 + OUT_SUB)
        y = (_dot(ya_ref[rs, :], wa_ref[...]) + _dot(yb_ref[rs, :], wb_ref[...])
             + _dot(yc_ref[rs, :], wc_ref[...]) + _dot(yd_ref[rs, :], wd_ref[...]))
        ms = jnp.mean(y * y, axis=-1, keepdims=True)
        yn = y * lax.rsqrt(ms + NORM_EPS) * pg_ref[...]
        x1 = x_ref[rs, :] + g1_ref[0] * yn
        x1_ref[rs, :] = x1
        ms2 = jnp.mean(x1 * x1, axis=-1, keepdims=True)
        h2 = x1 * lax.rsqrt(ms2 + NORM_EPS) * fg_ref[...]
        h2 = h2 * (1.0 + sc_ref[0]) + sh_ref[0]
        half = h2.shape[1] // 2
        lo = lax.bitcast_convert_type(h2[:, :half].astype(BF16).astype(F32), jnp.uint32)
        hi = lax.bitcast_convert_type(h2[:, half:].astype(BF16).astype(F32), jnp.uint32)
        h2_ref[rs, :] = (lo >> 16) | hi
        h_hi = h2.astype(BF16)
        h_lo = (h2 - h_hi.astype(F32)).astype(BF16)
        logits = (_dot(h_hi, wrh_ref[...]) + _dot(h_lo, wrh_ref[...]) + _dot(h_hi, wrl_ref[...])
                  + br_ref[...])
        lane = lax.broadcasted_iota(jnp.int32, logits.shape, 1)
        vals, idxs = [], []
        for _ in range(TOP_K):
            m = jnp.max(logits, axis=-1, keepdims=True)
            idx = jnp.min(jnp.where(logits == m, lane, 128), axis=-1, keepdims=True)
            vals.append(m)
            idxs.append(idx)
            logits = jnp.where(lane == idx, -3e38, logits)
        es = [jnp.exp(v - vals[0]) for v in vals]
        tot = es[0] + es[1] + es[2] + es[3]
        rt = jnp.zeros(lane.shape, F32)
        for k in range(TOP_K):
            rt = jnp.where(lane == k, idxs[k].astype(F32), rt)
            rt = jnp.where(lane == TOP_K + k, es[k] / tot, rt)
        rt_ref[rs, :] = rt[:, 0:2 * TOP_K]


def _out_proj_call(ya, yb, yc, yd, x, g1, sc, sh, pg, fg, wa, wb, wc, wd, wrh, wrl, br, seq_len, tm=512):
    n, d = x.shape
    per_seq = seq_len // tm
    row = lambda i: (i, 0)
    seq = lambda i: (i // per_seq, 0, 0)
    const = lambda i: (0, 0)

    def full(a):
        return pl.BlockSpec(a.shape, const)

    return pl.pallas_call(
        _out_proj_kernel,
        grid=(n // tm,),
        in_specs=[pl.BlockSpec((tm, ya.shape[1]), row), pl.BlockSpec((tm, yb.shape[1]), row),
                  pl.BlockSpec((tm, yc.shape[1]), row), pl.BlockSpec((tm, yd.shape[1]), row),
                  pl.BlockSpec((tm, d), row),
                  pl.BlockSpec((1, 1, d), seq), pl.BlockSpec((1, 1, d), seq),
                  pl.BlockSpec((1, 1, d), seq),
                  full(pg), full(fg), full(wa), full(wb), full(wc), full(wd), full(wrh), full(wrl), full(br)],
        out_specs=[pl.BlockSpec((tm, d), row), pl.BlockSpec((tm, d // 2), row),
                   pl.BlockSpec((tm, 2 * TOP_K), row)],
        out_shape=[jax.ShapeDtypeStruct((n, d), F32), jax.ShapeDtypeStruct((n, d // 2), jnp.uint32),
                   jax.ShapeDtypeStruct((n, 2 * TOP_K), F32)],
        compiler_params=_cparams(("parallel",)),
        name="out_proj",
    )(ya, yb, yc, yd, x, g1, sc, sh, pg, fg, wa, wb, wc, wd, wrh, wrl, br)


def _moe_kernel(nb_ref, ea_ref, eb_ref, sel_ref, bs_ref, bv_ref, idx_ref, gate_ref, h_ref,
                wgu_a, bgu_a, wdn_a, bdn_a, wgu_b, bgu_b, wdn_b, bdn_b, o_ref,
                xs0_ref, xs1_ref, ys0_ref, ys1_ref):
    t = pl.program_id(0)
    j = pl.program_id(1)
    nb = nb_ref[t]
    rows = MOE_ROWS
    nslot = pl.num_programs(1) + 1
    half = h_ref.shape[1]

    def gather(dst_ref, blk):
        base = bs_ref[t * nslot + blk]
        for r in range(rows):
            i = idx_ref[0, 0, base + r]
            dst_ref[r:r + 1, :] = h_ref[pl.ds(i, 1), :]

    def scatter(src_ref, blk, nvalid):
        base = bs_ref[t * nslot + blk]
        i0 = idx_ref[0, 0, base]
        for r0 in range(0, rows, MOE_GROUP):
            ks = range(MOE_GROUP)
            ok = [r0 + k < nvalid for k in ks]
            ii = [jnp.where(ok[k], idx_ref[0, 0, base + r0 + k], i0) for k in ks]
            gg = [jnp.where(ok[k], gate_ref[0, 0, base + r0 + k], 0.0) for k in ks]
            cur = [o_ref[pl.ds(ii[k], 1), :] for k in ks]
            for k in reversed(ks):
                o_ref[pl.ds(ii[k], 1), :] = cur[k] + gg[k] * src_ref[r0 + k:r0 + k + 1, :]

    def expert(xs_ref, ys_ref, wgu_ref, bgu_ref, wdn_ref, bdn_ref):
        xu = xs_ref[...]
        x_lo = lax.bitcast_convert_type(xu << 16, F32).astype(BF16)
        x_hi = lax.bitcast_convert_type(xu & jnp.uint32(0xFFFF0000), F32).astype(BF16)
        gu = _dot(x_lo, wgu_ref[0, 0:half, :]) + _dot(x_hi, wgu_ref[0, half:2 * half, :]) + bgu_ref[0]
        gate = jnp.minimum(gu[:, :D_EXPERT], SWIGLU_LIMIT)
        up = jnp.clip(gu[:, D_EXPERT:], -SWIGLU_LIMIT, SWIGLU_LIMIT)
        act = (up + 1.0) * gate * _sigmoid(SWIGLU_ALPHA * gate)
        ys_ref[...] = _dot(act.astype(BF16), wdn_ref[0]) + bdn_ref[0]

    @pl.when(j == 0)
    def _():
        o_ref[...] = jnp.zeros_like(o_ref)
        ys1_ref[...] = jnp.zeros_like(ys1_ref)
        gather(xs0_ref, 0)

    prev = jnp.maximum(j - 1, 0)
    nv_prev = jnp.where(j > 0, bv_ref[t * nslot + prev], 0)
    sel = sel_ref[t * nslot + j]
    bufs = ((xs0_ref, xs1_ref, ys0_ref, ys1_ref), (xs1_ref, xs0_ref, ys1_ref, ys0_ref))
    wsets = ((wgu_a, bgu_a, wdn_a, bdn_a), (wgu_b, bgu_b, wdn_b, bdn_b))
    for par, (xs_cur, xs_nxt, ys_cur, ys_prv) in enumerate(bufs):
        for ws, wset in enumerate(wsets):
            @pl.when((j < nb) & (j % 2 == par) & (sel == ws))
            def _(xs_cur=xs_cur, xs_nxt=xs_nxt, ys_cur=ys_cur, ys_prv=ys_prv, wset=wset):
                gather(xs_nxt, j + 1)
                expert(xs_cur, ys_cur, *wset)
                scatter(ys_prv, prev, nv_prev)

        @pl.when((j == nb) & (j % 2 == par))
        def _(ys_prv=ys_prv):
            scatter(ys_prv, prev, nv_prev)


def _moe_call(nb, ea, eb, sel, bs, bv, idx, gates, h2p, wgu, bgu, wdn, bdn, tile):
    n, half = h2p.shape
    d = 2 * half
    ntile = n // tile
    plen = idx.shape[-1]
    nslot = ea.shape[0] // ntile

    def amap(t, j, nb_ref, ea_ref, *_):
        return (ea_ref[t * nslot + j], 0, 0)

    def bmap(t, j, nb_ref, ea_ref, eb_ref, *_):
        return (eb_ref[t * nslot + j], 0, 0)

    def wspecs(m):
        return [pl.BlockSpec((1, d, 2 * D_EXPERT), m), pl.BlockSpec((1, 1, 2 * D_EXPERT), m),
                pl.BlockSpec((1, D_EXPERT, d), m), pl.BlockSpec((1, 1, d), m)]

    tmap3 = lambda t, j, *_: (t, 0, 0)
    tmap2 = lambda t, j, *_: (t, 0)
    grid_spec = pltpu.PrefetchScalarGridSpec(
        num_scalar_prefetch=6,
        grid=(ntile, nslot - 1),
        in_specs=[pl.BlockSpec((1, 1, plen), tmap3, memory_space=pltpu.SMEM),
                  pl.BlockSpec((1, 1, plen), tmap3, memory_space=pltpu.SMEM),
                  pl.BlockSpec((tile, half), tmap2, pipeline_mode=pl.Buffered(1))]
        + wspecs(amap) + wspecs(bmap),
        out_specs=pl.BlockSpec((tile, d), tmap2, pipeline_mode=pl.Buffered(1)),
        scratch_shapes=[pltpu.VMEM((MOE_ROWS, half), jnp.uint32)] * 2 + [pltpu.VMEM((MOE_ROWS, d), F32)] * 2,
    )
    return pl.pallas_call(
        _moe_kernel,
        grid_spec=grid_spec,
        out_shape=jax.ShapeDtypeStruct((n, d), F32),
        compiler_params=_cparams(("parallel", "arbitrary")),
        name="moe",
    )(nb, ea, eb, sel, bs, bv, idx, gates, h2p, wgu, bgu, wdn, bdn, wgu, bgu, wdn, bdn)


def _route(rt, tile):
    n = rt.shape[0]
    ntile = n // tile
    npair = tile * TOP_K
    nslot = npair // MOE_ROWS + N_EXPERTS + 2
    e = rt[:, :TOP_K].astype(jnp.int32).reshape(ntile, npair)
    g = rt[:, TOP_K:].reshape(ntile, npair)
    tok = jnp.broadcast_to((jnp.arange(npair, dtype=jnp.int32) // TOP_K)[None, :], (ntile, npair))
    _, tok_s, g_s = lax.sort((e, tok, g), dimension=1, is_stable=True, num_keys=1)
    experts = jnp.arange(N_EXPERTS, dtype=jnp.int32)
    counts = jnp.sum(e[:, :, None] == experts, axis=1, dtype=jnp.int32)
    nblocks = (counts + MOE_ROWS - 1) // MOE_ROWS
    bends = jnp.cumsum(nblocks, axis=-1)
    starts = jnp.cumsum(counts, axis=-1) - counts
    nb = bends[:, -1]
    slot = jnp.minimum(jnp.arange(nslot, dtype=jnp.int32)[None, :], nb[:, None] - 1)
    be = jnp.sum(slot[:, :, None] >= bends[:, None, :], axis=-1, dtype=jnp.int32)
    onehot = (be[:, :, None] == experts).astype(jnp.int32)
    pick = lambda v: jnp.sum(onehot * v[:, None, :], axis=-1)
    within = slot - pick(bends - nblocks)
    bs = pick(starts) + within * MOE_ROWS
    bv = jnp.clip(pick(counts) - within * MOE_ROWS, 0, MOE_ROWS)
    padw = ((0, 0), (0, MOE_ROWS))
    idx = jnp.pad(tok_s, padw).reshape(ntile, 1, -1)
    gl = jnp.pad(g_s, padw).reshape(ntile, 1, -1)
    present = counts > 0
    ordinal = jnp.cumsum(present.astype(jnp.int32), axis=-1) - 1
    later = present[:, None, :] & (experts[None, None, :] > experts[None, :, None])
    nxt = jnp.min(jnp.where(later, experts[None, None, :], N_EXPERTS), axis=-1)
    nxt = jnp.where(nxt == N_EXPERTS, experts[None, :], nxt)
    sel = pick(ordinal) % 2
    ea = jnp.where(sel == 0, be, pick(nxt))
    eb = jnp.where(sel == 1, be, pick(nxt))
    flat = lambda v: v.reshape(-1).astype(jnp.int32)
    return flat(nb), flat(ea), flat(eb), flat(sel), flat(bs), flat(bv), idx, gl


def _ffn_res_kernel(x_ref, f_ref, g2_ref, pg_ref, o_ref):
    f = f_ref[...]
    ms = jnp.mean(f * f, axis=-1, keepdims=True)
    o_ref[...] = x_ref[...] + g2_ref[0] * (f * lax.rsqrt(ms + NORM_EPS) * pg_ref[...])


def _ffn_res_call(x1, f, g2, pg, seq_len, tm=1024):
    n, d = x1.shape
    per_seq = seq_len // tm
    row = lambda i: (i, 0)
    return pl.pallas_call(
        _ffn_res_kernel,
        grid=(n // tm,),
        in_specs=[pl.BlockSpec((tm, d), row), pl.BlockSpec((tm, d), row),
                  pl.BlockSpec((1, 1, d), lambda i: (i // per_seq, 0, 0)),
                  pl.BlockSpec((1, d), lambda i: (0, 0))],
        out_specs=pl.BlockSpec((tm, d), row),
        out_shape=jax.ShapeDtypeStruct((n, d), F32),
        compiler_params=_cparams(("parallel",)),
        name="ffn_res",
    )(x1, f, g2, pg)


def _rope_tables(seq_len):
    half = 8
    inv = 1.0 / (ROPE_THETA ** (jnp.arange(half, dtype=F32) * (2.0 / 16)))
    ang = jnp.arange(seq_len, dtype=F32)[:, None] * inv[None, :]
    cos, sin = jnp.cos(ang), jnp.sin(ang)
    ones = jnp.ones((seq_len, 48), F32)
    zeros8 = jnp.zeros((seq_len, 8), F32)
    zeros48 = jnp.zeros((seq_len, 48), F32)
    rc = jnp.concatenate([cos, cos, ones], axis=1)
    rs1 = jnp.concatenate([zeros8, sin, zeros48], axis=1)
    rs2 = jnp.concatenate([-sin, zeros8, zeros48], axis=1)
    tile2 = lambda a: jnp.concatenate([a, a], axis=1)
    return tile2(rc), tile2(rs1), tile2(rs2)


def _pad_heads(w):
    lead = w.shape[:-1]
    w = w.reshape(lead + (ML_HEADS, ML_DIM))
    w = jnp.pad(w, [(0, 0)] * len(lead) + [(0, 0), (0, ML_PAD - ML_DIM)])
    return w.reshape(lead + (ML_HEADS * ML_PAD,))


def _layout_w_in(w_in):
    d = w_in.shape[0]
    offs = np.cumsum((0,) + IN_SPLITS)
    parts = [w_in[:, offs[i]:offs[i + 1]] for i in range(len(IN_SPLITS))]
    (z, xbc, dt, sq, sk, sv, mq, mk, mv, mo, mi, mf, dq, dk, dv) = parts
    zc = lambda k: jnp.zeros((d, k), w_in.dtype)
    gates = jnp.concatenate([dt, zc(4), mi, mf, zc(128 - 32)], axis=1)
    sq = sq.reshape(d, SWA_Q_HEADS, HEAD_DIM)[:, SWA_HEAD_ORDER, :].reshape(d, -1)
    w = jnp.concatenate([z, xbc, gates, sq, sk, sv,
                         _pad_heads(mq), _pad_heads(mk), _pad_heads(mv), _pad_heads(mo),
                         dq, dk, dv], axis=1)
    assert w.shape[1] == COL_END
    return w.astype(BF16)


def _lane_row(parts, width=128):
    row = jnp.zeros((width,), F32)
    for off, v in parts:
        row = row.at[off:off + v.shape[0]].set(v.astype(F32))
    return row.reshape(1, width)


def _layer(x, mod_l, p, l, tabs, seq_len):
    d = D_MODEL
    nseq = x.shape[0] // seq_len
    sh1, sc1, g1, sh2, sc2, g2 = [mod_l[:, i * d:(i + 1) * d].reshape(nseq, 1, d) for i in range(6)]
    row = lambda v: v.reshape(1, -1).astype(F32)

    ua, ug, ub, uc, ud = _in_proj_call(x, sc1, sh1, row(p['pre_mix_g'][l]), _layout_w_in(p['w_in'][l]),
                                       *tabs, seq_len)
    cw = jnp.pad(p['ssd_conv_w'][l], ((0, 8 - SSD_CONV), (0, 0)))
    ya = _ssd_call(ua, ug, cw, row(p['ssd_conv_b'][l]),
                   _lane_row([(0, p['ssd_a_log'][l].reshape(-1))]),
                   _lane_row([(0, p['ssd_dt_bias'][l].reshape(-1))]),
                   row(jnp.repeat(p['ssd_d'][l], HEAD_DIM)), row(p['ssd_norm_g'][l]), seq_len)
    yb = _swa_call(ub, p['swa_sink'][l].astype(F32), seq_len)
    gb = _lane_row([(LANE_IG, p['mlstm_i_bias'][l].reshape(-1)),
                    (LANE_FG, p['mlstm_f_bias'][l].reshape(-1))])
    yc = _mlstm_call(uc, ug, gb, row(_pad_heads(p['mlstm_norm_g'][l])), seq_len)
    yd = _dil_call(ud, seq_len)

    w_out = p['w_out'][l]
    wa = w_out[0:384].astype(BF16)
    wb = w_out[384:768].reshape(SWA_Q_HEADS, HEAD_DIM, d)[SWA_HEAD_ORDER, :, :].reshape(384, d).astype(BF16)
    wc = jnp.pad(w_out[768:1152].reshape(ML_HEADS, ML_DIM, d),
                 ((0, 0), (0, ML_PAD - ML_DIM), (0, 0))).reshape(ML_HEADS * ML_PAD, d).astype(BF16)
    wd = w_out[1152:1536].astype(BF16)
    wr = jnp.pad(p['w_router'][l], ((0, 0), (0, 128 - N_EXPERTS))).astype(F32)
    wrh = wr.astype(BF16)
    wrl = (wr - wrh.astype(F32)).astype(BF16)
    br = jnp.concatenate([p['b_router'][l].astype(F32), jnp.full((128 - N_EXPERTS,), NEG, F32)]).reshape(1, 128)
    x1, h2, rt = _out_proj_call(ya, yb, yc, yd, x, g1, sc2, sh2, row(p['post_mix_g'][l]),
                                row(p['pre_ffn_g'][l]), wa, wb, wc, wd, wrh, wrl, br, seq_len)

    tile = min(MOE_TILE, x.shape[0])
    nb, ea, eb, sel, bs, bv, idx, gs = _route(rt, tile)
    f = _moe_call(nb, ea, eb, sel, bs, bv, idx, gs, h2, p['w_gate_up'][l].astype(BF16),
                  p['b_gate_up'][l].reshape(N_EXPERTS, 1, -1), p['w_down'][l].astype(BF16),
                  p['b_down'][l].reshape(N_EXPERTS, 1, -1), tile)
    return _ffn_res_call(x1, f, g2, row(p['post_ffn_g'][l]), seq_len)


def _trunk(x, c, p):
    nseq, seq_len, d = x.shape
    depth = p['w_in'].shape[0]
    mod = _mod_call(c, p['w_mod'], p['b_mod'])
    tabs = _rope_tables(seq_len)
    x = x.reshape(nseq * seq_len, d)
    for l in range(depth):
        x = _layer(x, mod[l], p, l, tabs, seq_len)
    return x.reshape(nseq, seq_len, d)


def kernel(x_prompt, x_sample, c_prompt, c_sample, w_mod, b_mod, pre_mix_g, post_mix_g, pre_ffn_g, post_ffn_g, w_in, w_out, ssd_conv_w, ssd_conv_b, ssd_a_log, ssd_dt_bias, ssd_d, ssd_norm_g, swa_sink, mlstm_i_bias, mlstm_f_bias, mlstm_norm_g, w_router, b_router, w_gate_up, b_gate_up, w_down, b_down):
    p = dict(w_mod=w_mod, b_mod=b_mod, pre_mix_g=pre_mix_g, post_mix_g=post_mix_g, pre_ffn_g=pre_ffn_g,
             post_ffn_g=post_ffn_g, w_in=w_in, w_out=w_out, ssd_conv_w=ssd_conv_w, ssd_conv_b=ssd_conv_b,
             ssd_a_log=ssd_a_log, ssd_dt_bias=ssd_dt_bias, ssd_d=ssd_d, ssd_norm_g=ssd_norm_g,
             swa_sink=swa_sink, mlstm_i_bias=mlstm_i_bias, mlstm_f_bias=mlstm_f_bias,
             mlstm_norm_g=mlstm_norm_g, w_router=w_router, b_router=b_router, w_gate_up=w_gate_up,
             b_gate_up=b_gate_up, w_down=w_down, b_down=b_down)
    nb = x_prompt.shape[0]
    x = jnp.concatenate([x_prompt, x_sample], axis=0)
    c = jnp.concatenate([c_prompt, c_sample], axis=0)
    y = _trunk(x, c, p)
    return (y[:nb], y[nb:])
```

```python
import functools
import math

import jax
import jax.numpy as jnp
import numpy as np
from jax import lax
from jax.experimental import pallas as pl
from jax.experimental.pallas import tpu as pltpu

F32 = jnp.float32
BF16 = jnp.bfloat16

D_MODEL = 1024
HEAD_DIM = 64
ROPE_THETA = 500000.0
NORM_EPS = 1e-6
CHUNK = 128

SSD_HEADS = 6
SSD_INNER = 384
SSD_STATE = 64
SSD_CONV = 5
SSD_XBC = 640
SWA_Q_HEADS = 6
SWA_SIDE = 128
SWA_HEAD_ORDER = (0, 3, 1, 4, 2, 5)
ML_HEADS = 4
ML_DIM = 96
ML_PAD = 128
DIL_CONFIGS = ((128, 1), (512, 4), (2048, 16))
DIL_HEADS = 6
DIL_SIDE = 64
DIL_BATCH = 4
N_EXPERTS = 32
TOP_K = 4
D_EXPERT = 1024
SWIGLU_LIMIT = 7.0
SWIGLU_ALPHA = 1.702

IN_SPLITS = (384, 640, 12, 384, 128, 128, 384, 384, 384, 384, 8, 8, 1152, 384, 384)

COL_A = 0
COL_G = 1024
COL_B = 1152
COL_C = 1792
COL_D = 3840
COL_END = 5760
LANE_IG = 16
LANE_FG = 24

NEG = -1e30
VMEM_LIMIT = 56 * 1024 * 1024
OUT_SUB = 256
MOE_ROWS = 128
MOE_GROUP = 4
MOE_TILE = 4096


def _cparams(sem):
    return pltpu.CompilerParams(dimension_semantics=sem, vmem_limit_bytes=VMEM_LIMIT)


def _dot(a, b):
    return jnp.dot(a, b, preferred_element_type=F32)


def _dot_nt(a, b):
    return lax.dot_general(a, b, (((1,), (1,)), ((), ())), preferred_element_type=F32)


def _dot_tn(a, b):
    return lax.dot_general(a, b, (((0,), (0,)), ((), ())), preferred_element_type=F32)


def _sigmoid(x):
    return 1.0 / (1.0 + jnp.exp(-x))


def _softplus(x):
    return jnp.maximum(x, 0.0) + jnp.log(1.0 + jnp.exp(-jnp.abs(x)))


def _prefix_rows(x):
    row = lax.broadcasted_iota(jnp.int32, x.shape, 0)
    s = 1
    while s < x.shape[0]:
        x = x + jnp.where(row >= s, pltpu.roll(x, s, 0), 0.0)
        s *= 2
    return x


def _mod_kernel(c_ref, w_ref, b_ref, o_ref):
    c = c_ref[...]
    s = c * _sigmoid(c)
    o_ref[0] = jnp.dot(s, w_ref[0], preferred_element_type=F32,
                       precision=lax.Precision.HIGHEST) + b_ref[0]


def _mod_call(c, w_mod, b_mod):
    nb = c.shape[0]
    depth, d, cols = w_mod.shape
    tn = 1536
    return pl.pallas_call(
        _mod_kernel,
        grid=(depth, cols // tn),
        in_specs=[pl.BlockSpec((nb, d), lambda l, j: (0, 0)),
                  pl.BlockSpec((1, d, tn), lambda l, j: (l, 0, j)),
                  pl.BlockSpec((1, 1, tn), lambda l, j: (l, 0, j))],
        out_specs=pl.BlockSpec((1, nb, tn), lambda l, j: (l, 0, j)),
        out_shape=jax.ShapeDtypeStruct((depth, nb, cols), F32),
        compiler_params=_cparams(("parallel", "parallel")),
        name="mod",
    )(c, w_mod, b_mod.reshape(depth, 1, cols))


def _rope(a, rc, rs1, rs2):
    return a * rc + pltpu.roll(a, 8, 1) * rs1 + pltpu.roll(a, 120, 1) * rs2


def _in_proj_kernel(x_ref, sc_ref, sh_ref, g_ref, w_ref, rc_ref, rs1_ref, rs2_ref,
                    oa_ref, og_ref, ob_ref, oc_ref, od_ref):
    x = x_ref[...]
    ms = jnp.mean(x * x, axis=-1, keepdims=True)
    h = x * lax.rsqrt(ms + NORM_EPS) * g_ref[...]
    h = h * (1.0 + sc_ref[0]) + sh_ref[0]
    hb = h.astype(BF16)
    rc, rs1, rs2 = rc_ref[...], rs1_ref[...], rs2_ref[...]

    def mm(c0, width):
        return _dot(hb, w_ref[:, c0:c0 + width])

    def plain(o_ref, col0, dst0, width, scale=None, step=512, add=None):
        for c in range(0, width, step):
            wd = min(step, width - c)
            a = mm(col0 + c, wd)
            if scale is not None:
                a = a * scale
            if add is not None:
                a = a + add
            o_ref[:, dst0 + c:dst0 + c + wd] = a.astype(o_ref.dtype)

    def roped(o_ref, col0, dst0, width, scale):
        for c in range(0, width, 128):
            a = _rope(mm(col0 + c, 128), rc, rs1, rs2)
            if scale is not None:
                a = a * scale
            o_ref[:, dst0 + c:dst0 + c + 128] = a.astype(o_ref.dtype)

    qscale = HEAD_DIM ** -0.5
    plain(oa_ref, COL_A, 0, 1024)
    plain(og_ref, COL_G, 0, 128)
    roped(ob_ref, COL_B, 0, 384, qscale)
    roped(ob_ref, COL_B + 384, 384, 128, None)
    plain(ob_ref, COL_B + 512, 512, 128)
    plain(oc_ref, COL_C, 0, 512)
    plain(oc_ref, COL_C + 512, 512, 512, scale=ML_DIM ** -0.5)
    lane_in_head = lax.broadcasted_iota(jnp.int32, (1, 512), 1) % ML_PAD
    plain(oc_ref, COL_C + 1024, 1024, 512, add=jnp.where(lane_in_head == ML_DIM, 1.0, 0.0))
    plain(oc_ref, COL_C + 1536, 1536, 512)
    roped(od_ref, COL_D, 0, 1152, qscale)
    roped(od_ref, COL_D + 1152, 1152, 384, None)
    plain(od_ref, COL_D + 1536, 1536, 384, step=384)


def _in_proj_call(x, sc, sh, g, w, rc, rs1, rs2, seq_len, tm=512):
    n, d = x.shape
    per_seq = seq_len // tm
    row = lambda i: (i, 0)
    seq = lambda i: (i // per_seq, 0, 0)
    pos = lambda i: (i % per_seq, 0)
    const = lambda i: (0, 0)
    widths = (1024, 128, 640, 2048, 1920)
    dtypes = (BF16, F32, BF16, BF16, BF16)
    return pl.pallas_call(
        _in_proj_kernel,
        grid=(n // tm,),
        in_specs=[pl.BlockSpec((tm, d), row),
                  pl.BlockSpec((1, 1, d), seq),
                  pl.BlockSpec((1, 1, d), seq),
                  pl.BlockSpec((1, d), const),
                  pl.BlockSpec((d, COL_END), const, pipeline_mode=pl.Buffered(1)),
                  pl.BlockSpec((tm, 128), pos),
                  pl.BlockSpec((tm, 128), pos),
                  pl.BlockSpec((tm, 128), pos)],
        out_specs=[pl.BlockSpec((tm, wd), row) for wd in widths],
        out_shape=[jax.ShapeDtypeStruct((n, wd), dt) for wd, dt in zip(widths, dtypes)],
        compiler_params=_cparams(("parallel",)),
        name="in_proj",
    )(x, sc, sh, g, w, rc, rs1, rs2)


def _ssd_kernel(ua_ref, ug_ref, cw_ref, cb_ref, alog_ref, dtb_ref, dsk_ref, ng_ref, o_ref,
                xp_ref, xa_ref, cum_ref, dtd_ref, tr_ref, y_ref, y2_ref, sf_ref, sb_ref,
                *, seq_len):
    nc = seq_len // CHUNK
    q = CHUNK
    xp_ref[0:8, :] = jnp.zeros((8, SSD_XBC), F32)
    xp_ref[seq_len + 8:seq_len + 16, :] = jnp.zeros((8, SSD_XBC), F32)
    xp_ref[8:seq_len + 8, :] = ua_ref[:, SSD_INNER:SSD_INNER + SSD_XBC].astype(F32)
    sf_ref[...] = jnp.zeros_like(sf_ref)
    sb_ref[...] = jnp.zeros_like(sb_ref)
    a_row = -jnp.exp(alog_ref[...])
    lane = lax.broadcasted_iota(jnp.int32, (q, 128), 1)

    def prep(c, carry):
        r0 = pl.multiple_of(c * q, q)
        win = xp_ref[pl.ds(r0, q + 16), :]
        conv = cb_ref[...] + cw_ref[0:1, :] * win[6:6 + q]
        for k in range(1, SSD_CONV):
            conv = conv + cw_ref[k:k + 1, :] * win[6 + k:6 + k + q]
        xa_ref[pl.ds(r0, q), :] = conv * _sigmoid(conv)
        dt = _softplus(ug_ref[pl.ds(r0, q), :] + dtb_ref[...])
        dta = dt * a_row
        p = _prefix_rows(dta)
        s = p[q - 1:q, :] - p + dta
        cum = jnp.where(lane < SSD_HEADS, p, s)
        cum_ref[pl.ds(r0, q), :] = cum
        dtd_ref[pl.ds(r0, q), :] = dt
        packed = jnp.where(lane < 16, cum, pltpu.roll(dt, 16, 1))
        tr_ref[c] = packed.T
        return carry

    lax.fori_loop(0, nc, prep, 0)

    ri = lax.broadcasted_iota(jnp.int32, (q, q), 0)
    ci = lax.broadcasted_iota(jnp.int32, (q, q), 1)
    lower_incl = ci <= ri
    lower = ci < ri
    upper = ci > ri

    def main(c, carry):
        r0 = pl.multiple_of(c * q, q)
        cbk = nc - 1 - c
        r1 = pl.multiple_of(cbk * q, q)
        tr = tr_ref[c]
        visits = []
        for rr in (r0, r1):
            xa = xa_ref[pl.ds(rr, q), :]
            bt = xa[:, SSD_INNER:SSD_INNER + 128].T.astype(BF16)
            visits.append(dict(
                xa=xa, cum=cum_ref[pl.ds(rr, q), :], dtd=dtd_ref[pl.ds(rr, q), :],
                bgt=[bt[g * 64:(g + 1) * 64, :] for g in range(2)],
                bg=[xa[:, SSD_INNER + g * 64:SSD_INNER + (g + 1) * 64].astype(BF16) for g in range(2)],
                cg=[xa[:, SSD_INNER + 128 + g * 64:SSD_INNER + 128 + (g + 1) * 64].astype(BF16)
                    for g in range(2)]))
        fw, bw = visits
        heads = range(SSD_HEADS)
        cbs = [_dot_nt(fw['cg'][g], fw['bg'][g]) for g in range(2)]
        ws = []
        for h in heads:
            cum = fw['cum']
            cf_col, rb_col = cum[:, h:h + 1], cum[:, 6 + h:7 + h]
            cf_row, rb_row = tr[h:h + 1, :], tr[6 + h:7 + h, :]
            dtf_row, dtb_row = tr[16 + h:17 + h, :], tr[22 + h:23 + h, :]
            arg = jnp.where(lower_incl, cf_col - cf_row, rb_col - rb_row)
            fac = jnp.where(lower, dtf_row, jnp.where(upper, dtb_row, dtf_row + dtb_row))
            ws.append((cbs[h // 3] * jnp.exp(arg) * fac).astype(BF16))
        xf = [fw['xa'][:, h * 64:(h + 1) * 64] for h in heads]
        xb = [bw['xa'][:, h * 64:(h + 1) * 64] for h in heads]
        sfs = [sf_ref[h] for h in heads]
        sbs = [sb_ref[h] for h in heads]
        y_intra = [_dot(ws[h], xf[h].astype(BF16)) for h in heads]
        y_carry_f = [_dot(fw['cg'][h // 3], sfs[h].astype(BF16)) for h in heads]
        y_carry_b = [_dot(bw['cg'][h // 3], sbs[h].astype(BF16)) for h in heads]
        y_ref[pl.ds(r0, q), :] = jnp.concatenate(
            [y_intra[h] + y_carry_f[h] * jnp.exp(fw['cum'][:, h:h + 1]) for h in heads], axis=1)
        y2_ref[pl.ds(r1, q), :] = jnp.concatenate(
            [y_carry_b[h] * jnp.exp(bw['cum'][:, 6 + h:7 + h]) for h in heads], axis=1)
        xws, decays = [], []
        for h in heads:
            tf = fw['cum'][q - 1:q, h:h + 1]
            wcol = jnp.exp(tf - fw['cum'][:, h:h + 1]) * fw['dtd'][:, h:h + 1]
            xws.append((xf[h] * wcol).astype(BF16))
            decays.append(jnp.exp(tf))
        for h in heads:
            tb = bw['cum'][0:1, 6 + h:7 + h]
            wcol = jnp.exp(tb - bw['cum'][:, 6 + h:7 + h]) * bw['dtd'][:, 6 + h:7 + h]
            xws.append((xb[h] * wcol).astype(BF16))
            decays.append(jnp.exp(tb))
        for h in heads:
            sf_ref[h] = sfs[h] * decays[h] + _dot(fw['bgt'][h // 3], xws[h])
        for h in heads:
            sb_ref[h] = sbs[h] * decays[6 + h] + _dot(bw['bgt'][h // 3], xws[6 + h])
        return carry

    lax.fori_loop(0, nc, main, 0)

    def fin(c, carry):
        r0 = pl.multiple_of(c * q, q)
        xs = xa_ref[pl.ds(r0, q), 0:SSD_INNER]
        y = y_ref[pl.ds(r0, q), :] + y2_ref[pl.ds(r0, q), :] + xs * dsk_ref[...]
        z = ua_ref[pl.ds(r0, q), 0:SSD_INNER].astype(F32)
        v = y * (z * _sigmoid(z))
        ms = jnp.mean(v * v, axis=-1, keepdims=True)
        o_ref[pl.ds(r0, q), :] = (v * lax.rsqrt(ms + NORM_EPS) * ng_ref[...]).astype(o_ref.dtype)
        return carry

    lax.fori_loop(0, nc, fin, 0)


def _ssd_call(ua, ug, cw, cb, alog, dtb, dsk, ng, seq_len):
    n = ua.shape[0]
    nseq = n // seq_len
    nc = seq_len // CHUNK
    seq = lambda b: (b, 0)
    const = lambda b: (0, 0)
    return pl.pallas_call(
        functools.partial(_ssd_kernel, seq_len=seq_len),
        grid=(nseq,),
        in_specs=[pl.BlockSpec((seq_len, 1024), seq),
                  pl.BlockSpec((seq_len, 128), seq),
                  pl.BlockSpec((8, SSD_XBC), const),
                  pl.BlockSpec((1, SSD_XBC), const),
                  pl.BlockSpec((1, 128), const),
                  pl.BlockSpec((1, 128), const),
                  pl.BlockSpec((1, SSD_INNER), const),
                  pl.BlockSpec((1, SSD_INNER), const)],
        out_specs=pl.BlockSpec((seq_len, SSD_INNER), seq),
        out_shape=jax.ShapeDtypeStruct((n, SSD_INNER), BF16),
        scratch_shapes=[pltpu.VMEM((seq_len + 16, SSD_XBC), F32),
                        pltpu.VMEM((seq_len, SSD_XBC), F32),
                        pltpu.VMEM((seq_len, 128), F32),
                        pltpu.VMEM((seq_len, 128), F32),
                        pltpu.VMEM((nc, 128, CHUNK), F32),
                        pltpu.VMEM((seq_len, SSD_INNER), F32),
                        pltpu.VMEM((seq_len, SSD_INNER), F32),
                        pltpu.VMEM((SSD_HEADS, SSD_STATE, HEAD_DIM), F32),
                        pltpu.VMEM((SSD_HEADS, SSD_STATE, HEAD_DIM), F32)],
        compiler_params=_cparams(("parallel",)),
        name="ssd",
    )(ua, ug, cw, cb, alog, dtb, dsk, ng)


def _swa_kernel(sink_ref, u_ref, o_ref, kp_ref, vp_ref, bias_ref, *, seq_len):
    q = CHUNK
    nb = seq_len // q
    zeros = jnp.zeros((q, 128), BF16)
    kp_ref[0:q, :] = zeros
    vp_ref[0:q, :] = zeros
    kp_ref[seq_len + q:seq_len + 2 * q, :] = zeros
    vp_ref[seq_len + q:seq_len + 2 * q, :] = zeros
    kp_ref[q:seq_len + q, :] = u_ref[:, 384:512]
    vp_ref[q:seq_len + q, :] = u_ref[:, 512:640]
    row = lax.broadcasted_iota(jnp.int32, (3 * q, 3 * q), 0) & (q - 1)
    col = lax.broadcasted_iota(jnp.int32, (3 * q, 3 * q), 1)
    bias_ref[...] = jnp.where(jnp.abs(col - q - row) <= SWA_SIDE, 0.0, NEG)
    rgrp = lax.broadcasted_iota(jnp.int32, (3 * q, 1), 0) // q
    col1 = lax.broadcasted_iota(jnp.int32, (1, 3 * q), 1)
    lane1 = lax.broadcasted_iota(jnp.int32, (1, 128), 1)
    lo_f = jnp.where(lane1 < HEAD_DIM, 1.0, 0.0)
    hi_f = 1.0 - lo_f
    halves = ((lo_f.astype(BF16), lo_f), (hi_f.astype(BF16), hi_f))

    def body(qb, carry):
        q0 = pl.multiple_of(qb * q, q)
        qs = jnp.concatenate([u_ref[pl.ds(q0, q), j * 128:(j + 1) * 128] for j in range(3)], axis=0)
        kw = kp_ref[pl.ds(q0, 3 * q), :]
        vw = vp_ref[pl.ds(q0, 3 * q), :]
        inside = (col1 >= q - q0) & (col1 < seq_len + q - q0)
        bias = bias_ref[...] + jnp.where(inside, 0.0, NEG)
        o = None
        for hk, (sel_b, sel_f) in enumerate(halves):
            s = _dot_nt(qs * sel_b, kw) + bias
            sk = jnp.where(rgrp == 0, sink_ref[3 * hk],
                           jnp.where(rgrp == 1, sink_ref[3 * hk + 1], sink_ref[3 * hk + 2]))
            m = jnp.maximum(jnp.max(s, axis=-1, keepdims=True), sk)
            p = jnp.exp(s - m)
            l = jnp.sum(p, axis=-1, keepdims=True) + jnp.exp(sk - m)
            oh = (_dot(p.astype(BF16), vw) / l) * sel_f
            o = oh if o is None else o + oh
        o_ref[pl.ds(q0, q), :] = jnp.concatenate(
            [o[j * q:(j + 1) * q, :] for j in range(3)], axis=1).astype(o_ref.dtype)
        return carry

    lax.fori_loop(0, nb, body, 0, unroll=2)


def _swa_call(ub, sink, seq_len):
    n = ub.shape[0]
    nseq = n // seq_len
    return pl.pallas_call(
        functools.partial(_swa_kernel, seq_len=seq_len),
        grid=(nseq,),
        in_specs=[pl.BlockSpec(memory_space=pltpu.SMEM),
                  pl.BlockSpec((seq_len, 640), lambda b: (b, 0))],
        out_specs=pl.BlockSpec((seq_len, 384), lambda b: (b, 0)),
        out_shape=jax.ShapeDtypeStruct((n, 384), BF16),
        scratch_shapes=[pltpu.VMEM((seq_len + 2 * CHUNK, 128), BF16),
                        pltpu.VMEM((seq_len + 2 * CHUNK, 128), BF16),
                        pltpu.VMEM((3 * CHUNK, 3 * CHUNK), F32)],
        compiler_params=_cparams(("parallel",)),
        name="swa",
    )(sink, ub)


def _mlstm_kernel(uc_ref, ug_ref, gb_ref, ng_ref, o_ref,
                  gcol_ref, gtr_ref, hacc_ref, ct_ref, m_ref, mask_ref, *, seq_len):
    q = CHUNK
    nc = seq_len // q
    lane = lax.broadcasted_iota(jnp.int32, (q, 128), 1)
    ct_ref[...] = jnp.zeros_like(ct_ref)
    m_ref[...] = jnp.zeros_like(m_ref)
    hacc_ref[...] = jnp.zeros_like(hacc_ref)

    def prep(c, carry):
        r0 = pl.multiple_of(c * q, q)
        raw = ug_ref[pl.ds(r0, q), :] + gb_ref[...]
        lf = jnp.minimum(raw, 0.0) - jnp.log(1.0 + jnp.exp(-jnp.abs(raw)))
        p = _prefix_rows(lf)
        s = p[q - 1:q, :] - p + lf
        g = jnp.where(lane < LANE_FG, raw, jnp.where(lane < LANE_FG + ML_HEADS, p, s))
        gcol_ref[pl.ds(r0, q), :] = g
        gtr_ref[c] = g.T
        return carry

    lax.fori_loop(0, nc, prep, 0)

    ri = lax.broadcasted_iota(jnp.int32, (q, q), 0)
    ci = lax.broadcasted_iota(jnp.int32, (q, q), 1)
    mask_ref[0] = jnp.where(ci <= ri, 0.0, NEG)
    mask_ref[1] = jnp.where(ci >= ri, 0.0, NEG)
    keep_lanes = jnp.where(lax.broadcasted_iota(jnp.int32, (1, ML_PAD), 1) < ML_DIM, 1.0, 0.0)

    def main(c, carry):
        units = []
        for dirn in range(2):
            ck = c if dirn == 0 else nc - 1 - c
            r0 = pl.multiple_of(ck * q, q)
            g = gcol_ref[pl.ds(r0, q), :]
            gt = gtr_ref[ck]
            for h in range(ML_HEADS):
                s_idx = dirn * ML_HEADS + h
                li, lb = LANE_IG + s_idx, LANE_FG + s_idx
                edge = q - 1 if dirn == 0 else 0
                units.append(dict(
                    dirn=dirn, h=h, r0=r0, s_idx=s_idx,
                    b_col=g[:, lb:lb + 1], b_row=gt[lb:lb + 1, :], i_col=g[:, li:li + 1],
                    i_row=gt[li:li + 1, :], tot=g[edge:edge + 1, lb:lb + 1],
                    m_st=m_ref[s_idx][0:1, 0:1], ct=ct_ref[s_idx],
                    qh=uc_ref[pl.ds(r0, q), h * ML_PAD:(h + 1) * ML_PAD],
                    kh=uc_ref[pl.ds(r0, q), 512 + h * ML_PAD:512 + (h + 1) * ML_PAD],
                    vh=uc_ref[pl.ds(r0, q), 1024 + h * ML_PAD:1024 + (h + 1) * ML_PAD]))
        for u in units:
            u['logd'] = u['b_col'] - u['b_row'] + u['i_row'] + mask_ref[u['dirn']]
            u['m_inter'] = u['b_col'] + u['m_st']
        for u in units:
            u['m_t'] = jnp.maximum(jnp.max(u['logd'], axis=-1, keepdims=True), u['m_inter'])
        for u in units:
            u['qk'] = _dot_nt(u['qh'], u['kh'])
            u['qc'] = _dot(u['qh'], u['ct'].astype(BF16))
        for u in units:
            u['sm'] = (u['qk'] * jnp.exp(u['logd'] - u['m_t'])).astype(BF16)
            u['inter'] = jnp.exp(u['m_inter'] - u['m_t'])
        for u in units:
            u['num'] = _dot(u['sm'], u['vh']) + u['inter'] * u['qc']
        for u in units:
            den = u['num'][:, ML_DIM:ML_DIM + 1]
            hh = u['num'] / jnp.maximum(jnp.abs(den), jnp.exp(-u['m_t'])) * keep_lanes
            hacc_ref[pl.ds(u['r0'], q), u['h'] * ML_PAD:(u['h'] + 1) * ML_PAD] += hh
        for u in units:
            tot, m_st = u['tot'], u['m_st']
            m_new = jnp.maximum(tot + m_st,
                                jnp.max(tot - u['b_row'] + u['i_row'], axis=-1, keepdims=True))
            wk_col = jnp.exp(tot - u['b_col'] + u['i_col'] - m_new)
            u['keep'] = jnp.exp(tot + m_st - m_new)
            u['vw'] = (u['vh'].astype(F32) * wk_col).astype(BF16)
            m_ref[u['s_idx']] = jnp.broadcast_to(m_new, (8, 128))
        for u in units:
            ct_ref[u['s_idx']] = u['keep'] * u['ct'] + _dot_tn(u['kh'], u['vw'])
        return carry

    lax.fori_loop(0, nc, main, 0)

    def fin(c, carry):
        r0 = pl.multiple_of(c * q, q)
        outs = []
        for h in range(ML_HEADS):
            hs = hacc_ref[pl.ds(r0, q), h * ML_PAD:(h + 1) * ML_PAD]
            ms = jnp.sum(hs * hs, axis=-1, keepdims=True) * (1.0 / ML_DIM)
            hn = hs * lax.rsqrt(ms + NORM_EPS) * ng_ref[:, h * ML_PAD:(h + 1) * ML_PAD]
            og = uc_ref[pl.ds(r0, q), 1536 + h * ML_PAD:1536 + (h + 1) * ML_PAD].astype(F32)
            outs.append(hn * _sigmoid(og))
        o_ref[pl.ds(r0, q), :] = jnp.concatenate(outs, axis=1).astype(o_ref.dtype)
        return carry

    lax.fori_loop(0, nc, fin, 0)


def _mlstm_call(uc, ug, gb, ng, seq_len):
    n = uc.shape[0]
    nseq = n // seq_len
    nc = seq_len // CHUNK
    seq = lambda b: (b, 0)
    const = lambda b: (0, 0)
    width = ML_HEADS * ML_PAD
    return pl.pallas_call(
        functools.partial(_mlstm_kernel, seq_len=seq_len),
        grid=(nseq,),
        in_specs=[pl.BlockSpec((seq_len, 4 * width), seq),
                  pl.BlockSpec((seq_len, 128), seq),
                  pl.BlockSpec((1, 128), const),
                  pl.BlockSpec((1, width), const)],
        out_specs=pl.BlockSpec((seq_len, width), seq),
        out_shape=jax.ShapeDtypeStruct((n, width), BF16),
        scratch_shapes=[pltpu.VMEM((seq_len, 128), F32),
                        pltpu.VMEM((nc, 128, CHUNK), F32),
                        pltpu.VMEM((seq_len, width), F32),
                        pltpu.VMEM((2 * ML_HEADS, ML_PAD, ML_PAD), F32),
                        pltpu.VMEM((2 * ML_HEADS, 8, 128), F32),
                        pltpu.VMEM((2, CHUNK, CHUNK), F32)],
        compiler_params=_cparams(("parallel",)),
        name="mlstm",
    )(uc, ug, gb, ng)


def _attend_units(units, lo_b, hi_b, lo_f, hi_f):
    n = units[0][0].shape[0]
    ss = []
    for q, kw, _, bias in units:
        s2 = _dot_nt(jnp.concatenate([q * lo_b, q * hi_b], axis=0), kw)
        ss.append(jnp.concatenate([s2[0:n] + bias, s2[n:2 * n] + bias], axis=0))
    ms = [jnp.max(s, axis=-1, keepdims=True) for s in ss]
    ps = [jnp.exp(s - m) for s, m in zip(ss, ms)]
    ls = [jnp.sum(p, axis=-1, keepdims=True) for p in ps]
    os_ = [_dot(p.astype(BF16), u[2]) / l for p, u, l in zip(ps, units, ls)]
    res = []
    for o2, m, l in zip(os_, ms, ls):
        lse2 = m + jnp.log(l)
        res.append((o2[0:n] * lo_f + o2[n:2 * n] * hi_f, lse2[0:n] * lo_f + lse2[n:2 * n] * hi_f))
    return res


def _dil_kernel(q0_ref, q1_ref, q2_ref, k_ref, v_ref, o_ref,
                qf_ref, kf_ref, vf_ref, kc0_ref, vc0_ref, kc1_ref, vc1_ref, og_ref, lg_ref,
                qc2_ref, kc2_ref, vc2_ref, og2_ref, lg2_ref, band_ref, band2_ref, *, seq_len):
    qr = CHUNK
    pad = DIL_SIDE
    d1, d2 = DIL_CONFIGS[1][1], DIL_CONFIGS[2][1]
    n1, n2 = seq_len // d1, seq_len // d2
    assert n2 == qr and n1 % qr == 0
    qf_ref[0] = q1_ref[...].astype(F32)
    qf_ref[1] = q2_ref[...].astype(F32)
    kf_ref[...] = k_ref[...].astype(F32)
    vf_ref[...] = v_ref[...].astype(F32)
    zeros = jnp.zeros((pad, 128), BF16)
    kc0_ref[0:pad, :] = zeros
    vc0_ref[0:pad, :] = zeros
    kc0_ref[pad + seq_len:2 * pad + seq_len, :] = zeros
    vc0_ref[pad + seq_len:2 * pad + seq_len, :] = zeros
    kc0_ref[pad:pad + seq_len, :] = k_ref[...]
    vc0_ref[pad:pad + seq_len, :] = v_ref[...]
    for r in range(d1):
        kc1_ref[r, 0:pad, :] = zeros
        vc1_ref[r, 0:pad, :] = zeros
        kc1_ref[r, pad + n1:2 * pad + n1, :] = zeros
        vc1_ref[r, pad + n1:2 * pad + n1, :] = zeros
        kc1_ref[r, pad:pad + n1, :] = kf_ref[pl.ds(r, n1, stride=d1), :].astype(BF16)
        vc1_ref[r, pad:pad + n1, :] = vf_ref[pl.ds(r, n1, stride=d1), :].astype(BF16)

    row = lax.broadcasted_iota(jnp.int32, (qr, 2 * qr), 0)
    col = lax.broadcasted_iota(jnp.int32, (qr, 2 * qr), 1)
    band_ref[...] = jnp.where(jnp.abs(col - pad - row) <= DIL_SIDE, 0.0, NEG)
    row2 = lax.broadcasted_iota(jnp.int32, (qr, qr), 0)
    col2 = lax.broadcasted_iota(jnp.int32, (qr, qr), 1)
    band2_ref[...] = jnp.where(jnp.abs(col2 - row2) <= DIL_SIDE, 0.0, NEG)
    lane1 = lax.broadcasted_iota(jnp.int32, (1, 128), 1)
    col1 = lax.broadcasted_iota(jnp.int32, (1, 2 * qr), 1)
    lo_f = jnp.where(lane1 < HEAD_DIM, 1.0, 0.0)
    hi_f = 1.0 - lo_f
    sel = (lo_f.astype(BF16), hi_f.astype(BF16), lo_f, hi_f)

    def window_bias(u0, n):
        inside = (col1 >= pad - u0) & (col1 < n + pad - u0)
        return band_ref[...] + jnp.where(inside, 0.0, NEG)

    def batch0(b4, carry):
        units, starts = [], []
        for i in range(DIL_BATCH):
            u0 = pl.multiple_of((b4 * DIL_BATCH + i) * qr, qr)
            starts.append(u0)
            units.append((q0_ref[pl.ds(u0, qr), :], kc0_ref[pl.ds(u0, 2 * qr), :],
                          vc0_ref[pl.ds(u0, 2 * qr), :], window_bias(u0, seq_len)))
        for u0, (o, lse) in zip(starts, _attend_units(units, *sel)):
            og_ref[0, pl.ds(u0, qr), :] = o
            lg_ref[0, pl.ds(u0, qr), :] = lse
        return carry

    lax.fori_loop(0, seq_len // qr // DIL_BATCH, batch0, 0)

    nb1 = n1 // qr
    assert nb1 == DIL_BATCH

    def batch1(r, carry):
        units, rowsl = [], []
        for i in range(DIL_BATCH):
            u0 = i * qr
            rows = pl.ds(r + u0 * d1, qr, stride=d1)
            rowsl.append(rows)
            units.append((qf_ref[0, rows, :].astype(BF16), kc1_ref[r, u0:u0 + 2 * qr, :],
                          vc1_ref[r, u0:u0 + 2 * qr, :], window_bias(u0, n1)))
        for rows, (o, lse) in zip(rowsl, _attend_units(units, *sel)):
            og_ref[1, rows, :] = o
            lg_ref[1, rows, :] = lse
        return carry

    lax.fori_loop(0, d1, batch1, 0)

    for r in range(d2):
        rows = pl.ds(r, qr, stride=d2)
        qc2_ref[r] = qf_ref[1, rows, :].astype(BF16)
        kc2_ref[r] = kf_ref[rows, :].astype(BF16)
        vc2_ref[r] = vf_ref[rows, :].astype(BF16)

    def batch2(b4, carry):
        rs = [b4 * DIL_BATCH + i for i in range(DIL_BATCH)]
        units = [(qc2_ref[r], kc2_ref[r], vc2_ref[r], band2_ref[...]) for r in rs]
        for r, (o, lse) in zip(rs, _attend_units(units, *sel)):
            og2_ref[r] = o
            lg2_ref[r] = lse
        return carry

    lax.fori_loop(0, d2 // DIL_BATCH, batch2, 0)
    for r in range(d2):
        rows = pl.ds(r, qr, stride=d2)
        og_ref[2, rows, :] = og2_ref[r]
        lg_ref[2, rows, :] = lg2_ref[r]

    def fin(c, carry):
        r0 = pl.multiple_of(c * qr, qr)
        rows = pl.ds(r0, qr)
        l0, l1, l2 = lg_ref[0, rows, :], lg_ref[1, rows, :], lg_ref[2, rows, :]
        m = jnp.maximum(jnp.maximum(l0, l1), l2)
        w0, w1, w2 = jnp.exp(l0 - m), jnp.exp(l1 - m), jnp.exp(l2 - m)
        y = (og_ref[0, rows, :] * w0 + og_ref[1, rows, :] * w1 + og_ref[2, rows, :] * w2)
        o_ref[rows, :] = (y / (w0 + w1 + w2)).astype(o_ref.dtype)
        return carry

    lax.fori_loop(0, seq_len // qr, fin, 0, unroll=2)


def _dil_call(ud, seq_len):
    n = ud.shape[0]
    nseq = n // seq_len
    blk = (seq_len, 128)
    d1, d2 = DIL_CONFIGS[1][1], DIL_CONFIGS[2][1]
    n1 = seq_len // d1

    def col(off):
        return lambda b, p: (b, off + p)

    return pl.pallas_call(
        functools.partial(_dil_kernel, seq_len=seq_len),
        grid=(nseq, 3),
        in_specs=[pl.BlockSpec(blk, col(0)), pl.BlockSpec(blk, col(3)), pl.BlockSpec(blk, col(6)),
                  pl.BlockSpec(blk, col(9)), pl.BlockSpec(blk, col(12))],
        out_specs=pl.BlockSpec(blk, col(0)),
        out_shape=jax.ShapeDtypeStruct((n, 384), BF16),
        scratch_shapes=[pltpu.VMEM((2, seq_len, 128), F32),
                        pltpu.VMEM((seq_len, 128), F32),
                        pltpu.VMEM((seq_len, 128), F32),
                        pltpu.VMEM((seq_len + 2 * DIL_SIDE, 128), BF16),
                        pltpu.VMEM((seq_len + 2 * DIL_SIDE, 128), BF16),
                        pltpu.VMEM((d1, n1 + 2 * DIL_SIDE, 128), BF16),
                        pltpu.VMEM((d1, n1 + 2 * DIL_SIDE, 128), BF16),
                        pltpu.VMEM((3, seq_len, 128), F32),
                        pltpu.VMEM((3, seq_len, 128), F32)]
                       + [pltpu.VMEM((d2, CHUNK, 128), BF16)] * 3
                       + [pltpu.VMEM((d2, CHUNK, 128), F32)] * 2
                       + [pltpu.VMEM((CHUNK, 2 * CHUNK), F32), pltpu.VMEM((CHUNK, CHUNK), F32)],
        compiler_params=_cparams(("parallel", "parallel")),
        name="dil",
    )(ud, ud, ud, ud, ud)


def _out_proj_kernel(ya_ref, yb_ref, yc_ref, yd_ref, x_ref, g1_ref, sc_ref, sh_ref,
                     pg_ref, fg_ref, wa_ref, wb_ref, wc_ref, wd_ref, wrh_ref, wrl_ref, br_ref,
                     x1_ref, h2_ref, rt_ref):
    for r0 in range(0, x_ref.shape[0], OUT_SUB):
        rs = slice(r0, r0 + OUT_SUB)
        y = (_dot(ya_ref[rs, :], wa_ref[...]) + _dot(yb_ref[rs, :], wb_ref[...])
             + _dot(yc_ref[rs, :], wc_ref[...]) + _dot(yd_ref[rs, :], wd_ref[...]))
        ms = jnp.mean(y * y, axis=-1, keepdims=True)
        yn = y * lax.rsqrt(ms + NORM_EPS) * pg_ref[...]
        x1 = x_ref[rs, :] + g1_ref[0] * yn
        x1_ref[rs, :] = x1
        ms2 = jnp.mean(x1 * x1, axis=-1, keepdims=True)
        h2 = x1 * lax.rsqrt(ms2 + NORM_EPS) * fg_ref[...]
        h2 = h2 * (1.0 + sc_ref[0]) + sh_ref[0]
        half = h2.shape[1] // 2
        lo = lax.bitcast_convert_type(h2[:, :half].astype(BF16).astype(F32), jnp.uint32)
        hi = lax.bitcast_convert_type(h2[:, half:].astype(BF16).astype(F32), jnp.uint32)
        h2_ref[rs, :] = (lo >> 16) | hi
        h_hi = h2.astype(BF16)
        h_lo = (h2 - h_hi.astype(F32)).astype(BF16)
        logits = (_dot(h_hi, wrh_ref[...]) + _dot(h_lo, wrh_ref[...]) + _dot(h_hi, wrl_ref[...])
                  + br_ref[...])
        lane = lax.broadcasted_iota(jnp.int32, logits.shape, 1)
        vals, idxs = [], []
        for _ in range(TOP_K):
            m = jnp.max(logits, axis=-1, keepdims=True)
            idx = jnp.min(jnp.where(logits == m, lane, 128), axis=-1, keepdims=True)
            vals.append(m)
            idxs.append(idx)
            logits = jnp.where(lane == idx, -3e38, logits)
        es = [jnp.exp(v - vals[0]) for v in vals]
        tot = es[0] + es[1] + es[2] + es[3]
        rt = jnp.zeros(lane.shape, F32)
        for k in range(TOP_K):
            rt = jnp.where(lane == k, idxs[k].astype(F32), rt)
            rt = jnp.where(lane == TOP_K + k, es[k] / tot, rt)
        rt_ref[rs, :] = rt[:, 0:2 * TOP_K]


def _out_proj_call(ya, yb, yc, yd, x, g1, sc, sh, pg, fg, wa, wb, wc, wd, wrh, wrl, br, seq_len, tm=512):
    n, d = x.shape
    per_seq = seq_len // tm
    row = lambda i: (i, 0)
    seq = lambda i: (i // per_seq, 0, 0)
    const = lambda i: (0, 0)

    def full(a):
        return pl.BlockSpec(a.shape, const)

    return pl.pallas_call(
        _out_proj_kernel,
        grid=(n // tm,),
        in_specs=[pl.BlockSpec((tm, ya.shape[1]), row), pl.BlockSpec((tm, yb.shape[1]), row),
                  pl.BlockSpec((tm, yc.shape[1]), row), pl.BlockSpec((tm, yd.shape[1]), row),
                  pl.BlockSpec((tm, d), row),
                  pl.BlockSpec((1, 1, d), seq), pl.BlockSpec((1, 1, d), seq),
                  pl.BlockSpec((1, 1, d), seq),
                  full(pg), full(fg), full(wa), full(wb), full(wc), full(wd), full(wrh), full(wrl), full(br)],
        out_specs=[pl.BlockSpec((tm, d), row), pl.BlockSpec((tm, d // 2), row),
                   pl.BlockSpec((tm, 2 * TOP_K), row)],
        out_shape=[jax.ShapeDtypeStruct((n, d), F32), jax.ShapeDtypeStruct((n, d // 2), jnp.uint32),
                   jax.ShapeDtypeStruct((n, 2 * TOP_K), F32)],
        compiler_params=_cparams(("parallel",)),
        name="out_proj",
    )(ya, yb, yc, yd, x, g1, sc, sh, pg, fg, wa, wb, wc, wd, wrh, wrl, br)


def _moe_slots(tile):
    return tile * TOP_K // MOE_ROWS + N_EXPERTS + 2


def _moe_kernel(nb_ref, bf_ref, bs_ref, bv_ref, idx_ref, gate_ref, h_ref,
                wgu_ref, bgu_ref, wdn_ref, bdn_ref, o_ref, xs0_ref, xs1_ref, ys0_ref, ys1_ref):
    t = pl.program_id(0)
    e = pl.program_id(1)
    nb = nb_ref[t]
    rows = MOE_ROWS
    nslot = _moe_slots(h_ref.shape[0])
    half = h_ref.shape[1]

    def gather(dst_ref, blk):
        base = bs_ref[t * nslot + blk]
        for r in range(rows):
            i = idx_ref[0, 0, base + r]
            dst_ref[r:r + 1, :] = h_ref[pl.ds(i, 1), :]

    def scatter(src_ref, blk, nvalid):
        base = bs_ref[t * nslot + blk]
        i0 = idx_ref[0, 0, base]
        for r0 in range(0, rows, MOE_GROUP):
            ks = range(MOE_GROUP)
            ok = [r0 + k < nvalid for k in ks]
            ii = [jnp.where(ok[k], idx_ref[0, 0, base + r0 + k], i0) for k in ks]
            gg = [jnp.where(ok[k], gate_ref[0, 0, base + r0 + k], 0.0) for k in ks]
            cur = [o_ref[pl.ds(ii[k], 1), :] for k in ks]
            for k in reversed(ks):
                o_ref[pl.ds(ii[k], 1), :] = cur[k] + gg[k] * src_ref[r0 + k:r0 + k + 1, :]

    def expert(xs_ref, ys_ref):
        xu = xs_ref[...]
        x_lo = lax.bitcast_convert_type(xu << 16, F32).astype(BF16)
        x_hi = lax.bitcast_convert_type(xu & jnp.uint32(0xFFFF0000), F32).astype(BF16)
        gu = _dot(x_lo, wgu_ref[0, 0:half, :]) + _dot(x_hi, wgu_ref[0, half:2 * half, :]) + bgu_ref[0]
        gate = jnp.minimum(gu[:, :D_EXPERT], SWIGLU_LIMIT)
        up = jnp.clip(gu[:, D_EXPERT:], -SWIGLU_LIMIT, SWIGLU_LIMIT)
        act = (up + 1.0) * gate * _sigmoid(SWIGLU_ALPHA * gate)
        ys_ref[...] = _dot(act.astype(BF16), wdn_ref[0]) + bdn_ref[0]

    @pl.when(e == 0)
    def _():
        o_ref[...] = jnp.zeros_like(o_ref)
        ys1_ref[...] = jnp.zeros_like(ys1_ref)
        gather(xs0_ref, 0)

    bufs = ((xs0_ref, xs1_ref, ys0_ref, ys1_ref), (xs1_ref, xs0_ref, ys1_ref, ys0_ref))

    def block(b, carry):
        prev = jnp.maximum(b - 1, 0)
        nv_prev = jnp.where(b > 0, bv_ref[t * nslot + prev], 0)
        for par, (xs_cur, xs_nxt, ys_cur, ys_prv) in enumerate(bufs):
            @pl.when(b % 2 == par)
            def _(xs_cur=xs_cur, xs_nxt=xs_nxt, ys_cur=ys_cur, ys_prv=ys_prv):
                gather(xs_nxt, b + 1)
                expert(xs_cur, ys_cur)
                scatter(ys_prv, prev, nv_prev)
        return carry

    lax.fori_loop(bf_ref[t * (N_EXPERTS + 1) + e], bf_ref[t * (N_EXPERTS + 1) + e + 1], block, 0)

    for par, (_, _, _, ys_prv) in enumerate(bufs):
        @pl.when((e == N_EXPERTS - 1) & (nb % 2 == par))
        def _(ys_prv=ys_prv):
            scatter(ys_prv, nb - 1, bv_ref[t * nslot + nb - 1])


def _moe_call(nb, bfirst, bs, bv, idx, gates, h2p, wgu, bgu, wdn, bdn, tile):
    n, half = h2p.shape
    d = 2 * half
    ntile = n // tile
    plen = idx.shape[-1]
    wmap = lambda t, e, *_: (e, 0, 0)
    tmap3 = lambda t, e, *_: (t, 0, 0)
    tmap2 = lambda t, e, *_: (t, 0)
    grid_spec = pltpu.PrefetchScalarGridSpec(
        num_scalar_prefetch=4,
        grid=(ntile, N_EXPERTS),
        in_specs=[pl.BlockSpec((1, 1, plen), tmap3, memory_space=pltpu.SMEM),
                  pl.BlockSpec((1, 1, plen), tmap3, memory_space=pltpu.SMEM),
                  pl.BlockSpec((tile, half), tmap2, pipeline_mode=pl.Buffered(1)),
                  pl.BlockSpec((1, d, 2 * D_EXPERT), wmap), pl.BlockSpec((1, 1, 2 * D_EXPERT), wmap),
                  pl.BlockSpec((1, D_EXPERT, d), wmap), pl.BlockSpec((1, 1, d), wmap)],
        out_specs=pl.BlockSpec((tile, d), tmap2, pipeline_mode=pl.Buffered(1)),
        scratch_shapes=[pltpu.VMEM((MOE_ROWS, half), jnp.uint32)] * 2 + [pltpu.VMEM((MOE_ROWS, d), F32)] * 2,
    )
    return pl.pallas_call(
        _moe_kernel,
        grid_spec=grid_spec,
        out_shape=jax.ShapeDtypeStruct((n, d), F32),
        compiler_params=_cparams(("parallel", "arbitrary")),
        name="moe",
    )(nb, bfirst, bs, bv, idx, gates, h2p, wgu, bgu, wdn, bdn)


def _route(rt, tile):
    n = rt.shape[0]
    ntile = n // tile
    npair = tile * TOP_K
    nslot = _moe_slots(tile)
    e = rt[:, :TOP_K].astype(jnp.int32).reshape(ntile, npair)
    g = rt[:, TOP_K:].reshape(ntile, npair)
    tok = jnp.broadcast_to((jnp.arange(npair, dtype=jnp.int32) // TOP_K)[None, :], (ntile, npair))
    _, tok_s, g_s = lax.sort((e, tok, g), dimension=1, is_stable=True, num_keys=1)
    experts = jnp.arange(N_EXPERTS, dtype=jnp.int32)
    counts = jnp.sum(e[:, :, None] == experts, axis=1, dtype=jnp.int32)
    nblocks = (counts + MOE_ROWS - 1) // MOE_ROWS
    bends = jnp.cumsum(nblocks, axis=-1)
    starts = jnp.cumsum(counts, axis=-1) - counts
    nb = bends[:, -1]
    slot = jnp.minimum(jnp.arange(nslot, dtype=jnp.int32)[None, :], nb[:, None] - 1)
    be = jnp.sum(slot[:, :, None] >= bends[:, None, :], axis=-1, dtype=jnp.int32)
    onehot = (be[:, :, None] == experts).astype(jnp.int32)
    pick = lambda v: jnp.sum(onehot * v[:, None, :], axis=-1)
    within = slot - pick(bends - nblocks)
    bs = pick(starts) + within * MOE_ROWS
    bv = jnp.clip(pick(counts) - within * MOE_ROWS, 0, MOE_ROWS)
    padw = ((0, 0), (0, MOE_ROWS))
    idx = jnp.pad(tok_s, padw).reshape(ntile, 1, -1)
    gl = jnp.pad(g_s, padw).reshape(ntile, 1, -1)
    bfirst = jnp.concatenate([jnp.zeros((ntile, 1), jnp.int32), bends], axis=-1)
    flat = lambda v: v.reshape(-1).astype(jnp.int32)
    return flat(nb), flat(bfirst), flat(bs), flat(bv), idx, gl


def _ffn_res_kernel(x_ref, f_ref, g2_ref, pg_ref, o_ref):
    f = f_ref[...]
    ms = jnp.mean(f * f, axis=-1, keepdims=True)
    o_ref[...] = x_ref[...] + g2_ref[0] * (f * lax.rsqrt(ms + NORM_EPS) * pg_ref[...])


def _ffn_res_call(x1, f, g2, pg, seq_len, tm=1024):
    n, d = x1.shape
    per_seq = seq_len // tm
    row = lambda i: (i, 0)
    return pl.pallas_call(
        _ffn_res_kernel,
        grid=(n // tm,),
        in_specs=[pl.BlockSpec((tm, d), row), pl.BlockSpec((tm, d), row),
                  pl.BlockSpec((1, 1, d), lambda i: (i // per_seq, 0, 0)),
                  pl.BlockSpec((1, d), lambda i: (0, 0))],
        out_specs=pl.BlockSpec((tm, d), row),
        out_shape=jax.ShapeDtypeStruct((n, d), F32),
        compiler_params=_cparams(("parallel",)),
        name="ffn_res",
    )(x1, f, g2, pg)


def _rope_tables(seq_len):
    half = 8
    inv = 1.0 / (ROPE_THETA ** (jnp.arange(half, dtype=F32) * (2.0 / 16)))
    ang = jnp.arange(seq_len, dtype=F32)[:, None] * inv[None, :]
    cos, sin = jnp.cos(ang), jnp.sin(ang)
    ones = jnp.ones((seq_len, 48), F32)
    zeros8 = jnp.zeros((seq_len, 8), F32)
    zeros48 = jnp.zeros((seq_len, 48), F32)
    rc = jnp.concatenate([cos, cos, ones], axis=1)
    rs1 = jnp.concatenate([zeros8, sin, zeros48], axis=1)
    rs2 = jnp.concatenate([-sin, zeros8, zeros48], axis=1)
    tile2 = lambda a: jnp.concatenate([a, a], axis=1)
    return tile2(rc), tile2(rs1), tile2(rs2)


def _pad_heads(w):
    lead = w.shape[:-1]
    w = w.reshape(lead + (ML_HEADS, ML_DIM))
    w = jnp.pad(w, [(0, 0)] * len(lead) + [(0, 0), (0, ML_PAD - ML_DIM)])
    return w.reshape(lead + (ML_HEADS * ML_PAD,))


def _layout_w_in(w_in):
    d = w_in.shape[0]
    offs = np.cumsum((0,) + IN_SPLITS)
    parts = [w_in[:, offs[i]:offs[i + 1]] for i in range(len(IN_SPLITS))]
    (z, xbc, dt, sq, sk, sv, mq, mk, mv, mo, mi, mf, dq, dk, dv) = parts
    zc = lambda k: jnp.zeros((d, k), w_in.dtype)
    gates = jnp.concatenate([dt, zc(4), mi, mf, zc(128 - 32)], axis=1)
    sq = sq.reshape(d, SWA_Q_HEADS, HEAD_DIM)[:, SWA_HEAD_ORDER, :].reshape(d, -1)
    w = jnp.concatenate([z, xbc, gates, sq, sk, sv,
                         _pad_heads(mq), _pad_heads(mk), _pad_heads(mv), _pad_heads(mo),
                         dq, dk, dv], axis=1)
    assert w.shape[1] == COL_END
    return w.astype(BF16)


def _lane_row(parts, width=128):
    row = jnp.zeros((width,), F32)
    for off, v in parts:
        row = row.at[off:off + v.shape[0]].set(v.astype(F32))
    return row.reshape(1, width)


def _layer(x, mod_l, p, l, tabs, seq_len):
    d = D_MODEL
    nseq = x.shape[0] // seq_len
    sh1, sc1, g1, sh2, sc2, g2 = [mod_l[:, i * d:(i + 1) * d].reshape(nseq, 1, d) for i in range(6)]
    row = lambda v: v.reshape(1, -1).astype(F32)

    ua, ug, ub, uc, ud = _in_proj_call(x, sc1, sh1, row(p['pre_mix_g'][l]), _layout_w_in(p['w_in'][l]),
                                       *tabs, seq_len)
    cw = jnp.pad(p['ssd_conv_w'][l], ((0, 8 - SSD_CONV), (0, 0)))
    ya = _ssd_call(ua, ug, cw, row(p['ssd_conv_b'][l]),
                   _lane_row([(0, p['ssd_a_log'][l].reshape(-1))]),
                   _lane_row([(0, p['ssd_dt_bias'][l].reshape(-1))]),
                   row(jnp.repeat(p['ssd_d'][l], HEAD_DIM)), row(p['ssd_norm_g'][l]), seq_len)
    yb = _swa_call(ub, p['swa_sink'][l].astype(F32), seq_len)
    gb = _lane_row([(LANE_IG, p['mlstm_i_bias'][l].reshape(-1)),
                    (LANE_FG, p['mlstm_f_bias'][l].reshape(-1))])
    yc = _mlstm_call(uc, ug, gb, row(_pad_heads(p['mlstm_norm_g'][l])), seq_len)
    yd = _dil_call(ud, seq_len)

    w_out = p['w_out'][l]
    wa = w_out[0:384].astype(BF16)
    wb = w_out[384:768].reshape(SWA_Q_HEADS, HEAD_DIM, d)[SWA_HEAD_ORDER, :, :].reshape(384, d).astype(BF16)
    wc = jnp.pad(w_out[768:1152].reshape(ML_HEADS, ML_DIM, d),
                 ((0, 0), (0, ML_PAD - ML_DIM), (0, 0))).reshape(ML_HEADS * ML_PAD, d).astype(BF16)
    wd = w_out[1152:1536].astype(BF16)
    wr = jnp.pad(p['w_router'][l], ((0, 0), (0, 128 - N_EXPERTS))).astype(F32)
    wrh = wr.astype(BF16)
    wrl = (wr - wrh.astype(F32)).astype(BF16)
    br = jnp.concatenate([p['b_router'][l].astype(F32), jnp.full((128 - N_EXPERTS,), NEG, F32)]).reshape(1, 128)
    x1, h2, rt = _out_proj_call(ya, yb, yc, yd, x, g1, sc2, sh2, row(p['post_mix_g'][l]),
                                row(p['pre_ffn_g'][l]), wa, wb, wc, wd, wrh, wrl, br, seq_len)

    tile = min(MOE_TILE, x.shape[0])
    nb, bfirst, bs, bv, idx, gs = _route(rt, tile)
    f = _moe_call(nb, bfirst, bs, bv, idx, gs, h2, p['w_gate_up'][l].astype(BF16),
                  p['b_gate_up'][l].reshape(N_EXPERTS, 1, -1), p['w_down'][l].astype(BF16),
                  p['b_down'][l].reshape(N_EXPERTS, 1, -1), tile)
    return _ffn_res_call(x1, f, g2, row(p['post_ffn_g'][l]), seq_len)


def _trunk(x, c, p):
    nseq, seq_len, d = x.shape
    depth = p['w_in'].shape[0]
    mod = _mod_call(c, p['w_mod'], p['b_mod'])
    tabs = _rope_tables(seq_len)
    x = x.reshape(nseq * seq_len, d)
    for l in range(depth):
        x = _layer(x, mod[l], p, l, tabs, seq_len)
    return x.reshape(nseq, seq_len, d)


def kernel(x_prompt, x_sample, c_prompt, c_sample, w_mod, b_mod, pre_mix_g, post_mix_g, pre_ffn_g, post_ffn_g, w_in, w_out, ssd_conv_w, ssd_conv_b, ssd_a_log, ssd_dt_bias, ssd_d, ssd_norm_g, swa_sink, mlstm_i_bias, mlstm_f_bias, mlstm_norm_g, w_router, b_router, w_gate_up, b_gate_up, w_down, b_down):
    p = dict(w_mod=w_mod, b_mod=b_mod, pre_mix_g=pre_mix_g, post_mix_g=post_mix_g, pre_ffn_g=pre_ffn_g,
             post_ffn_g=post_ffn_g, w_in=w_in, w_out=w_out, ssd_conv_w=ssd_conv_w, ssd_conv_b=ssd_conv_b,
             ssd_a_log=ssd_a_log, ssd_dt_bias=ssd_dt_bias, ssd_d=ssd_d, ssd_norm_g=ssd_norm_g,
             swa_sink=swa_sink, mlstm_i_bias=mlstm_i_bias, mlstm_f_bias=mlstm_f_bias,
             mlstm_norm_g=mlstm_norm_g, w_router=w_router, b_router=b_router, w_gate_up=w_gate_up,
             b_gate_up=b_gate_up, w_down=w_down, b_down=b_down)
    nb = x_prompt.shape[0]
    x = jnp.concatenate([x_prompt, x_sample], axis=0)
    c = jnp.concatenate([c_prompt, c_sample], axis=0)
    y = _trunk(x, c, p)
    return (y[:nb], y[nb:])
```

```python
import functools
import math

import jax
import jax.numpy as jnp
import numpy as np
from jax import lax
from jax.experimental import pallas as pl
from jax.experimental.pallas import tpu as pltpu

F32 = jnp.float32
BF16 = jnp.bfloat16

D_MODEL = 1024
HEAD_DIM = 64
ROPE_THETA = 500000.0
NORM_EPS = 1e-6
CHUNK = 128

SSD_HEADS = 6
SSD_INNER = 384
SSD_STATE = 64
SSD_CONV = 5
SSD_XBC = 640
SWA_Q_HEADS = 6
SWA_SIDE = 128
SWA_HEAD_ORDER = (0, 3, 1, 4, 2, 5)
ML_HEADS = 4
ML_DIM = 96
ML_PAD = 128
DIL_CONFIGS = ((128, 1), (512, 4), (2048, 16))
DIL_HEADS = 6
DIL_SIDE = 64
DIL_BATCH = 4
N_EXPERTS = 32
TOP_K = 4
D_EXPERT = 1024
SWIGLU_LIMIT = 7.0
SWIGLU_ALPHA = 1.702

IN_SPLITS = (384, 640, 12, 384, 128, 128, 384, 384, 384, 384, 8, 8, 1152, 384, 384)

COL_A = 0
COL_G = 1024
COL_B = 1152
COL_C = 1792
COL_D = 3840
COL_END = 5760
LANE_IG = 16
LANE_FG = 24

NEG = -1e30
VMEM_LIMIT = 56 * 1024 * 1024
OUT_SUB = 256
MOE_ROWS = 128
MOE_GROUP = 4
MOE_TILE = 4096


def _cparams(sem):
    return pltpu.CompilerParams(dimension_semantics=sem, vmem_limit_bytes=VMEM_LIMIT)


def _dot(a, b):
    return jnp.dot(a, b, preferred_element_type=F32)


def _dot_nt(a, b):
    return lax.dot_general(a, b, (((1,), (1,)), ((), ())), preferred_element_type=F32)


def _dot_tn(a, b):
    return lax.dot_general(a, b, (((0,), (0,)), ((), ())), preferred_element_type=F32)


def _sigmoid(x):
    return 1.0 / (1.0 + jnp.exp(-x))


def _softplus(x):
    return jnp.maximum(x, 0.0) + jnp.log(1.0 + jnp.exp(-jnp.abs(x)))


def _prefix_rows(x):
    row = lax.broadcasted_iota(jnp.int32, x.shape, 0)
    s = 1
    while s < x.shape[0]:
        x = x + jnp.where(row >= s, pltpu.roll(x, s, 0), 0.0)
        s *= 2
    return x


def _mod_kernel(c_ref, w_ref, b_ref, o_ref):
    c = c_ref[...]
    s = c * _sigmoid(c)
    o_ref[0] = jnp.dot(s, w_ref[0], preferred_element_type=F32,
                       precision=lax.Precision.HIGHEST) + b_ref[0]


def _mod_call(c, w_mod, b_mod):
    nb = c.shape[0]
    depth, d, cols = w_mod.shape
    tn = 1536
    return pl.pallas_call(
        _mod_kernel,
        grid=(depth, cols // tn),
        in_specs=[pl.BlockSpec((nb, d), lambda l, j: (0, 0)),
                  pl.BlockSpec((1, d, tn), lambda l, j: (l, 0, j)),
                  pl.BlockSpec((1, 1, tn), lambda l, j: (l, 0, j))],
        out_specs=pl.BlockSpec((1, nb, tn), lambda l, j: (l, 0, j)),
        out_shape=jax.ShapeDtypeStruct((depth, nb, cols), F32),
        compiler_params=_cparams(("parallel", "parallel")),
        name="mod",
    )(c, w_mod, b_mod.reshape(depth, 1, cols))


def _rope(a, rc, rs1, rs2):
    return a * rc + pltpu.roll(a, 8, 1) * rs1 + pltpu.roll(a, 120, 1) * rs2


def _in_proj_kernel(*refs, fused):
    if fused:
        (x_ref, f_ref, g2_ref, pg_ref, sc_ref, sh_ref, g_ref, w_ref, rc_ref, rs1_ref, rs2_ref,
         oa_ref, og_ref, ob_ref, oc_ref, od_ref, x2_ref) = refs
        f = f_ref[...]
        fms = jnp.mean(f * f, axis=-1, keepdims=True)
        x = x_ref[...] + g2_ref[0] * (f * lax.rsqrt(fms + NORM_EPS) * pg_ref[...])
        x2_ref[...] = x
    else:
        (x_ref, sc_ref, sh_ref, g_ref, w_ref, rc_ref, rs1_ref, rs2_ref,
         oa_ref, og_ref, ob_ref, oc_ref, od_ref) = refs
        x = x_ref[...]
    ms = jnp.mean(x * x, axis=-1, keepdims=True)
    h = x * lax.rsqrt(ms + NORM_EPS) * g_ref[...]
    h = h * (1.0 + sc_ref[0]) + sh_ref[0]
    hb = h.astype(BF16)
    rc, rs1, rs2 = rc_ref[...], rs1_ref[...], rs2_ref[...]

    def mm(c0, width):
        return _dot(hb, w_ref[:, c0:c0 + width])

    def plain(o_ref, col0, dst0, width, scale=None, step=512, add=None):
        for c in range(0, width, step):
            wd = min(step, width - c)
            a = mm(col0 + c, wd)
            if scale is not None:
                a = a * scale
            if add is not None:
                a = a + add
            o_ref[:, dst0 + c:dst0 + c + wd] = a.astype(o_ref.dtype)

    def roped(o_ref, col0, dst0, width, scale):
        for c in range(0, width, 128):
            a = _rope(mm(col0 + c, 128), rc, rs1, rs2)
            if scale is not None:
                a = a * scale
            o_ref[:, dst0 + c:dst0 + c + 128] = a.astype(o_ref.dtype)

    qscale = HEAD_DIM ** -0.5
    plain(oa_ref, COL_A, 0, 1024)
    plain(og_ref, COL_G, 0, 128)
    roped(ob_ref, COL_B, 0, 384, qscale)
    roped(ob_ref, COL_B + 384, 384, 128, None)
    plain(ob_ref, COL_B + 512, 512, 128)
    plain(oc_ref, COL_C, 0, 512)
    plain(oc_ref, COL_C + 512, 512, 512, scale=ML_DIM ** -0.5)
    lane_in_head = lax.broadcasted_iota(jnp.int32, (1, 512), 1) % ML_PAD
    plain(oc_ref, COL_C + 1024, 1024, 512, add=jnp.where(lane_in_head == ML_DIM, 1.0, 0.0))
    plain(oc_ref, COL_C + 1536, 1536, 512)
    roped(od_ref, COL_D, 0, 1152, qscale)
    roped(od_ref, COL_D + 1152, 1152, 384, None)
    plain(od_ref, COL_D + 1536, 1536, 384, step=384)


def _in_proj_call(x, sc, sh, g, w, rc, rs1, rs2, seq_len, ffn=None, tm=512):
    n, d = x.shape
    per_seq = seq_len // tm
    row = lambda i: (i, 0)
    seq = lambda i: (i // per_seq, 0, 0)
    pos = lambda i: (i % per_seq, 0)
    const = lambda i: (0, 0)
    widths = (1024, 128, 640, 2048, 1920)
    dtypes = (BF16, F32, BF16, BF16, BF16)
    fused = ffn is not None
    x_specs = [pl.BlockSpec((tm, d), row)]
    x_args = [x]
    out_specs = [pl.BlockSpec((tm, wd), row) for wd in widths]
    out_shape = [jax.ShapeDtypeStruct((n, wd), dt) for wd, dt in zip(widths, dtypes)]
    if fused:
        x_specs += [pl.BlockSpec((tm, d), row), pl.BlockSpec((1, 1, d), seq), pl.BlockSpec((1, d), const)]
        x_args += list(ffn)
        out_specs.append(pl.BlockSpec((tm, d), row))
        out_shape.append(jax.ShapeDtypeStruct((n, d), F32))
    return pl.pallas_call(
        functools.partial(_in_proj_kernel, fused=fused),
        grid=(n // tm,),
        in_specs=x_specs + [pl.BlockSpec((1, 1, d), seq),
                            pl.BlockSpec((1, 1, d), seq),
                            pl.BlockSpec((1, d), const),
                            pl.BlockSpec((d, COL_END), const, pipeline_mode=pl.Buffered(1)),
                            pl.BlockSpec((tm, 128), pos),
                            pl.BlockSpec((tm, 128), pos),
                            pl.BlockSpec((tm, 128), pos)],
        out_specs=out_specs,
        out_shape=out_shape,
        compiler_params=_cparams(("parallel",)),
        name="in_proj",
    )(*x_args, sc, sh, g, w, rc, rs1, rs2)


def _ssd_kernel(ua_ref, ug_ref, cw_ref, cb_ref, alog_ref, dtb_ref, dsk_ref, ng_ref, o_ref,
                xp_ref, xa_ref, cum_ref, dtd_ref, tr_ref, y_ref, y2_ref, sf_ref, sb_ref,
                *, seq_len):
    nc = seq_len // CHUNK
    q = CHUNK
    xp_ref[0:8, :] = jnp.zeros((8, SSD_XBC), F32)
    xp_ref[seq_len + 8:seq_len + 16, :] = jnp.zeros((8, SSD_XBC), F32)
    xp_ref[8:seq_len + 8, :] = ua_ref[:, SSD_INNER:SSD_INNER + SSD_XBC].astype(F32)
    sf_ref[...] = jnp.zeros_like(sf_ref)
    sb_ref[...] = jnp.zeros_like(sb_ref)
    a_row = -jnp.exp(alog_ref[...])
    lane = lax.broadcasted_iota(jnp.int32, (q, 128), 1)

    def prep(c, carry):
        r0 = pl.multiple_of(c * q, q)
        win = xp_ref[pl.ds(r0, q + 16), :]
        conv = cb_ref[...] + cw_ref[0:1, :] * win[6:6 + q]
        for k in range(1, SSD_CONV):
            conv = conv + cw_ref[k:k + 1, :] * win[6 + k:6 + k + q]
        xa_ref[pl.ds(r0, q), :] = conv * _sigmoid(conv)
        dt = _softplus(ug_ref[pl.ds(r0, q), :] + dtb_ref[...])
        dta = dt * a_row
        p = _prefix_rows(dta)
        s = p[q - 1:q, :] - p + dta
        cum = jnp.where(lane < SSD_HEADS, p, s)
        cum_ref[pl.ds(r0, q), :] = cum
        dtd_ref[pl.ds(r0, q), :] = dt
        packed = jnp.where(lane < 16, cum, pltpu.roll(dt, 16, 1))
        tr_ref[c] = packed.T
        return carry

    lax.fori_loop(0, nc, prep, 0)

    ri = lax.broadcasted_iota(jnp.int32, (q, q), 0)
    ci = lax.broadcasted_iota(jnp.int32, (q, q), 1)
    lower_incl = ci <= ri
    lower = ci < ri
    upper = ci > ri

    def main(c, carry):
        r0 = pl.multiple_of(c * q, q)
        cbk = nc - 1 - c
        r1 = pl.multiple_of(cbk * q, q)
        tr = tr_ref[c]
        visits = []
        for rr in (r0, r1):
            xa = xa_ref[pl.ds(rr, q), :]
            bt = xa[:, SSD_INNER:SSD_INNER + 128].T.astype(BF16)
            visits.append(dict(
                xa=xa, cum=cum_ref[pl.ds(rr, q), :], dtd=dtd_ref[pl.ds(rr, q), :],
                bgt=[bt[g * 64:(g + 1) * 64, :] for g in range(2)],
                bg=[xa[:, SSD_INNER + g * 64:SSD_INNER + (g + 1) * 64].astype(BF16) for g in range(2)],
                cg=[xa[:, SSD_INNER + 128 + g * 64:SSD_INNER + 128 + (g + 1) * 64].astype(BF16)
                    for g in range(2)]))
        fw, bw = visits
        heads = range(SSD_HEADS)
        cbs = [_dot_nt(fw['cg'][g], fw['bg'][g]) for g in range(2)]
        ws = []
        for h in heads:
            cum = fw['cum']
            cf_col, rb_col = cum[:, h:h + 1], cum[:, 6 + h:7 + h]
            cf_row, rb_row = tr[h:h + 1, :], tr[6 + h:7 + h, :]
            dtf_row, dtb_row = tr[16 + h:17 + h, :], tr[22 + h:23 + h, :]
            arg = jnp.where(lower_incl, cf_col - cf_row, rb_col - rb_row)
            fac = jnp.where(lower, dtf_row, jnp.where(upper, dtb_row, dtf_row + dtb_row))
            ws.append((cbs[h // 3] * jnp.exp(arg) * fac).astype(BF16))
        xf = [fw['xa'][:, h * 64:(h + 1) * 64] for h in heads]
        xb = [bw['xa'][:, h * 64:(h + 1) * 64] for h in heads]
        sfs = [sf_ref[h] for h in heads]
        sbs = [sb_ref[h] for h in heads]
        y_intra = [_dot(ws[h], xf[h].astype(BF16)) for h in heads]
        y_carry_f = [_dot(fw['cg'][h // 3], sfs[h].astype(BF16)) for h in heads]
        y_carry_b = [_dot(bw['cg'][h // 3], sbs[h].astype(BF16)) for h in heads]
        y_ref[pl.ds(r0, q), :] = jnp.concatenate(
            [y_intra[h] + y_carry_f[h] * jnp.exp(fw['cum'][:, h:h + 1]) for h in heads], axis=1)
        y2_ref[pl.ds(r1, q), :] = jnp.concatenate(
            [y_carry_b[h] * jnp.exp(bw['cum'][:, 6 + h:7 + h]) for h in heads], axis=1)
        xws, decays = [], []
        for h in heads:
            tf = fw['cum'][q - 1:q, h:h + 1]
            wcol = jnp.exp(tf - fw['cum'][:, h:h + 1]) * fw['dtd'][:, h:h + 1]
            xws.append((xf[h] * wcol).astype(BF16))
            decays.append(jnp.exp(tf))
        for h in heads:
            tb = bw['cum'][0:1, 6 + h:7 + h]
            wcol = jnp.exp(tb - bw['cum'][:, 6 + h:7 + h]) * bw['dtd'][:, 6 + h:7 + h]
            xws.append((xb[h] * wcol).astype(BF16))
            decays.append(jnp.exp(tb))
        for h in heads:
            sf_ref[h] = sfs[h] * decays[h] + _dot(fw['bgt'][h // 3], xws[h])
        for h in heads:
            sb_ref[h] = sbs[h] * decays[6 + h] + _dot(bw['bgt'][h // 3], xws[6 + h])
        return carry

    lax.fori_loop(0, nc, main, 0)

    def fin(c, carry):
        r0 = pl.multiple_of(c * q, q)
        xs = xa_ref[pl.ds(r0, q), 0:SSD_INNER]
        y = y_ref[pl.ds(r0, q), :] + y2_ref[pl.ds(r0, q), :] + xs * dsk_ref[...]
        z = ua_ref[pl.ds(r0, q), 0:SSD_INNER].astype(F32)
        v = y * (z * _sigmoid(z))
        ms = jnp.mean(v * v, axis=-1, keepdims=True)
        o_ref[pl.ds(r0, q), :] = (v * lax.rsqrt(ms + NORM_EPS) * ng_ref[...]).astype(o_ref.dtype)
        return carry

    lax.fori_loop(0, nc, fin, 0)


def _ssd_call(ua, ug, cw, cb, alog, dtb, dsk, ng, seq_len):
    n = ua.shape[0]
    nseq = n // seq_len
    nc = seq_len // CHUNK
    seq = lambda b: (b, 0)
    const = lambda b: (0, 0)
    return pl.pallas_call(
        functools.partial(_ssd_kernel, seq_len=seq_len),
        grid=(nseq,),
        in_specs=[pl.BlockSpec((seq_len, 1024), seq),
                  pl.BlockSpec((seq_len, 128), seq),
                  pl.BlockSpec((8, SSD_XBC), const),
                  pl.BlockSpec((1, SSD_XBC), const),
                  pl.BlockSpec((1, 128), const),
                  pl.BlockSpec((1, 128), const),
                  pl.BlockSpec((1, SSD_INNER), const),
                  pl.BlockSpec((1, SSD_INNER), const)],
        out_specs=pl.BlockSpec((seq_len, SSD_INNER), seq),
        out_shape=jax.ShapeDtypeStruct((n, SSD_INNER), BF16),
        scratch_shapes=[pltpu.VMEM((seq_len + 16, SSD_XBC), F32),
                        pltpu.VMEM((seq_len, SSD_XBC), F32),
                        pltpu.VMEM((seq_len, 128), F32),
                        pltpu.VMEM((seq_len, 128), F32),
                        pltpu.VMEM((nc, 128, CHUNK), F32),
                        pltpu.VMEM((seq_len, SSD_INNER), F32),
                        pltpu.VMEM((seq_len, SSD_INNER), F32),
                        pltpu.VMEM((SSD_HEADS, SSD_STATE, HEAD_DIM), F32),
                        pltpu.VMEM((SSD_HEADS, SSD_STATE, HEAD_DIM), F32)],
        compiler_params=_cparams(("parallel",)),
        name="ssd",
    )(ua, ug, cw, cb, alog, dtb, dsk, ng)


def _swa_kernel(sink_ref, u_ref, o_ref, kp_ref, vp_ref, bias_ref, *, seq_len):
    q = CHUNK
    nb = seq_len // q
    zeros = jnp.zeros((q, 128), BF16)
    kp_ref[0:q, :] = zeros
    vp_ref[0:q, :] = zeros
    kp_ref[seq_len + q:seq_len + 2 * q, :] = zeros
    vp_ref[seq_len + q:seq_len + 2 * q, :] = zeros
    kp_ref[q:seq_len + q, :] = u_ref[:, 384:512]
    vp_ref[q:seq_len + q, :] = u_ref[:, 512:640]
    row = lax.broadcasted_iota(jnp.int32, (3 * q, 3 * q), 0) & (q - 1)
    col = lax.broadcasted_iota(jnp.int32, (3 * q, 3 * q), 1)
    bias_ref[...] = jnp.where(jnp.abs(col - q - row) <= SWA_SIDE, 0.0, NEG)
    rgrp = lax.broadcasted_iota(jnp.int32, (3 * q, 1), 0) // q
    col1 = lax.broadcasted_iota(jnp.int32, (1, 3 * q), 1)
    lane1 = lax.broadcasted_iota(jnp.int32, (1, 128), 1)
    lo_f = jnp.where(lane1 < HEAD_DIM, 1.0, 0.0)
    hi_f = 1.0 - lo_f
    halves = ((lo_f.astype(BF16), lo_f), (hi_f.astype(BF16), hi_f))

    def body(qb, carry):
        q0 = pl.multiple_of(qb * q, q)
        qs = jnp.concatenate([u_ref[pl.ds(q0, q), j * 128:(j + 1) * 128] for j in range(3)], axis=0)
        kw = kp_ref[pl.ds(q0, 3 * q), :]
        vw = vp_ref[pl.ds(q0, 3 * q), :]
        inside = (col1 >= q - q0) & (col1 < seq_len + q - q0)
        bias = bias_ref[...] + jnp.where(inside, 0.0, NEG)
        o = None
        for hk, (sel_b, sel_f) in enumerate(halves):
            s = _dot_nt(qs * sel_b, kw) + bias
            sk = jnp.where(rgrp == 0, sink_ref[3 * hk],
                           jnp.where(rgrp == 1, sink_ref[3 * hk + 1], sink_ref[3 * hk + 2]))
            m = jnp.maximum(jnp.max(s, axis=-1, keepdims=True), sk)
            p = jnp.exp(s - m)
            l = jnp.sum(p, axis=-1, keepdims=True) + jnp.exp(sk - m)
            oh = (_dot(p.astype(BF16), vw) / l) * sel_f
            o = oh if o is None else o + oh
        o_ref[pl.ds(q0, q), :] = jnp.concatenate(
            [o[j * q:(j + 1) * q, :] for j in range(3)], axis=1).astype(o_ref.dtype)
        return carry

    lax.fori_loop(0, nb, body, 0, unroll=2)


def _swa_call(ub, sink, seq_len):
    n = ub.shape[0]
    nseq = n // seq_len
    return pl.pallas_call(
        functools.partial(_swa_kernel, seq_len=seq_len),
        grid=(nseq,),
        in_specs=[pl.BlockSpec(memory_space=pltpu.SMEM),
                  pl.BlockSpec((seq_len, 640), lambda b: (b, 0))],
        out_specs=pl.BlockSpec((seq_len, 384), lambda b: (b, 0)),
        out_shape=jax.ShapeDtypeStruct((n, 384), BF16),
        scratch_shapes=[pltpu.VMEM((seq_len + 2 * CHUNK, 128), BF16),
                        pltpu.VMEM((seq_len + 2 * CHUNK, 128), BF16),
                        pltpu.VMEM((3 * CHUNK, 3 * CHUNK), F32)],
        compiler_params=_cparams(("parallel",)),
        name="swa",
    )(sink, ub)


def _mlstm_kernel(uc_ref, ug_ref, gb_ref, ng_ref, o_ref,
                  gcol_ref, gtr_ref, hacc_ref, ct_ref, m_ref, mask_ref, *, seq_len):
    q = CHUNK
    nc = seq_len // q
    lane = lax.broadcasted_iota(jnp.int32, (q, 128), 1)
    ct_ref[...] = jnp.zeros_like(ct_ref)
    m_ref[...] = jnp.zeros_like(m_ref)
    hacc_ref[...] = jnp.zeros_like(hacc_ref)

    def prep(c, carry):
        r0 = pl.multiple_of(c * q, q)
        raw = ug_ref[pl.ds(r0, q), :] + gb_ref[...]
        lf = jnp.minimum(raw, 0.0) - jnp.log(1.0 + jnp.exp(-jnp.abs(raw)))
        p = _prefix_rows(lf)
        s = p[q - 1:q, :] - p + lf
        g = jnp.where(lane < LANE_FG, raw, jnp.where(lane < LANE_FG + ML_HEADS, p, s))
        gcol_ref[pl.ds(r0, q), :] = g
        gtr_ref[c] = g.T
        return carry

    lax.fori_loop(0, nc, prep, 0)

    ri = lax.broadcasted_iota(jnp.int32, (q, q), 0)
    ci = lax.broadcasted_iota(jnp.int32, (q, q), 1)
    mask_ref[0] = jnp.where(ci <= ri, 0.0, NEG)
    mask_ref[1] = jnp.where(ci >= ri, 0.0, NEG)
    keep_lanes = jnp.where(lax.broadcasted_iota(jnp.int32, (1, ML_PAD), 1) < ML_DIM, 1.0, 0.0)

    def main(c, carry):
        units = []
        for dirn in range(2):
            ck = c if dirn == 0 else nc - 1 - c
            r0 = pl.multiple_of(ck * q, q)
            g = gcol_ref[pl.ds(r0, q), :]
            gt = gtr_ref[ck]
            for h in range(ML_HEADS):
                s_idx = dirn * ML_HEADS + h
                li, lb = LANE_IG + s_idx, LANE_FG + s_idx
                edge = q - 1 if dirn == 0 else 0
                units.append(dict(
                    dirn=dirn, h=h, r0=r0, s_idx=s_idx,
                    b_col=g[:, lb:lb + 1], b_row=gt[lb:lb + 1, :], i_col=g[:, li:li + 1],
                    i_row=gt[li:li + 1, :], tot=g[edge:edge + 1, lb:lb + 1],
                    m_st=m_ref[s_idx][0:1, 0:1], ct=ct_ref[s_idx],
                    qh=uc_ref[pl.ds(r0, q), h * ML_PAD:(h + 1) * ML_PAD],
                    kh=uc_ref[pl.ds(r0, q), 512 + h * ML_PAD:512 + (h + 1) * ML_PAD],
                    vh=uc_ref[pl.ds(r0, q), 1024 + h * ML_PAD:1024 + (h + 1) * ML_PAD]))
        for u in units:
            u['logd'] = u['b_col'] - u['b_row'] + u['i_row'] + mask_ref[u['dirn']]
            u['m_inter'] = u['b_col'] + u['m_st']
        for u in units:
            u['m_t'] = jnp.maximum(jnp.max(u['logd'], axis=-1, keepdims=True), u['m_inter'])
        for u in units:
            u['qk'] = _dot_nt(u['qh'], u['kh'])
            u['qc'] = _dot(u['qh'], u['ct'].astype(BF16))
        for u in units:
            u['sm'] = (u['qk'] * jnp.exp(u['logd'] - u['m_t'])).astype(BF16)
            u['inter'] = jnp.exp(u['m_inter'] - u['m_t'])
        for u in units:
            u['num'] = _dot(u['sm'], u['vh']) + u['inter'] * u['qc']
        for u in units:
            den = u['num'][:, ML_DIM:ML_DIM + 1]
            hh = u['num'] / jnp.maximum(jnp.abs(den), jnp.exp(-u['m_t'])) * keep_lanes
            hacc_ref[pl.ds(u['r0'], q), u['h'] * ML_PAD:(u['h'] + 1) * ML_PAD] += hh
        for u in units:
            tot, m_st = u['tot'], u['m_st']
            m_new = jnp.maximum(tot + m_st,
                                jnp.max(tot - u['b_row'] + u['i_row'], axis=-1, keepdims=True))
            wk_col = jnp.exp(tot - u['b_col'] + u['i_col'] - m_new)
            u['keep'] = jnp.exp(tot + m_st - m_new)
            u['vw'] = (u['vh'].astype(F32) * wk_col).astype(BF16)
            m_ref[u['s_idx']] = jnp.broadcast_to(m_new, (8, 128))
        for u in units:
            ct_ref[u['s_idx']] = u['keep'] * u['ct'] + _dot_tn(u['kh'], u['vw'])
        return carry

    lax.fori_loop(0, nc, main, 0)

    def fin(c, carry):
        r0 = pl.multiple_of(c * q, q)
        outs = []
        for h in range(ML_HEADS):
            hs = hacc_ref[pl.ds(r0, q), h * ML_PAD:(h + 1) * ML_PAD]
            ms = jnp.sum(hs * hs, axis=-1, keepdims=True) * (1.0 / ML_DIM)
            hn = hs * lax.rsqrt(ms + NORM_EPS) * ng_ref[:, h * ML_PAD:(h + 1) * ML_PAD]
            og = uc_ref[pl.ds(r0, q), 1536 + h * ML_PAD:1536 + (h + 1) * ML_PAD].astype(F32)
            outs.append(hn * _sigmoid(og))
        o_ref[pl.ds(r0, q), :] = jnp.concatenate(outs, axis=1).astype(o_ref.dtype)
        return carry

    lax.fori_loop(0, nc, fin, 0)


def _mlstm_call(uc, ug, gb, ng, seq_len):
    n = uc.shape[0]
    nseq = n // seq_len
    nc = seq_len // CHUNK
    seq = lambda b: (b, 0)
    const = lambda b: (0, 0)
    width = ML_HEADS * ML_PAD
    return pl.pallas_call(
        functools.partial(_mlstm_kernel, seq_len=seq_len),
        grid=(nseq,),
        in_specs=[pl.BlockSpec((seq_len, 4 * width), seq),
                  pl.BlockSpec((seq_len, 128), seq),
                  pl.BlockSpec((1, 128), const),
                  pl.BlockSpec((1, width), const)],
        out_specs=pl.BlockSpec((seq_len, width), seq),
        out_shape=jax.ShapeDtypeStruct((n, width), BF16),
        scratch_shapes=[pltpu.VMEM((seq_len, 128), F32),
                        pltpu.VMEM((nc, 128, CHUNK), F32),
                        pltpu.VMEM((seq_len, width), F32),
                        pltpu.VMEM((2 * ML_HEADS, ML_PAD, ML_PAD), F32),
                        pltpu.VMEM((2 * ML_HEADS, 8, 128), F32),
                        pltpu.VMEM((2, CHUNK, CHUNK), F32)],
        compiler_params=_cparams(("parallel",)),
        name="mlstm",
    )(uc, ug, gb, ng)


def _attend_units(units, lo_b, hi_b, lo_f, hi_f):
    n = units[0][0].shape[0]
    ss = []
    for q, kw, _, bias in units:
        s2 = _dot_nt(jnp.concatenate([q * lo_b, q * hi_b], axis=0), kw)
        ss.append(jnp.concatenate([s2[0:n] + bias, s2[n:2 * n] + bias], axis=0))
    ms = [jnp.max(s, axis=-1, keepdims=True) for s in ss]
    ps = [jnp.exp(s - m) for s, m in zip(ss, ms)]
    ls = [jnp.sum(p, axis=-1, keepdims=True) for p in ps]
    os_ = [_dot(p.astype(BF16), u[2]) / l for p, u, l in zip(ps, units, ls)]
    res = []
    for o2, m, l in zip(os_, ms, ls):
        lse2 = m + jnp.log(l)
        res.append((o2[0:n] * lo_f + o2[n:2 * n] * hi_f, lse2[0:n] * lo_f + lse2[n:2 * n] * hi_f))
    return res


def _dil_kernel(q0_ref, q1_ref, q2_ref, k_ref, v_ref, o_ref,
                qf_ref, kf_ref, vf_ref, kc0_ref, vc0_ref, kc1_ref, vc1_ref, og_ref, lg_ref,
                qc2_ref, kc2_ref, vc2_ref, og2_ref, lg2_ref, band_ref, band2_ref, *, seq_len):
    qr = CHUNK
    pad = DIL_SIDE
    d1, d2 = DIL_CONFIGS[1][1], DIL_CONFIGS[2][1]
    n1, n2 = seq_len // d1, seq_len // d2
    assert n2 == qr and n1 % qr == 0
    qf_ref[0] = q1_ref[...].astype(F32)
    qf_ref[1] = q2_ref[...].astype(F32)
    kf_ref[...] = k_ref[...].astype(F32)
    vf_ref[...] = v_ref[...].astype(F32)
    zeros = jnp.zeros((pad, 128), BF16)
    kc0_ref[0:pad, :] = zeros
    vc0_ref[0:pad, :] = zeros
    kc0_ref[pad + seq_len:2 * pad + seq_len, :] = zeros
    vc0_ref[pad + seq_len:2 * pad + seq_len, :] = zeros
    kc0_ref[pad:pad + seq_len, :] = k_ref[...]
    vc0_ref[pad:pad + seq_len, :] = v_ref[...]
    for r in range(d1):
        kc1_ref[r, 0:pad, :] = zeros
        vc1_ref[r, 0:pad, :] = zeros
        kc1_ref[r, pad + n1:2 * pad + n1, :] = zeros
        vc1_ref[r, pad + n1:2 * pad + n1, :] = zeros
        kc1_ref[r, pad:pad + n1, :] = kf_ref[pl.ds(r, n1, stride=d1), :].astype(BF16)
        vc1_ref[r, pad:pad + n1, :] = vf_ref[pl.ds(r, n1, stride=d1), :].astype(BF16)

    row = lax.broadcasted_iota(jnp.int32, (qr, 2 * qr), 0)
    col = lax.broadcasted_iota(jnp.int32, (qr, 2 * qr), 1)
    band_ref[...] = jnp.where(jnp.abs(col - pad - row) <= DIL_SIDE, 0.0, NEG)
    row2 = lax.broadcasted_iota(jnp.int32, (qr, qr), 0)
    col2 = lax.broadcasted_iota(jnp.int32, (qr, qr), 1)
    band2_ref[...] = jnp.where(jnp.abs(col2 - row2) <= DIL_SIDE, 0.0, NEG)
    lane1 = lax.broadcasted_iota(jnp.int32, (1, 128), 1)
    col1 = lax.broadcasted_iota(jnp.int32, (1, 2 * qr), 1)
    lo_f = jnp.where(lane1 < HEAD_DIM, 1.0, 0.0)
    hi_f = 1.0 - lo_f
    sel = (lo_f.astype(BF16), hi_f.astype(BF16), lo_f, hi_f)

    def window_bias(u0, n):
        inside = (col1 >= pad - u0) & (col1 < n + pad - u0)
        return band_ref[...] + jnp.where(inside, 0.0, NEG)

    def batch0(b4, carry):
        units, starts = [], []
        for i in range(DIL_BATCH):
            u0 = pl.multiple_of((b4 * DIL_BATCH + i) * qr, qr)
            starts.append(u0)
            units.append((q0_ref[pl.ds(u0, qr), :], kc0_ref[pl.ds(u0, 2 * qr), :],
                          vc0_ref[pl.ds(u0, 2 * qr), :], window_bias(u0, seq_len)))
        for u0, (o, lse) in zip(starts, _attend_units(units, *sel)):
            og_ref[0, pl.ds(u0, qr), :] = o
            lg_ref[0, pl.ds(u0, qr), :] = lse
        return carry

    lax.fori_loop(0, seq_len // qr // DIL_BATCH, batch0, 0)

    nb1 = n1 // qr
    assert nb1 == DIL_BATCH

    def batch1(r, carry):
        units, rowsl = [], []
        for i in range(DIL_BATCH):
            u0 = i * qr
            rows = pl.ds(r + u0 * d1, qr, stride=d1)
            rowsl.append(rows)
            units.append((qf_ref[0, rows, :].astype(BF16), kc1_ref[r, u0:u0 + 2 * qr, :],
                          vc1_ref[r, u0:u0 + 2 * qr, :], window_bias(u0, n1)))
        for rows, (o, lse) in zip(rowsl, _attend_units(units, *sel)):
            og_ref[1, rows, :] = o
            lg_ref[1, rows, :] = lse
        return carry

    lax.fori_loop(0, d1, batch1, 0)

    for r in range(d2):
        rows = pl.ds(r, qr, stride=d2)
        qc2_ref[r] = qf_ref[1, rows, :].astype(BF16)
        kc2_ref[r] = kf_ref[rows, :].astype(BF16)
        vc2_ref[r] = vf_ref[rows, :].astype(BF16)

    def batch2(b4, carry):
        rs = [b4 * DIL_BATCH + i for i in range(DIL_BATCH)]
        units = [(qc2_ref[r], kc2_ref[r], vc2_ref[r], band2_ref[...]) for r in rs]
        for r, (o, lse) in zip(rs, _attend_units(units, *sel)):
            og2_ref[r] = o
            lg2_ref[r] = lse
        return carry

    lax.fori_loop(0, d2 // DIL_BATCH, batch2, 0)
    for r in range(d2):
        rows = pl.ds(r, qr, stride=d2)
        og_ref[2, rows, :] = og2_ref[r]
        lg_ref[2, rows, :] = lg2_ref[r]

    def fin(c, carry):
        r0 = pl.multiple_of(c * qr, qr)
        rows = pl.ds(r0, qr)
        l0, l1, l2 = lg_ref[0, rows, :], lg_ref[1, rows, :], lg_ref[2, rows, :]
        m = jnp.maximum(jnp.maximum(l0, l1), l2)
        w0, w1, w2 = jnp.exp(l0 - m), jnp.exp(l1 - m), jnp.exp(l2 - m)
        y = (og_ref[0, rows, :] * w0 + og_ref[1, rows, :] * w1 + og_ref[2, rows, :] * w2)
        o_ref[rows, :] = (y / (w0 + w1 + w2)).astype(o_ref.dtype)
        return carry

    lax.fori_loop(0, seq_len // qr, fin, 0, unroll=2)


def _dil_call(ud, seq_len):
    n = ud.shape[0]
    nseq = n // seq_len
    blk = (seq_len, 128)
    d1, d2 = DIL_CONFIGS[1][1], DIL_CONFIGS[2][1]
    n1 = seq_len // d1

    def col(off):
        return lambda b, p: (b, off + p)

    return pl.pallas_call(
        functools.partial(_dil_kernel, seq_len=seq_len),
        grid=(nseq, 3),
        in_specs=[pl.BlockSpec(blk, col(0)), pl.BlockSpec(blk, col(3)), pl.BlockSpec(blk, col(6)),
                  pl.BlockSpec(blk, col(9)), pl.BlockSpec(blk, col(12))],
        out_specs=pl.BlockSpec(blk, col(0)),
        out_shape=jax.ShapeDtypeStruct((n, 384), BF16),
        scratch_shapes=[pltpu.VMEM((2, seq_len, 128), F32),
                        pltpu.VMEM((seq_len, 128), F32),
                        pltpu.VMEM((seq_len, 128), F32),
                        pltpu.VMEM((seq_len + 2 * DIL_SIDE, 128), BF16),
                        pltpu.VMEM((seq_len + 2 * DIL_SIDE, 128), BF16),
                        pltpu.VMEM((d1, n1 + 2 * DIL_SIDE, 128), BF16),
                        pltpu.VMEM((d1, n1 + 2 * DIL_SIDE, 128), BF16),
                        pltpu.VMEM((3, seq_len, 128), F32),
                        pltpu.VMEM((3, seq_len, 128), F32)]
                       + [pltpu.VMEM((d2, CHUNK, 128), BF16)] * 3
                       + [pltpu.VMEM((d2, CHUNK, 128), F32)] * 2
                       + [pltpu.VMEM((CHUNK, 2 * CHUNK), F32), pltpu.VMEM((CHUNK, CHUNK), F32)],
        compiler_params=_cparams(("parallel", "parallel")),
        name="dil",
    )(ud, ud, ud, ud, ud)


def _out_proj_kernel(ya_ref, yb_ref, yc_ref, yd_ref, x_ref, g1_ref, sc_ref, sh_ref,
                     pg_ref, fg_ref, wa_ref, wb_ref, wc_ref, wd_ref, wrh_ref, wrl_ref, br_ref,
                     x1_ref, h2_ref, rt_ref):
    for r0 in range(0, x_ref.shape[0], OUT_SUB):
        rs = slice(r0, r0 + OUT_SUB)
        y = (_dot(ya_ref[rs, :], wa_ref[...]) + _dot(yb_ref[rs, :], wb_ref[...])
             + _dot(yc_ref[rs, :], wc_ref[...]) + _dot(yd_ref[rs, :], wd_ref[...]))
        ms = jnp.mean(y * y, axis=-1, keepdims=True)
        yn = y * lax.rsqrt(ms + NORM_EPS) * pg_ref[...]
        x1 = x_ref[rs, :] + g1_ref[0] * yn
        x1_ref[rs, :] = x1
        ms2 = jnp.mean(x1 * x1, axis=-1, keepdims=True)
        h2 = x1 * lax.rsqrt(ms2 + NORM_EPS) * fg_ref[...]
        h2 = h2 * (1.0 + sc_ref[0]) + sh_ref[0]
        half = h2.shape[1] // 2
        lo = lax.bitcast_convert_type(h2[:, :half].astype(BF16).astype(F32), jnp.uint32)
        hi = lax.bitcast_convert_type(h2[:, half:].astype(BF16).astype(F32), jnp.uint32)
        h2_ref[rs, :] = (lo >> 16) | hi
        h_hi = h2.astype(BF16)
        h_lo = (h2 - h_hi.astype(F32)).astype(BF16)
        logits = (_dot(h_hi, wrh_ref[...]) + _dot(h_lo, wrh_ref[...]) + _dot(h_hi, wrl_ref[...])
                  + br_ref[...])
        lane = lax.broadcasted_iota(jnp.int32, logits.shape, 1)
        vals, idxs = [], []
        for _ in range(TOP_K):
            m = jnp.max(logits, axis=-1, keepdims=True)
            idx = jnp.min(jnp.where(logits == m, lane, 128), axis=-1, keepdims=True)
            vals.append(m)
            idxs.append(idx)
            logits = jnp.where(lane == idx, -3e38, logits)
        es = [jnp.exp(v - vals[0]) for v in vals]
        tot = es[0] + es[1] + es[2] + es[3]
        rt = jnp.zeros(lane.shape, F32)
        for k in range(TOP_K):
            rt = jnp.where(lane == k, idxs[k].astype(F32), rt)
            rt = jnp.where(lane == TOP_K + k, es[k] / tot, rt)
        rt_ref[rs, :] = rt[:, 0:2 * TOP_K]


def _out_proj_call(ya, yb, yc, yd, x, g1, sc, sh, pg, fg, wa, wb, wc, wd, wrh, wrl, br, seq_len, tm=512):
    n, d = x.shape
    per_seq = seq_len // tm
    row = lambda i: (i, 0)
    seq = lambda i: (i // per_seq, 0, 0)
    const = lambda i: (0, 0)

    def full(a):
        return pl.BlockSpec(a.shape, const)

    return pl.pallas_call(
        _out_proj_kernel,
        grid=(n // tm,),
        in_specs=[pl.BlockSpec((tm, ya.shape[1]), row), pl.BlockSpec((tm, yb.shape[1]), row),
                  pl.BlockSpec((tm, yc.shape[1]), row), pl.BlockSpec((tm, yd.shape[1]), row),
                  pl.BlockSpec((tm, d), row),
                  pl.BlockSpec((1, 1, d), seq), pl.BlockSpec((1, 1, d), seq),
                  pl.BlockSpec((1, 1, d), seq),
                  full(pg), full(fg), full(wa), full(wb), full(wc), full(wd), full(wrh), full(wrl), full(br)],
        out_specs=[pl.BlockSpec((tm, d), row), pl.BlockSpec((tm, d // 2), row),
                   pl.BlockSpec((tm, 2 * TOP_K), row)],
        out_shape=[jax.ShapeDtypeStruct((n, d), F32), jax.ShapeDtypeStruct((n, d // 2), jnp.uint32),
                   jax.ShapeDtypeStruct((n, 2 * TOP_K), F32)],
        compiler_params=_cparams(("parallel",)),
        name="out_proj",
    )(ya, yb, yc, yd, x, g1, sc, sh, pg, fg, wa, wb, wc, wd, wrh, wrl, br)


def _moe_slots(tile):
    return tile * TOP_K // MOE_ROWS + N_EXPERTS + 2


def _moe_kernel(nb_ref, bf_ref, bs_ref, bv_ref, idx_ref, gate_ref, h_ref,
                wgu_ref, bgu_ref, wdn_ref, bdn_ref, o_ref, xs0_ref, xs1_ref, ys0_ref, ys1_ref):
    t = pl.program_id(0)
    e = pl.program_id(1)
    nb = nb_ref[t]
    rows = MOE_ROWS
    nslot = _moe_slots(h_ref.shape[0])
    half = h_ref.shape[1]

    def gather(dst_ref, blk):
        base = bs_ref[t * nslot + blk]
        for r in range(rows):
            i = idx_ref[0, 0, base + r]
            dst_ref[r:r + 1, :] = h_ref[pl.ds(i, 1), :]

    def scatter(src_ref, blk, nvalid):
        base = bs_ref[t * nslot + blk]
        i0 = idx_ref[0, 0, base]
        for r0 in range(0, rows, MOE_GROUP):
            ks = range(MOE_GROUP)
            ok = [r0 + k < nvalid for k in ks]
            ii = [jnp.where(ok[k], idx_ref[0, 0, base + r0 + k], i0) for k in ks]
            gg = [jnp.where(ok[k], gate_ref[0, 0, base + r0 + k], 0.0) for k in ks]
            cur = [o_ref[pl.ds(ii[k], 1), :] for k in ks]
            for k in reversed(ks):
                o_ref[pl.ds(ii[k], 1), :] = cur[k] + gg[k] * src_ref[r0 + k:r0 + k + 1, :]

    def expert(xs_ref, ys_ref):
        xu = xs_ref[...]
        x_lo = lax.bitcast_convert_type(xu << 16, F32).astype(BF16)
        x_hi = lax.bitcast_convert_type(xu & jnp.uint32(0xFFFF0000), F32).astype(BF16)
        gu = _dot(x_lo, wgu_ref[0, 0:half, :]) + _dot(x_hi, wgu_ref[0, half:2 * half, :]) + bgu_ref[0]
        gate = jnp.minimum(gu[:, :D_EXPERT], SWIGLU_LIMIT)
        up = jnp.clip(gu[:, D_EXPERT:], -SWIGLU_LIMIT, SWIGLU_LIMIT)
        act = (up + 1.0) * gate * _sigmoid(SWIGLU_ALPHA * gate)
        ys_ref[...] = _dot(act.astype(BF16), wdn_ref[0]) + bdn_ref[0]

    @pl.when(e == 0)
    def _():
        o_ref[...] = jnp.zeros_like(o_ref)
        ys1_ref[...] = jnp.zeros_like(ys1_ref)
        gather(xs0_ref, 0)

    bufs = ((xs0_ref, xs1_ref, ys0_ref, ys1_ref), (xs1_ref, xs0_ref, ys1_ref, ys0_ref))

    def block(b, carry):
        prev = jnp.maximum(b - 1, 0)
        nv_prev = jnp.where(b > 0, bv_ref[t * nslot + prev], 0)
        for par, (xs_cur, xs_nxt, ys_cur, ys_prv) in enumerate(bufs):
            @pl.when(b % 2 == par)
            def _(xs_cur=xs_cur, xs_nxt=xs_nxt, ys_cur=ys_cur, ys_prv=ys_prv):
                scatter(ys_prv, prev, nv_prev)
                gather(xs_nxt, b + 1)
                expert(xs_cur, ys_cur)
        return carry

    lax.fori_loop(bf_ref[t * (N_EXPERTS + 1) + e], bf_ref[t * (N_EXPERTS + 1) + e + 1], block, 0)

    for par, (_, _, _, ys_prv) in enumerate(bufs):
        @pl.when((e == N_EXPERTS - 1) & (nb % 2 == par))
        def _(ys_prv=ys_prv):
            scatter(ys_prv, nb - 1, bv_ref[t * nslot + nb - 1])


def _moe_call(nb, bfirst, bs, bv, idx, gates, h2p, wgu, bgu, wdn, bdn, tile):
    n, half = h2p.shape
    d = 2 * half
    ntile = n // tile
    plen = idx.shape[-1]
    wmap = lambda t, e, *_: (e, 0, 0)
    tmap3 = lambda t, e, *_: (t, 0, 0)
    tmap2 = lambda t, e, *_: (t, 0)
    grid_spec = pltpu.PrefetchScalarGridSpec(
        num_scalar_prefetch=4,
        grid=(ntile, N_EXPERTS),
        in_specs=[pl.BlockSpec((1, 1, plen), tmap3, memory_space=pltpu.SMEM),
                  pl.BlockSpec((1, 1, plen), tmap3, memory_space=pltpu.SMEM),
                  pl.BlockSpec((tile, half), tmap2, pipeline_mode=pl.Buffered(1)),
                  pl.BlockSpec((1, d, 2 * D_EXPERT), wmap), pl.BlockSpec((1, 1, 2 * D_EXPERT), wmap),
                  pl.BlockSpec((1, D_EXPERT, d), wmap), pl.BlockSpec((1, 1, d), wmap)],
        out_specs=pl.BlockSpec((tile, d), tmap2, pipeline_mode=pl.Buffered(1)),
        scratch_shapes=[pltpu.VMEM((MOE_ROWS, half), jnp.uint32)] * 2 + [pltpu.VMEM((MOE_ROWS, d), F32)] * 2,
    )
    return pl.pallas_call(
        _moe_kernel,
        grid_spec=grid_spec,
        out_shape=jax.ShapeDtypeStruct((n, d), F32),
        compiler_params=_cparams(("parallel", "arbitrary")),
        name="moe",
    )(nb, bfirst, bs, bv, idx, gates, h2p, wgu, bgu, wdn, bdn)


def _route(rt, tile):
    n = rt.shape[0]
    ntile = n // tile
    npair = tile * TOP_K
    nslot = _moe_slots(tile)
    e = rt[:, :TOP_K].astype(jnp.int32).reshape(ntile, npair)
    g = rt[:, TOP_K:].reshape(ntile, npair)
    tok = jnp.broadcast_to((jnp.arange(npair, dtype=jnp.int32) // TOP_K)[None, :], (ntile, npair))
    _, tok_s, g_s = lax.sort((e, tok, g), dimension=1, is_stable=True, num_keys=1)
    experts = jnp.arange(N_EXPERTS, dtype=jnp.int32)
    counts = jnp.sum(e[:, :, None] == experts, axis=1, dtype=jnp.int32)
    nblocks = (counts + MOE_ROWS - 1) // MOE_ROWS
    bends = jnp.cumsum(nblocks, axis=-1)
    starts = jnp.cumsum(counts, axis=-1) - counts
    nb = bends[:, -1]
    slot = jnp.minimum(jnp.arange(nslot, dtype=jnp.int32)[None, :], nb[:, None] - 1)
    be = jnp.sum(slot[:, :, None] >= bends[:, None, :], axis=-1, dtype=jnp.int32)
    onehot = (be[:, :, None] == experts).astype(jnp.int32)
    pick = lambda v: jnp.sum(onehot * v[:, None, :], axis=-1)
    within = slot - pick(bends - nblocks)
    bs = pick(starts) + within * MOE_ROWS
    bv = jnp.clip(pick(counts) - within * MOE_ROWS, 0, MOE_ROWS)
    padw = ((0, 0), (0, MOE_ROWS))
    idx = jnp.pad(tok_s, padw).reshape(ntile, 1, -1)
    gl = jnp.pad(g_s, padw).reshape(ntile, 1, -1)
    bfirst = jnp.concatenate([jnp.zeros((ntile, 1), jnp.int32), bends], axis=-1)
    flat = lambda v: v.reshape(-1).astype(jnp.int32)
    return flat(nb), flat(bfirst), flat(bs), flat(bv), idx, gl


def _ffn_res_kernel(x_ref, f_ref, g2_ref, pg_ref, oa_ref, ob_ref, *, tiles_a):
    f = f_ref[...]
    ms = jnp.mean(f * f, axis=-1, keepdims=True)
    y = x_ref[...] + g2_ref[0] * (f * lax.rsqrt(ms + NORM_EPS) * pg_ref[...])
    i = pl.program_id(0)

    @pl.when(i < tiles_a)
    def _():
        oa_ref[...] = y

    @pl.when(i >= tiles_a)
    def _():
        ob_ref[...] = y


def _ffn_res_call(x1, f, g2, pg, seq_len, rows_a, tm=1024):
    n, d = x1.shape
    per_seq = seq_len // tm
    tiles_a = rows_a // tm
    row = lambda i: (i, 0)
    return pl.pallas_call(
        functools.partial(_ffn_res_kernel, tiles_a=tiles_a),
        grid=(n // tm,),
        in_specs=[pl.BlockSpec((tm, d), row), pl.BlockSpec((tm, d), row),
                  pl.BlockSpec((1, 1, d), lambda i: (i // per_seq, 0, 0)),
                  pl.BlockSpec((1, d), lambda i: (0, 0))],
        out_specs=[pl.BlockSpec((tm, d), lambda i: (jnp.minimum(i, tiles_a - 1), 0)),
                   pl.BlockSpec((tm, d), lambda i: (jnp.maximum(i - tiles_a, 0), 0))],
        out_shape=[jax.ShapeDtypeStruct((rows_a, d), F32), jax.ShapeDtypeStruct((n - rows_a, d), F32)],
        compiler_params=_cparams(("arbitrary",)),
        name="ffn_res",
    )(x1, f, g2, pg)


def _rope_tables(seq_len):
    half = 8
    inv = 1.0 / (ROPE_THETA ** (jnp.arange(half, dtype=F32) * (2.0 / 16)))
    ang = jnp.arange(seq_len, dtype=F32)[:, None] * inv[None, :]
    cos, sin = jnp.cos(ang), jnp.sin(ang)
    ones = jnp.ones((seq_len, 48), F32)
    zeros8 = jnp.zeros((seq_len, 8), F32)
    zeros48 = jnp.zeros((seq_len, 48), F32)
    rc = jnp.concatenate([cos, cos, ones], axis=1)
    rs1 = jnp.concatenate([zeros8, sin, zeros48], axis=1)
    rs2 = jnp.concatenate([-sin, zeros8, zeros48], axis=1)
    tile2 = lambda a: jnp.concatenate([a, a], axis=1)
    return tile2(rc), tile2(rs1), tile2(rs2)


def _pad_heads(w):
    lead = w.shape[:-1]
    w = w.reshape(lead + (ML_HEADS, ML_DIM))
    w = jnp.pad(w, [(0, 0)] * len(lead) + [(0, 0), (0, ML_PAD - ML_DIM)])
    return w.reshape(lead + (ML_HEADS * ML_PAD,))


def _layout_w_in(w_in):
    d = w_in.shape[0]
    offs = np.cumsum((0,) + IN_SPLITS)
    parts = [w_in[:, offs[i]:offs[i + 1]] for i in range(len(IN_SPLITS))]
    (z, xbc, dt, sq, sk, sv, mq, mk, mv, mo, mi, mf, dq, dk, dv) = parts
    zc = lambda k: jnp.zeros((d, k), w_in.dtype)
    gates = jnp.concatenate([dt, zc(4), mi, mf, zc(128 - 32)], axis=1)
    sq = sq.reshape(d, SWA_Q_HEADS, HEAD_DIM)[:, SWA_HEAD_ORDER, :].reshape(d, -1)
    w = jnp.concatenate([z, xbc, gates, sq, sk, sv,
                         _pad_heads(mq), _pad_heads(mk), _pad_heads(mv), _pad_heads(mo),
                         dq, dk, dv], axis=1)
    assert w.shape[1] == COL_END
    return w.astype(BF16)


def _lane_row(parts, width=128):
    row = jnp.zeros((width,), F32)
    for off, v in parts:
        row = row.at[off:off + v.shape[0]].set(v.astype(F32))
    return row.reshape(1, width)


def _layer(x, pending, mod_l, p, l, tabs, seq_len):
    d = D_MODEL
    nseq = x.shape[0] // seq_len
    sh1, sc1, g1, sh2, sc2, g2 = [mod_l[:, i * d:(i + 1) * d].reshape(nseq, 1, d) for i in range(6)]
    row = lambda v: v.reshape(1, -1).astype(F32)

    outs = _in_proj_call(x, sc1, sh1, row(p['pre_mix_g'][l]), _layout_w_in(p['w_in'][l]),
                         *tabs, seq_len, ffn=pending)
    if pending is not None:
        x = outs[5]
    ua, ug, ub, uc, ud = outs[:5]
    cw = jnp.pad(p['ssd_conv_w'][l], ((0, 8 - SSD_CONV), (0, 0)))
    ya = _ssd_call(ua, ug, cw, row(p['ssd_conv_b'][l]),
                   _lane_row([(0, p['ssd_a_log'][l].reshape(-1))]),
                   _lane_row([(0, p['ssd_dt_bias'][l].reshape(-1))]),
                   row(jnp.repeat(p['ssd_d'][l], HEAD_DIM)), row(p['ssd_norm_g'][l]), seq_len)
    yb = _swa_call(ub, p['swa_sink'][l].astype(F32), seq_len)
    gb = _lane_row([(LANE_IG, p['mlstm_i_bias'][l].reshape(-1)),
                    (LANE_FG, p['mlstm_f_bias'][l].reshape(-1))])
    yc = _mlstm_call(uc, ug, gb, row(_pad_heads(p['mlstm_norm_g'][l])), seq_len)
    yd = _dil_call(ud, seq_len)

    w_out = p['w_out'][l]
    wa = w_out[0:384].astype(BF16)
    wb = w_out[384:768].reshape(SWA_Q_HEADS, HEAD_DIM, d)[SWA_HEAD_ORDER, :, :].reshape(384, d).astype(BF16)
    wc = jnp.pad(w_out[768:1152].reshape(ML_HEADS, ML_DIM, d),
                 ((0, 0), (0, ML_PAD - ML_DIM), (0, 0))).reshape(ML_HEADS * ML_PAD, d).astype(BF16)
    wd = w_out[1152:1536].astype(BF16)
    wr = jnp.pad(p['w_router'][l], ((0, 0), (0, 128 - N_EXPERTS))).astype(F32)
    wrh = wr.astype(BF16)
    wrl = (wr - wrh.astype(F32)).astype(BF16)
    br = jnp.concatenate([p['b_router'][l].astype(F32), jnp.full((128 - N_EXPERTS,), NEG, F32)]).reshape(1, 128)
    x1, h2, rt = _out_proj_call(ya, yb, yc, yd, x, g1, sc2, sh2, row(p['post_mix_g'][l]),
                                row(p['pre_ffn_g'][l]), wa, wb, wc, wd, wrh, wrl, br, seq_len)

    tile = min(MOE_TILE, x.shape[0])
    nb, bfirst, bs, bv, idx, gs = _route(rt, tile)
    f = _moe_call(nb, bfirst, bs, bv, idx, gs, h2, p['w_gate_up'][l].astype(BF16),
                  p['b_gate_up'][l].reshape(N_EXPERTS, 1, -1), p['w_down'][l].astype(BF16),
                  p['b_down'][l].reshape(N_EXPERTS, 1, -1), tile)
    return x1, (f, g2, row(p['post_ffn_g'][l]))


def _trunk(x, c, p, nseq_a):
    nseq, seq_len, d = x.shape
    depth = p['w_in'].shape[0]
    mod = _mod_call(c, p['w_mod'], p['b_mod'])
    tabs = _rope_tables(seq_len)
    x = x.reshape(nseq * seq_len, d)
    pending = None
    for l in range(depth):
        x, pending = _layer(x, pending, mod[l], p, l, tabs, seq_len)
    ya, yb = _ffn_res_call(x, *pending, seq_len, nseq_a * seq_len)
    return ya.reshape(nseq_a, seq_len, d), yb.reshape(nseq - nseq_a, seq_len, d)


def kernel(x_prompt, x_sample, c_prompt, c_sample, w_mod, b_mod, pre_mix_g, post_mix_g, pre_ffn_g, post_ffn_g, w_in, w_out, ssd_conv_w, ssd_conv_b, ssd_a_log, ssd_dt_bias, ssd_d, ssd_norm_g, swa_sink, mlstm_i_bias, mlstm_f_bias, mlstm_norm_g, w_router, b_router, w_gate_up, b_gate_up, w_down, b_down):
    p = dict(w_mod=w_mod, b_mod=b_mod, pre_mix_g=pre_mix_g, post_mix_g=post_mix_g, pre_ffn_g=pre_ffn_g,
             post_ffn_g=post_ffn_g, w_in=w_in, w_out=w_out, ssd_conv_w=ssd_conv_w, ssd_conv_b=ssd_conv_b,
             ssd_a_log=ssd_a_log, ssd_dt_bias=ssd_dt_bias, ssd_d=ssd_d, ssd_norm_g=ssd_norm_g,
             swa_sink=swa_sink, mlstm_i_bias=mlstm_i_bias, mlstm_f_bias=mlstm_f_bias,
             mlstm_norm_g=mlstm_norm_g, w_router=w_router, b_router=b_router, w_gate_up=w_gate_up,
             b_gate_up=b_gate_up, w_down=w_down, b_down=b_down)
    nb = x_prompt.shape[0]
    x = jnp.concatenate([x_prompt, x_sample], axis=0)
    c = jnp.concatenate([c_prompt, c_sample], axis=0)
    return _trunk(x, c, p, nb)
```

```python
import functools
import math

import jax
import jax.numpy as jnp
import numpy as np
from jax import lax
from jax.experimental import pallas as pl
from jax.experimental.pallas import tpu as pltpu

F32 = jnp.float32
BF16 = jnp.bfloat16

D_MODEL = 1024
HEAD_DIM = 64
ROPE_THETA = 500000.0
NORM_EPS = 1e-6
CHUNK = 128

SSD_HEADS = 6
SSD_INNER = 384
SSD_STATE = 64
SSD_CONV = 5
SSD_XBC = 640
SWA_Q_HEADS = 6
SWA_SIDE = 128
SWA_HEAD_ORDER = (0, 3, 1, 4, 2, 5)
ML_HEADS = 4
ML_DIM = 96
ML_PAD = 128
DIL_CONFIGS = ((128, 1), (512, 4), (2048, 16))
DIL_HEADS = 6
DIL_SIDE = 64
DIL_BATCH = 4
N_EXPERTS = 32
TOP_K = 4
D_EXPERT = 1024
SWIGLU_LIMIT = 7.0
SWIGLU_ALPHA = 1.702

IN_SPLITS = (384, 640, 12, 384, 128, 128, 384, 384, 384, 384, 8, 8, 1152, 384, 384)

COL_A = 0
COL_G = 1024
COL_B = 1152
COL_C = 1792
COL_D = 3840
COL_END = 5760
LANE_IG = 16
LANE_FG = 24

NEG = -1e30
VMEM_LIMIT = 56 * 1024 * 1024
OUT_SUB = 256
MOE_ROWS = 128
MOE_GROUP = 4
MOE_TILE = 4096


def _cparams(sem):
    return pltpu.CompilerParams(dimension_semantics=sem, vmem_limit_bytes=VMEM_LIMIT)


def _dot(a, b):
    return jnp.dot(a, b, preferred_element_type=F32)


def _dot_nt(a, b):
    return lax.dot_general(a, b, (((1,), (1,)), ((), ())), preferred_element_type=F32)


def _dot_tn(a, b):
    return lax.dot_general(a, b, (((0,), (0,)), ((), ())), preferred_element_type=F32)


def _sigmoid(x):
    return 1.0 / (1.0 + jnp.exp(-x))


def _softplus(x):
    return jnp.maximum(x, 0.0) + jnp.log(1.0 + jnp.exp(-jnp.abs(x)))


def _prefix_rows(x):
    row = lax.broadcasted_iota(jnp.int32, x.shape, 0)
    s = 1
    while s < x.shape[0]:
        x = x + jnp.where(row >= s, pltpu.roll(x, s, 0), 0.0)
        s *= 2
    return x


def _mod_kernel(c_ref, w_ref, b_ref, o_ref):
    c = c_ref[...]
    s = c * _sigmoid(c)
    o_ref[0] = jnp.dot(s, w_ref[0], preferred_element_type=F32,
                       precision=lax.Precision.HIGHEST) + b_ref[0]


def _mod_call(c, w_mod, b_mod):
    nb = c.shape[0]
    depth, d, cols = w_mod.shape
    tn = 1536
    return pl.pallas_call(
        _mod_kernel,
        grid=(depth, cols // tn),
        in_specs=[pl.BlockSpec((nb, d), lambda l, j: (0, 0)),
                  pl.BlockSpec((1, d, tn), lambda l, j: (l, 0, j)),
                  pl.BlockSpec((1, 1, tn), lambda l, j: (l, 0, j))],
        out_specs=pl.BlockSpec((1, nb, tn), lambda l, j: (l, 0, j)),
        out_shape=jax.ShapeDtypeStruct((depth, nb, cols), F32),
        compiler_params=_cparams(("parallel", "parallel")),
        name="mod",
    )(c, w_mod, b_mod.reshape(depth, 1, cols))


def _rope(a, rc, rs1, rs2):
    return a * rc + pltpu.roll(a, 8, 1) * rs1 + pltpu.roll(a, 120, 1) * rs2


def _in_proj_kernel(*refs, fused):
    if fused:
        (x_ref, f_ref, g2_ref, pg_ref, sc_ref, sh_ref, g_ref, w_ref, rc_ref, rs1_ref, rs2_ref,
         oa_ref, og_ref, ob_ref, oc_ref, od_ref, x2_ref) = refs
        f = f_ref[...]
        fms = jnp.mean(f * f, axis=-1, keepdims=True)
        x = x_ref[...] + g2_ref[0] * (f * lax.rsqrt(fms + NORM_EPS) * pg_ref[...])
        x2_ref[...] = x
    else:
        (x_ref, sc_ref, sh_ref, g_ref, w_ref, rc_ref, rs1_ref, rs2_ref,
         oa_ref, og_ref, ob_ref, oc_ref, od_ref) = refs
        x = x_ref[...]
    ms = jnp.mean(x * x, axis=-1, keepdims=True)
    h = x * lax.rsqrt(ms + NORM_EPS) * g_ref[...]
    h = h * (1.0 + sc_ref[0]) + sh_ref[0]
    hb = h.astype(BF16)
    rc, rs1, rs2 = rc_ref[...], rs1_ref[...], rs2_ref[...]

    def mm(c0, width):
        return _dot(hb, w_ref[:, c0:c0 + width])

    def plain(o_ref, col0, dst0, width, scale=None, step=512, add=None):
        for c in range(0, width, step):
            wd = min(step, width - c)
            a = mm(col0 + c, wd)
            if scale is not None:
                a = a * scale
            if add is not None:
                a = a + add
            o_ref[:, dst0 + c:dst0 + c + wd] = a.astype(o_ref.dtype)

    def roped(o_ref, col0, dst0, width, scale):
        for c in range(0, width, 128):
            a = _rope(mm(col0 + c, 128), rc, rs1, rs2)
            if scale is not None:
                a = a * scale
            o_ref[:, dst0 + c:dst0 + c + 128] = a.astype(o_ref.dtype)

    qscale = HEAD_DIM ** -0.5
    plain(oa_ref, COL_A, 0, 1024)
    plain(og_ref, COL_G, 0, 128)
    roped(ob_ref, COL_B, 0, 384, qscale)
    roped(ob_ref, COL_B + 384, 384, 128, None)
    plain(ob_ref, COL_B + 512, 512, 128)
    plain(oc_ref, COL_C, 0, 512)
    plain(oc_ref, COL_C + 512, 512, 512, scale=ML_DIM ** -0.5)
    lane_in_head = lax.broadcasted_iota(jnp.int32, (1, 512), 1) % ML_PAD
    plain(oc_ref, COL_C + 1024, 1024, 512, add=jnp.where(lane_in_head == ML_DIM, 1.0, 0.0))
    plain(oc_ref, COL_C + 1536, 1536, 512)
    roped(od_ref, COL_D, 0, 1152, qscale)
    roped(od_ref, COL_D + 1152, 1152, 384, None)
    plain(od_ref, COL_D + 1536, 1536, 384, step=384)


def _in_proj_call(x, sc, sh, g, w, rc, rs1, rs2, seq_len, ffn=None, tm=512):
    n, d = x.shape
    per_seq = seq_len // tm
    row = lambda i: (i, 0)
    seq = lambda i: (i // per_seq, 0, 0)
    pos = lambda i: (i % per_seq, 0)
    const = lambda i: (0, 0)
    widths = (1024, 128, 640, 2048, 1920)
    dtypes = (BF16, F32, BF16, BF16, BF16)
    fused = ffn is not None
    x_specs = [pl.BlockSpec((tm, d), row)]
    x_args = [x]
    out_specs = [pl.BlockSpec((tm, wd), row) for wd in widths]
    out_shape = [jax.ShapeDtypeStruct((n, wd), dt) for wd, dt in zip(widths, dtypes)]
    if fused:
        x_specs += [pl.BlockSpec((tm, d), row), pl.BlockSpec((1, 1, d), seq), pl.BlockSpec((1, d), const)]
        x_args += list(ffn)
        out_specs.append(pl.BlockSpec((tm, d), row))
        out_shape.append(jax.ShapeDtypeStruct((n, d), F32))
    return pl.pallas_call(
        functools.partial(_in_proj_kernel, fused=fused),
        grid=(n // tm,),
        in_specs=x_specs + [pl.BlockSpec((1, 1, d), seq),
                            pl.BlockSpec((1, 1, d), seq),
                            pl.BlockSpec((1, d), const),
                            pl.BlockSpec((d, COL_END), const, pipeline_mode=pl.Buffered(1)),
                            pl.BlockSpec((tm, 128), pos),
                            pl.BlockSpec((tm, 128), pos),
                            pl.BlockSpec((tm, 128), pos)],
        out_specs=out_specs,
        out_shape=out_shape,
        compiler_params=_cparams(("parallel",)),
        name="in_proj",
    )(*x_args, sc, sh, g, w, rc, rs1, rs2)


def _ssd_kernel(ua_ref, ug_ref, cw_ref, cb_ref, alog_ref, dtb_ref, dsk_ref, ng_ref, o_ref,
                xp_ref, xa_ref, cum_ref, dtd_ref, tr_ref, y_ref, y2_ref, sf_ref, sb_ref,
                *, seq_len):
    nc = seq_len // CHUNK
    q = CHUNK
    xp_ref[0:8, :] = jnp.zeros((8, SSD_XBC), F32)
    xp_ref[seq_len + 8:seq_len + 16, :] = jnp.zeros((8, SSD_XBC), F32)
    xp_ref[8:seq_len + 8, :] = ua_ref[:, SSD_INNER:SSD_INNER + SSD_XBC].astype(F32)
    sf_ref[...] = jnp.zeros_like(sf_ref)
    sb_ref[...] = jnp.zeros_like(sb_ref)
    a_row = -jnp.exp(alog_ref[...])
    lane = lax.broadcasted_iota(jnp.int32, (q, 128), 1)

    def prep(c, carry):
        r0 = pl.multiple_of(c * q, q)
        win = xp_ref[pl.ds(r0, q + 16), :]
        conv = cb_ref[...] + cw_ref[0:1, :] * win[6:6 + q]
        for k in range(1, SSD_CONV):
            conv = conv + cw_ref[k:k + 1, :] * win[6 + k:6 + k + q]
        xa_ref[pl.ds(r0, q), :] = conv * _sigmoid(conv)
        dt = _softplus(ug_ref[pl.ds(r0, q), :] + dtb_ref[...])
        dta = dt * a_row
        p = _prefix_rows(dta)
        s = p[q - 1:q, :] - p + dta
        cum = jnp.where(lane < SSD_HEADS, p, s)
        cum_ref[pl.ds(r0, q), :] = cum
        dtd_ref[pl.ds(r0, q), :] = dt
        packed = jnp.where(lane < 16, cum, pltpu.roll(dt, 16, 1))
        tr_ref[c] = packed.T
        return carry

    lax.fori_loop(0, nc, prep, 0)

    ri = lax.broadcasted_iota(jnp.int32, (q, q), 0)
    ci = lax.broadcasted_iota(jnp.int32, (q, q), 1)
    lower_incl = ci <= ri
    lower = ci < ri
    upper = ci > ri

    def main(c, carry):
        r0 = pl.multiple_of(c * q, q)
        cbk = nc - 1 - c
        r1 = pl.multiple_of(cbk * q, q)
        tr = tr_ref[c]
        visits = []
        for rr in (r0, r1):
            xa = xa_ref[pl.ds(rr, q), :]
            bt = xa[:, SSD_INNER:SSD_INNER + 128].T.astype(BF16)
            visits.append(dict(
                xa=xa, cum=cum_ref[pl.ds(rr, q), :], dtd=dtd_ref[pl.ds(rr, q), :],
                bgt=[bt[g * 64:(g + 1) * 64, :] for g in range(2)],
                bg=[xa[:, SSD_INNER + g * 64:SSD_INNER + (g + 1) * 64].astype(BF16) for g in range(2)],
                cg=[xa[:, SSD_INNER + 128 + g * 64:SSD_INNER + 128 + (g + 1) * 64].astype(BF16)
                    for g in range(2)]))
        fw, bw = visits
        heads = range(SSD_HEADS)
        cbs = [_dot_nt(fw['cg'][g], fw['bg'][g]) for g in range(2)]
        ws = []
        for h in heads:
            cum = fw['cum']
            cf_col, rb_col = cum[:, h:h + 1], cum[:, 6 + h:7 + h]
            cf_row, rb_row = tr[h:h + 1, :], tr[6 + h:7 + h, :]
            dtf_row, dtb_row = tr[16 + h:17 + h, :], tr[22 + h:23 + h, :]
            arg = jnp.where(lower_incl, cf_col - cf_row, rb_col - rb_row)
            fac = jnp.where(lower, dtf_row, jnp.where(upper, dtb_row, dtf_row + dtb_row))
            ws.append((cbs[h // 3] * jnp.exp(arg) * fac).astype(BF16))
        xf = [fw['xa'][:, h * 64:(h + 1) * 64] for h in heads]
        xb = [bw['xa'][:, h * 64:(h + 1) * 64] for h in heads]
        sfs = [sf_ref[h] for h in heads]
        sbs = [sb_ref[h] for h in heads]
        y_intra = [_dot(ws[h], xf[h].astype(BF16)) for h in heads]
        y_carry_f = [_dot(fw['cg'][h // 3], sfs[h].astype(BF16)) for h in heads]
        y_carry_b = [_dot(bw['cg'][h // 3], sbs[h].astype(BF16)) for h in heads]
        y_ref[pl.ds(r0, q), :] = jnp.concatenate(
            [y_intra[h] + y_carry_f[h] * jnp.exp(fw['cum'][:, h:h + 1]) for h in heads], axis=1)
        y2_ref[pl.ds(r1, q), :] = jnp.concatenate(
            [y_carry_b[h] * jnp.exp(bw['cum'][:, 6 + h:7 + h]) for h in heads], axis=1)
        xws, decays = [], []
        for h in heads:
            tf = fw['cum'][q - 1:q, h:h + 1]
            wcol = jnp.exp(tf - fw['cum'][:, h:h + 1]) * fw['dtd'][:, h:h + 1]
            xws.append((xf[h] * wcol).astype(BF16))
            decays.append(jnp.exp(tf))
        for h in heads:
            tb = bw['cum'][0:1, 6 + h:7 + h]
            wcol = jnp.exp(tb - bw['cum'][:, 6 + h:7 + h]) * bw['dtd'][:, 6 + h:7 + h]
            xws.append((xb[h] * wcol).astype(BF16))
            decays.append(jnp.exp(tb))
        for h in heads:
            sf_ref[h] = sfs[h] * decays[h] + _dot(fw['bgt'][h // 3], xws[h])
        for h in heads:
            sb_ref[h] = sbs[h] * decays[6 + h] + _dot(bw['bgt'][h // 3], xws[6 + h])
        return carry

    lax.fori_loop(0, nc, main, 0)

    def fin(c, carry):
        r0 = pl.multiple_of(c * q, q)
        xs = xa_ref[pl.ds(r0, q), 0:SSD_INNER]
        y = y_ref[pl.ds(r0, q), :] + y2_ref[pl.ds(r0, q), :] + xs * dsk_ref[...]
        z = ua_ref[pl.ds(r0, q), 0:SSD_INNER].astype(F32)
        v = y * (z * _sigmoid(z))
        ms = jnp.mean(v * v, axis=-1, keepdims=True)
        o_ref[pl.ds(r0, q), :] = (v * lax.rsqrt(ms + NORM_EPS) * ng_ref[...]).astype(o_ref.dtype)
        return carry

    lax.fori_loop(0, nc, fin, 0)


def _ssd_call(ua, ug, cw, cb, alog, dtb, dsk, ng, seq_len):
    n = ua.shape[0]
    nseq = n // seq_len
    nc = seq_len // CHUNK
    seq = lambda b: (b, 0)
    const = lambda b: (0, 0)
    return pl.pallas_call(
        functools.partial(_ssd_kernel, seq_len=seq_len),
        grid=(nseq,),
        in_specs=[pl.BlockSpec((seq_len, 1024), seq),
                  pl.BlockSpec((seq_len, 128), seq),
                  pl.BlockSpec((8, SSD_XBC), const),
                  pl.BlockSpec((1, SSD_XBC), const),
                  pl.BlockSpec((1, 128), const),
                  pl.BlockSpec((1, 128), const),
                  pl.BlockSpec((1, SSD_INNER), const),
                  pl.BlockSpec((1, SSD_INNER), const)],
        out_specs=pl.BlockSpec((seq_len, SSD_INNER), seq),
        out_shape=jax.ShapeDtypeStruct((n, SSD_INNER), BF16),
        scratch_shapes=[pltpu.VMEM((seq_len + 16, SSD_XBC), F32),
                        pltpu.VMEM((seq_len, SSD_XBC), F32),
                        pltpu.VMEM((seq_len, 128), F32),
                        pltpu.VMEM((seq_len, 128), F32),
                        pltpu.VMEM((nc, 128, CHUNK), F32),
                        pltpu.VMEM((seq_len, SSD_INNER), F32),
                        pltpu.VMEM((seq_len, SSD_INNER), F32),
                        pltpu.VMEM((SSD_HEADS, SSD_STATE, HEAD_DIM), F32),
                        pltpu.VMEM((SSD_HEADS, SSD_STATE, HEAD_DIM), F32)],
        compiler_params=_cparams(("parallel",)),
        name="ssd",
    )(ua, ug, cw, cb, alog, dtb, dsk, ng)


def _swa_kernel(sink_ref, u_ref, o_ref, kp_ref, vp_ref, bias_ref, *, seq_len):
    q = CHUNK
    nb = seq_len // q
    zeros = jnp.zeros((q, 128), BF16)
    kp_ref[0:q, :] = zeros
    vp_ref[0:q, :] = zeros
    kp_ref[seq_len + q:seq_len + 2 * q, :] = zeros
    vp_ref[seq_len + q:seq_len + 2 * q, :] = zeros
    kp_ref[q:seq_len + q, :] = u_ref[:, 384:512]
    vp_ref[q:seq_len + q, :] = u_ref[:, 512:640]
    row = lax.broadcasted_iota(jnp.int32, (3 * q, 3 * q), 0) & (q - 1)
    col = lax.broadcasted_iota(jnp.int32, (3 * q, 3 * q), 1)
    bias_ref[...] = jnp.where(jnp.abs(col - q - row) <= SWA_SIDE, 0.0, NEG)
    rgrp = lax.broadcasted_iota(jnp.int32, (3 * q, 1), 0) // q
    col1 = lax.broadcasted_iota(jnp.int32, (1, 3 * q), 1)
    lane1 = lax.broadcasted_iota(jnp.int32, (1, 128), 1)
    lo_f = jnp.where(lane1 < HEAD_DIM, 1.0, 0.0)
    hi_f = 1.0 - lo_f
    halves = ((lo_f.astype(BF16), lo_f), (hi_f.astype(BF16), hi_f))

    def body(qb, carry):
        q0 = pl.multiple_of(qb * q, q)
        qs = jnp.concatenate([u_ref[pl.ds(q0, q), j * 128:(j + 1) * 128] for j in range(3)], axis=0)
        kw = kp_ref[pl.ds(q0, 3 * q), :]
        vw = vp_ref[pl.ds(q0, 3 * q), :]
        inside = (col1 >= q - q0) & (col1 < seq_len + q - q0)
        bias = bias_ref[...] + jnp.where(inside, 0.0, NEG)
        o = None
        for hk, (sel_b, sel_f) in enumerate(halves):
            s = _dot_nt(qs * sel_b, kw) + bias
            sk = jnp.where(rgrp == 0, sink_ref[3 * hk],
                           jnp.where(rgrp == 1, sink_ref[3 * hk + 1], sink_ref[3 * hk + 2]))
            m = jnp.maximum(jnp.max(s, axis=-1, keepdims=True), sk)
            p = jnp.exp(s - m)
            l = jnp.sum(p, axis=-1, keepdims=True) + jnp.exp(sk - m)
            oh = (_dot(p.astype(BF16), vw) / l) * sel_f
            o = oh if o is None else o + oh
        o_ref[pl.ds(q0, q), :] = jnp.concatenate(
            [o[j * q:(j + 1) * q, :] for j in range(3)], axis=1).astype(o_ref.dtype)
        return carry

    lax.fori_loop(0, nb, body, 0, unroll=2)


def _swa_call(ub, sink, seq_len):
    n = ub.shape[0]
    nseq = n // seq_len
    return pl.pallas_call(
        functools.partial(_swa_kernel, seq_len=seq_len),
        grid=(nseq,),
        in_specs=[pl.BlockSpec(memory_space=pltpu.SMEM),
                  pl.BlockSpec((seq_len, 640), lambda b: (b, 0))],
        out_specs=pl.BlockSpec((seq_len, 384), lambda b: (b, 0)),
        out_shape=jax.ShapeDtypeStruct((n, 384), BF16),
        scratch_shapes=[pltpu.VMEM((seq_len + 2 * CHUNK, 128), BF16),
                        pltpu.VMEM((seq_len + 2 * CHUNK, 128), BF16),
                        pltpu.VMEM((3 * CHUNK, 3 * CHUNK), F32)],
        compiler_params=_cparams(("parallel",)),
        name="swa",
    )(sink, ub)


def _mlstm_kernel(uc_ref, ug_ref, gb_ref, ng_ref, o_ref,
                  gcol_ref, gtr_ref, hacc_ref, ct_ref, m_ref, mask_ref, *, seq_len):
    q = CHUNK
    nc = seq_len // q
    lane = lax.broadcasted_iota(jnp.int32, (q, 128), 1)
    ct_ref[...] = jnp.zeros_like(ct_ref)
    m_ref[...] = jnp.zeros_like(m_ref)
    hacc_ref[...] = jnp.zeros_like(hacc_ref)

    def prep(c, carry):
        r0 = pl.multiple_of(c * q, q)
        raw = ug_ref[pl.ds(r0, q), :] + gb_ref[...]
        lf = jnp.minimum(raw, 0.0) - jnp.log(1.0 + jnp.exp(-jnp.abs(raw)))
        p = _prefix_rows(lf)
        s = p[q - 1:q, :] - p + lf
        g = jnp.where(lane < LANE_FG, raw, jnp.where(lane < LANE_FG + ML_HEADS, p, s))
        gcol_ref[pl.ds(r0, q), :] = g
        gtr_ref[c] = g.T
        return carry

    lax.fori_loop(0, nc, prep, 0)

    ri = lax.broadcasted_iota(jnp.int32, (q, q), 0)
    ci = lax.broadcasted_iota(jnp.int32, (q, q), 1)
    mask_ref[0] = jnp.where(ci <= ri, 0.0, NEG)
    mask_ref[1] = jnp.where(ci >= ri, 0.0, NEG)
    keep_lanes = jnp.where(lax.broadcasted_iota(jnp.int32, (1, ML_PAD), 1) < ML_DIM, 1.0, 0.0)

    def main(c, carry):
        units = []
        for dirn in range(2):
            ck = c if dirn == 0 else nc - 1 - c
            r0 = pl.multiple_of(ck * q, q)
            g = gcol_ref[pl.ds(r0, q), :]
            gt = gtr_ref[ck]
            for h in range(ML_HEADS):
                s_idx = dirn * ML_HEADS + h
                li, lb = LANE_IG + s_idx, LANE_FG + s_idx
                edge = q - 1 if dirn == 0 else 0
                units.append(dict(
                    dirn=dirn, h=h, r0=r0, s_idx=s_idx,
                    b_col=g[:, lb:lb + 1], b_row=gt[lb:lb + 1, :], i_col=g[:, li:li + 1],
                    i_row=gt[li:li + 1, :], tot=g[edge:edge + 1, lb:lb + 1],
                    m_st=m_ref[s_idx][0:1, 0:1], ct=ct_ref[s_idx],
                    qh=uc_ref[pl.ds(r0, q), h * ML_PAD:(h + 1) * ML_PAD],
                    kh=uc_ref[pl.ds(r0, q), 512 + h * ML_PAD:512 + (h + 1) * ML_PAD],
                    vh=uc_ref[pl.ds(r0, q), 1024 + h * ML_PAD:1024 + (h + 1) * ML_PAD]))
        for u in units:
            u['logd'] = u['b_col'] - u['b_row'] + u['i_row'] + mask_ref[u['dirn']]
            u['m_inter'] = u['b_col'] + u['m_st']
        for u in units:
            u['m_t'] = jnp.maximum(jnp.max(u['logd'], axis=-1, keepdims=True), u['m_inter'])
        for u in units:
            u['qk'] = _dot_nt(u['qh'], u['kh'])
            u['qc'] = _dot(u['qh'], u['ct'].astype(BF16))
        for u in units:
            u['sm'] = (u['qk'] * jnp.exp(u['logd'] - u['m_t'])).astype(BF16)
            u['inter'] = jnp.exp(u['m_inter'] - u['m_t'])
        for u in units:
            u['num'] = _dot(u['sm'], u['vh']) + u['inter'] * u['qc']
        for u in units:
            den = u['num'][:, ML_DIM:ML_DIM + 1]
            hh = u['num'] / jnp.maximum(jnp.abs(den), jnp.exp(-u['m_t'])) * keep_lanes
            hacc_ref[pl.ds(u['r0'], q), u['h'] * ML_PAD:(u['h'] + 1) * ML_PAD] += hh
        for u in units:
            tot, m_st = u['tot'], u['m_st']
            m_new = jnp.maximum(tot + m_st,
                                jnp.max(tot - u['b_row'] + u['i_row'], axis=-1, keepdims=True))
            wk_col = jnp.exp(tot - u['b_col'] + u['i_col'] - m_new)
            u['keep'] = jnp.exp(tot + m_st - m_new)
            u['vw'] = (u['vh'].astype(F32) * wk_col).astype(BF16)
            m_ref[u['s_idx']] = jnp.broadcast_to(m_new, (8, 128))
        for u in units:
            ct_ref[u['s_idx']] = u['keep'] * u['ct'] + _dot_tn(u['kh'], u['vw'])
        return carry

    lax.fori_loop(0, nc, main, 0)

    def fin(c, carry):
        r0 = pl.multiple_of(c * q, q)
        outs = []
        for h in range(ML_HEADS):
            hs = hacc_ref[pl.ds(r0, q), h * ML_PAD:(h + 1) * ML_PAD]
            ms = jnp.sum(hs * hs, axis=-1, keepdims=True) * (1.0 / ML_DIM)
            hn = hs * lax.rsqrt(ms + NORM_EPS) * ng_ref[:, h * ML_PAD:(h + 1) * ML_PAD]
            og = uc_ref[pl.ds(r0, q), 1536 + h * ML_PAD:1536 + (h + 1) * ML_PAD].astype(F32)
            outs.append(hn * _sigmoid(og))
        o_ref[pl.ds(r0, q), :] = jnp.concatenate(outs, axis=1).astype(o_ref.dtype)
        return carry

    lax.fori_loop(0, nc, fin, 0)


def _mlstm_call(uc, ug, gb, ng, seq_len):
    n = uc.shape[0]
    nseq = n // seq_len
    nc = seq_len // CHUNK
    seq = lambda b: (b, 0)
    const = lambda b: (0, 0)
    width = ML_HEADS * ML_PAD
    return pl.pallas_call(
        functools.partial(_mlstm_kernel, seq_len=seq_len),
        grid=(nseq,),
        in_specs=[pl.BlockSpec((seq_len, 4 * width), seq),
                  pl.BlockSpec((seq_len, 128), seq),
                  pl.BlockSpec((1, 128), const),
                  pl.BlockSpec((1, width), const)],
        out_specs=pl.BlockSpec((seq_len, width), seq),
        out_shape=jax.ShapeDtypeStruct((n, width), BF16),
        scratch_shapes=[pltpu.VMEM((seq_len, 128), F32),
                        pltpu.VMEM((nc, 128, CHUNK), F32),
                        pltpu.VMEM((seq_len, width), F32),
                        pltpu.VMEM((2 * ML_HEADS, ML_PAD, ML_PAD), F32),
                        pltpu.VMEM((2 * ML_HEADS, 8, 128), F32),
                        pltpu.VMEM((2, CHUNK, CHUNK), F32)],
        compiler_params=_cparams(("parallel",)),
        name="mlstm",
    )(uc, ug, gb, ng)


def _attend_units(units, lo_b, hi_b, lo_f, hi_f):
    n = units[0][0].shape[0]
    ss = []
    for q, kw, _, bias in units:
        s2 = _dot_nt(jnp.concatenate([q * lo_b, q * hi_b], axis=0), kw)
        ss.append(jnp.concatenate([s2[0:n] + bias, s2[n:2 * n] + bias], axis=0))
    ms = [jnp.max(s, axis=-1, keepdims=True) for s in ss]
    ps = [jnp.exp(s - m) for s, m in zip(ss, ms)]
    ls = [jnp.sum(p, axis=-1, keepdims=True) for p in ps]
    os_ = [_dot(p.astype(BF16), u[2]) / l for p, u, l in zip(ps, units, ls)]
    res = []
    for o2, m, l in zip(os_, ms, ls):
        lse2 = m + jnp.log(l)
        res.append((o2[0:n] * lo_f + o2[n:2 * n] * hi_f, lse2[0:n] * lo_f + lse2[n:2 * n] * hi_f))
    return res


def _dil_kernel(q0_ref, q1_ref, q2_ref, k_ref, v_ref, o_ref,
                qf_ref, kf_ref, vf_ref, kc0_ref, vc0_ref, kc1_ref, vc1_ref, og_ref, lg_ref,
                qc2_ref, kc2_ref, vc2_ref, og2_ref, lg2_ref, band_ref, band2_ref, *, seq_len):
    qr = CHUNK
    pad = DIL_SIDE
    d1, d2 = DIL_CONFIGS[1][1], DIL_CONFIGS[2][1]
    n1, n2 = seq_len // d1, seq_len // d2
    assert n2 == qr and n1 % qr == 0
    qf_ref[0] = q1_ref[...].astype(F32)
    qf_ref[1] = q2_ref[...].astype(F32)
    kf_ref[...] = k_ref[...].astype(F32)
    vf_ref[...] = v_ref[...].astype(F32)
    zeros = jnp.zeros((pad, 128), BF16)
    kc0_ref[0:pad, :] = zeros
    vc0_ref[0:pad, :] = zeros
    kc0_ref[pad + seq_len:2 * pad + seq_len, :] = zeros
    vc0_ref[pad + seq_len:2 * pad + seq_len, :] = zeros
    kc0_ref[pad:pad + seq_len, :] = k_ref[...]
    vc0_ref[pad:pad + seq_len, :] = v_ref[...]
    for r in range(d1):
        kc1_ref[r, 0:pad, :] = zeros
        vc1_ref[r, 0:pad, :] = zeros
        kc1_ref[r, pad + n1:2 * pad + n1, :] = zeros
        vc1_ref[r, pad + n1:2 * pad + n1, :] = zeros
        kc1_ref[r, pad:pad + n1, :] = kf_ref[pl.ds(r, n1, stride=d1), :].astype(BF16)
        vc1_ref[r, pad:pad + n1, :] = vf_ref[pl.ds(r, n1, stride=d1), :].astype(BF16)

    row = lax.broadcasted_iota(jnp.int32, (qr, 2 * qr), 0)
    col = lax.broadcasted_iota(jnp.int32, (qr, 2 * qr), 1)
    band_ref[...] = jnp.where(jnp.abs(col - pad - row) <= DIL_SIDE, 0.0, NEG)
    row2 = lax.broadcasted_iota(jnp.int32, (qr, qr), 0)
    col2 = lax.broadcasted_iota(jnp.int32, (qr, qr), 1)
    band2_ref[...] = jnp.where(jnp.abs(col2 - row2) <= DIL_SIDE, 0.0, NEG)
    lane1 = lax.broadcasted_iota(jnp.int32, (1, 128), 1)
    col1 = lax.broadcasted_iota(jnp.int32, (1, 2 * qr), 1)
    lo_f = jnp.where(lane1 < HEAD_DIM, 1.0, 0.0)
    hi_f = 1.0 - lo_f
    sel = (lo_f.astype(BF16), hi_f.astype(BF16), lo_f, hi_f)

    def window_bias(u0, n):
        inside = (col1 >= pad - u0) & (col1 < n + pad - u0)
        return band_ref[...] + jnp.where(inside, 0.0, NEG)

    def batch0(b4, carry):
        units, starts = [], []
        for i in range(DIL_BATCH):
            u0 = pl.multiple_of((b4 * DIL_BATCH + i) * qr, qr)
            starts.append(u0)
            units.append((q0_ref[pl.ds(u0, qr), :], kc0_ref[pl.ds(u0, 2 * qr), :],
                          vc0_ref[pl.ds(u0, 2 * qr), :], window_bias(u0, seq_len)))
        for u0, (o, lse) in zip(starts, _attend_units(units, *sel)):
            og_ref[0, pl.ds(u0, qr), :] = o
            lg_ref[0, pl.ds(u0, qr), :] = lse
        return carry

    lax.fori_loop(0, seq_len // qr // DIL_BATCH, batch0, 0)

    nb1 = n1 // qr
    assert nb1 == DIL_BATCH

    def batch1(r, carry):
        units, rowsl = [], []
        for i in range(DIL_BATCH):
            u0 = i * qr
            rows = pl.ds(r + u0 * d1, qr, stride=d1)
            rowsl.append(rows)
            units.append((qf_ref[0, rows, :].astype(BF16), kc1_ref[r, u0:u0 + 2 * qr, :],
                          vc1_ref[r, u0:u0 + 2 * qr, :], window_bias(u0, n1)))
        for rows, (o, lse) in zip(rowsl, _attend_units(units, *sel)):
            og_ref[1, rows, :] = o
            lg_ref[1, rows, :] = lse
        return carry

    lax.fori_loop(0, d1, batch1, 0)

    for r in range(d2):
        rows = pl.ds(r, qr, stride=d2)
        qc2_ref[r] = qf_ref[1, rows, :].astype(BF16)
        kc2_ref[r] = kf_ref[rows, :].astype(BF16)
        vc2_ref[r] = vf_ref[rows, :].astype(BF16)

    def batch2(b4, carry):
        rs = [b4 * DIL_BATCH + i for i in range(DIL_BATCH)]
        units = [(qc2_ref[r], kc2_ref[r], vc2_ref[r], band2_ref[...]) for r in rs]
        for r, (o, lse) in zip(rs, _attend_units(units, *sel)):
            og2_ref[r] = o
            lg2_ref[r] = lse
        return carry

    lax.fori_loop(0, d2 // DIL_BATCH, batch2, 0)
    for r in range(d2):
        rows = pl.ds(r, qr, stride=d2)
        og_ref[2, rows, :] = og2_ref[r]
        lg_ref[2, rows, :] = lg2_ref[r]

    def fin(c, carry):
        r0 = pl.multiple_of(c * qr, qr)
        rows = pl.ds(r0, qr)
        l0, l1, l2 = lg_ref[0, rows, :], lg_ref[1, rows, :], lg_ref[2, rows, :]
        m = jnp.maximum(jnp.maximum(l0, l1), l2)
        w0, w1, w2 = jnp.exp(l0 - m), jnp.exp(l1 - m), jnp.exp(l2 - m)
        y = (og_ref[0, rows, :] * w0 + og_ref[1, rows, :] * w1 + og_ref[2, rows, :] * w2)
        o_ref[rows, :] = (y / (w0 + w1 + w2)).astype(o_ref.dtype)
        return carry

    lax.fori_loop(0, seq_len // qr, fin, 0, unroll=2)


def _dil_call(ud, seq_len):
    n = ud.shape[0]
    nseq = n // seq_len
    blk = (seq_len, 128)
    d1, d2 = DIL_CONFIGS[1][1], DIL_CONFIGS[2][1]
    n1 = seq_len // d1

    def col(off):
        return lambda b, p: (b, off + p)

    return pl.pallas_call(
        functools.partial(_dil_kernel, seq_len=seq_len),
        grid=(nseq, 3),
        in_specs=[pl.BlockSpec(blk, col(0)), pl.BlockSpec(blk, col(3)), pl.BlockSpec(blk, col(6)),
                  pl.BlockSpec(blk, col(9)), pl.BlockSpec(blk, col(12))],
        out_specs=pl.BlockSpec(blk, col(0)),
        out_shape=jax.ShapeDtypeStruct((n, 384), BF16),
        scratch_shapes=[pltpu.VMEM((2, seq_len, 128), F32),
                        pltpu.VMEM((seq_len, 128), F32),
                        pltpu.VMEM((seq_len, 128), F32),
                        pltpu.VMEM((seq_len + 2 * DIL_SIDE, 128), BF16),
                        pltpu.VMEM((seq_len + 2 * DIL_SIDE, 128), BF16),
                        pltpu.VMEM((d1, n1 + 2 * DIL_SIDE, 128), BF16),
                        pltpu.VMEM((d1, n1 + 2 * DIL_SIDE, 128), BF16),
                        pltpu.VMEM((3, seq_len, 128), F32),
                        pltpu.VMEM((3, seq_len, 128), F32)]
                       + [pltpu.VMEM((d2, CHUNK, 128), BF16)] * 3
                       + [pltpu.VMEM((d2, CHUNK, 128), F32)] * 2
                       + [pltpu.VMEM((CHUNK, 2 * CHUNK), F32), pltpu.VMEM((CHUNK, CHUNK), F32)],
        compiler_params=_cparams(("parallel", "parallel")),
        name="dil",
    )(ud, ud, ud, ud, ud)


def _out_proj_kernel(ya_ref, yb_ref, yc_ref, yd_ref, x_ref, g1_ref, sc_ref, sh_ref,
                     pg_ref, fg_ref, wa_ref, wb_ref, wc_ref, wd_ref, wrh_ref, wrl_ref, br_ref,
                     x1_ref, h2_ref, rt_ref):
    for r0 in range(0, x_ref.shape[0], OUT_SUB):
        rs = slice(r0, r0 + OUT_SUB)
        y = (_dot(ya_ref[rs, :], wa_ref[...]) + _dot(yb_ref[rs, :], wb_ref[...])
             + _dot(yc_ref[rs, :], wc_ref[...]) + _dot(yd_ref[rs, :], wd_ref[...]))
        ms = jnp.mean(y * y, axis=-1, keepdims=True)
        yn = y * lax.rsqrt(ms + NORM_EPS) * pg_ref[...]
        x1 = x_ref[rs, :] + g1_ref[0] * yn
        x1_ref[rs, :] = x1
        ms2 = jnp.mean(x1 * x1, axis=-1, keepdims=True)
        h2 = x1 * lax.rsqrt(ms2 + NORM_EPS) * fg_ref[...]
        h2 = h2 * (1.0 + sc_ref[0]) + sh_ref[0]
        half = h2.shape[1] // 2
        lo = lax.bitcast_convert_type(h2[:, :half].astype(BF16).astype(F32), jnp.uint32)
        hi = lax.bitcast_convert_type(h2[:, half:].astype(BF16).astype(F32), jnp.uint32)
        h2_ref[rs, :] = (lo >> 16) | hi
        h_hi = h2.astype(BF16)
        h_lo = (h2 - h_hi.astype(F32)).astype(BF16)
        logits = (_dot(h_hi, wrh_ref[...]) + _dot(h_lo, wrh_ref[...]) + _dot(h_hi, wrl_ref[...])
                  + br_ref[...])
        lane = lax.broadcasted_iota(jnp.int32, logits.shape, 1)
        vals, idxs = [], []
        for _ in range(TOP_K):
            m = jnp.max(logits, axis=-1, keepdims=True)
            idx = jnp.min(jnp.where(logits == m, lane, 128), axis=-1, keepdims=True)
            vals.append(m)
            idxs.append(idx)
            logits = jnp.where(lane == idx, -3e38, logits)
        es = [jnp.exp(v - vals[0]) for v in vals]
        tot = es[0] + es[1] + es[2] + es[3]
        rt = jnp.zeros(lane.shape, F32)
        for k in range(TOP_K):
            rt = jnp.where(lane == k, idxs[k].astype(F32), rt)
            rt = jnp.where(lane == TOP_K + k, es[k] / tot, rt)
        rt_ref[rs, :] = rt[:, 0:2 * TOP_K]


def _out_proj_call(ya, yb, yc, yd, x, g1, sc, sh, pg, fg, wa, wb, wc, wd, wrh, wrl, br, seq_len, tm=512):
    n, d = x.shape
    per_seq = seq_len // tm
    row = lambda i: (i, 0)
    seq = lambda i: (i // per_seq, 0, 0)
    const = lambda i: (0, 0)

    def full(a):
        return pl.BlockSpec(a.shape, const)

    return pl.pallas_call(
        _out_proj_kernel,
        grid=(n // tm,),
        in_specs=[pl.BlockSpec((tm, ya.shape[1]), row), pl.BlockSpec((tm, yb.shape[1]), row),
                  pl.BlockSpec((tm, yc.shape[1]), row), pl.BlockSpec((tm, yd.shape[1]), row),
                  pl.BlockSpec((tm, d), row),
                  pl.BlockSpec((1, 1, d), seq), pl.BlockSpec((1, 1, d), seq),
                  pl.BlockSpec((1, 1, d), seq),
                  full(pg), full(fg), full(wa), full(wb), full(wc), full(wd), full(wrh), full(wrl), full(br)],
        out_specs=[pl.BlockSpec((tm, d), row), pl.BlockSpec((tm, d // 2), row),
                   pl.BlockSpec((tm, 2 * TOP_K), row)],
        out_shape=[jax.ShapeDtypeStruct((n, d), F32), jax.ShapeDtypeStruct((n, d // 2), jnp.uint32),
                   jax.ShapeDtypeStruct((n, 2 * TOP_K), F32)],
        compiler_params=_cparams(("parallel",)),
        name="out_proj",
    )(ya, yb, yc, yd, x, g1, sc, sh, pg, fg, wa, wb, wc, wd, wrh, wrl, br)


def _moe_slots(tile):
    return tile * TOP_K // MOE_ROWS + N_EXPERTS + 2


def _moe_kernel(nb_ref, bf_ref, bs_ref, bv_ref, idx_ref, gate_ref, h_ref,
                wgu_ref, bgu_ref, wdn_ref, bdn_ref, o_ref, xs0_ref, xs1_ref, ys0_ref, ys1_ref):
    t = pl.program_id(0)
    e = pl.program_id(1)
    nb = nb_ref[t]
    rows = MOE_ROWS
    nslot = _moe_slots(h_ref.shape[0])
    half = h_ref.shape[1]

    def gather(dst_ref, blk):
        base = bs_ref[t * nslot + blk]
        for r in range(rows):
            i = idx_ref[0, 0, base + r]
            dst_ref[r:r + 1, :] = h_ref[pl.ds(i, 1), :]

    def scatter(src_ref, blk, nvalid):
        base = bs_ref[t * nslot + blk]
        i0 = idx_ref[0, 0, base]
        for r0 in range(0, rows, MOE_GROUP):
            ks = range(MOE_GROUP)
            ok = [r0 + k < nvalid for k in ks]
            ii = [jnp.where(ok[k], idx_ref[0, 0, base + r0 + k], i0) for k in ks]
            gg = [jnp.where(ok[k], gate_ref[0, 0, base + r0 + k], 0.0) for k in ks]
            cur = [o_ref[pl.ds(ii[k], 1), :] for k in ks]
            for k in reversed(ks):
                o_ref[pl.ds(ii[k], 1), :] = cur[k] + gg[k] * src_ref[r0 + k:r0 + k + 1, :]

    def expert(xs_ref, ys_ref):
        xu = xs_ref[...]
        x_lo = lax.bitcast_convert_type(xu << 16, F32).astype(BF16)
        x_hi = lax.bitcast_convert_type(xu & jnp.uint32(0xFFFF0000), F32).astype(BF16)
        gu = _dot(x_lo, wgu_ref[0, 0:half, :]) + _dot(x_hi, wgu_ref[0, half:2 * half, :]) + bgu_ref[0]
        gate = jnp.minimum(gu[:, :D_EXPERT], SWIGLU_LIMIT)
        up = jnp.clip(gu[:, D_EXPERT:], -SWIGLU_LIMIT, SWIGLU_LIMIT)
        act = (up + 1.0) * gate * _sigmoid(SWIGLU_ALPHA * gate)
        ab = act.astype(BF16)
        hk = D_EXPERT // 2
        ys_ref[...] = (_dot(ab[:, :hk], wdn_ref[0, 0:hk, :]) + _dot(ab[:, hk:], wdn_ref[0, hk:D_EXPERT, :])
                       + bdn_ref[0])

    @pl.when(e == 0)
    def _():
        o_ref[...] = jnp.zeros_like(o_ref)
        ys1_ref[...] = jnp.zeros_like(ys1_ref)
        gather(xs0_ref, 0)

    bufs = ((xs0_ref, xs1_ref, ys0_ref, ys1_ref), (xs1_ref, xs0_ref, ys1_ref, ys0_ref))

    def block(b, carry):
        prev = jnp.maximum(b - 1, 0)
        nv_prev = jnp.where(b > 0, bv_ref[t * nslot + prev], 0)
        for par, (xs_cur, xs_nxt, ys_cur, ys_prv) in enumerate(bufs):
            @pl.when(b % 2 == par)
            def _(xs_cur=xs_cur, xs_nxt=xs_nxt, ys_cur=ys_cur, ys_prv=ys_prv):
                scatter(ys_prv, prev, nv_prev)
                gather(xs_nxt, b + 1)
                expert(xs_cur, ys_cur)
        return carry

    lax.fori_loop(bf_ref[t * (N_EXPERTS + 1) + e], bf_ref[t * (N_EXPERTS + 1) + e + 1], block, 0)

    for par, (_, _, _, ys_prv) in enumerate(bufs):
        @pl.when((e == N_EXPERTS - 1) & (nb % 2 == par))
        def _(ys_prv=ys_prv):
            scatter(ys_prv, nb - 1, bv_ref[t * nslot + nb - 1])


def _moe_call(nb, bfirst, bs, bv, idx, gates, h2p, wgu, bgu, wdn, bdn, tile):
    n, half = h2p.shape
    d = 2 * half
    ntile = n // tile
    plen = idx.shape[-1]
    wmap = lambda t, e, *_: (e, 0, 0)
    tmap3 = lambda t, e, *_: (t, 0, 0)
    tmap2 = lambda t, e, *_: (t, 0)
    grid_spec = pltpu.PrefetchScalarGridSpec(
        num_scalar_prefetch=4,
        grid=(ntile, N_EXPERTS),
        in_specs=[pl.BlockSpec((1, 1, plen), tmap3, memory_space=pltpu.SMEM),
                  pl.BlockSpec((1, 1, plen), tmap3, memory_space=pltpu.SMEM),
                  pl.BlockSpec((tile, half), tmap2, pipeline_mode=pl.Buffered(1)),
                  pl.BlockSpec((1, d, 2 * D_EXPERT), wmap), pl.BlockSpec((1, 1, 2 * D_EXPERT), wmap),
                  pl.BlockSpec((1, D_EXPERT, d), wmap), pl.BlockSpec((1, 1, d), wmap)],
        out_specs=pl.BlockSpec((tile, d), tmap2, pipeline_mode=pl.Buffered(1)),
        scratch_shapes=[pltpu.VMEM((MOE_ROWS, half), jnp.uint32)] * 2 + [pltpu.VMEM((MOE_ROWS, d), F32)] * 2,
    )
    return pl.pallas_call(
        _moe_kernel,
        grid_spec=grid_spec,
        out_shape=jax.ShapeDtypeStruct((n, d), F32),
        compiler_params=_cparams(("parallel", "arbitrary")),
        name="moe",
    )(nb, bfirst, bs, bv, idx, gates, h2p, wgu, bgu, wdn, bdn)


def _route(rt, tile):
    n = rt.shape[0]
    ntile = n // tile
    npair = tile * TOP_K
    nslot = _moe_slots(tile)
    e = rt[:, :TOP_K].astype(jnp.int32).reshape(ntile, npair)
    g = rt[:, TOP_K:].reshape(ntile, npair)
    tok = jnp.broadcast_to((jnp.arange(npair, dtype=jnp.int32) // TOP_K)[None, :], (ntile, npair))
    _, tok_s, g_s = lax.sort((e, tok, g), dimension=1, is_stable=True, num_keys=1)
    experts = jnp.arange(N_EXPERTS, dtype=jnp.int32)
    counts = jnp.sum(e[:, :, None] == experts, axis=1, dtype=jnp.int32)
    nblocks = (counts + MOE_ROWS - 1) // MOE_ROWS
    bends = jnp.cumsum(nblocks, axis=-1)
    starts = jnp.cumsum(counts, axis=-1) - counts
    nb = bends[:, -1]
    slot = jnp.minimum(jnp.arange(nslot, dtype=jnp.int32)[None, :], nb[:, None] - 1)
    be = jnp.sum(slot[:, :, None] >= bends[:, None, :], axis=-1, dtype=jnp.int32)
    onehot = (be[:, :, None] == experts).astype(jnp.int32)
    pick = lambda v: jnp.sum(onehot * v[:, None, :], axis=-1)
    within = slot - pick(bends - nblocks)
    bs = pick(starts) + within * MOE_ROWS
    bv = jnp.clip(pick(counts) - within * MOE_ROWS, 0, MOE_ROWS)
    padw = ((0, 0), (0, MOE_ROWS))
    idx = jnp.pad(tok_s, padw).reshape(ntile, 1, -1)
    gl = jnp.pad(g_s, padw).reshape(ntile, 1, -1)
    bfirst = jnp.concatenate([jnp.zeros((ntile, 1), jnp.int32), bends], axis=-1)
    flat = lambda v: v.reshape(-1).astype(jnp.int32)
    return flat(nb), flat(bfirst), flat(bs), flat(bv), idx, gl


def _ffn_res_kernel(x_ref, f_ref, g2_ref, pg_ref, oa_ref, ob_ref, *, tiles_a):
    f = f_ref[...]
    ms = jnp.mean(f * f, axis=-1, keepdims=True)
    y = x_ref[...] + g2_ref[0] * (f * lax.rsqrt(ms + NORM_EPS) * pg_ref[...])
    i = pl.program_id(0)

    @pl.when(i < tiles_a)
    def _():
        oa_ref[...] = y

    @pl.when(i >= tiles_a)
    def _():
        ob_ref[...] = y


def _ffn_res_call(x1, f, g2, pg, seq_len, rows_a, tm=1024):
    n, d = x1.shape
    per_seq = seq_len // tm
    tiles_a = rows_a // tm
    row = lambda i: (i, 0)
    return pl.pallas_call(
        functools.partial(_ffn_res_kernel, tiles_a=tiles_a),
        grid=(n // tm,),
        in_specs=[pl.BlockSpec((tm, d), row), pl.BlockSpec((tm, d), row),
                  pl.BlockSpec((1, 1, d), lambda i: (i // per_seq, 0, 0)),
                  pl.BlockSpec((1, d), lambda i: (0, 0))],
        out_specs=[pl.BlockSpec((tm, d), lambda i: (jnp.minimum(i, tiles_a - 1), 0)),
                   pl.BlockSpec((tm, d), lambda i: (jnp.maximum(i - tiles_a, 0), 0))],
        out_shape=[jax.ShapeDtypeStruct((rows_a, d), F32), jax.ShapeDtypeStruct((n - rows_a, d), F32)],
        compiler_params=_cparams(("arbitrary",)),
        name="ffn_res",
    )(x1, f, g2, pg)


def _rope_tables(seq_len):
    half = 8
    inv = 1.0 / (ROPE_THETA ** (jnp.arange(half, dtype=F32) * (2.0 / 16)))
    ang = jnp.arange(seq_len, dtype=F32)[:, None] * inv[None, :]
    cos, sin = jnp.cos(ang), jnp.sin(ang)
    ones = jnp.ones((seq_len, 48), F32)
    zeros8 = jnp.zeros((seq_len, 8), F32)
    zeros48 = jnp.zeros((seq_len, 48), F32)
    rc = jnp.concatenate([cos, cos, ones], axis=1)
    rs1 = jnp.concatenate([zeros8, sin, zeros48], axis=1)
    rs2 = jnp.concatenate([-sin, zeros8, zeros48], axis=1)
    tile2 = lambda a: jnp.concatenate([a, a], axis=1)
    return tile2(rc), tile2(rs1), tile2(rs2)


def _pad_heads(w):
    lead = w.shape[:-1]
    w = w.reshape(lead + (ML_HEADS, ML_DIM))
    w = jnp.pad(w, [(0, 0)] * len(lead) + [(0, 0), (0, ML_PAD - ML_DIM)])
    return w.reshape(lead + (ML_HEADS * ML_PAD,))


def _layout_w_in(w_in):
    d = w_in.shape[0]
    offs = np.cumsum((0,) + IN_SPLITS)
    parts = [w_in[:, offs[i]:offs[i + 1]] for i in range(len(IN_SPLITS))]
    (z, xbc, dt, sq, sk, sv, mq, mk, mv, mo, mi, mf, dq, dk, dv) = parts
    zc = lambda k: jnp.zeros((d, k), w_in.dtype)
    gates = jnp.concatenate([dt, zc(4), mi, mf, zc(128 - 32)], axis=1)
    sq = sq.reshape(d, SWA_Q_HEADS, HEAD_DIM)[:, SWA_HEAD_ORDER, :].reshape(d, -1)
    w = jnp.concatenate([z, xbc, gates, sq, sk, sv,
                         _pad_heads(mq), _pad_heads(mk), _pad_heads(mv), _pad_heads(mo),
                         dq, dk, dv], axis=1)
    assert w.shape[1] == COL_END
    return w.astype(BF16)


def _lane_row(parts, width=128):
    row = jnp.zeros((width,), F32)
    for off, v in parts:
        row = row.at[off:off + v.shape[0]].set(v.astype(F32))
    return row.reshape(1, width)


def _layer(x, pending, mod_l, p, l, tabs, seq_len):
    d = D_MODEL
    nseq = x.shape[0] // seq_len
    sh1, sc1, g1, sh2, sc2, g2 = [mod_l[:, i * d:(i + 1) * d].reshape(nseq, 1, d) for i in range(6)]
    row = lambda v: v.reshape(1, -1).astype(F32)

    outs = _in_proj_call(x, sc1, sh1, row(p['pre_mix_g'][l]), _layout_w_in(p['w_in'][l]),
                         *tabs, seq_len, ffn=pending)
    if pending is not None:
        x = outs[5]
    ua, ug, ub, uc, ud = outs[:5]
    cw = jnp.pad(p['ssd_conv_w'][l], ((0, 8 - SSD_CONV), (0, 0)))
    ya = _ssd_call(ua, ug, cw, row(p['ssd_conv_b'][l]),
                   _lane_row([(0, p['ssd_a_log'][l].reshape(-1))]),
                   _lane_row([(0, p['ssd_dt_bias'][l].reshape(-1))]),
                   row(jnp.repeat(p['ssd_d'][l], HEAD_DIM)), row(p['ssd_norm_g'][l]), seq_len)
    yb = _swa_call(ub, p['swa_sink'][l].astype(F32), seq_len)
    gb = _lane_row([(LANE_IG, p['mlstm_i_bias'][l].reshape(-1)),
                    (LANE_FG, p['mlstm_f_bias'][l].reshape(-1))])
    yc = _mlstm_call(uc, ug, gb, row(_pad_heads(p['mlstm_norm_g'][l])), seq_len)
    yd = _dil_call(ud, seq_len)

    w_out = p['w_out'][l]
    wa = w_out[0:384].astype(BF16)
    wb = w_out[384:768].reshape(SWA_Q_HEADS, HEAD_DIM, d)[SWA_HEAD_ORDER, :, :].reshape(384, d).astype(BF16)
    wc = jnp.pad(w_out[768:1152].reshape(ML_HEADS, ML_DIM, d),
                 ((0, 0), (0, ML_PAD - ML_DIM), (0, 0))).reshape(ML_HEADS * ML_PAD, d).astype(BF16)
    wd = w_out[1152:1536].astype(BF16)
    wr = jnp.pad(p['w_router'][l], ((0, 0), (0, 128 - N_EXPERTS))).astype(F32)
    wrh = wr.astype(BF16)
    wrl = (wr - wrh.astype(F32)).astype(BF16)
    br = jnp.concatenate([p['b_router'][l].astype(F32), jnp.full((128 - N_EXPERTS,), NEG, F32)]).reshape(1, 128)
    x1, h2, rt = _out_proj_call(ya, yb, yc, yd, x, g1, sc2, sh2, row(p['post_mix_g'][l]),
                                row(p['pre_ffn_g'][l]), wa, wb, wc, wd, wrh, wrl, br, seq_len)

    tile = min(MOE_TILE, x.shape[0])
    nb, bfirst, bs, bv, idx, gs = _route(rt, tile)
    f = _moe_call(nb, bfirst, bs, bv, idx, gs, h2, p['w_gate_up'][l].astype(BF16),
                  p['b_gate_up'][l].reshape(N_EXPERTS, 1, -1), p['w_down'][l].astype(BF16),
                  p['b_down'][l].reshape(N_EXPERTS, 1, -1), tile)
    return x1, (f, g2, row(p['post_ffn_g'][l]))


def _trunk(x, c, p, nseq_a):
    nseq, seq_len, d = x.shape
    depth = p['w_in'].shape[0]
    mod = _mod_call(c, p['w_mod'], p['b_mod'])
    tabs = _rope_tables(seq_len)
    x = x.reshape(nseq * seq_len, d)
    pending = None
    for l in range(depth):
        x, pending = _layer(x, pending, mod[l], p, l, tabs, seq_len)
    ya, yb = _ffn_res_call(x, *pending, seq_len, nseq_a * seq_len)
    return ya.reshape(nseq_a, seq_len, d), yb.reshape(nseq - nseq_a, seq_len, d)


def kernel(x_prompt, x_sample, c_prompt, c_sample, w_mod, b_mod, pre_mix_g, post_mix_g, pre_ffn_g, post_ffn_g, w_in, w_out, ssd_conv_w, ssd_conv_b, ssd_a_log, ssd_dt_bias, ssd_d, ssd_norm_g, swa_sink, mlstm_i_bias, mlstm_f_bias, mlstm_norm_g, w_router, b_router, w_gate_up, b_gate_up, w_down, b_down):
    p = dict(w_mod=w_mod, b_mod=b_mod, pre_mix_g=pre_mix_g, post_mix_g=post_mix_g, pre_ffn_g=pre_ffn_g,
             post_ffn_g=post_ffn_g, w_in=w_in, w_out=w_out, ssd_conv_w=ssd_conv_w, ssd_conv_b=ssd_conv_b,
             ssd_a_log=ssd_a_log, ssd_dt_bias=ssd_dt_bias, ssd_d=ssd_d, ssd_norm_g=ssd_norm_g,
             swa_sink=swa_sink, mlstm_i_bias=mlstm_i_bias, mlstm_f_bias=mlstm_f_bias,
             mlstm_norm_g=mlstm_norm_g, w_router=w_router, b_router=b_router, w_gate_up=w_gate_up,
             b_gate_up=b_gate_up, w_down=w_down, b_down=b_down)
    nb = x_prompt.shape[0]
    x = jnp.concatenate([x_prompt, x_sample], axis=0)
    c = jnp.concatenate([c_prompt, c_sample], axis=0)
    return _trunk(x, c, p, nb)
```

```python
import functools
import math

import jax
import jax.numpy as jnp
import numpy as np
from jax import lax
from jax.experimental import pallas as pl
from jax.experimental.pallas import tpu as pltpu

F32 = jnp.float32
BF16 = jnp.bfloat16

D_MODEL = 1024
HEAD_DIM = 64
ROPE_THETA = 500000.0
NORM_EPS = 1e-6
CHUNK = 128

SSD_HEADS = 6
SSD_INNER = 384
SSD_STATE = 64
SSD_CONV = 5
SSD_XBC = 640
SWA_Q_HEADS = 6
SWA_SIDE = 128
SWA_HEAD_ORDER = (0, 3, 1, 4, 2, 5)
ML_HEADS = 4
ML_DIM = 96
ML_PAD = 128
DIL_CONFIGS = ((128, 1), (512, 4), (2048, 16))
DIL_HEADS = 6
DIL_SIDE = 64
DIL_BATCH = 4
N_EXPERTS = 32
TOP_K = 4
D_EXPERT = 1024
SWIGLU_LIMIT = 7.0
SWIGLU_ALPHA = 1.702

IN_SPLITS = (384, 640, 12, 384, 128, 128, 384, 384, 384, 384, 8, 8, 1152, 384, 384)

COL_A = 0
COL_G = 1024
COL_B = 1152
COL_C = 1792
COL_D = 3840
COL_END = 5760
LANE_IG = 16
LANE_FG = 24

NEG = -1e30
VMEM_LIMIT = 56 * 1024 * 1024
OUT_SUB = 256
MOE_ROWS = 128
MOE_GROUP = 4
MOE_TILE = 4096


def _cparams(sem):
    return pltpu.CompilerParams(dimension_semantics=sem, vmem_limit_bytes=VMEM_LIMIT)


def _dot(a, b):
    return jnp.dot(a, b, preferred_element_type=F32)


def _dot_nt(a, b):
    return lax.dot_general(a, b, (((1,), (1,)), ((), ())), preferred_element_type=F32)


def _dot_tn(a, b):
    return lax.dot_general(a, b, (((0,), (0,)), ((), ())), preferred_element_type=F32)


def _sigmoid(x):
    return 1.0 / (1.0 + jnp.exp(-x))


def _softplus(x):
    return jnp.maximum(x, 0.0) + jnp.log(1.0 + jnp.exp(-jnp.abs(x)))


def _prefix_rows(x):
    row = lax.broadcasted_iota(jnp.int32, x.shape, 0)
    s = 1
    while s < x.shape[0]:
        x = x + jnp.where(row >= s, pltpu.roll(x, s, 0), 0.0)
        s *= 2
    return x


def _mod_kernel(c_ref, w_ref, b_ref, o_ref):
    c = c_ref[...]
    s = c * _sigmoid(c)
    o_ref[0] = jnp.dot(s, w_ref[0], preferred_element_type=F32,
                       precision=lax.Precision.HIGHEST) + b_ref[0]


def _mod_call(c, w_mod, b_mod):
    nb = c.shape[0]
    depth, d, cols = w_mod.shape
    tn = 1536
    return pl.pallas_call(
        _mod_kernel,
        grid=(depth, cols // tn),
        in_specs=[pl.BlockSpec((nb, d), lambda l, j: (0, 0)),
                  pl.BlockSpec((1, d, tn), lambda l, j: (l, 0, j)),
                  pl.BlockSpec((1, 1, tn), lambda l, j: (l, 0, j))],
        out_specs=pl.BlockSpec((1, nb, tn), lambda l, j: (l, 0, j)),
        out_shape=jax.ShapeDtypeStruct((depth, nb, cols), F32),
        compiler_params=_cparams(("parallel", "parallel")),
        name="mod",
    )(c, w_mod, b_mod.reshape(depth, 1, cols))


def _rope(a, rc, rs1, rs2):
    return a * rc + pltpu.roll(a, 8, 1) * rs1 + pltpu.roll(a, 120, 1) * rs2


def _in_proj_kernel(*refs, fused, tiles_a):
    if fused:
        (x_ref, f_ref, g2_ref, pg_ref, sc_ref, sh_ref, g_ref, w_ref, rc_ref, rs1_ref, rs2_ref,
         oa_ref, og_ref, ob_ref, oc_ref, od_ref, x2_ref) = refs
        f = f_ref[...]
        fms = jnp.mean(f * f, axis=-1, keepdims=True)
        x = x_ref[...] + g2_ref[0] * (f * lax.rsqrt(fms + NORM_EPS) * pg_ref[...])
        x2_ref[...] = x
    elif tiles_a is not None:
        (xa_ref, xb_ref, sc_ref, sh_ref, g_ref, w_ref, rc_ref, rs1_ref, rs2_ref,
         oa_ref, og_ref, ob_ref, oc_ref, od_ref) = refs
        x = jnp.where(pl.program_id(0) < tiles_a, xa_ref[...], xb_ref[...])
    else:
        (x_ref, sc_ref, sh_ref, g_ref, w_ref, rc_ref, rs1_ref, rs2_ref,
         oa_ref, og_ref, ob_ref, oc_ref, od_ref) = refs
        x = x_ref[...]
    ms = jnp.mean(x * x, axis=-1, keepdims=True)
    h = x * lax.rsqrt(ms + NORM_EPS) * g_ref[...]
    h = h * (1.0 + sc_ref[0]) + sh_ref[0]
    hb = h.astype(BF16)
    rc, rs1, rs2 = rc_ref[...], rs1_ref[...], rs2_ref[...]

    def mm(c0, width):
        return _dot(hb, w_ref[:, c0:c0 + width])

    def plain(o_ref, col0, dst0, width, scale=None, step=512, add=None):
        for c in range(0, width, step):
            wd = min(step, width - c)
            a = mm(col0 + c, wd)
            if scale is not None:
                a = a * scale
            if add is not None:
                a = a + add
            o_ref[:, dst0 + c:dst0 + c + wd] = a.astype(o_ref.dtype)

    def roped(o_ref, col0, dst0, width, scale):
        for c in range(0, width, 128):
            a = _rope(mm(col0 + c, 128), rc, rs1, rs2)
            if scale is not None:
                a = a * scale
            o_ref[:, dst0 + c:dst0 + c + 128] = a.astype(o_ref.dtype)

    qscale = HEAD_DIM ** -0.5
    plain(oa_ref, COL_A, 0, 1024)
    plain(og_ref, COL_G, 0, 128)
    roped(ob_ref, COL_B, 0, 384, qscale)
    roped(ob_ref, COL_B + 384, 384, 128, None)
    plain(ob_ref, COL_B + 512, 512, 128)
    plain(oc_ref, COL_C, 0, 512)
    plain(oc_ref, COL_C + 512, 512, 512, scale=ML_DIM ** -0.5)
    lane_in_head = lax.broadcasted_iota(jnp.int32, (1, 512), 1) % ML_PAD
    plain(oc_ref, COL_C + 1024, 1024, 512, add=jnp.where(lane_in_head == ML_DIM, 1.0, 0.0))
    plain(oc_ref, COL_C + 1536, 1536, 512)
    roped(od_ref, COL_D, 0, 1152, qscale)
    roped(od_ref, COL_D + 1152, 1152, 384, None)
    plain(od_ref, COL_D + 1536, 1536, 384, step=384)


def _split_specs(xa, xb, tm):
    tiles_a = xa.shape[0] // tm
    d = xa.shape[1]
    return tiles_a, [pl.BlockSpec((tm, d), lambda i: (jnp.minimum(i, tiles_a - 1), 0)),
                     pl.BlockSpec((tm, d), lambda i: (jnp.maximum(i - tiles_a, 0), 0))]


def _in_proj_call(x, sc, sh, g, w, rc, rs1, rs2, seq_len, ffn=None, tm=512):
    split = isinstance(x, tuple)
    n = sum(a.shape[0] for a in x) if split else x.shape[0]
    d = D_MODEL
    per_seq = seq_len // tm
    row = lambda i: (i, 0)
    seq = lambda i: (i // per_seq, 0, 0)
    pos = lambda i: (i % per_seq, 0)
    const = lambda i: (0, 0)
    widths = (1024, 128, 640, 2048, 1920)
    dtypes = (BF16, F32, BF16, BF16, BF16)
    fused = ffn is not None
    tiles_a = None
    if split:
        tiles_a, x_specs = _split_specs(x[0], x[1], tm)
        x_args = list(x)
    else:
        x_specs = [pl.BlockSpec((tm, d), row)]
        x_args = [x]
    out_specs = [pl.BlockSpec((tm, wd), row) for wd in widths]
    out_shape = [jax.ShapeDtypeStruct((n, wd), dt) for wd, dt in zip(widths, dtypes)]
    if fused:
        x_specs += [pl.BlockSpec((tm, d), row), pl.BlockSpec((1, 1, d), seq), pl.BlockSpec((1, d), const)]
        x_args += list(ffn)
        out_specs.append(pl.BlockSpec((tm, d), row))
        out_shape.append(jax.ShapeDtypeStruct((n, d), F32))
    return pl.pallas_call(
        functools.partial(_in_proj_kernel, fused=fused, tiles_a=tiles_a),
        grid=(n // tm,),
        in_specs=x_specs + [pl.BlockSpec((1, 1, d), seq),
                            pl.BlockSpec((1, 1, d), seq),
                            pl.BlockSpec((1, d), const),
                            pl.BlockSpec((d, COL_END), const, pipeline_mode=pl.Buffered(1)),
                            pl.BlockSpec((tm, 128), pos),
                            pl.BlockSpec((tm, 128), pos),
                            pl.BlockSpec((tm, 128), pos)],
        out_specs=out_specs,
        out_shape=out_shape,
        compiler_params=_cparams(("parallel",)),
        name="in_proj",
    )(*x_args, sc, sh, g, w, rc, rs1, rs2)


def _ssd_kernel(ua_ref, ug_ref, cw_ref, cb_ref, alog_ref, dtb_ref, dsk_ref, ng_ref, o_ref,
                xp_ref, xa_ref, cum_ref, dtd_ref, tr_ref, y_ref, y2_ref, sf_ref, sb_ref,
                *, seq_len):
    nc = seq_len // CHUNK
    q = CHUNK
    xp_ref[0:8, :] = jnp.zeros((8, SSD_XBC), F32)
    xp_ref[seq_len + 8:seq_len + 16, :] = jnp.zeros((8, SSD_XBC), F32)
    xp_ref[8:seq_len + 8, :] = ua_ref[:, SSD_INNER:SSD_INNER + SSD_XBC].astype(F32)
    sf_ref[...] = jnp.zeros_like(sf_ref)
    sb_ref[...] = jnp.zeros_like(sb_ref)
    a_row = -jnp.exp(alog_ref[...])
    lane = lax.broadcasted_iota(jnp.int32, (q, 128), 1)

    def prep(c, carry):
        r0 = pl.multiple_of(c * q, q)
        win = xp_ref[pl.ds(r0, q + 16), :]
        conv = cb_ref[...] + cw_ref[0:1, :] * win[6:6 + q]
        for k in range(1, SSD_CONV):
            conv = conv + cw_ref[k:k + 1, :] * win[6 + k:6 + k + q]
        xa_ref[pl.ds(r0, q), :] = conv * _sigmoid(conv)
        dt = _softplus(ug_ref[pl.ds(r0, q), :] + dtb_ref[...])
        dta = dt * a_row
        p = _prefix_rows(dta)
        s = p[q - 1:q, :] - p + dta
        cum = jnp.where(lane < SSD_HEADS, p, s)
        cum_ref[pl.ds(r0, q), :] = cum
        dtd_ref[pl.ds(r0, q), :] = dt
        packed = jnp.where(lane < 16, cum, pltpu.roll(dt, 16, 1))
        tr_ref[c] = packed.T
        return carry

    lax.fori_loop(0, nc, prep, 0)

    ri = lax.broadcasted_iota(jnp.int32, (q, q), 0)
    ci = lax.broadcasted_iota(jnp.int32, (q, q), 1)
    lower_incl = ci <= ri
    lower = ci < ri
    upper = ci > ri

    def main(c, carry):
        r0 = pl.multiple_of(c * q, q)
        cbk = nc - 1 - c
        r1 = pl.multiple_of(cbk * q, q)
        tr = tr_ref[c]
        visits = []
        for rr in (r0, r1):
            xa = xa_ref[pl.ds(rr, q), :]
            bt = xa[:, SSD_INNER:SSD_INNER + 128].T.astype(BF16)
            visits.append(dict(
                xa=xa, cum=cum_ref[pl.ds(rr, q), :], dtd=dtd_ref[pl.ds(rr, q), :],
                bgt=[bt[g * 64:(g + 1) * 64, :] for g in range(2)],
                bg=[xa[:, SSD_INNER + g * 64:SSD_INNER + (g + 1) * 64].astype(BF16) for g in range(2)],
                cg=[xa[:, SSD_INNER + 128 + g * 64:SSD_INNER + 128 + (g + 1) * 64].astype(BF16)
                    for g in range(2)]))
        fw, bw = visits
        heads = range(SSD_HEADS)
        cbs = [_dot_nt(fw['cg'][g], fw['bg'][g]) for g in range(2)]
        ws = []
        for h in heads:
            cum = fw['cum']
            cf_col, rb_col = cum[:, h:h + 1], cum[:, 6 + h:7 + h]
            cf_row, rb_row = tr[h:h + 1, :], tr[6 + h:7 + h, :]
            dtf_row, dtb_row = tr[16 + h:17 + h, :], tr[22 + h:23 + h, :]
            arg = jnp.where(lower_incl, cf_col - cf_row, rb_col - rb_row)
            fac = jnp.where(lower, dtf_row, jnp.where(upper, dtb_row, dtf_row + dtb_row))
            ws.append((cbs[h // 3] * jnp.exp(arg) * fac).astype(BF16))
        xf = [fw['xa'][:, h * 64:(h + 1) * 64] for h in heads]
        xb = [bw['xa'][:, h * 64:(h + 1) * 64] for h in heads]
        sfs = [sf_ref[h] for h in heads]
        sbs = [sb_ref[h] for h in heads]
        y_intra = [_dot(ws[h], xf[h].astype(BF16)) for h in heads]
        y_carry_f = [_dot(fw['cg'][h // 3], sfs[h].astype(BF16)) for h in heads]
        y_carry_b = [_dot(bw['cg'][h // 3], sbs[h].astype(BF16)) for h in heads]
        y_ref[pl.ds(r0, q), :] = jnp.concatenate(
            [y_intra[h] + y_carry_f[h] * jnp.exp(fw['cum'][:, h:h + 1]) for h in heads], axis=1)
        y2_ref[pl.ds(r1, q), :] = jnp.concatenate(
            [y_carry_b[h] * jnp.exp(bw['cum'][:, 6 + h:7 + h]) for h in heads], axis=1)
        xws, decays = [], []
        for h in heads:
            tf = fw['cum'][q - 1:q, h:h + 1]
            wcol = jnp.exp(tf - fw['cum'][:, h:h + 1]) * fw['dtd'][:, h:h + 1]
            xws.append((xf[h] * wcol).astype(BF16))
            decays.append(jnp.exp(tf))
        for h in heads:
            tb = bw['cum'][0:1, 6 + h:7 + h]
            wcol = jnp.exp(tb - bw['cum'][:, 6 + h:7 + h]) * bw['dtd'][:, 6 + h:7 + h]
            xws.append((xb[h] * wcol).astype(BF16))
            decays.append(jnp.exp(tb))
        for h in heads:
            sf_ref[h] = sfs[h] * decays[h] + _dot(fw['bgt'][h // 3], xws[h])
        for h in heads:
            sb_ref[h] = sbs[h] * decays[6 + h] + _dot(bw['bgt'][h // 3], xws[6 + h])
        return carry

    lax.fori_loop(0, nc, main, 0)

    def fin(c, carry):
        r0 = pl.multiple_of(c * q, q)
        xs = xa_ref[pl.ds(r0, q), 0:SSD_INNER]
        y = y_ref[pl.ds(r0, q), :] + y2_ref[pl.ds(r0, q), :] + xs * dsk_ref[...]
        z = ua_ref[pl.ds(r0, q), 0:SSD_INNER].astype(F32)
        v = y * (z * _sigmoid(z))
        ms = jnp.mean(v * v, axis=-1, keepdims=True)
        o_ref[pl.ds(r0, q), :] = (v * lax.rsqrt(ms + NORM_EPS) * ng_ref[...]).astype(o_ref.dtype)
        return carry

    lax.fori_loop(0, nc, fin, 0)


def _ssd_call(ua, ug, cw, cb, alog, dtb, dsk, ng, seq_len):
    n = ua.shape[0]
    nseq = n // seq_len
    nc = seq_len // CHUNK
    seq = lambda b: (b, 0)
    const = lambda b: (0, 0)
    return pl.pallas_call(
        functools.partial(_ssd_kernel, seq_len=seq_len),
        grid=(nseq,),
        in_specs=[pl.BlockSpec((seq_len, 1024), seq),
                  pl.BlockSpec((seq_len, 128), seq),
                  pl.BlockSpec((8, SSD_XBC), const),
                  pl.BlockSpec((1, SSD_XBC), const),
                  pl.BlockSpec((1, 128), const),
                  pl.BlockSpec((1, 128), const),
                  pl.BlockSpec((1, SSD_INNER), const),
                  pl.BlockSpec((1, SSD_INNER), const)],
        out_specs=pl.BlockSpec((seq_len, SSD_INNER), seq),
        out_shape=jax.ShapeDtypeStruct((n, SSD_INNER), BF16),
        scratch_shapes=[pltpu.VMEM((seq_len + 16, SSD_XBC), F32),
                        pltpu.VMEM((seq_len, SSD_XBC), F32),
                        pltpu.VMEM((seq_len, 128), F32),
                        pltpu.VMEM((seq_len, 128), F32),
                        pltpu.VMEM((nc, 128, CHUNK), F32),
                        pltpu.VMEM((seq_len, SSD_INNER), F32),
                        pltpu.VMEM((seq_len, SSD_INNER), F32),
                        pltpu.VMEM((SSD_HEADS, SSD_STATE, HEAD_DIM), F32),
                        pltpu.VMEM((SSD_HEADS, SSD_STATE, HEAD_DIM), F32)],
        compiler_params=_cparams(("parallel",)),
        name="ssd",
    )(ua, ug, cw, cb, alog, dtb, dsk, ng)


def _swa_kernel(sink_ref, u_ref, o_ref, kp_ref, vp_ref, bias_ref, *, seq_len):
    q = CHUNK
    nb = seq_len // q
    zeros = jnp.zeros((q, 128), BF16)
    kp_ref[0:q, :] = zeros
    vp_ref[0:q, :] = zeros
    kp_ref[seq_len + q:seq_len + 2 * q, :] = zeros
    vp_ref[seq_len + q:seq_len + 2 * q, :] = zeros
    kp_ref[q:seq_len + q, :] = u_ref[:, 384:512]
    vp_ref[q:seq_len + q, :] = u_ref[:, 512:640]
    row = lax.broadcasted_iota(jnp.int32, (3 * q, 3 * q), 0) & (q - 1)
    col = lax.broadcasted_iota(jnp.int32, (3 * q, 3 * q), 1)
    bias_ref[...] = jnp.where(jnp.abs(col - q - row) <= SWA_SIDE, 0.0, NEG)
    rgrp = lax.broadcasted_iota(jnp.int32, (3 * q, 1), 0) // q
    col1 = lax.broadcasted_iota(jnp.int32, (1, 3 * q), 1)
    lane1 = lax.broadcasted_iota(jnp.int32, (1, 128), 1)
    lo_f = jnp.where(lane1 < HEAD_DIM, 1.0, 0.0)
    hi_f = 1.0 - lo_f
    halves = ((lo_f.astype(BF16), lo_f), (hi_f.astype(BF16), hi_f))

    def body(qb, carry):
        q0 = pl.multiple_of(qb * q, q)
        qs = jnp.concatenate([u_ref[pl.ds(q0, q), j * 128:(j + 1) * 128] for j in range(3)], axis=0)
        kw = kp_ref[pl.ds(q0, 3 * q), :]
        vw = vp_ref[pl.ds(q0, 3 * q), :]
        inside = (col1 >= q - q0) & (col1 < seq_len + q - q0)
        bias = bias_ref[...] + jnp.where(inside, 0.0, NEG)
        o = None
        for hk, (sel_b, sel_f) in enumerate(halves):
            s = _dot_nt(qs * sel_b, kw) + bias
            sk = jnp.where(rgrp == 0, sink_ref[3 * hk],
                           jnp.where(rgrp == 1, sink_ref[3 * hk + 1], sink_ref[3 * hk + 2]))
            m = jnp.maximum(jnp.max(s, axis=-1, keepdims=True), sk)
            p = jnp.exp(s - m)
            l = jnp.sum(p, axis=-1, keepdims=True) + jnp.exp(sk - m)
            oh = (_dot(p.astype(BF16), vw) / l) * sel_f
            o = oh if o is None else o + oh
        o_ref[pl.ds(q0, q), :] = jnp.concatenate(
            [o[j * q:(j + 1) * q, :] for j in range(3)], axis=1).astype(o_ref.dtype)
        return carry

    lax.fori_loop(0, nb, body, 0, unroll=2)


def _swa_call(ub, sink, seq_len):
    n = ub.shape[0]
    nseq = n // seq_len
    return pl.pallas_call(
        functools.partial(_swa_kernel, seq_len=seq_len),
        grid=(nseq,),
        in_specs=[pl.BlockSpec(memory_space=pltpu.SMEM),
                  pl.BlockSpec((seq_len, 640), lambda b: (b, 0))],
        out_specs=pl.BlockSpec((seq_len, 384), lambda b: (b, 0)),
        out_shape=jax.ShapeDtypeStruct((n, 384), BF16),
        scratch_shapes=[pltpu.VMEM((seq_len + 2 * CHUNK, 128), BF16),
                        pltpu.VMEM((seq_len + 2 * CHUNK, 128), BF16),
                        pltpu.VMEM((3 * CHUNK, 3 * CHUNK), F32)],
        compiler_params=_cparams(("parallel",)),
        name="swa",
    )(sink, ub)


def _mlstm_kernel(uc_ref, ug_ref, gb_ref, ng_ref, o_ref,
                  gcol_ref, gtr_ref, hacc_ref, ct_ref, m_ref, mask_ref, *, seq_len):
    q = CHUNK
    nc = seq_len // q
    lane = lax.broadcasted_iota(jnp.int32, (q, 128), 1)
    ct_ref[...] = jnp.zeros_like(ct_ref)
    m_ref[...] = jnp.zeros_like(m_ref)
    hacc_ref[...] = jnp.zeros_like(hacc_ref)

    def prep(c, carry):
        r0 = pl.multiple_of(c * q, q)
        raw = ug_ref[pl.ds(r0, q), :] + gb_ref[...]
        lf = jnp.minimum(raw, 0.0) - jnp.log(1.0 + jnp.exp(-jnp.abs(raw)))
        p = _prefix_rows(lf)
        s = p[q - 1:q, :] - p + lf
        g = jnp.where(lane < LANE_FG, raw, jnp.where(lane < LANE_FG + ML_HEADS, p, s))
        gcol_ref[pl.ds(r0, q), :] = g
        gtr_ref[c] = g.T
        return carry

    lax.fori_loop(0, nc, prep, 0)

    ri = lax.broadcasted_iota(jnp.int32, (q, q), 0)
    ci = lax.broadcasted_iota(jnp.int32, (q, q), 1)
    mask_ref[0] = jnp.where(ci <= ri, 0.0, NEG)
    mask_ref[1] = jnp.where(ci >= ri, 0.0, NEG)
    keep_lanes = jnp.where(lax.broadcasted_iota(jnp.int32, (1, ML_PAD), 1) < ML_DIM, 1.0, 0.0)

    def main(c, carry):
        units = []
        for dirn in range(2):
            ck = c if dirn == 0 else nc - 1 - c
            r0 = pl.multiple_of(ck * q, q)
            g = gcol_ref[pl.ds(r0, q), :]
            gt = gtr_ref[ck]
            for h in range(ML_HEADS):
                s_idx = dirn * ML_HEADS + h
                li, lb = LANE_IG + s_idx, LANE_FG + s_idx
                edge = q - 1 if dirn == 0 else 0
                units.append(dict(
                    dirn=dirn, h=h, r0=r0, s_idx=s_idx,
                    b_col=g[:, lb:lb + 1], b_row=gt[lb:lb + 1, :], i_col=g[:, li:li + 1],
                    i_row=gt[li:li + 1, :], tot=g[edge:edge + 1, lb:lb + 1],
                    m_st=m_ref[s_idx][0:1, 0:1], ct=ct_ref[s_idx],
                    qh=uc_ref[pl.ds(r0, q), h * ML_PAD:(h + 1) * ML_PAD],
                    kh=uc_ref[pl.ds(r0, q), 512 + h * ML_PAD:512 + (h + 1) * ML_PAD],
                    vh=uc_ref[pl.ds(r0, q), 1024 + h * ML_PAD:1024 + (h + 1) * ML_PAD]))
        for u in units:
            u['logd'] = u['b_col'] - u['b_row'] + u['i_row'] + mask_ref[u['dirn']]
            u['m_inter'] = u['b_col'] + u['m_st']
        for u in units:
            u['m_t'] = jnp.maximum(jnp.max(u['logd'], axis=-1, keepdims=True), u['m_inter'])
        for u in units:
            u['qk'] = _dot_nt(u['qh'], u['kh'])
            u['qc'] = _dot(u['qh'], u['ct'].astype(BF16))
        for u in units:
            u['sm'] = (u['qk'] * jnp.exp(u['logd'] - u['m_t'])).astype(BF16)
            u['inter'] = jnp.exp(u['m_inter'] - u['m_t'])
        for u in units:
            u['num'] = _dot(u['sm'], u['vh']) + u['inter'] * u['qc']
        for u in units:
            den = u['num'][:, ML_DIM:ML_DIM + 1]
            hh = u['num'] / jnp.maximum(jnp.abs(den), jnp.exp(-u['m_t'])) * keep_lanes
            hacc_ref[pl.ds(u['r0'], q), u['h'] * ML_PAD:(u['h'] + 1) * ML_PAD] += hh
        for u in units:
            tot, m_st = u['tot'], u['m_st']
            m_new = jnp.maximum(tot + m_st,
                                jnp.max(tot - u['b_row'] + u['i_row'], axis=-1, keepdims=True))
            wk_col = jnp.exp(tot - u['b_col'] + u['i_col'] - m_new)
            u['keep'] = jnp.exp(tot + m_st - m_new)
            u['vw'] = (u['vh'].astype(F32) * wk_col).astype(BF16)
            m_ref[u['s_idx']] = jnp.broadcast_to(m_new, (8, 128))
        for u in units:
            ct_ref[u['s_idx']] = u['keep'] * u['ct'] + _dot_tn(u['kh'], u['vw'])
        return carry

    lax.fori_loop(0, nc, main, 0)

    def fin(c, carry):
        r0 = pl.multiple_of(c * q, q)
        outs = []
        for h in range(ML_HEADS):
            hs = hacc_ref[pl.ds(r0, q), h * ML_PAD:(h + 1) * ML_PAD]
            ms = jnp.sum(hs * hs, axis=-1, keepdims=True) * (1.0 / ML_DIM)
            hn = hs * lax.rsqrt(ms + NORM_EPS) * ng_ref[:, h * ML_PAD:(h + 1) * ML_PAD]
            og = uc_ref[pl.ds(r0, q), 1536 + h * ML_PAD:1536 + (h + 1) * ML_PAD].astype(F32)
            outs.append(hn * _sigmoid(og))
        o_ref[pl.ds(r0, q), :] = jnp.concatenate(outs, axis=1).astype(o_ref.dtype)
        return carry

    lax.fori_loop(0, nc, fin, 0)


def _mlstm_call(uc, ug, gb, ng, seq_len):
    n = uc.shape[0]
    nseq = n // seq_len
    nc = seq_len // CHUNK
    seq = lambda b: (b, 0)
    const = lambda b: (0, 0)
    width = ML_HEADS * ML_PAD
    return pl.pallas_call(
        functools.partial(_mlstm_kernel, seq_len=seq_len),
        grid=(nseq,),
        in_specs=[pl.BlockSpec((seq_len, 4 * width), seq),
                  pl.BlockSpec((seq_len, 128), seq),
                  pl.BlockSpec((1, 128), const),
                  pl.BlockSpec((1, width), const)],
        out_specs=pl.BlockSpec((seq_len, width), seq),
        out_shape=jax.ShapeDtypeStruct((n, width), BF16),
        scratch_shapes=[pltpu.VMEM((seq_len, 128), F32),
                        pltpu.VMEM((nc, 128, CHUNK), F32),
                        pltpu.VMEM((seq_len, width), F32),
                        pltpu.VMEM((2 * ML_HEADS, ML_PAD, ML_PAD), F32),
                        pltpu.VMEM((2 * ML_HEADS, 8, 128), F32),
                        pltpu.VMEM((2, CHUNK, CHUNK), F32)],
        compiler_params=_cparams(("parallel",)),
        name="mlstm",
    )(uc, ug, gb, ng)


def _attend_units(units, lo_b, hi_b, lo_f, hi_f):
    n = units[0][0].shape[0]
    ss = []
    for q, kw, _, bias in units:
        s2 = _dot_nt(jnp.concatenate([q * lo_b, q * hi_b], axis=0), kw)
        ss.append(jnp.concatenate([s2[0:n] + bias, s2[n:2 * n] + bias], axis=0))
    ms = [jnp.max(s, axis=-1, keepdims=True) for s in ss]
    ps = [jnp.exp(s - m) for s, m in zip(ss, ms)]
    ls = [jnp.sum(p, axis=-1, keepdims=True) for p in ps]
    os_ = [_dot(p.astype(BF16), u[2]) / l for p, u, l in zip(ps, units, ls)]
    res = []
    for o2, m, l in zip(os_, ms, ls):
        lse2 = m + jnp.log(l)
        res.append((o2[0:n] * lo_f + o2[n:2 * n] * hi_f, lse2[0:n] * lo_f + lse2[n:2 * n] * hi_f))
    return res


def _dil_kernel(q0_ref, q1_ref, q2_ref, k_ref, v_ref, o_ref,
                qf_ref, kf_ref, vf_ref, kc0_ref, vc0_ref, kc1_ref, vc1_ref, og_ref, lg_ref,
                qc2_ref, kc2_ref, vc2_ref, og2_ref, lg2_ref, band_ref, band2_ref, *, seq_len):
    qr = CHUNK
    pad = DIL_SIDE
    d1, d2 = DIL_CONFIGS[1][1], DIL_CONFIGS[2][1]
    n1, n2 = seq_len // d1, seq_len // d2
    assert n2 == qr and n1 % qr == 0
    qf_ref[0] = q1_ref[...].astype(F32)
    qf_ref[1] = q2_ref[...].astype(F32)
    kf_ref[...] = k_ref[...].astype(F32)
    vf_ref[...] = v_ref[...].astype(F32)
    zeros = jnp.zeros((pad, 128), BF16)
    kc0_ref[0:pad, :] = zeros
    vc0_ref[0:pad, :] = zeros
    kc0_ref[pad + seq_len:2 * pad + seq_len, :] = zeros
    vc0_ref[pad + seq_len:2 * pad + seq_len, :] = zeros
    kc0_ref[pad:pad + seq_len, :] = k_ref[...]
    vc0_ref[pad:pad + seq_len, :] = v_ref[...]
    for r in range(d1):
        kc1_ref[r, 0:pad, :] = zeros
        vc1_ref[r, 0:pad, :] = zeros
        kc1_ref[r, pad + n1:2 * pad + n1, :] = zeros
        vc1_ref[r, pad + n1:2 * pad + n1, :] = zeros
        kc1_ref[r, pad:pad + n1, :] = kf_ref[pl.ds(r, n1, stride=d1), :].astype(BF16)
        vc1_ref[r, pad:pad + n1, :] = vf_ref[pl.ds(r, n1, stride=d1), :].astype(BF16)

    row = lax.broadcasted_iota(jnp.int32, (qr, 2 * qr), 0)
    col = lax.broadcasted_iota(jnp.int32, (qr, 2 * qr), 1)
    band_ref[...] = jnp.where(jnp.abs(col - pad - row) <= DIL_SIDE, 0.0, NEG)
    row2 = lax.broadcasted_iota(jnp.int32, (qr, qr), 0)
    col2 = lax.broadcasted_iota(jnp.int32, (qr, qr), 1)
    band2_ref[...] = jnp.where(jnp.abs(col2 - row2) <= DIL_SIDE, 0.0, NEG)
    lane1 = lax.broadcasted_iota(jnp.int32, (1, 128), 1)
    col1 = lax.broadcasted_iota(jnp.int32, (1, 2 * qr), 1)
    lo_f = jnp.where(lane1 < HEAD_DIM, 1.0, 0.0)
    hi_f = 1.0 - lo_f
    sel = (lo_f.astype(BF16), hi_f.astype(BF16), lo_f, hi_f)

    def window_bias(u0, n):
        inside = (col1 >= pad - u0) & (col1 < n + pad - u0)
        return band_ref[...] + jnp.where(inside, 0.0, NEG)

    def batch0(b4, carry):
        units, starts = [], []
        for i in range(DIL_BATCH):
            u0 = pl.multiple_of((b4 * DIL_BATCH + i) * qr, qr)
            starts.append(u0)
            units.append((q0_ref[pl.ds(u0, qr), :], kc0_ref[pl.ds(u0, 2 * qr), :],
                          vc0_ref[pl.ds(u0, 2 * qr), :], window_bias(u0, seq_len)))
        for u0, (o, lse) in zip(starts, _attend_units(units, *sel)):
            og_ref[0, pl.ds(u0, qr), :] = o
            lg_ref[0, pl.ds(u0, qr), :] = lse
        return carry

    lax.fori_loop(0, seq_len // qr // DIL_BATCH, batch0, 0)

    nb1 = n1 // qr
    assert nb1 == DIL_BATCH

    def batch1(r, carry):
        units, rowsl = [], []
        for i in range(DIL_BATCH):
            u0 = i * qr
            rows = pl.ds(r + u0 * d1, qr, stride=d1)
            rowsl.append(rows)
            units.append((qf_ref[0, rows, :].astype(BF16), kc1_ref[r, u0:u0 + 2 * qr, :],
                          vc1_ref[r, u0:u0 + 2 * qr, :], window_bias(u0, n1)))
        for rows, (o, lse) in zip(rowsl, _attend_units(units, *sel)):
            og_ref[1, rows, :] = o
            lg_ref[1, rows, :] = lse
        return carry

    lax.fori_loop(0, d1, batch1, 0)

    for r in range(d2):
        rows = pl.ds(r, qr, stride=d2)
        qc2_ref[r] = qf_ref[1, rows, :].astype(BF16)
        kc2_ref[r] = kf_ref[rows, :].astype(BF16)
        vc2_ref[r] = vf_ref[rows, :].astype(BF16)

    def batch2(b4, carry):
        rs = [b4 * DIL_BATCH + i for i in range(DIL_BATCH)]
        units = [(qc2_ref[r], kc2_ref[r], vc2_ref[r], band2_ref[...]) for r in rs]
        for r, (o, lse) in zip(rs, _attend_units(units, *sel)):
            og2_ref[r] = o
            lg2_ref[r] = lse
        return carry

    lax.fori_loop(0, d2 // DIL_BATCH, batch2, 0)
    for r in range(d2):
        rows = pl.ds(r, qr, stride=d2)
        og_ref[2, rows, :] = og2_ref[r]
        lg_ref[2, rows, :] = lg2_ref[r]

    def fin(c, carry):
        r0 = pl.multiple_of(c * qr, qr)
        rows = pl.ds(r0, qr)
        l0, l1, l2 = lg_ref[0, rows, :], lg_ref[1, rows, :], lg_ref[2, rows, :]
        m = jnp.maximum(jnp.maximum(l0, l1), l2)
        w0, w1, w2 = jnp.exp(l0 - m), jnp.exp(l1 - m), jnp.exp(l2 - m)
        y = (og_ref[0, rows, :] * w0 + og_ref[1, rows, :] * w1 + og_ref[2, rows, :] * w2)
        o_ref[rows, :] = (y / (w0 + w1 + w2)).astype(o_ref.dtype)
        return carry

    lax.fori_loop(0, seq_len // qr, fin, 0, unroll=2)


def _dil_call(ud, seq_len):
    n = ud.shape[0]
    nseq = n // seq_len
    blk = (seq_len, 128)
    d1, d2 = DIL_CONFIGS[1][1], DIL_CONFIGS[2][1]
    n1 = seq_len // d1

    def col(off):
        return lambda b, p: (b, off + p)

    return pl.pallas_call(
        functools.partial(_dil_kernel, seq_len=seq_len),
        grid=(nseq, 3),
        in_specs=[pl.BlockSpec(blk, col(0)), pl.BlockSpec(blk, col(3)), pl.BlockSpec(blk, col(6)),
                  pl.BlockSpec(blk, col(9)), pl.BlockSpec(blk, col(12))],
        out_specs=pl.BlockSpec(blk, col(0)),
        out_shape=jax.ShapeDtypeStruct((n, 384), BF16),
        scratch_shapes=[pltpu.VMEM((2, seq_len, 128), F32),
                        pltpu.VMEM((seq_len, 128), F32),
                        pltpu.VMEM((seq_len, 128), F32),
                        pltpu.VMEM((seq_len + 2 * DIL_SIDE, 128), BF16),
                        pltpu.VMEM((seq_len + 2 * DIL_SIDE, 128), BF16),
                        pltpu.VMEM((d1, n1 + 2 * DIL_SIDE, 128), BF16),
                        pltpu.VMEM((d1, n1 + 2 * DIL_SIDE, 128), BF16),
                        pltpu.VMEM((3, seq_len, 128), F32),
                        pltpu.VMEM((3, seq_len, 128), F32)]
                       + [pltpu.VMEM((d2, CHUNK, 128), BF16)] * 3
                       + [pltpu.VMEM((d2, CHUNK, 128), F32)] * 2
                       + [pltpu.VMEM((CHUNK, 2 * CHUNK), F32), pltpu.VMEM((CHUNK, CHUNK), F32)],
        compiler_params=_cparams(("parallel", "parallel")),
        name="dil",
    )(ud, ud, ud, ud, ud)


def _out_proj_kernel(ya_ref, yb_ref, yc_ref, yd_ref, *refs, tiles_a):
    if tiles_a is None:
        x_ref = refs[0]
        refs = refs[1:]
    else:
        xa_ref, xb_ref = refs[0:2]
        refs = refs[2:]
    (g1_ref, sc_ref, sh_ref, pg_ref, fg_ref, wa_ref, wb_ref, wc_ref, wd_ref, wrh_ref, wrl_ref, br_ref,
     x1_ref, h2_ref, rt_ref) = refs
    for r0 in range(0, x1_ref.shape[0], OUT_SUB):
        rs = slice(r0, r0 + OUT_SUB)
        if tiles_a is None:
            x_in = x_ref[rs, :]
        else:
            x_in = jnp.where(pl.program_id(0) < tiles_a, xa_ref[rs, :], xb_ref[rs, :])
        y = (_dot(ya_ref[rs, :], wa_ref[...]) + _dot(yb_ref[rs, :], wb_ref[...])
             + _dot(yc_ref[rs, :], wc_ref[...]) + _dot(yd_ref[rs, :], wd_ref[...]))
        ms = jnp.mean(y * y, axis=-1, keepdims=True)
        yn = y * lax.rsqrt(ms + NORM_EPS) * pg_ref[...]
        x1 = x_in + g1_ref[0] * yn
        x1_ref[rs, :] = x1
        ms2 = jnp.mean(x1 * x1, axis=-1, keepdims=True)
        h2 = x1 * lax.rsqrt(ms2 + NORM_EPS) * fg_ref[...]
        h2 = h2 * (1.0 + sc_ref[0]) + sh_ref[0]
        half = h2.shape[1] // 2
        lo = lax.bitcast_convert_type(h2[:, :half].astype(BF16).astype(F32), jnp.uint32)
        hi = lax.bitcast_convert_type(h2[:, half:].astype(BF16).astype(F32), jnp.uint32)
        h2_ref[rs, :] = (lo >> 16) | hi
        h_hi = h2.astype(BF16)
        h_lo = (h2 - h_hi.astype(F32)).astype(BF16)
        logits = (_dot(h_hi, wrh_ref[...]) + _dot(h_lo, wrh_ref[...]) + _dot(h_hi, wrl_ref[...])
                  + br_ref[...])
        lane = lax.broadcasted_iota(jnp.int32, logits.shape, 1)
        vals, idxs = [], []
        for _ in range(TOP_K):
            m = jnp.max(logits, axis=-1, keepdims=True)
            idx = jnp.min(jnp.where(logits == m, lane, 128), axis=-1, keepdims=True)
            vals.append(m)
            idxs.append(idx)
            logits = jnp.where(lane == idx, -3e38, logits)
        es = [jnp.exp(v - vals[0]) for v in vals]
        tot = es[0] + es[1] + es[2] + es[3]
        rt = jnp.zeros(lane.shape, F32)
        for k in range(TOP_K):
            rt = jnp.where(lane == k, idxs[k].astype(F32), rt)
            rt = jnp.where(lane == TOP_K + k, es[k] / tot, rt)
        rt_ref[rs, :] = rt[:, 0:2 * TOP_K]


def _out_proj_call(ya, yb, yc, yd, x, g1, sc, sh, pg, fg, wa, wb, wc, wd, wrh, wrl, br, seq_len, tm=512):
    split = isinstance(x, tuple)
    n, d = ya.shape[0], D_MODEL
    tiles_a = None
    if split:
        tiles_a, x_specs = _split_specs(x[0], x[1], tm)
        x_args = list(x)
    else:
        x_specs = [pl.BlockSpec((tm, d), lambda i: (i, 0))]
        x_args = [x]
    per_seq = seq_len // tm
    row = lambda i: (i, 0)
    seq = lambda i: (i // per_seq, 0, 0)
    const = lambda i: (0, 0)

    def full(a):
        return pl.BlockSpec(a.shape, const)

    return pl.pallas_call(
        functools.partial(_out_proj_kernel, tiles_a=tiles_a),
        grid=(n // tm,),
        in_specs=[pl.BlockSpec((tm, ya.shape[1]), row), pl.BlockSpec((tm, yb.shape[1]), row),
                  pl.BlockSpec((tm, yc.shape[1]), row), pl.BlockSpec((tm, yd.shape[1]), row)]
        + x_specs
        + [pl.BlockSpec((1, 1, d), seq), pl.BlockSpec((1, 1, d), seq),
                  pl.BlockSpec((1, 1, d), seq),
                  full(pg), full(fg), full(wa), full(wb), full(wc), full(wd), full(wrh), full(wrl), full(br)],
        out_specs=[pl.BlockSpec((tm, d), row), pl.BlockSpec((tm, d // 2), row),
                   pl.BlockSpec((tm, 2 * TOP_K), row)],
        out_shape=[jax.ShapeDtypeStruct((n, d), F32), jax.ShapeDtypeStruct((n, d // 2), jnp.uint32),
                   jax.ShapeDtypeStruct((n, 2 * TOP_K), F32)],
        compiler_params=_cparams(("parallel",)),
        name="out_proj",
    )(ya, yb, yc, yd, *x_args, g1, sc, sh, pg, fg, wa, wb, wc, wd, wrh, wrl, br)


def _moe_slots(tile):
    return tile * TOP_K // MOE_ROWS + N_EXPERTS + 2


def _moe_kernel(nb_ref, bf_ref, bs_ref, bv_ref, idx_ref, gate_ref, h_ref,
                wgu_ref, bgu_ref, wdn_ref, bdn_ref, o_ref, xs0_ref, xs1_ref, ys0_ref, ys1_ref):
    t = pl.program_id(0)
    e = pl.program_id(1)
    nb = nb_ref[t]
    rows = MOE_ROWS
    nslot = _moe_slots(h_ref.shape[0])
    half = h_ref.shape[1]

    def gather(dst_ref, blk):
        base = bs_ref[t * nslot + blk]
        for r in range(rows):
            i = idx_ref[0, 0, base + r]
            dst_ref[r:r + 1, :] = h_ref[pl.ds(i, 1), :]

    def scatter(src_ref, blk, nvalid):
        base = bs_ref[t * nslot + blk]
        i0 = idx_ref[0, 0, base]
        for r0 in range(0, rows, MOE_GROUP):
            ks = range(MOE_GROUP)
            ok = [r0 + k < nvalid for k in ks]
            ii = [jnp.where(ok[k], idx_ref[0, 0, base + r0 + k], i0) for k in ks]
            gg = [jnp.where(ok[k], gate_ref[0, 0, base + r0 + k], 0.0) for k in ks]
            cur = [o_ref[pl.ds(ii[k], 1), :] for k in ks]
            for k in reversed(ks):
                o_ref[pl.ds(ii[k], 1), :] = cur[k] + gg[k] * src_ref[r0 + k:r0 + k + 1, :]

    def expert(xs_ref, ys_ref):
        xu = xs_ref[...]
        x_lo = lax.bitcast_convert_type(xu << 16, F32).astype(BF16)
        x_hi = lax.bitcast_convert_type(xu & jnp.uint32(0xFFFF0000), F32).astype(BF16)
        gu = _dot(x_lo, wgu_ref[0, 0:half, :]) + _dot(x_hi, wgu_ref[0, half:2 * half, :]) + bgu_ref[0]
        gate = jnp.minimum(gu[:, :D_EXPERT], SWIGLU_LIMIT)
        up = jnp.clip(gu[:, D_EXPERT:], -SWIGLU_LIMIT, SWIGLU_LIMIT)
        act = (up + 1.0) * gate * _sigmoid(SWIGLU_ALPHA * gate)
        ys_ref[...] = _dot(act.astype(BF16), wdn_ref[0]) + bdn_ref[0]

    @pl.when(e == 0)
    def _():
        o_ref[...] = jnp.zeros_like(o_ref)
        ys1_ref[...] = jnp.zeros_like(ys1_ref)
        gather(xs0_ref, 0)

    bufs = ((xs0_ref, xs1_ref, ys0_ref, ys1_ref), (xs1_ref, xs0_ref, ys1_ref, ys0_ref))

    def block(b, carry):
        prev = jnp.maximum(b - 1, 0)
        nv_prev = jnp.where(b > 0, bv_ref[t * nslot + prev], 0)
        for par, (xs_cur, xs_nxt, ys_cur, ys_prv) in enumerate(bufs):
            @pl.when(b % 2 == par)
            def _(xs_cur=xs_cur, xs_nxt=xs_nxt, ys_cur=ys_cur, ys_prv=ys_prv):
                scatter(ys_prv, prev, nv_prev)
                expert(xs_cur, ys_cur)
                gather(xs_nxt, b + 1)
        return carry

    lax.fori_loop(bf_ref[t * (N_EXPERTS + 1) + e], bf_ref[t * (N_EXPERTS + 1) + e + 1], block, 0)

    for par, (_, _, _, ys_prv) in enumerate(bufs):
        @pl.when((e == N_EXPERTS - 1) & (nb % 2 == par))
        def _(ys_prv=ys_prv):
            scatter(ys_prv, nb - 1, bv_ref[t * nslot + nb - 1])


def _moe_call(nb, bfirst, bs, bv, idx, gates, h2p, wgu, bgu, wdn, bdn, tile):
    n, half = h2p.shape
    d = 2 * half
    ntile = n // tile
    plen = idx.shape[-1]
    wmap = lambda t, e, *_: (e, 0, 0)
    tmap3 = lambda t, e, *_: (t, 0, 0)
    tmap2 = lambda t, e, *_: (t, 0)
    grid_spec = pltpu.PrefetchScalarGridSpec(
        num_scalar_prefetch=4,
        grid=(ntile, N_EXPERTS),
        in_specs=[pl.BlockSpec((1, 1, plen), tmap3, memory_space=pltpu.SMEM),
                  pl.BlockSpec((1, 1, plen), tmap3, memory_space=pltpu.SMEM),
                  pl.BlockSpec((tile, half), tmap2, pipeline_mode=pl.Buffered(1)),
                  pl.BlockSpec((1, d, 2 * D_EXPERT), wmap), pl.BlockSpec((1, 1, 2 * D_EXPERT), wmap),
                  pl.BlockSpec((1, D_EXPERT, d), wmap), pl.BlockSpec((1, 1, d), wmap)],
        out_specs=pl.BlockSpec((tile, d), tmap2, pipeline_mode=pl.Buffered(1)),
        scratch_shapes=[pltpu.VMEM((MOE_ROWS, half), jnp.uint32)] * 2 + [pltpu.VMEM((MOE_ROWS, d), F32)] * 2,
    )
    return pl.pallas_call(
        _moe_kernel,
        grid_spec=grid_spec,
        out_shape=jax.ShapeDtypeStruct((n, d), F32),
        compiler_params=_cparams(("parallel", "arbitrary")),
        name="moe",
    )(nb, bfirst, bs, bv, idx, gates, h2p, wgu, bgu, wdn, bdn)


def _route(rt, tile):
    n = rt.shape[0]
    ntile = n // tile
    npair = tile * TOP_K
    nslot = _moe_slots(tile)
    e = rt[:, :TOP_K].astype(jnp.int32).reshape(ntile, npair)
    g = rt[:, TOP_K:].reshape(ntile, npair)
    tok = jnp.broadcast_to((jnp.arange(npair, dtype=jnp.int32) // TOP_K)[None, :], (ntile, npair))
    _, tok_s, g_s = lax.sort((e, tok, g), dimension=1, is_stable=True, num_keys=1)
    experts = jnp.arange(N_EXPERTS, dtype=jnp.int32)
    counts = jnp.sum(e[:, :, None] == experts, axis=1, dtype=jnp.int32)
    nblocks = (counts + MOE_ROWS - 1) // MOE_ROWS
    bends = jnp.cumsum(nblocks, axis=-1)
    starts = jnp.cumsum(counts, axis=-1) - counts
    nb = bends[:, -1]
    slot = jnp.minimum(jnp.arange(nslot, dtype=jnp.int32)[None, :], nb[:, None] - 1)
    be = jnp.sum(slot[:, :, None] >= bends[:, None, :], axis=-1, dtype=jnp.int32)
    onehot = (be[:, :, None] == experts).astype(jnp.int32)
    pick = lambda v: jnp.sum(onehot * v[:, None, :], axis=-1)
    within = slot - pick(bends - nblocks)
    bs = pick(starts) + within * MOE_ROWS
    bv = jnp.clip(pick(counts) - within * MOE_ROWS, 0, MOE_ROWS)
    padw = ((0, 0), (0, MOE_ROWS))
    idx = jnp.pad(tok_s, padw).reshape(ntile, 1, -1)
    gl = jnp.pad(g_s, padw).reshape(ntile, 1, -1)
    bfirst = jnp.concatenate([jnp.zeros((ntile, 1), jnp.int32), bends], axis=-1)
    flat = lambda v: v.reshape(-1).astype(jnp.int32)
    return flat(nb), flat(bfirst), flat(bs), flat(bv), idx, gl


def _ffn_res_kernel(x_ref, f_ref, g2_ref, pg_ref, oa_ref, ob_ref, *, tiles_a):
    f = f_ref[...]
    ms = jnp.mean(f * f, axis=-1, keepdims=True)
    y = x_ref[...] + g2_ref[0] * (f * lax.rsqrt(ms + NORM_EPS) * pg_ref[...])
    i = pl.program_id(0)

    @pl.when(i < tiles_a)
    def _():
        oa_ref[...] = y

    @pl.when(i >= tiles_a)
    def _():
        ob_ref[...] = y


def _ffn_res_call(x1, f, g2, pg, seq_len, rows_a, tm=1024):
    n, d = x1.shape
    per_seq = seq_len // tm
    tiles_a = rows_a // tm
    row = lambda i: (i, 0)
    return pl.pallas_call(
        functools.partial(_ffn_res_kernel, tiles_a=tiles_a),
        grid=(n // tm,),
        in_specs=[pl.BlockSpec((tm, d), row), pl.BlockSpec((tm, d), row),
                  pl.BlockSpec((1, 1, d), lambda i: (i // per_seq, 0, 0)),
                  pl.BlockSpec((1, d), lambda i: (0, 0))],
        out_specs=[pl.BlockSpec((tm, d), lambda i: (jnp.minimum(i, tiles_a - 1), 0)),
                   pl.BlockSpec((tm, d), lambda i: (jnp.maximum(i - tiles_a, 0), 0))],
        out_shape=[jax.ShapeDtypeStruct((rows_a, d), F32), jax.ShapeDtypeStruct((n - rows_a, d), F32)],
        compiler_params=_cparams(("arbitrary",)),
        name="ffn_res",
    )(x1, f, g2, pg)


def _rope_tables(seq_len):
    half = 8
    inv = 1.0 / (ROPE_THETA ** (jnp.arange(half, dtype=F32) * (2.0 / 16)))
    ang = jnp.arange(seq_len, dtype=F32)[:, None] * inv[None, :]
    cos, sin = jnp.cos(ang), jnp.sin(ang)
    ones = jnp.ones((seq_len, 48), F32)
    zeros8 = jnp.zeros((seq_len, 8), F32)
    zeros48 = jnp.zeros((seq_len, 48), F32)
    rc = jnp.concatenate([cos, cos, ones], axis=1)
    rs1 = jnp.concatenate([zeros8, sin, zeros48], axis=1)
    rs2 = jnp.concatenate([-sin, zeros8, zeros48], axis=1)
    tile2 = lambda a: jnp.concatenate([a, a], axis=1)
    return tile2(rc), tile2(rs1), tile2(rs2)


def _pad_heads(w):
    lead = w.shape[:-1]
    w = w.reshape(lead + (ML_HEADS, ML_DIM))
    w = jnp.pad(w, [(0, 0)] * len(lead) + [(0, 0), (0, ML_PAD - ML_DIM)])
    return w.reshape(lead + (ML_HEADS * ML_PAD,))


def _layout_w_in(w_in):
    d = w_in.shape[0]
    offs = np.cumsum((0,) + IN_SPLITS)
    parts = [w_in[:, offs[i]:offs[i + 1]] for i in range(len(IN_SPLITS))]
    (z, xbc, dt, sq, sk, sv, mq, mk, mv, mo, mi, mf, dq, dk, dv) = parts
    zc = lambda k: jnp.zeros((d, k), w_in.dtype)
    gates = jnp.concatenate([dt, zc(4), mi, mf, zc(128 - 32)], axis=1)
    sq = sq.reshape(d, SWA_Q_HEADS, HEAD_DIM)[:, SWA_HEAD_ORDER, :].reshape(d, -1)
    w = jnp.concatenate([z, xbc, gates, sq, sk, sv,
                         _pad_heads(mq), _pad_heads(mk), _pad_heads(mv), _pad_heads(mo),
                         dq, dk, dv], axis=1)
    assert w.shape[1] == COL_END
    return w.astype(BF16)


def _lane_row(parts, width=128):
    row = jnp.zeros((width,), F32)
    for off, v in parts:
        row = row.at[off:off + v.shape[0]].set(v.astype(F32))
    return row.reshape(1, width)


def _layer(x, pending, mod_l, p, l, tabs, seq_len):
    d = D_MODEL
    nseq = mod_l.shape[0]
    sh1, sc1, g1, sh2, sc2, g2 = [mod_l[:, i * d:(i + 1) * d].reshape(nseq, 1, d) for i in range(6)]
    row = lambda v: v.reshape(1, -1).astype(F32)

    outs = _in_proj_call(x, sc1, sh1, row(p['pre_mix_g'][l]), _layout_w_in(p['w_in'][l]),
                         *tabs, seq_len, ffn=pending)
    if pending is not None:
        x = outs[5]
    ua, ug, ub, uc, ud = outs[:5]
    cw = jnp.pad(p['ssd_conv_w'][l], ((0, 8 - SSD_CONV), (0, 0)))
    ya = _ssd_call(ua, ug, cw, row(p['ssd_conv_b'][l]),
                   _lane_row([(0, p['ssd_a_log'][l].reshape(-1))]),
                   _lane_row([(0, p['ssd_dt_bias'][l].reshape(-1))]),
                   row(jnp.repeat(p['ssd_d'][l], HEAD_DIM)), row(p['ssd_norm_g'][l]), seq_len)
    yb = _swa_call(ub, p['swa_sink'][l].astype(F32), seq_len)
    gb = _lane_row([(LANE_IG, p['mlstm_i_bias'][l].reshape(-1)),
                    (LANE_FG, p['mlstm_f_bias'][l].reshape(-1))])
    yc = _mlstm_call(uc, ug, gb, row(_pad_heads(p['mlstm_norm_g'][l])), seq_len)
    yd = _dil_call(ud, seq_len)

    w_out = p['w_out'][l]
    wa = w_out[0:384].astype(BF16)
    wb = w_out[384:768].reshape(SWA_Q_HEADS, HEAD_DIM, d)[SWA_HEAD_ORDER, :, :].reshape(384, d).astype(BF16)
    wc = jnp.pad(w_out[768:1152].reshape(ML_HEADS, ML_DIM, d),
                 ((0, 0), (0, ML_PAD - ML_DIM), (0, 0))).reshape(ML_HEADS * ML_PAD, d).astype(BF16)
    wd = w_out[1152:1536].astype(BF16)
    wr = jnp.pad(p['w_router'][l], ((0, 0), (0, 128 - N_EXPERTS))).astype(F32)
    wrh = wr.astype(BF16)
    wrl = (wr - wrh.astype(F32)).astype(BF16)
    br = jnp.concatenate([p['b_router'][l].astype(F32), jnp.full((128 - N_EXPERTS,), NEG, F32)]).reshape(1, 128)
    x1, h2, rt = _out_proj_call(ya, yb, yc, yd, x, g1, sc2, sh2, row(p['post_mix_g'][l]),
                                row(p['pre_ffn_g'][l]), wa, wb, wc, wd, wrh, wrl, br, seq_len)

    tile = min(MOE_TILE, x1.shape[0])
    nb, bfirst, bs, bv, idx, gs = _route(rt, tile)
    f = _moe_call(nb, bfirst, bs, bv, idx, gs, h2, p['w_gate_up'][l].astype(BF16),
                  p['b_gate_up'][l].reshape(N_EXPERTS, 1, -1), p['w_down'][l].astype(BF16),
                  p['b_down'][l].reshape(N_EXPERTS, 1, -1), tile)
    return x1, (f, g2, row(p['post_ffn_g'][l]))


def _trunk(xa, xb, c, p):
    nseq_a, seq_len, d = xa.shape
    nseq = nseq_a + xb.shape[0]
    depth = p['w_in'].shape[0]
    mod = _mod_call(c, p['w_mod'], p['b_mod'])
    tabs = _rope_tables(seq_len)
    x = (xa.reshape(-1, d), xb.reshape(-1, d))
    pending = None
    for l in range(depth):
        x, pending = _layer(x, pending, mod[l], p, l, tabs, seq_len)
    ya, yb = _ffn_res_call(x, *pending, seq_len, nseq_a * seq_len)
    return ya.reshape(nseq_a, seq_len, d), yb.reshape(nseq - nseq_a, seq_len, d)


def kernel(x_prompt, x_sample, c_prompt, c_sample, w_mod, b_mod, pre_mix_g, post_mix_g, pre_ffn_g, post_ffn_g, w_in, w_out, ssd_conv_w, ssd_conv_b, ssd_a_log, ssd_dt_bias, ssd_d, ssd_norm_g, swa_sink, mlstm_i_bias, mlstm_f_bias, mlstm_norm_g, w_router, b_router, w_gate_up, b_gate_up, w_down, b_down):
    p = dict(w_mod=w_mod, b_mod=b_mod, pre_mix_g=pre_mix_g, post_mix_g=post_mix_g, pre_ffn_g=pre_ffn_g,
             post_ffn_g=post_ffn_g, w_in=w_in, w_out=w_out, ssd_conv_w=ssd_conv_w, ssd_conv_b=ssd_conv_b,
             ssd_a_log=ssd_a_log, ssd_dt_bias=ssd_dt_bias, ssd_d=ssd_d, ssd_norm_g=ssd_norm_g,
             swa_sink=swa_sink, mlstm_i_bias=mlstm_i_bias, mlstm_f_bias=mlstm_f_bias,
             mlstm_norm_g=mlstm_norm_g, w_router=w_router, b_router=b_router, w_gate_up=w_gate_up,
             b_gate_up=b_gate_up, w_down=w_down, b_down=b_down)
    c = jnp.concatenate([c_prompt, c_sample], axis=0)
    return _trunk(x_prompt, x_sample, c, p)
```

```python
import functools
import math

import jax
import jax.numpy as jnp
import numpy as np
from jax import lax
from jax.experimental import pallas as pl
from jax.experimental.pallas import tpu as pltpu

F32 = jnp.float32
BF16 = jnp.bfloat16

D_MODEL = 1024
HEAD_DIM = 64
ROPE_THETA = 500000.0
NORM_EPS = 1e-6
CHUNK = 128

SSD_HEADS = 6
SSD_INNER = 384
SSD_STATE = 64
SSD_CONV = 5
SSD_XBC = 640
SWA_Q_HEADS = 6
SWA_SIDE = 128
SWA_HEAD_ORDER = (0, 3, 1, 4, 2, 5)
ML_HEADS = 4
ML_DIM = 96
ML_PAD = 128
DIL_CONFIGS = ((128, 1), (512, 4), (2048, 16))
DIL_HEADS = 6
DIL_SIDE = 64
DIL_BATCH = 4
N_EXPERTS = 32
TOP_K = 4
D_EXPERT = 1024
SWIGLU_LIMIT = 7.0
SWIGLU_ALPHA = 1.702

IN_SPLITS = (384, 640, 12, 384, 128, 128, 384, 384, 384, 384, 8, 8, 1152, 384, 384)

COL_A = 0
COL_G = 1024
COL_B = 1152
COL_C = 1792
COL_D = 3840
COL_END = 5760
LANE_IG = 16
LANE_FG = 24

NEG = -1e30
VMEM_LIMIT = 56 * 1024 * 1024
OUT_SUB = 128
MOE_ROWS = 128
MOE_GROUP = 4
MOE_TILE = 4096


def _cparams(sem):
    return pltpu.CompilerParams(dimension_semantics=sem, vmem_limit_bytes=VMEM_LIMIT)


def _dot(a, b):
    return jnp.dot(a, b, preferred_element_type=F32)


def _dot_nt(a, b):
    return lax.dot_general(a, b, (((1,), (1,)), ((), ())), preferred_element_type=F32)


def _dot_tn(a, b):
    return lax.dot_general(a, b, (((0,), (0,)), ((), ())), preferred_element_type=F32)


def _sigmoid(x):
    return 1.0 / (1.0 + jnp.exp(-x))


def _softplus(x):
    return jnp.maximum(x, 0.0) + jnp.log(1.0 + jnp.exp(-jnp.abs(x)))


def _prefix_rows(x):
    row = lax.broadcasted_iota(jnp.int32, x.shape, 0)
    s = 1
    while s < x.shape[0]:
        x = x + jnp.where(row >= s, pltpu.roll(x, s, 0), 0.0)
        s *= 2
    return x


def _mod_kernel(c_ref, w_ref, b_ref, o_ref):
    c = c_ref[...]
    s = c * _sigmoid(c)
    o_ref[0] = jnp.dot(s, w_ref[0], preferred_element_type=F32,
                       precision=lax.Precision.HIGHEST) + b_ref[0]


def _mod_call(c, w_mod, b_mod):
    nb = c.shape[0]
    depth, d, cols = w_mod.shape
    tn = 1536
    return pl.pallas_call(
        _mod_kernel,
        grid=(depth, cols // tn),
        in_specs=[pl.BlockSpec((nb, d), lambda l, j: (0, 0)),
                  pl.BlockSpec((1, d, tn), lambda l, j: (l, 0, j)),
                  pl.BlockSpec((1, 1, tn), lambda l, j: (l, 0, j))],
        out_specs=pl.BlockSpec((1, nb, tn), lambda l, j: (l, 0, j)),
        out_shape=jax.ShapeDtypeStruct((depth, nb, cols), F32),
        compiler_params=_cparams(("parallel", "parallel")),
        name="mod",
    )(c, w_mod, b_mod.reshape(depth, 1, cols))


def _rope(a, rc, rs1, rs2):
    return a * rc + pltpu.roll(a, 8, 1) * rs1 + pltpu.roll(a, 120, 1) * rs2


def _in_proj_kernel(*refs, fused, tiles_a):
    if fused:
        (x_ref, f_ref, g2_ref, pg_ref, sc_ref, sh_ref, g_ref, w_ref, rc_ref, rs1_ref, rs2_ref,
         oa_ref, og_ref, ob_ref, oc_ref, od_ref, x2_ref) = refs
        f = f_ref[...]
        fms = jnp.mean(f * f, axis=-1, keepdims=True)
        x = x_ref[...] + g2_ref[0] * (f * lax.rsqrt(fms + NORM_EPS) * pg_ref[...])
        x2_ref[...] = x
    elif tiles_a is not None:
        (xa_ref, xb_ref, sc_ref, sh_ref, g_ref, w_ref, rc_ref, rs1_ref, rs2_ref,
         oa_ref, og_ref, ob_ref, oc_ref, od_ref) = refs
        x = jnp.where(pl.program_id(0) < tiles_a, xa_ref[...], xb_ref[...])
    else:
        (x_ref, sc_ref, sh_ref, g_ref, w_ref, rc_ref, rs1_ref, rs2_ref,
         oa_ref, og_ref, ob_ref, oc_ref, od_ref) = refs
        x = x_ref[...]
    ms = jnp.mean(x * x, axis=-1, keepdims=True)
    h = x * lax.rsqrt(ms + NORM_EPS) * g_ref[...]
    h = h * (1.0 + sc_ref[0]) + sh_ref[0]
    hb = h.astype(BF16)
    rc, rs1, rs2 = rc_ref[...], rs1_ref[...], rs2_ref[...]

    def mm(c0, width):
        return _dot(hb, w_ref[:, c0:c0 + width])

    def plain(o_ref, col0, dst0, width, scale=None, step=512, add=None):
        for c in range(0, width, step):
            wd = min(step, width - c)
            a = mm(col0 + c, wd)
            if scale is not None:
                a = a * scale
            if add is not None:
                a = a + add
            o_ref[:, dst0 + c:dst0 + c + wd] = a.astype(o_ref.dtype)

    def roped(o_ref, col0, dst0, width, scale):
        for c in range(0, width, 128):
            a = _rope(mm(col0 + c, 128), rc, rs1, rs2)
            if scale is not None:
                a = a * scale
            o_ref[:, dst0 + c:dst0 + c + 128] = a.astype(o_ref.dtype)

    qscale = HEAD_DIM ** -0.5
    plain(oa_ref, COL_A, 0, 1024)
    plain(og_ref, COL_G, 0, 128)
    roped(ob_ref, COL_B, 0, 384, qscale)
    roped(ob_ref, COL_B + 384, 384, 128, None)
    plain(ob_ref, COL_B + 512, 512, 128)
    plain(oc_ref, COL_C, 0, 512)
    plain(oc_ref, COL_C + 512, 512, 512, scale=ML_DIM ** -0.5)
    lane_in_head = lax.broadcasted_iota(jnp.int32, (1, 512), 1) % ML_PAD
    plain(oc_ref, COL_C + 1024, 1024, 512, add=jnp.where(lane_in_head == ML_DIM, 1.0, 0.0))
    plain(oc_ref, COL_C + 1536, 1536, 512)
    roped(od_ref, COL_D, 0, 1152, qscale)
    roped(od_ref, COL_D + 1152, 1152, 384, None)
    plain(od_ref, COL_D + 1536, 1536, 384, step=384)


def _split_specs(xa, xb, tm):
    tiles_a = xa.shape[0] // tm
    d = xa.shape[1]
    return tiles_a, [pl.BlockSpec((tm, d), lambda i: (jnp.minimum(i, tiles_a - 1), 0)),
                     pl.BlockSpec((tm, d), lambda i: (jnp.maximum(i - tiles_a, 0), 0))]


def _in_proj_call(x, sc, sh, g, w, rc, rs1, rs2, seq_len, ffn=None, tm=512):
    split = isinstance(x, tuple)
    n = sum(a.shape[0] for a in x) if split else x.shape[0]
    d = D_MODEL
    per_seq = seq_len // tm
    row = lambda i: (i, 0)
    seq = lambda i: (i // per_seq, 0, 0)
    pos = lambda i: (i % per_seq, 0)
    const = lambda i: (0, 0)
    widths = (1024, 128, 640, 2048, 1920)
    dtypes = (BF16, F32, BF16, BF16, BF16)
    fused = ffn is not None
    tiles_a = None
    if split:
        tiles_a, x_specs = _split_specs(x[0], x[1], tm)
        x_args = list(x)
    else:
        x_specs = [pl.BlockSpec((tm, d), row)]
        x_args = [x]
    out_specs = [pl.BlockSpec((tm, wd), row) for wd in widths]
    out_shape = [jax.ShapeDtypeStruct((n, wd), dt) for wd, dt in zip(widths, dtypes)]
    if fused:
        x_specs += [pl.BlockSpec((tm, d), row), pl.BlockSpec((1, 1, d), seq), pl.BlockSpec((1, d), const)]
        x_args += list(ffn)
        out_specs.append(pl.BlockSpec((tm, d), row))
        out_shape.append(jax.ShapeDtypeStruct((n, d), F32))
    return pl.pallas_call(
        functools.partial(_in_proj_kernel, fused=fused, tiles_a=tiles_a),
        grid=(n // tm,),
        in_specs=x_specs + [pl.BlockSpec((1, 1, d), seq),
                            pl.BlockSpec((1, 1, d), seq),
                            pl.BlockSpec((1, d), const),
                            pl.BlockSpec((d, COL_END), const, pipeline_mode=pl.Buffered(1)),
                            pl.BlockSpec((tm, 128), pos),
                            pl.BlockSpec((tm, 128), pos),
                            pl.BlockSpec((tm, 128), pos)],
        out_specs=out_specs,
        out_shape=out_shape,
        compiler_params=_cparams(("parallel",)),
        name="in_proj",
    )(*x_args, sc, sh, g, w, rc, rs1, rs2)


def _ssd_kernel(ua_ref, ug_ref, cw_ref, cb_ref, alog_ref, dtb_ref, dsk_ref, ng_ref, o_ref,
                xp_ref, xa_ref, cum_ref, dtd_ref, tr_ref, y_ref, y2_ref, sf_ref, sb_ref,
                *, seq_len):
    nc = seq_len // CHUNK
    q = CHUNK
    xp_ref[0:8, :] = jnp.zeros((8, SSD_XBC), F32)
    xp_ref[seq_len + 8:seq_len + 16, :] = jnp.zeros((8, SSD_XBC), F32)
    xp_ref[8:seq_len + 8, :] = ua_ref[:, SSD_INNER:SSD_INNER + SSD_XBC].astype(F32)
    sf_ref[...] = jnp.zeros_like(sf_ref)
    sb_ref[...] = jnp.zeros_like(sb_ref)
    a_row = -jnp.exp(alog_ref[...])
    lane = lax.broadcasted_iota(jnp.int32, (q, 128), 1)

    def prep(c, carry):
        r0 = pl.multiple_of(c * q, q)
        win = xp_ref[pl.ds(r0, q + 16), :]
        conv = cb_ref[...] + cw_ref[0:1, :] * win[6:6 + q]
        for k in range(1, SSD_CONV):
            conv = conv + cw_ref[k:k + 1, :] * win[6 + k:6 + k + q]
        xa_ref[pl.ds(r0, q), :] = conv * _sigmoid(conv)
        dt = _softplus(ug_ref[pl.ds(r0, q), :] + dtb_ref[...])
        dta = dt * a_row
        p = _prefix_rows(dta)
        s = p[q - 1:q, :] - p + dta
        cum = jnp.where(lane < SSD_HEADS, p, s)
        cum_ref[pl.ds(r0, q), :] = cum
        dtd_ref[pl.ds(r0, q), :] = dt
        packed = jnp.where(lane < 16, cum, pltpu.roll(dt, 16, 1))
        tr_ref[c] = packed.T
        return carry

    lax.fori_loop(0, nc, prep, 0)

    ri = lax.broadcasted_iota(jnp.int32, (q, q), 0)
    ci = lax.broadcasted_iota(jnp.int32, (q, q), 1)
    lower_incl = ci <= ri
    lower = ci < ri
    upper = ci > ri

    def main(c, carry):
        r0 = pl.multiple_of(c * q, q)
        cbk = nc - 1 - c
        r1 = pl.multiple_of(cbk * q, q)
        tr = tr_ref[c]
        visits = []
        for rr in (r0, r1):
            xa = xa_ref[pl.ds(rr, q), :]
            bt = xa[:, SSD_INNER:SSD_INNER + 128].T.astype(BF16)
            visits.append(dict(
                xa=xa, cum=cum_ref[pl.ds(rr, q), :], dtd=dtd_ref[pl.ds(rr, q), :],
                bgt=[bt[g * 64:(g + 1) * 64, :] for g in range(2)],
                bg=[xa[:, SSD_INNER + g * 64:SSD_INNER + (g + 1) * 64].astype(BF16) for g in range(2)],
                cg=[xa[:, SSD_INNER + 128 + g * 64:SSD_INNER + 128 + (g + 1) * 64].astype(BF16)
                    for g in range(2)]))
        fw, bw = visits
        heads = range(SSD_HEADS)
        cbs = [_dot_nt(fw['cg'][g], fw['bg'][g]) for g in range(2)]
        ws = []
        for h in heads:
            cum = fw['cum']
            cf_col, rb_col = cum[:, h:h + 1], cum[:, 6 + h:7 + h]
            cf_row, rb_row = tr[h:h + 1, :], tr[6 + h:7 + h, :]
            dtf_row, dtb_row = tr[16 + h:17 + h, :], tr[22 + h:23 + h, :]
            arg = jnp.where(lower_incl, cf_col - cf_row, rb_col - rb_row)
            fac = jnp.where(lower, dtf_row, jnp.where(upper, dtb_row, dtf_row + dtb_row))
            ws.append((cbs[h // 3] * jnp.exp(arg) * fac).astype(BF16))
        xf = [fw['xa'][:, h * 64:(h + 1) * 64] for h in heads]
        xb = [bw['xa'][:, h * 64:(h + 1) * 64] for h in heads]
        sfs = [sf_ref[h] for h in heads]
        sbs = [sb_ref[h] for h in heads]
        y_intra = [_dot(ws[h], xf[h].astype(BF16)) for h in heads]
        y_carry_f = [_dot(fw['cg'][h // 3], sfs[h].astype(BF16)) for h in heads]
        y_carry_b = [_dot(bw['cg'][h // 3], sbs[h].astype(BF16)) for h in heads]
        y_ref[pl.ds(r0, q), :] = jnp.concatenate(
            [y_intra[h] + y_carry_f[h] * jnp.exp(fw['cum'][:, h:h + 1]) for h in heads], axis=1)
        y2_ref[pl.ds(r1, q), :] = jnp.concatenate(
            [y_carry_b[h] * jnp.exp(bw['cum'][:, 6 + h:7 + h]) for h in heads], axis=1)
        xws, decays = [], []
        for h in heads:
            tf = fw['cum'][q - 1:q, h:h + 1]
            wcol = jnp.exp(tf - fw['cum'][:, h:h + 1]) * fw['dtd'][:, h:h + 1]
            xws.append((xf[h] * wcol).astype(BF16))
            decays.append(jnp.exp(tf))
        for h in heads:
            tb = bw['cum'][0:1, 6 + h:7 + h]
            wcol = jnp.exp(tb - bw['cum'][:, 6 + h:7 + h]) * bw['dtd'][:, 6 + h:7 + h]
            xws.append((xb[h] * wcol).astype(BF16))
            decays.append(jnp.exp(tb))
        for h in heads:
            sf_ref[h] = sfs[h] * decays[h] + _dot(fw['bgt'][h // 3], xws[h])
        for h in heads:
            sb_ref[h] = sbs[h] * decays[6 + h] + _dot(bw['bgt'][h // 3], xws[6 + h])
        return carry

    lax.fori_loop(0, nc, main, 0)

    def fin(c, carry):
        r0 = pl.multiple_of(c * q, q)
        xs = xa_ref[pl.ds(r0, q), 0:SSD_INNER]
        y = y_ref[pl.ds(r0, q), :] + y2_ref[pl.ds(r0, q), :] + xs * dsk_ref[...]
        z = ua_ref[pl.ds(r0, q), 0:SSD_INNER].astype(F32)
        v = y * (z * _sigmoid(z))
        ms = jnp.mean(v * v, axis=-1, keepdims=True)
        o_ref[pl.ds(r0, q), :] = (v * lax.rsqrt(ms + NORM_EPS) * ng_ref[...]).astype(o_ref.dtype)
        return carry

    lax.fori_loop(0, nc, fin, 0)


def _ssd_call(ua, ug, cw, cb, alog, dtb, dsk, ng, seq_len):
    n = ua.shape[0]
    nseq = n // seq_len
    nc = seq_len // CHUNK
    seq = lambda b: (b, 0)
    const = lambda b: (0, 0)
    return pl.pallas_call(
        functools.partial(_ssd_kernel, seq_len=seq_len),
        grid=(nseq,),
        in_specs=[pl.BlockSpec((seq_len, 1024), seq),
                  pl.BlockSpec((seq_len, 128), seq),
                  pl.BlockSpec((8, SSD_XBC), const),
                  pl.BlockSpec((1, SSD_XBC), const),
                  pl.BlockSpec((1, 128), const),
                  pl.BlockSpec((1, 128), const),
                  pl.BlockSpec((1, SSD_INNER), const),
                  pl.BlockSpec((1, SSD_INNER), const)],
        out_specs=pl.BlockSpec((seq_len, SSD_INNER), seq),
        out_shape=jax.ShapeDtypeStruct((n, SSD_INNER), BF16),
        scratch_shapes=[pltpu.VMEM((seq_len + 16, SSD_XBC), F32),
                        pltpu.VMEM((seq_len, SSD_XBC), F32),
                        pltpu.VMEM((seq_len, 128), F32),
                        pltpu.VMEM((seq_len, 128), F32),
                        pltpu.VMEM((nc, 128, CHUNK), F32),
                        pltpu.VMEM((seq_len, SSD_INNER), F32),
                        pltpu.VMEM((seq_len, SSD_INNER), F32),
                        pltpu.VMEM((SSD_HEADS, SSD_STATE, HEAD_DIM), F32),
                        pltpu.VMEM((SSD_HEADS, SSD_STATE, HEAD_DIM), F32)],
        compiler_params=_cparams(("parallel",)),
        name="ssd",
    )(ua, ug, cw, cb, alog, dtb, dsk, ng)


def _swa_kernel(sink_ref, u_ref, o_ref, kp_ref, vp_ref, bias_ref, *, seq_len):
    q = CHUNK
    nb = seq_len // q
    zeros = jnp.zeros((q, 128), BF16)
    kp_ref[0:q, :] = zeros
    vp_ref[0:q, :] = zeros
    kp_ref[seq_len + q:seq_len + 2 * q, :] = zeros
    vp_ref[seq_len + q:seq_len + 2 * q, :] = zeros
    kp_ref[q:seq_len + q, :] = u_ref[:, 384:512]
    vp_ref[q:seq_len + q, :] = u_ref[:, 512:640]
    row = lax.broadcasted_iota(jnp.int32, (3 * q, 3 * q), 0) & (q - 1)
    col = lax.broadcasted_iota(jnp.int32, (3 * q, 3 * q), 1)
    bias_ref[...] = jnp.where(jnp.abs(col - q - row) <= SWA_SIDE, 0.0, NEG)
    rgrp = lax.broadcasted_iota(jnp.int32, (3 * q, 1), 0) // q
    col1 = lax.broadcasted_iota(jnp.int32, (1, 3 * q), 1)
    lane1 = lax.broadcasted_iota(jnp.int32, (1, 128), 1)
    lo_f = jnp.where(lane1 < HEAD_DIM, 1.0, 0.0)
    hi_f = 1.0 - lo_f
    halves = ((lo_f.astype(BF16), lo_f), (hi_f.astype(BF16), hi_f))

    def body(qb, carry):
        q0 = pl.multiple_of(qb * q, q)
        qs = jnp.concatenate([u_ref[pl.ds(q0, q), j * 128:(j + 1) * 128] for j in range(3)], axis=0)
        kw = kp_ref[pl.ds(q0, 3 * q), :]
        vw = vp_ref[pl.ds(q0, 3 * q), :]
        inside = (col1 >= q - q0) & (col1 < seq_len + q - q0)
        bias = bias_ref[...] + jnp.where(inside, 0.0, NEG)
        o = None
        for hk, (sel_b, sel_f) in enumerate(halves):
            s = _dot_nt(qs * sel_b, kw) + bias
            sk = jnp.where(rgrp == 0, sink_ref[3 * hk],
                           jnp.where(rgrp == 1, sink_ref[3 * hk + 1], sink_ref[3 * hk + 2]))
            m = jnp.maximum(jnp.max(s, axis=-1, keepdims=True), sk)
            p = jnp.exp(s - m)
            l = jnp.sum(p, axis=-1, keepdims=True) + jnp.exp(sk - m)
            oh = (_dot(p.astype(BF16), vw) / l) * sel_f
            o = oh if o is None else o + oh
        o_ref[pl.ds(q0, q), :] = jnp.concatenate(
            [o[j * q:(j + 1) * q, :] for j in range(3)], axis=1).astype(o_ref.dtype)
        return carry

    lax.fori_loop(0, nb, body, 0, unroll=2)


def _swa_call(ub, sink, seq_len):
    n = ub.shape[0]
    nseq = n // seq_len
    return pl.pallas_call(
        functools.partial(_swa_kernel, seq_len=seq_len),
        grid=(nseq,),
        in_specs=[pl.BlockSpec(memory_space=pltpu.SMEM),
                  pl.BlockSpec((seq_len, 640), lambda b: (b, 0))],
        out_specs=pl.BlockSpec((seq_len, 384), lambda b: (b, 0)),
        out_shape=jax.ShapeDtypeStruct((n, 384), BF16),
        scratch_shapes=[pltpu.VMEM((seq_len + 2 * CHUNK, 128), BF16),
                        pltpu.VMEM((seq_len + 2 * CHUNK, 128), BF16),
                        pltpu.VMEM((3 * CHUNK, 3 * CHUNK), F32)],
        compiler_params=_cparams(("parallel",)),
        name="swa",
    )(sink, ub)


def _mlstm_kernel(uc_ref, ug_ref, gb_ref, ng_ref, o_ref,
                  gcol_ref, gtr_ref, hacc_ref, ct_ref, m_ref, mask_ref, *, seq_len):
    q = CHUNK
    nc = seq_len // q
    lane = lax.broadcasted_iota(jnp.int32, (q, 128), 1)
    ct_ref[...] = jnp.zeros_like(ct_ref)
    m_ref[...] = jnp.zeros_like(m_ref)
    hacc_ref[...] = jnp.zeros_like(hacc_ref)

    def prep(c, carry):
        r0 = pl.multiple_of(c * q, q)
        raw = ug_ref[pl.ds(r0, q), :] + gb_ref[...]
        lf = jnp.minimum(raw, 0.0) - jnp.log(1.0 + jnp.exp(-jnp.abs(raw)))
        p = _prefix_rows(lf)
        s = p[q - 1:q, :] - p + lf
        g = jnp.where(lane < LANE_FG, raw, jnp.where(lane < LANE_FG + ML_HEADS, p, s))
        gcol_ref[pl.ds(r0, q), :] = g
        gtr_ref[c] = g.T
        return carry

    lax.fori_loop(0, nc, prep, 0)

    ri = lax.broadcasted_iota(jnp.int32, (q, q), 0)
    ci = lax.broadcasted_iota(jnp.int32, (q, q), 1)
    mask_ref[0] = jnp.where(ci <= ri, 0.0, NEG)
    mask_ref[1] = jnp.where(ci >= ri, 0.0, NEG)
    keep_lanes = jnp.where(lax.broadcasted_iota(jnp.int32, (1, ML_PAD), 1) < ML_DIM, 1.0, 0.0)

    def main(c, carry):
        units = []
        for dirn in range(2):
            ck = c if dirn == 0 else nc - 1 - c
            r0 = pl.multiple_of(ck * q, q)
            g = gcol_ref[pl.ds(r0, q), :]
            gt = gtr_ref[ck]
            for h in range(ML_HEADS):
                s_idx = dirn * ML_HEADS + h
                li, lb = LANE_IG + s_idx, LANE_FG + s_idx
                edge = q - 1 if dirn == 0 else 0
                units.append(dict(
                    dirn=dirn, h=h, r0=r0, s_idx=s_idx,
                    b_col=g[:, lb:lb + 1], b_row=gt[lb:lb + 1, :], i_col=g[:, li:li + 1],
                    i_row=gt[li:li + 1, :], tot=g[edge:edge + 1, lb:lb + 1],
                    m_st=m_ref[s_idx][0:1, 0:1], ct=ct_ref[s_idx],
                    qh=uc_ref[pl.ds(r0, q), h * ML_PAD:(h + 1) * ML_PAD],
                    kh=uc_ref[pl.ds(r0, q), 512 + h * ML_PAD:512 + (h + 1) * ML_PAD],
                    vh=uc_ref[pl.ds(r0, q), 1024 + h * ML_PAD:1024 + (h + 1) * ML_PAD]))
        for u in units:
            u['logd'] = u['b_col'] - u['b_row'] + u['i_row'] + mask_ref[u['dirn']]
            u['m_inter'] = u['b_col'] + u['m_st']
        for u in units:
            u['m_t'] = jnp.maximum(jnp.max(u['logd'], axis=-1, keepdims=True), u['m_inter'])
        for u in units:
            u['qk'] = _dot_nt(u['qh'], u['kh'])
            u['qc'] = _dot(u['qh'], u['ct'].astype(BF16))
        for u in units:
            u['sm'] = (u['qk'] * jnp.exp(u['logd'] - u['m_t'])).astype(BF16)
            u['inter'] = jnp.exp(u['m_inter'] - u['m_t'])
        for u in units:
            u['num'] = _dot(u['sm'], u['vh']) + u['inter'] * u['qc']
        for u in units:
            den = u['num'][:, ML_DIM:ML_DIM + 1]
            hh = u['num'] / jnp.maximum(jnp.abs(den), jnp.exp(-u['m_t'])) * keep_lanes
            hacc_ref[pl.ds(u['r0'], q), u['h'] * ML_PAD:(u['h'] + 1) * ML_PAD] += hh
        for u in units:
            tot, m_st = u['tot'], u['m_st']
            m_new = jnp.maximum(tot + m_st,
                                jnp.max(tot - u['b_row'] + u['i_row'], axis=-1, keepdims=True))
            wk_col = jnp.exp(tot - u['b_col'] + u['i_col'] - m_new)
            u['keep'] = jnp.exp(tot + m_st - m_new)
            u['vw'] = (u['vh'].astype(F32) * wk_col).astype(BF16)
            m_ref[u['s_idx']] = jnp.broadcast_to(m_new, (8, 128))
        for u in units:
            ct_ref[u['s_idx']] = u['keep'] * u['ct'] + _dot_tn(u['kh'], u['vw'])
        return carry

    lax.fori_loop(0, nc, main, 0)

    def fin(c, carry):
        r0 = pl.multiple_of(c * q, q)
        outs = []
        for h in range(ML_HEADS):
            hs = hacc_ref[pl.ds(r0, q), h * ML_PAD:(h + 1) * ML_PAD]
            ms = jnp.sum(hs * hs, axis=-1, keepdims=True) * (1.0 / ML_DIM)
            hn = hs * lax.rsqrt(ms + NORM_EPS) * ng_ref[:, h * ML_PAD:(h + 1) * ML_PAD]
            og = uc_ref[pl.ds(r0, q), 1536 + h * ML_PAD:1536 + (h + 1) * ML_PAD].astype(F32)
            outs.append(hn * _sigmoid(og))
        o_ref[pl.ds(r0, q), :] = jnp.concatenate(outs, axis=1).astype(o_ref.dtype)
        return carry

    lax.fori_loop(0, nc, fin, 0)


def _mlstm_call(uc, ug, gb, ng, seq_len):
    n = uc.shape[0]
    nseq = n // seq_len
    nc = seq_len // CHUNK
    seq = lambda b: (b, 0)
    const = lambda b: (0, 0)
    width = ML_HEADS * ML_PAD
    return pl.pallas_call(
        functools.partial(_mlstm_kernel, seq_len=seq_len),
        grid=(nseq,),
        in_specs=[pl.BlockSpec((seq_len, 4 * width), seq),
                  pl.BlockSpec((seq_len, 128), seq),
                  pl.BlockSpec((1, 128), const),
                  pl.BlockSpec((1, width), const)],
        out_specs=pl.BlockSpec((seq_len, width), seq),
        out_shape=jax.ShapeDtypeStruct((n, width), BF16),
        scratch_shapes=[pltpu.VMEM((seq_len, 128), F32),
                        pltpu.VMEM((nc, 128, CHUNK), F32),
                        pltpu.VMEM((seq_len, width), F32),
                        pltpu.VMEM((2 * ML_HEADS, ML_PAD, ML_PAD), F32),
                        pltpu.VMEM((2 * ML_HEADS, 8, 128), F32),
                        pltpu.VMEM((2, CHUNK, CHUNK), F32)],
        compiler_params=_cparams(("parallel",)),
        name="mlstm",
    )(uc, ug, gb, ng)


def _attend_units(units, lo_b, hi_b, lo_f, hi_f):
    n = units[0][0].shape[0]
    ss = []
    for q, kw, _, bias in units:
        s2 = _dot_nt(jnp.concatenate([q * lo_b, q * hi_b], axis=0), kw)
        ss.append(jnp.concatenate([s2[0:n] + bias, s2[n:2 * n] + bias], axis=0))
    ms = [jnp.max(s, axis=-1, keepdims=True) for s in ss]
    ps = [jnp.exp(s - m) for s, m in zip(ss, ms)]
    ls = [jnp.sum(p, axis=-1, keepdims=True) for p in ps]
    os_ = [_dot(p.astype(BF16), u[2]) / l for p, u, l in zip(ps, units, ls)]
    res = []
    for o2, m, l in zip(os_, ms, ls):
        lse2 = m + jnp.log(l)
        res.append((o2[0:n] * lo_f + o2[n:2 * n] * hi_f, lse2[0:n] * lo_f + lse2[n:2 * n] * hi_f))
    return res


def _dil_kernel(q0_ref, q1_ref, q2_ref, k_ref, v_ref, o_ref,
                qf_ref, kf_ref, vf_ref, kc0_ref, vc0_ref, kc1_ref, vc1_ref, og_ref, lg_ref,
                qc2_ref, kc2_ref, vc2_ref, og2_ref, lg2_ref, band_ref, band2_ref, *, seq_len):
    qr = CHUNK
    pad = DIL_SIDE
    d1, d2 = DIL_CONFIGS[1][1], DIL_CONFIGS[2][1]
    n1, n2 = seq_len // d1, seq_len // d2
    assert n2 == qr and n1 % qr == 0
    qf_ref[0] = q1_ref[...].astype(F32)
    qf_ref[1] = q2_ref[...].astype(F32)
    kf_ref[...] = k_ref[...].astype(F32)
    vf_ref[...] = v_ref[...].astype(F32)
    zeros = jnp.zeros((pad, 128), BF16)
    kc0_ref[0:pad, :] = zeros
    vc0_ref[0:pad, :] = zeros
    kc0_ref[pad + seq_len:2 * pad + seq_len, :] = zeros
    vc0_ref[pad + seq_len:2 * pad + seq_len, :] = zeros
    kc0_ref[pad:pad + seq_len, :] = k_ref[...]
    vc0_ref[pad:pad + seq_len, :] = v_ref[...]
    for r in range(d1):
        kc1_ref[r, 0:pad, :] = zeros
        vc1_ref[r, 0:pad, :] = zeros
        kc1_ref[r, pad + n1:2 * pad + n1, :] = zeros
        vc1_ref[r, pad + n1:2 * pad + n1, :] = zeros
        kc1_ref[r, pad:pad + n1, :] = kf_ref[pl.ds(r, n1, stride=d1), :].astype(BF16)
        vc1_ref[r, pad:pad + n1, :] = vf_ref[pl.ds(r, n1, stride=d1), :].astype(BF16)

    row = lax.broadcasted_iota(jnp.int32, (qr, 2 * qr), 0)
    col = lax.broadcasted_iota(jnp.int32, (qr, 2 * qr), 1)
    band_ref[...] = jnp.where(jnp.abs(col - pad - row) <= DIL_SIDE, 0.0, NEG)
    row2 = lax.broadcasted_iota(jnp.int32, (qr, qr), 0)
    col2 = lax.broadcasted_iota(jnp.int32, (qr, qr), 1)
    band2_ref[...] = jnp.where(jnp.abs(col2 - row2) <= DIL_SIDE, 0.0, NEG)
    lane1 = lax.broadcasted_iota(jnp.int32, (1, 128), 1)
    col1 = lax.broadcasted_iota(jnp.int32, (1, 2 * qr), 1)
    lo_f = jnp.where(lane1 < HEAD_DIM, 1.0, 0.0)
    hi_f = 1.0 - lo_f
    sel = (lo_f.astype(BF16), hi_f.astype(BF16), lo_f, hi_f)

    def window_bias(u0, n):
        inside = (col1 >= pad - u0) & (col1 < n + pad - u0)
        return band_ref[...] + jnp.where(inside, 0.0, NEG)

    def batch0(b4, carry):
        units, starts = [], []
        for i in range(DIL_BATCH):
            u0 = pl.multiple_of((b4 * DIL_BATCH + i) * qr, qr)
            starts.append(u0)
            units.append((q0_ref[pl.ds(u0, qr), :], kc0_ref[pl.ds(u0, 2 * qr), :],
                          vc0_ref[pl.ds(u0, 2 * qr), :], window_bias(u0, seq_len)))
        for u0, (o, lse) in zip(starts, _attend_units(units, *sel)):
            og_ref[0, pl.ds(u0, qr), :] = o
            lg_ref[0, pl.ds(u0, qr), :] = lse
        return carry

    lax.fori_loop(0, seq_len // qr // DIL_BATCH, batch0, 0)

    nb1 = n1 // qr
    assert nb1 == DIL_BATCH

    def batch1(r, carry):
        units, rowsl = [], []
        for i in range(DIL_BATCH):
            u0 = i * qr
            rows = pl.ds(r + u0 * d1, qr, stride=d1)
            rowsl.append(rows)
            units.append((qf_ref[0, rows, :].astype(BF16), kc1_ref[r, u0:u0 + 2 * qr, :],
                          vc1_ref[r, u0:u0 + 2 * qr, :], window_bias(u0, n1)))
        for rows, (o, lse) in zip(rowsl, _attend_units(units, *sel)):
            og_ref[1, rows, :] = o
            lg_ref[1, rows, :] = lse
        return carry

    lax.fori_loop(0, d1, batch1, 0)

    for r in range(d2):
        rows = pl.ds(r, qr, stride=d2)
        qc2_ref[r] = qf_ref[1, rows, :].astype(BF16)
        kc2_ref[r] = kf_ref[rows, :].astype(BF16)
        vc2_ref[r] = vf_ref[rows, :].astype(BF16)

    def batch2(b4, carry):
        rs = [b4 * DIL_BATCH + i for i in range(DIL_BATCH)]
        units = [(qc2_ref[r], kc2_ref[r], vc2_ref[r], band2_ref[...]) for r in rs]
        for r, (o, lse) in zip(rs, _attend_units(units, *sel)):
            og2_ref[r] = o
            lg2_ref[r] = lse
        return carry

    lax.fori_loop(0, d2 // DIL_BATCH, batch2, 0)
    for r in range(d2):
        rows = pl.ds(r, qr, stride=d2)
        og_ref[2, rows, :] = og2_ref[r]
        lg_ref[2, rows, :] = lg2_ref[r]

    def fin(c, carry):
        r0 = pl.multiple_of(c * qr, qr)
        rows = pl.ds(r0, qr)
        l0, l1, l2 = lg_ref[0, rows, :], lg_ref[1, rows, :], lg_ref[2, rows, :]
        m = jnp.maximum(jnp.maximum(l0, l1), l2)
        w0, w1, w2 = jnp.exp(l0 - m), jnp.exp(l1 - m), jnp.exp(l2 - m)
        y = (og_ref[0, rows, :] * w0 + og_ref[1, rows, :] * w1 + og_ref[2, rows, :] * w2)
        o_ref[rows, :] = (y / (w0 + w1 + w2)).astype(o_ref.dtype)
        return carry

    lax.fori_loop(0, seq_len // qr, fin, 0, unroll=2)


def _dil_call(ud, seq_len):
    n = ud.shape[0]
    nseq = n // seq_len
    blk = (seq_len, 128)
    d1, d2 = DIL_CONFIGS[1][1], DIL_CONFIGS[2][1]
    n1 = seq_len // d1

    def col(off):
        return lambda b, p: (b, off + p)

    return pl.pallas_call(
        functools.partial(_dil_kernel, seq_len=seq_len),
        grid=(nseq, 3),
        in_specs=[pl.BlockSpec(blk, col(0)), pl.BlockSpec(blk, col(3)), pl.BlockSpec(blk, col(6)),
                  pl.BlockSpec(blk, col(9)), pl.BlockSpec(blk, col(12))],
        out_specs=pl.BlockSpec(blk, col(0)),
        out_shape=jax.ShapeDtypeStruct((n, 384), BF16),
        scratch_shapes=[pltpu.VMEM((2, seq_len, 128), F32),
                        pltpu.VMEM((seq_len, 128), F32),
                        pltpu.VMEM((seq_len, 128), F32),
                        pltpu.VMEM((seq_len + 2 * DIL_SIDE, 128), BF16),
                        pltpu.VMEM((seq_len + 2 * DIL_SIDE, 128), BF16),
                        pltpu.VMEM((d1, n1 + 2 * DIL_SIDE, 128), BF16),
                        pltpu.VMEM((d1, n1 + 2 * DIL_SIDE, 128), BF16),
                        pltpu.VMEM((3, seq_len, 128), F32),
                        pltpu.VMEM((3, seq_len, 128), F32)]
                       + [pltpu.VMEM((d2, CHUNK, 128), BF16)] * 3
                       + [pltpu.VMEM((d2, CHUNK, 128), F32)] * 2
                       + [pltpu.VMEM((CHUNK, 2 * CHUNK), F32), pltpu.VMEM((CHUNK, CHUNK), F32)],
        compiler_params=_cparams(("parallel", "parallel")),
        name="dil",
    )(ud, ud, ud, ud, ud)


def _out_proj_kernel(ya_ref, yb_ref, yc_ref, yd_ref, *refs, tiles_a):
    if tiles_a is None:
        x_ref = refs[0]
        refs = refs[1:]
    else:
        xa_ref, xb_ref = refs[0:2]
        refs = refs[2:]
    (g1_ref, sc_ref, sh_ref, pg_ref, fg_ref, wa_ref, wb_ref, wc_ref, wd_ref, wrh_ref, wrl_ref, br_ref,
     x1_ref, h2_ref, rt_ref) = refs
    for r0 in range(0, x1_ref.shape[0], OUT_SUB):
        rs = slice(r0, r0 + OUT_SUB)
        if tiles_a is None:
            x_in = x_ref[rs, :]
        else:
            x_in = jnp.where(pl.program_id(0) < tiles_a, xa_ref[rs, :], xb_ref[rs, :])
        y = (_dot(ya_ref[rs, :], wa_ref[...]) + _dot(yb_ref[rs, :], wb_ref[...])
             + _dot(yc_ref[rs, :], wc_ref[...]) + _dot(yd_ref[rs, :], wd_ref[...]))
        ms = jnp.mean(y * y, axis=-1, keepdims=True)
        yn = y * lax.rsqrt(ms + NORM_EPS) * pg_ref[...]
        x1 = x_in + g1_ref[0] * yn
        x1_ref[rs, :] = x1
        ms2 = jnp.mean(x1 * x1, axis=-1, keepdims=True)
        h2 = x1 * lax.rsqrt(ms2 + NORM_EPS) * fg_ref[...]
        h2 = h2 * (1.0 + sc_ref[0]) + sh_ref[0]
        half = h2.shape[1] // 2
        lo = lax.bitcast_convert_type(h2[:, :half].astype(BF16).astype(F32), jnp.uint32)
        hi = lax.bitcast_convert_type(h2[:, half:].astype(BF16).astype(F32), jnp.uint32)
        h2_ref[rs, :] = (lo >> 16) | hi
        h_hi = h2.astype(BF16)
        h_lo = (h2 - h_hi.astype(F32)).astype(BF16)
        logits = (_dot(h_hi, wrh_ref[...]) + _dot(h_lo, wrh_ref[...]) + _dot(h_hi, wrl_ref[...])
                  + br_ref[...])
        lane = lax.broadcasted_iota(jnp.int32, logits.shape, 1)
        vals, idxs = [], []
        for _ in range(TOP_K):
            m = jnp.max(logits, axis=-1, keepdims=True)
            idx = jnp.min(jnp.where(logits == m, lane, 128), axis=-1, keepdims=True)
            vals.append(m)
            idxs.append(idx)
            logits = jnp.where(lane == idx, -3e38, logits)
        es = [jnp.exp(v - vals[0]) for v in vals]
        tot = es[0] + es[1] + es[2] + es[3]
        rt = jnp.zeros(lane.shape, F32)
        for k in range(TOP_K):
            rt = jnp.where(lane == k, idxs[k].astype(F32), rt)
            rt = jnp.where(lane == TOP_K + k, es[k] / tot, rt)
        rt_ref[rs, :] = rt[:, 0:2 * TOP_K]


def _out_proj_call(ya, yb, yc, yd, x, g1, sc, sh, pg, fg, wa, wb, wc, wd, wrh, wrl, br, seq_len, tm=512):
    split = isinstance(x, tuple)
    n, d = ya.shape[0], D_MODEL
    tiles_a = None
    if split:
        tiles_a, x_specs = _split_specs(x[0], x[1], tm)
        x_args = list(x)
    else:
        x_specs = [pl.BlockSpec((tm, d), lambda i: (i, 0))]
        x_args = [x]
    per_seq = seq_len // tm
    row = lambda i: (i, 0)
    seq = lambda i: (i // per_seq, 0, 0)
    const = lambda i: (0, 0)

    def full(a):
        return pl.BlockSpec(a.shape, const)

    return pl.pallas_call(
        functools.partial(_out_proj_kernel, tiles_a=tiles_a),
        grid=(n // tm,),
        in_specs=[pl.BlockSpec((tm, ya.shape[1]), row), pl.BlockSpec((tm, yb.shape[1]), row),
                  pl.BlockSpec((tm, yc.shape[1]), row), pl.BlockSpec((tm, yd.shape[1]), row)]
        + x_specs
        + [pl.BlockSpec((1, 1, d), seq), pl.BlockSpec((1, 1, d), seq),
                  pl.BlockSpec((1, 1, d), seq),
                  full(pg), full(fg), full(wa), full(wb), full(wc), full(wd), full(wrh), full(wrl), full(br)],
        out_specs=[pl.BlockSpec((tm, d), row), pl.BlockSpec((tm, d // 2), row),
                   pl.BlockSpec((tm, 2 * TOP_K), row)],
        out_shape=[jax.ShapeDtypeStruct((n, d), F32), jax.ShapeDtypeStruct((n, d // 2), jnp.uint32),
                   jax.ShapeDtypeStruct((n, 2 * TOP_K), F32)],
        compiler_params=_cparams(("parallel",)),
        name="out_proj",
    )(ya, yb, yc, yd, *x_args, g1, sc, sh, pg, fg, wa, wb, wc, wd, wrh, wrl, br)


def _moe_slots(tile):
    return tile * TOP_K // MOE_ROWS + N_EXPERTS + 2


def _moe_kernel(nb_ref, bf_ref, bs_ref, bv_ref, idx_ref, gate_ref, h_ref,
                wgu_ref, bgu_ref, wdn_ref, bdn_ref, o_ref, xs0_ref, xs1_ref, ys0_ref, ys1_ref):
    t = pl.program_id(0)
    e = pl.program_id(1)
    nb = nb_ref[t]
    rows = MOE_ROWS
    nslot = _moe_slots(h_ref.shape[0])
    half = h_ref.shape[1]

    def gather(dst_ref, blk):
        base = bs_ref[t * nslot + blk]
        for r in range(rows):
            i = idx_ref[0, 0, base + r]
            dst_ref[r:r + 1, :] = h_ref[pl.ds(i, 1), :]

    def scatter(src_ref, blk, nvalid):
        base = bs_ref[t * nslot + blk]
        i0 = idx_ref[0, 0, base]
        for r0 in range(0, rows, MOE_GROUP):
            ks = range(MOE_GROUP)
            ok = [r0 + k < nvalid for k in ks]
            ii = [jnp.where(ok[k], idx_ref[0, 0, base + r0 + k], i0) for k in ks]
            gg = [jnp.where(ok[k], gate_ref[0, 0, base + r0 + k], 0.0) for k in ks]
            cur = [o_ref[pl.ds(ii[k], 1), :] for k in ks]
            for k in reversed(ks):
                o_ref[pl.ds(ii[k], 1), :] = cur[k] + gg[k] * src_ref[r0 + k:r0 + k + 1, :]

    def expert(xs_ref, ys_ref):
        xu = xs_ref[...]
        x_lo = lax.bitcast_convert_type(xu << 16, F32).astype(BF16)
        x_hi = lax.bitcast_convert_type(xu & jnp.uint32(0xFFFF0000), F32).astype(BF16)
        gu = _dot(x_lo, wgu_ref[0, 0:half, :]) + _dot(x_hi, wgu_ref[0, half:2 * half, :]) + bgu_ref[0]
        gate = jnp.minimum(gu[:, :D_EXPERT], SWIGLU_LIMIT)
        up = jnp.clip(gu[:, D_EXPERT:], -SWIGLU_LIMIT, SWIGLU_LIMIT)
        act = (up + 1.0) * gate * _sigmoid(SWIGLU_ALPHA * gate)
        ys_ref[...] = _dot(act.astype(BF16), wdn_ref[0]) + bdn_ref[0]

    @pl.when(e == 0)
    def _():
        o_ref[...] = jnp.zeros_like(o_ref)
        ys1_ref[...] = jnp.zeros_like(ys1_ref)
        gather(xs0_ref, 0)

    bufs = ((xs0_ref, xs1_ref, ys0_ref, ys1_ref), (xs1_ref, xs0_ref, ys1_ref, ys0_ref))

    def block(b, carry):
        prev = jnp.maximum(b - 1, 0)
        nv_prev = jnp.where(b > 0, bv_ref[t * nslot + prev], 0)
        for par, (xs_cur, xs_nxt, ys_cur, ys_prv) in enumerate(bufs):
            @pl.when(b % 2 == par)
            def _(xs_cur=xs_cur, xs_nxt=xs_nxt, ys_cur=ys_cur, ys_prv=ys_prv):
                expert(xs_cur, ys_cur)
                scatter(ys_prv, prev, nv_prev)
                gather(xs_nxt, b + 1)
        return carry

    lax.fori_loop(bf_ref[t * (N_EXPERTS + 1) + e], bf_ref[t * (N_EXPERTS + 1) + e + 1], block, 0)

    for par, (_, _, _, ys_prv) in enumerate(bufs):
        @pl.when((e == N_EXPERTS - 1) & (nb % 2 == par))
        def _(ys_prv=ys_prv):
            scatter(ys_prv, nb - 1, bv_ref[t * nslot + nb - 1])


def _moe_call(nb, bfirst, bs, bv, idx, gates, h2p, wgu, bgu, wdn, bdn, tile):
    n, half = h2p.shape
    d = 2 * half
    ntile = n // tile
    plen = idx.shape[-1]
    wmap = lambda t, e, *_: (e, 0, 0)
    tmap3 = lambda t, e, *_: (t, 0, 0)
    tmap2 = lambda t, e, *_: (t, 0)
    grid_spec = pltpu.PrefetchScalarGridSpec(
        num_scalar_prefetch=4,
        grid=(ntile, N_EXPERTS),
        in_specs=[pl.BlockSpec((1, 1, plen), tmap3, memory_space=pltpu.SMEM),
                  pl.BlockSpec((1, 1, plen), tmap3, memory_space=pltpu.SMEM),
                  pl.BlockSpec((tile, half), tmap2, pipeline_mode=pl.Buffered(1)),
                  pl.BlockSpec((1, d, 2 * D_EXPERT), wmap), pl.BlockSpec((1, 1, 2 * D_EXPERT), wmap),
                  pl.BlockSpec((1, D_EXPERT, d), wmap), pl.BlockSpec((1, 1, d), wmap)],
        out_specs=pl.BlockSpec((tile, d), tmap2, pipeline_mode=pl.Buffered(1)),
        scratch_shapes=[pltpu.VMEM((MOE_ROWS, half), jnp.uint32)] * 2 + [pltpu.VMEM((MOE_ROWS, d), F32)] * 2,
    )
    return pl.pallas_call(
        _moe_kernel,
        grid_spec=grid_spec,
        out_shape=jax.ShapeDtypeStruct((n, d), F32),
        compiler_params=_cparams(("parallel", "arbitrary")),
        name="moe",
    )(nb, bfirst, bs, bv, idx, gates, h2p, wgu, bgu, wdn, bdn)


def _route(rt, tile):
    n = rt.shape[0]
    ntile = n // tile
    npair = tile * TOP_K
    nslot = _moe_slots(tile)
    e = rt[:, :TOP_K].astype(jnp.int32).reshape(ntile, npair)
    g = rt[:, TOP_K:].reshape(ntile, npair)
    tok = jnp.broadcast_to((jnp.arange(npair, dtype=jnp.int32) // TOP_K)[None, :], (ntile, npair))
    _, tok_s, g_s = lax.sort((e, tok, g), dimension=1, is_stable=True, num_keys=1)
    experts = jnp.arange(N_EXPERTS, dtype=jnp.int32)
    counts = jnp.sum(e[:, :, None] == experts, axis=1, dtype=jnp.int32)
    nblocks = (counts + MOE_ROWS - 1) // MOE_ROWS
    bends = jnp.cumsum(nblocks, axis=-1)
    starts = jnp.cumsum(counts, axis=-1) - counts
    nb = bends[:, -1]
    slot = jnp.minimum(jnp.arange(nslot, dtype=jnp.int32)[None, :], nb[:, None] - 1)
    be = jnp.sum(slot[:, :, None] >= bends[:, None, :], axis=-1, dtype=jnp.int32)
    onehot = (be[:, :, None] == experts).astype(jnp.int32)
    pick = lambda v: jnp.sum(onehot * v[:, None, :], axis=-1)
    within = slot - pick(bends - nblocks)
    bs = pick(starts) + within * MOE_ROWS
    bv = jnp.clip(pick(counts) - within * MOE_ROWS, 0, MOE_ROWS)
    padw = ((0, 0), (0, MOE_ROWS))
    idx = jnp.pad(tok_s, padw).reshape(ntile, 1, -1)
    gl = jnp.pad(g_s, padw).reshape(ntile, 1, -1)
    bfirst = jnp.concatenate([jnp.zeros((ntile, 1), jnp.int32), bends], axis=-1)
    flat = lambda v: v.reshape(-1).astype(jnp.int32)
    return flat(nb), flat(bfirst), flat(bs), flat(bv), idx, gl


def _ffn_res_kernel(x_ref, f_ref, g2_ref, pg_ref, oa_ref, ob_ref, *, tiles_a):
    f = f_ref[...]
    ms = jnp.mean(f * f, axis=-1, keepdims=True)
    y = x_ref[...] + g2_ref[0] * (f * lax.rsqrt(ms + NORM_EPS) * pg_ref[...])
    i = pl.program_id(0)

    @pl.when(i < tiles_a)
    def _():
        oa_ref[...] = y

    @pl.when(i >= tiles_a)
    def _():
        ob_ref[...] = y


def _ffn_res_call(x1, f, g2, pg, seq_len, rows_a, tm=1024):
    n, d = x1.shape
    per_seq = seq_len // tm
    tiles_a = rows_a // tm
    row = lambda i: (i, 0)
    return pl.pallas_call(
        functools.partial(_ffn_res_kernel, tiles_a=tiles_a),
        grid=(n // tm,),
        in_specs=[pl.BlockSpec((tm, d), row), pl.BlockSpec((tm, d), row),
                  pl.BlockSpec((1, 1, d), lambda i: (i // per_seq, 0, 0)),
                  pl.BlockSpec((1, d), lambda i: (0, 0))],
        out_specs=[pl.BlockSpec((tm, d), lambda i: (jnp.minimum(i, tiles_a - 1), 0)),
                   pl.BlockSpec((tm, d), lambda i: (jnp.maximum(i - tiles_a, 0), 0))],
        out_shape=[jax.ShapeDtypeStruct((rows_a, d), F32), jax.ShapeDtypeStruct((n - rows_a, d), F32)],
        compiler_params=_cparams(("arbitrary",)),
        name="ffn_res",
    )(x1, f, g2, pg)


def _rope_tables(seq_len):
    half = 8
    inv = 1.0 / (ROPE_THETA ** (jnp.arange(half, dtype=F32) * (2.0 / 16)))
    ang = jnp.arange(seq_len, dtype=F32)[:, None] * inv[None, :]
    cos, sin = jnp.cos(ang), jnp.sin(ang)
    ones = jnp.ones((seq_len, 48), F32)
    zeros8 = jnp.zeros((seq_len, 8), F32)
    zeros48 = jnp.zeros((seq_len, 48), F32)
    rc = jnp.concatenate([cos, cos, ones], axis=1)
    rs1 = jnp.concatenate([zeros8, sin, zeros48], axis=1)
    rs2 = jnp.concatenate([-sin, zeros8, zeros48], axis=1)
    tile2 = lambda a: jnp.concatenate([a, a], axis=1)
    return tile2(rc), tile2(rs1), tile2(rs2)


def _pad_heads(w):
    lead = w.shape[:-1]
    w = w.reshape(lead + (ML_HEADS, ML_DIM))
    w = jnp.pad(w, [(0, 0)] * len(lead) + [(0, 0), (0, ML_PAD - ML_DIM)])
    return w.reshape(lead + (ML_HEADS * ML_PAD,))


def _layout_w_in(w_in):
    d = w_in.shape[0]
    offs = np.cumsum((0,) + IN_SPLITS)
    parts = [w_in[:, offs[i]:offs[i + 1]] for i in range(len(IN_SPLITS))]
    (z, xbc, dt, sq, sk, sv, mq, mk, mv, mo, mi, mf, dq, dk, dv) = parts
    zc = lambda k: jnp.zeros((d, k), w_in.dtype)
    gates = jnp.concatenate([dt, zc(4), mi, mf, zc(128 - 32)], axis=1)
    sq = sq.reshape(d, SWA_Q_HEADS, HEAD_DIM)[:, SWA_HEAD_ORDER, :].reshape(d, -1)
    w = jnp.concatenate([z, xbc, gates, sq, sk, sv,
                         _pad_heads(mq), _pad_heads(mk), _pad_heads(mv), _pad_heads(mo),
                         dq, dk, dv], axis=1)
    assert w.shape[1] == COL_END
    return w.astype(BF16)


def _lane_row(parts, width=128):
    row = jnp.zeros((width,), F32)
    for off, v in parts:
        row = row.at[off:off + v.shape[0]].set(v.astype(F32))
    return row.reshape(1, width)


def _layer(x, pending, mod_l, p, l, tabs, seq_len):
    d = D_MODEL
    nseq = mod_l.shape[0]
    sh1, sc1, g1, sh2, sc2, g2 = [mod_l[:, i * d:(i + 1) * d].reshape(nseq, 1, d) for i in range(6)]
    row = lambda v: v.reshape(1, -1).astype(F32)

    outs = _in_proj_call(x, sc1, sh1, row(p['pre_mix_g'][l]), _layout_w_in(p['w_in'][l]),
                         *tabs, seq_len, ffn=pending)
    if pending is not None:
        x = outs[5]
    ua, ug, ub, uc, ud = outs[:5]
    cw = jnp.pad(p['ssd_conv_w'][l], ((0, 8 - SSD_CONV), (0, 0)))
    ya = _ssd_call(ua, ug, cw, row(p['ssd_conv_b'][l]),
                   _lane_row([(0, p['ssd_a_log'][l].reshape(-1))]),
                   _lane_row([(0, p['ssd_dt_bias'][l].reshape(-1))]),
                   row(jnp.repeat(p['ssd_d'][l], HEAD_DIM)), row(p['ssd_norm_g'][l]), seq_len)
    yb = _swa_call(ub, p['swa_sink'][l].astype(F32), seq_len)
    gb = _lane_row([(LANE_IG, p['mlstm_i_bias'][l].reshape(-1)),
                    (LANE_FG, p['mlstm_f_bias'][l].reshape(-1))])
    yc = _mlstm_call(uc, ug, gb, row(_pad_heads(p['mlstm_norm_g'][l])), seq_len)
    yd = _dil_call(ud, seq_len)

    w_out = p['w_out'][l]
    wa = w_out[0:384].astype(BF16)
    wb = w_out[384:768].reshape(SWA_Q_HEADS, HEAD_DIM, d)[SWA_HEAD_ORDER, :, :].reshape(384, d).astype(BF16)
    wc = jnp.pad(w_out[768:1152].reshape(ML_HEADS, ML_DIM, d),
                 ((0, 0), (0, ML_PAD - ML_DIM), (0, 0))).reshape(ML_HEADS * ML_PAD, d).astype(BF16)
    wd = w_out[1152:1536].astype(BF16)
    wr = jnp.pad(p['w_router'][l], ((0, 0), (0, 128 - N_EXPERTS))).astype(F32)
    wrh = wr.astype(BF16)
    wrl = (wr - wrh.astype(F32)).astype(BF16)
    br = jnp.concatenate([p['b_router'][l].astype(F32), jnp.full((128 - N_EXPERTS,), NEG, F32)]).reshape(1, 128)
    x1, h2, rt = _out_proj_call(ya, yb, yc, yd, x, g1, sc2, sh2, row(p['post_mix_g'][l]),
                                row(p['pre_ffn_g'][l]), wa, wb, wc, wd, wrh, wrl, br, seq_len)

    tile = min(MOE_TILE, x1.shape[0])
    nb, bfirst, bs, bv, idx, gs = _route(rt, tile)
    f = _moe_call(nb, bfirst, bs, bv, idx, gs, h2, p['w_gate_up'][l].astype(BF16),
                  p['b_gate_up'][l].reshape(N_EXPERTS, 1, -1), p['w_down'][l].astype(BF16),
                  p['b_down'][l].reshape(N_EXPERTS, 1, -1), tile)
    return x1, (f, g2, row(p['post_ffn_g'][l]))


def _trunk(xa, xb, c, p):
    nseq_a, seq_len, d = xa.shape
    nseq = nseq_a + xb.shape[0]
    depth = p['w_in'].shape[0]
    mod = _mod_call(c, p['w_mod'], p['b_mod'])
    tabs = _rope_tables(seq_len)
    x = (xa.reshape(-1, d), xb.reshape(-1, d))
    pending = None
    for l in range(depth):
        x, pending = _layer(x, pending, mod[l], p, l, tabs, seq_len)
    ya, yb = _ffn_res_call(x, *pending, seq_len, nseq_a * seq_len)
    return ya.reshape(nseq_a, seq_len, d), yb.reshape(nseq - nseq_a, seq_len, d)


def kernel(x_prompt, x_sample, c_prompt, c_sample, w_mod, b_mod, pre_mix_g, post_mix_g, pre_ffn_g, post_ffn_g, w_in, w_out, ssd_conv_w, ssd_conv_b, ssd_a_log, ssd_dt_bias, ssd_d, ssd_norm_g, swa_sink, mlstm_i_bias, mlstm_f_bias, mlstm_norm_g, w_router, b_router, w_gate_up, b_gate_up, w_down, b_down):
    p = dict(w_mod=w_mod, b_mod=b_mod, pre_mix_g=pre_mix_g, post_mix_g=post_mix_g, pre_ffn_g=pre_ffn_g,
             post_ffn_g=post_ffn_g, w_in=w_in, w_out=w_out, ssd_conv_w=ssd_conv_w, ssd_conv_b=ssd_conv_b,
             ssd_a_log=ssd_a_log, ssd_dt_bias=ssd_dt_bias, ssd_d=ssd_d, ssd_norm_g=ssd_norm_g,
             swa_sink=swa_sink, mlstm_i_bias=mlstm_i_bias, mlstm_f_bias=mlstm_f_bias,
             mlstm_norm_g=mlstm_norm_g, w_router=w_router, b_router=b_router, w_gate_up=w_gate_up,
             b_gate_up=b_gate_up, w_down=w_down, b_down=b_down)
    c = jnp.concatenate([c_prompt, c_sample], axis=0)
    return _trunk(x_prompt, x_sample, c, p)
```

```python
import functools
import math

import jax
import jax.numpy as jnp
import numpy as np
from jax import lax
from jax.experimental import pallas as pl
from jax.experimental.pallas import tpu as pltpu

F32 = jnp.float32
BF16 = jnp.bfloat16

D_MODEL = 1024
HEAD_DIM = 64
ROPE_THETA = 500000.0
NORM_EPS = 1e-6
CHUNK = 128

SSD_HEADS = 6
SSD_INNER = 384
SSD_STATE = 64
SSD_CONV = 5
SSD_XBC = 640
SWA_Q_HEADS = 6
SWA_SIDE = 128
SWA_HEAD_ORDER = (0, 3, 1, 4, 2, 5)
ML_HEADS = 4
ML_DIM = 96
ML_PAD = 128
DIL_CONFIGS = ((128, 1), (512, 4), (2048, 16))
DIL_HEADS = 6
DIL_SIDE = 64
DIL_BATCH = 4
N_EXPERTS = 32
TOP_K = 4
D_EXPERT = 1024
SWIGLU_LIMIT = 7.0
SWIGLU_ALPHA = 1.702

IN_SPLITS = (384, 640, 12, 384, 128, 128, 384, 384, 384, 384, 8, 8, 1152, 384, 384)

COL_A = 0
COL_G = 1024
COL_B = 1152
COL_C = 1792
COL_D = 3840
COL_END = 5760
LANE_IG = 16
LANE_FG = 24

NEG = -1e30
VMEM_LIMIT = 56 * 1024 * 1024
OUT_SUB = 256
MOE_ROWS = 128
MOE_GROUP = 4
MOE_TILE = 4096


def _cparams(sem):
    return pltpu.CompilerParams(dimension_semantics=sem, vmem_limit_bytes=VMEM_LIMIT)


def _dot(a, b):
    return jnp.dot(a, b, preferred_element_type=F32)


def _dot_nt(a, b):
    return lax.dot_general(a, b, (((1,), (1,)), ((), ())), preferred_element_type=F32)


def _dot_tn(a, b):
    return lax.dot_general(a, b, (((0,), (0,)), ((), ())), preferred_element_type=F32)


def _sigmoid(x):
    return 1.0 / (1.0 + jnp.exp(-x))


def _softplus(x):
    return jnp.maximum(x, 0.0) + jnp.log(1.0 + jnp.exp(-jnp.abs(x)))


def _prefix_rows(x):
    row = lax.broadcasted_iota(jnp.int32, x.shape, 0)
    s = 1
    while s < x.shape[0]:
        x = x + jnp.where(row >= s, pltpu.roll(x, s, 0), 0.0)
        s *= 2
    return x


def _mod_kernel(c_ref, w_ref, b_ref, o_ref):
    c = c_ref[...]
    s = c * _sigmoid(c)
    o_ref[0] = jnp.dot(s, w_ref[0], preferred_element_type=F32,
                       precision=lax.Precision.HIGHEST) + b_ref[0]


def _mod_call(c, w_mod, b_mod):
    nb = c.shape[0]
    depth, d, cols = w_mod.shape
    tn = 1536
    return pl.pallas_call(
        _mod_kernel,
        grid=(depth, cols // tn),
        in_specs=[pl.BlockSpec((nb, d), lambda l, j: (0, 0)),
                  pl.BlockSpec((1, d, tn), lambda l, j: (l, 0, j)),
                  pl.BlockSpec((1, 1, tn), lambda l, j: (l, 0, j))],
        out_specs=pl.BlockSpec((1, nb, tn), lambda l, j: (l, 0, j)),
        out_shape=jax.ShapeDtypeStruct((depth, nb, cols), F32),
        compiler_params=_cparams(("parallel", "parallel")),
        name="mod",
    )(c, w_mod, b_mod.reshape(depth, 1, cols))


def _rope(a, rc, rs1, rs2):
    return a * rc + pltpu.roll(a, 8, 1) * rs1 + pltpu.roll(a, 120, 1) * rs2


def _in_proj_kernel(*refs, fused, tiles_a):
    if fused:
        (x_ref, f_ref, g2_ref, pg_ref, sc_ref, sh_ref, g_ref, w_ref, rc_ref, rs1_ref, rs2_ref,
         oa_ref, og_ref, ob_ref, oc_ref, od_ref, x2_ref) = refs
        f = f_ref[...]
        fms = jnp.mean(f * f, axis=-1, keepdims=True)
        x = x_ref[...] + g2_ref[0] * (f * lax.rsqrt(fms + NORM_EPS) * pg_ref[...])
        x2_ref[...] = x
    elif tiles_a is not None:
        (xa_ref, xb_ref, sc_ref, sh_ref, g_ref, w_ref, rc_ref, rs1_ref, rs2_ref,
         oa_ref, og_ref, ob_ref, oc_ref, od_ref) = refs
        x = jnp.where(pl.program_id(0) < tiles_a, xa_ref[...], xb_ref[...])
    else:
        (x_ref, sc_ref, sh_ref, g_ref, w_ref, rc_ref, rs1_ref, rs2_ref,
         oa_ref, og_ref, ob_ref, oc_ref, od_ref) = refs
        x = x_ref[...]
    ms = jnp.mean(x * x, axis=-1, keepdims=True)
    h = x * lax.rsqrt(ms + NORM_EPS) * g_ref[...]
    h = h * (1.0 + sc_ref[0]) + sh_ref[0]
    hb = h.astype(BF16)
    rc, rs1, rs2 = rc_ref[...], rs1_ref[...], rs2_ref[...]

    def mm(c0, width):
        return _dot(hb, w_ref[:, c0:c0 + width])

    def plain(o_ref, col0, dst0, width, scale=None, step=512, add=None):
        for c in range(0, width, step):
            wd = min(step, width - c)
            a = mm(col0 + c, wd)
            if scale is not None:
                a = a * scale
            if add is not None:
                a = a + add
            o_ref[:, dst0 + c:dst0 + c + wd] = a.astype(o_ref.dtype)

    def roped(o_ref, col0, dst0, width, scale):
        for c in range(0, width, 128):
            a = _rope(mm(col0 + c, 128), rc, rs1, rs2)
            if scale is not None:
                a = a * scale
            o_ref[:, dst0 + c:dst0 + c + 128] = a.astype(o_ref.dtype)

    qscale = HEAD_DIM ** -0.5
    plain(oa_ref, COL_A, 0, 1024)
    plain(og_ref, COL_G, 0, 128)
    roped(ob_ref, COL_B, 0, 384, qscale)
    roped(ob_ref, COL_B + 384, 384, 128, None)
    plain(ob_ref, COL_B + 512, 512, 128)
    plain(oc_ref, COL_C, 0, 512)
    plain(oc_ref, COL_C + 512, 512, 512, scale=ML_DIM ** -0.5)
    lane_in_head = lax.broadcasted_iota(jnp.int32, (1, 512), 1) % ML_PAD
    plain(oc_ref, COL_C + 1024, 1024, 512, add=jnp.where(lane_in_head == ML_DIM, 1.0, 0.0))
    plain(oc_ref, COL_C + 1536, 1536, 512)
    roped(od_ref, COL_D, 0, 1152, qscale)
    roped(od_ref, COL_D + 1152, 1152, 384, None)
    plain(od_ref, COL_D + 1536, 1536, 384, step=384)


def _split_specs(xa, xb, tm):
    tiles_a = xa.shape[0] // tm
    d = xa.shape[1]
    return tiles_a, [pl.BlockSpec((tm, d), lambda i: (jnp.minimum(i, tiles_a - 1), 0)),
                     pl.BlockSpec((tm, d), lambda i: (jnp.maximum(i - tiles_a, 0), 0))]


def _in_proj_call(x, sc, sh, g, w, rc, rs1, rs2, seq_len, ffn=None, tm=512):
    split = isinstance(x, tuple)
    n = sum(a.shape[0] for a in x) if split else x.shape[0]
    d = D_MODEL
    per_seq = seq_len // tm
    row = lambda i: (i, 0)
    seq = lambda i: (i // per_seq, 0, 0)
    pos = lambda i: (i % per_seq, 0)
    const = lambda i: (0, 0)
    widths = (1024, 128, 640, 2048, 1920)
    dtypes = (BF16, F32, BF16, BF16, BF16)
    fused = ffn is not None
    tiles_a = None
    if split:
        tiles_a, x_specs = _split_specs(x[0], x[1], tm)
        x_args = list(x)
    else:
        x_specs = [pl.BlockSpec((tm, d), row)]
        x_args = [x]
    out_specs = [pl.BlockSpec((tm, wd), row) for wd in widths]
    out_shape = [jax.ShapeDtypeStruct((n, wd), dt) for wd, dt in zip(widths, dtypes)]
    if fused:
        x_specs += [pl.BlockSpec((tm, d), row), pl.BlockSpec((1, 1, d), seq), pl.BlockSpec((1, d), const)]
        x_args += list(ffn)
        out_specs.append(pl.BlockSpec((tm, d), row))
        out_shape.append(jax.ShapeDtypeStruct((n, d), F32))
    return pl.pallas_call(
        functools.partial(_in_proj_kernel, fused=fused, tiles_a=tiles_a),
        grid=(n // tm,),
        in_specs=x_specs + [pl.BlockSpec((1, 1, d), seq),
                            pl.BlockSpec((1, 1, d), seq),
                            pl.BlockSpec((1, d), const),
                            pl.BlockSpec((d, COL_END), const, pipeline_mode=pl.Buffered(1)),
                            pl.BlockSpec((tm, 128), pos),
                            pl.BlockSpec((tm, 128), pos),
                            pl.BlockSpec((tm, 128), pos)],
        out_specs=out_specs,
        out_shape=out_shape,
        compiler_params=_cparams(("parallel",)),
        name="in_proj",
    )(*x_args, sc, sh, g, w, rc, rs1, rs2)


def _ssd_kernel(ua_ref, ug_ref, cw_ref, cb_ref, alog_ref, dtb_ref, dsk_ref, ng_ref, o_ref,
                xp_ref, xa_ref, cum_ref, dtd_ref, tr_ref, y_ref, y2_ref, sf_ref, sb_ref,
                *, seq_len):
    nc = seq_len // CHUNK
    q = CHUNK
    xp_ref[0:8, :] = jnp.zeros((8, SSD_XBC), F32)
    xp_ref[seq_len + 8:seq_len + 16, :] = jnp.zeros((8, SSD_XBC), F32)
    xp_ref[8:seq_len + 8, :] = ua_ref[:, SSD_INNER:SSD_INNER + SSD_XBC].astype(F32)
    sf_ref[...] = jnp.zeros_like(sf_ref)
    sb_ref[...] = jnp.zeros_like(sb_ref)
    a_row = -jnp.exp(alog_ref[...])
    lane = lax.broadcasted_iota(jnp.int32, (q, 128), 1)

    def prep(c, carry):
        r0 = pl.multiple_of(c * q, q)
        win = xp_ref[pl.ds(r0, q + 16), :]
        conv = cb_ref[...] + cw_ref[0:1, :] * win[6:6 + q]
        for k in range(1, SSD_CONV):
            conv = conv + cw_ref[k:k + 1, :] * win[6 + k:6 + k + q]
        xa_ref[pl.ds(r0, q), :] = conv * _sigmoid(conv)
        dt = _softplus(ug_ref[pl.ds(r0, q), :] + dtb_ref[...])
        dta = dt * a_row
        p = _prefix_rows(dta)
        s = p[q - 1:q, :] - p + dta
        cum = jnp.where(lane < SSD_HEADS, p, s)
        cum_ref[pl.ds(r0, q), :] = cum
        dtd_ref[pl.ds(r0, q), :] = dt
        packed = jnp.where(lane < 16, cum, pltpu.roll(dt, 16, 1))
        tr_ref[c] = packed.T
        return carry

    lax.fori_loop(0, nc, prep, 0)

    ri = lax.broadcasted_iota(jnp.int32, (q, q), 0)
    ci = lax.broadcasted_iota(jnp.int32, (q, q), 1)
    lower_incl = ci <= ri
    lower = ci < ri
    upper = ci > ri

    def main(c, carry):
        r0 = pl.multiple_of(c * q, q)
        cbk = nc - 1 - c
        r1 = pl.multiple_of(cbk * q, q)
        tr = tr_ref[c]
        visits = []
        for rr in (r0, r1):
            xa = xa_ref[pl.ds(rr, q), :]
            bt = xa[:, SSD_INNER:SSD_INNER + 128].T.astype(BF16)
            visits.append(dict(
                xa=xa, cum=cum_ref[pl.ds(rr, q), :], dtd=dtd_ref[pl.ds(rr, q), :],
                bgt=[bt[g * 64:(g + 1) * 64, :] for g in range(2)],
                bg=[xa[:, SSD_INNER + g * 64:SSD_INNER + (g + 1) * 64].astype(BF16) for g in range(2)],
                cg=[xa[:, SSD_INNER + 128 + g * 64:SSD_INNER + 128 + (g + 1) * 64].astype(BF16)
                    for g in range(2)]))
        fw, bw = visits
        heads = range(SSD_HEADS)
        cbs = [_dot_nt(fw['cg'][g], fw['bg'][g]) for g in range(2)]
        ws = []
        for h in heads:
            cum = fw['cum']
            cf_col, rb_col = cum[:, h:h + 1], cum[:, 6 + h:7 + h]
            cf_row, rb_row = tr[h:h + 1, :], tr[6 + h:7 + h, :]
            dtf_row, dtb_row = tr[16 + h:17 + h, :], tr[22 + h:23 + h, :]
            arg = jnp.where(lower_incl, cf_col - cf_row, rb_col - rb_row)
            fac = jnp.where(lower, dtf_row, jnp.where(upper, dtb_row, dtf_row + dtb_row))
            ws.append((cbs[h // 3] * jnp.exp(arg) * fac).astype(BF16))
        xf = [fw['xa'][:, h * 64:(h + 1) * 64] for h in heads]
        xb = [bw['xa'][:, h * 64:(h + 1) * 64] for h in heads]
        sfs = [sf_ref[h] for h in heads]
        sbs = [sb_ref[h] for h in heads]
        y_intra = [_dot(ws[h], xf[h].astype(BF16)) for h in heads]
        y_carry_f = [_dot(fw['cg'][h // 3], sfs[h].astype(BF16)) for h in heads]
        y_carry_b = [_dot(bw['cg'][h // 3], sbs[h].astype(BF16)) for h in heads]
        y_ref[pl.ds(r0, q), :] = jnp.concatenate(
            [y_intra[h] + y_carry_f[h] * jnp.exp(fw['cum'][:, h:h + 1]) for h in heads], axis=1)
        y2_ref[pl.ds(r1, q), :] = jnp.concatenate(
            [y_carry_b[h] * jnp.exp(bw['cum'][:, 6 + h:7 + h]) for h in heads], axis=1)
        xws, decays = [], []
        for h in heads:
            tf = fw['cum'][q - 1:q, h:h + 1]
            wcol = jnp.exp(tf - fw['cum'][:, h:h + 1]) * fw['dtd'][:, h:h + 1]
            xws.append((xf[h] * wcol).astype(BF16))
            decays.append(jnp.exp(tf))
        for h in heads:
            tb = bw['cum'][0:1, 6 + h:7 + h]
            wcol = jnp.exp(tb - bw['cum'][:, 6 + h:7 + h]) * bw['dtd'][:, 6 + h:7 + h]
            xws.append((xb[h] * wcol).astype(BF16))
            decays.append(jnp.exp(tb))
        for h in heads:
            sf_ref[h] = sfs[h] * decays[h] + _dot(fw['bgt'][h // 3], xws[h])
        for h in heads:
            sb_ref[h] = sbs[h] * decays[6 + h] + _dot(bw['bgt'][h // 3], xws[6 + h])
        return carry

    lax.fori_loop(0, nc, main, 0)

    def fin(c, carry):
        r0 = pl.multiple_of(c * q, q)
        xs = xa_ref[pl.ds(r0, q), 0:SSD_INNER]
        y = y_ref[pl.ds(r0, q), :] + y2_ref[pl.ds(r0, q), :] + xs * dsk_ref[...]
        z = ua_ref[pl.ds(r0, q), 0:SSD_INNER].astype(F32)
        v = y * (z * _sigmoid(z))
        ms = jnp.mean(v * v, axis=-1, keepdims=True)
        o_ref[pl.ds(r0, q), :] = (v * lax.rsqrt(ms + NORM_EPS) * ng_ref[...]).astype(o_ref.dtype)
        return carry

    lax.fori_loop(0, nc, fin, 0)


def _ssd_call(ua, ug, cw, cb, alog, dtb, dsk, ng, seq_len):
    n = ua.shape[0]
    nseq = n // seq_len
    nc = seq_len // CHUNK
    seq = lambda b: (b, 0)
    const = lambda b: (0, 0)
    return pl.pallas_call(
        functools.partial(_ssd_kernel, seq_len=seq_len),
        grid=(nseq,),
        in_specs=[pl.BlockSpec((seq_len, 1024), seq),
                  pl.BlockSpec((seq_len, 128), seq),
                  pl.BlockSpec((8, SSD_XBC), const),
                  pl.BlockSpec((1, SSD_XBC), const),
                  pl.BlockSpec((1, 128), const),
                  pl.BlockSpec((1, 128), const),
                  pl.BlockSpec((1, SSD_INNER), const),
                  pl.BlockSpec((1, SSD_INNER), const)],
        out_specs=pl.BlockSpec((seq_len, SSD_INNER), seq),
        out_shape=jax.ShapeDtypeStruct((n, SSD_INNER), BF16),
        scratch_shapes=[pltpu.VMEM((seq_len + 16, SSD_XBC), F32),
                        pltpu.VMEM((seq_len, SSD_XBC), F32),
                        pltpu.VMEM((seq_len, 128), F32),
                        pltpu.VMEM((seq_len, 128), F32),
                        pltpu.VMEM((nc, 128, CHUNK), F32),
                        pltpu.VMEM((seq_len, SSD_INNER), F32),
                        pltpu.VMEM((seq_len, SSD_INNER), F32),
                        pltpu.VMEM((SSD_HEADS, SSD_STATE, HEAD_DIM), F32),
                        pltpu.VMEM((SSD_HEADS, SSD_STATE, HEAD_DIM), F32)],
        compiler_params=_cparams(("parallel",)),
        name="ssd",
    )(ua, ug, cw, cb, alog, dtb, dsk, ng)


def _swa_kernel(sink_ref, u_ref, o_ref, kp_ref, vp_ref, bias_ref, *, seq_len):
    q = CHUNK
    nb = seq_len // q
    zeros = jnp.zeros((q, 128), BF16)
    kp_ref[0:q, :] = zeros
    vp_ref[0:q, :] = zeros
    kp_ref[seq_len + q:seq_len + 2 * q, :] = zeros
    vp_ref[seq_len + q:seq_len + 2 * q, :] = zeros
    kp_ref[q:seq_len + q, :] = u_ref[:, 384:512]
    vp_ref[q:seq_len + q, :] = u_ref[:, 512:640]
    row = lax.broadcasted_iota(jnp.int32, (3 * q, 3 * q), 0) & (q - 1)
    col = lax.broadcasted_iota(jnp.int32, (3 * q, 3 * q), 1)
    bias_ref[...] = jnp.where(jnp.abs(col - q - row) <= SWA_SIDE, 0.0, NEG)
    rgrp = lax.broadcasted_iota(jnp.int32, (3 * q, 1), 0) // q
    col1 = lax.broadcasted_iota(jnp.int32, (1, 3 * q), 1)
    lane1 = lax.broadcasted_iota(jnp.int32, (1, 128), 1)
    lo_f = jnp.where(lane1 < HEAD_DIM, 1.0, 0.0)
    hi_f = 1.0 - lo_f
    halves = ((lo_f.astype(BF16), lo_f), (hi_f.astype(BF16), hi_f))

    sinks = [jnp.where(rgrp == 0, sink_ref[3 * hk],
                       jnp.where(rgrp == 1, sink_ref[3 * hk + 1], sink_ref[3 * hk + 2])) for hk in range(2)]

    def body(qb2, carry):
        units = []
        for i in range(2):
            q0 = pl.multiple_of((2 * qb2 + i) * q, q)
            qs = jnp.concatenate([u_ref[pl.ds(q0, q), j * 128:(j + 1) * 128] for j in range(3)], axis=0)
            kw = kp_ref[pl.ds(q0, 3 * q), :]
            vw = vp_ref[pl.ds(q0, 3 * q), :]
            inside = (col1 >= q - q0) & (col1 < seq_len + q - q0)
            bias = bias_ref[...] + jnp.where(inside, 0.0, NEG)
            for hk, (sel_b, sel_f) in enumerate(halves):
                units.append(dict(q0=q0, qs=qs * sel_b, kw=kw, vw=vw, bias=bias, sk=sinks[hk], sel_f=sel_f))
        for u in units:
            u['s'] = _dot_nt(u['qs'], u['kw']) + u['bias']
        for u in units:
            u['m'] = jnp.maximum(jnp.max(u['s'], axis=-1, keepdims=True), u['sk'])
        for u in units:
            u['p'] = jnp.exp(u['s'] - u['m'])
        for u in units:
            u['l'] = jnp.sum(u['p'], axis=-1, keepdims=True) + jnp.exp(u['sk'] - u['m'])
        for u in units:
            u['o'] = (_dot(u['p'].astype(BF16), u['vw']) / u['l']) * u['sel_f']
        for i in range(2):
            o = units[2 * i]['o'] + units[2 * i + 1]['o']
            o_ref[pl.ds(units[2 * i]['q0'], q), :] = jnp.concatenate(
                [o[j * q:(j + 1) * q, :] for j in range(3)], axis=1).astype(o_ref.dtype)
        return carry

    lax.fori_loop(0, nb // 2, body, 0)


def _swa_call(ub, sink, seq_len):
    n = ub.shape[0]
    nseq = n // seq_len
    return pl.pallas_call(
        functools.partial(_swa_kernel, seq_len=seq_len),
        grid=(nseq,),
        in_specs=[pl.BlockSpec(memory_space=pltpu.SMEM),
                  pl.BlockSpec((seq_len, 640), lambda b: (b, 0))],
        out_specs=pl.BlockSpec((seq_len, 384), lambda b: (b, 0)),
        out_shape=jax.ShapeDtypeStruct((n, 384), BF16),
        scratch_shapes=[pltpu.VMEM((seq_len + 2 * CHUNK, 128), BF16),
                        pltpu.VMEM((seq_len + 2 * CHUNK, 128), BF16),
                        pltpu.VMEM((3 * CHUNK, 3 * CHUNK), F32)],
        compiler_params=_cparams(("parallel",)),
        name="swa",
    )(sink, ub)


def _mlstm_kernel(uc_ref, ug_ref, gb_ref, ng_ref, o_ref,
                  gcol_ref, gtr_ref, hacc_ref, ct_ref, m_ref, mask_ref, *, seq_len):
    q = CHUNK
    nc = seq_len // q
    lane = lax.broadcasted_iota(jnp.int32, (q, 128), 1)
    ct_ref[...] = jnp.zeros_like(ct_ref)
    m_ref[...] = jnp.zeros_like(m_ref)
    hacc_ref[...] = jnp.zeros_like(hacc_ref)

    def prep(c, carry):
        r0 = pl.multiple_of(c * q, q)
        raw = ug_ref[pl.ds(r0, q), :] + gb_ref[...]
        lf = jnp.minimum(raw, 0.0) - jnp.log(1.0 + jnp.exp(-jnp.abs(raw)))
        p = _prefix_rows(lf)
        s = p[q - 1:q, :] - p + lf
        g = jnp.where(lane < LANE_FG, raw, jnp.where(lane < LANE_FG + ML_HEADS, p, s))
        gcol_ref[pl.ds(r0, q), :] = g
        gtr_ref[c] = g.T
        return carry

    lax.fori_loop(0, nc, prep, 0)

    ri = lax.broadcasted_iota(jnp.int32, (q, q), 0)
    ci = lax.broadcasted_iota(jnp.int32, (q, q), 1)
    mask_ref[0] = jnp.where(ci <= ri, 0.0, NEG)
    mask_ref[1] = jnp.where(ci >= ri, 0.0, NEG)
    keep_lanes = jnp.where(lax.broadcasted_iota(jnp.int32, (1, ML_PAD), 1) < ML_DIM, 1.0, 0.0)

    def main(c, carry):
        units = []
        for dirn in range(2):
            ck = c if dirn == 0 else nc - 1 - c
            r0 = pl.multiple_of(ck * q, q)
            g = gcol_ref[pl.ds(r0, q), :]
            gt = gtr_ref[ck]
            for h in range(ML_HEADS):
                s_idx = dirn * ML_HEADS + h
                li, lb = LANE_IG + s_idx, LANE_FG + s_idx
                edge = q - 1 if dirn == 0 else 0
                units.append(dict(
                    dirn=dirn, h=h, r0=r0, s_idx=s_idx,
                    b_col=g[:, lb:lb + 1], b_row=gt[lb:lb + 1, :], i_col=g[:, li:li + 1],
                    i_row=gt[li:li + 1, :], tot=g[edge:edge + 1, lb:lb + 1],
                    m_st=m_ref[s_idx][0:1, 0:1], ct=ct_ref[s_idx],
                    qh=uc_ref[pl.ds(r0, q), h * ML_PAD:(h + 1) * ML_PAD],
                    kh=uc_ref[pl.ds(r0, q), 512 + h * ML_PAD:512 + (h + 1) * ML_PAD],
                    vh=uc_ref[pl.ds(r0, q), 1024 + h * ML_PAD:1024 + (h + 1) * ML_PAD]))
        for u in units:
            u['logd'] = u['b_col'] - u['b_row'] + u['i_row'] + mask_ref[u['dirn']]
            u['m_inter'] = u['b_col'] + u['m_st']
        for u in units:
            u['m_t'] = jnp.maximum(jnp.max(u['logd'], axis=-1, keepdims=True), u['m_inter'])
        for u in units:
            u['qk'] = _dot_nt(u['qh'], u['kh'])
            u['qc'] = _dot(u['qh'], u['ct'].astype(BF16))
        for u in units:
            u['sm'] = (u['qk'] * jnp.exp(u['logd'] - u['m_t'])).astype(BF16)
            u['inter'] = jnp.exp(u['m_inter'] - u['m_t'])
        for u in units:
            u['num'] = _dot(u['sm'], u['vh']) + u['inter'] * u['qc']
        for u in units:
            den = u['num'][:, ML_DIM:ML_DIM + 1]
            hh = u['num'] / jnp.maximum(jnp.abs(den), jnp.exp(-u['m_t'])) * keep_lanes
            hacc_ref[pl.ds(u['r0'], q), u['h'] * ML_PAD:(u['h'] + 1) * ML_PAD] += hh
        for u in units:
            tot, m_st = u['tot'], u['m_st']
            m_new = jnp.maximum(tot + m_st,
                                jnp.max(tot - u['b_row'] + u['i_row'], axis=-1, keepdims=True))
            wk_col = jnp.exp(tot - u['b_col'] + u['i_col'] - m_new)
            u['keep'] = jnp.exp(tot + m_st - m_new)
            u['vw'] = (u['vh'].astype(F32) * wk_col).astype(BF16)
            m_ref[u['s_idx']] = jnp.broadcast_to(m_new, (8, 128))
        for u in units:
            ct_ref[u['s_idx']] = u['keep'] * u['ct'] + _dot_tn(u['kh'], u['vw'])
        return carry

    lax.fori_loop(0, nc, main, 0)

    def fin(c, carry):
        r0 = pl.multiple_of(c * q, q)
        outs = []
        for h in range(ML_HEADS):
            hs = hacc_ref[pl.ds(r0, q), h * ML_PAD:(h + 1) * ML_PAD]
            ms = jnp.sum(hs * hs, axis=-1, keepdims=True) * (1.0 / ML_DIM)
            hn = hs * lax.rsqrt(ms + NORM_EPS) * ng_ref[:, h * ML_PAD:(h + 1) * ML_PAD]
            og = uc_ref[pl.ds(r0, q), 1536 + h * ML_PAD:1536 + (h + 1) * ML_PAD].astype(F32)
            outs.append(hn * _sigmoid(og))
        o_ref[pl.ds(r0, q), :] = jnp.concatenate(outs, axis=1).astype(o_ref.dtype)
        return carry

    lax.fori_loop(0, nc, fin, 0)


def _mlstm_call(uc, ug, gb, ng, seq_len):
    n = uc.shape[0]
    nseq = n // seq_len
    nc = seq_len // CHUNK
    seq = lambda b: (b, 0)
    const = lambda b: (0, 0)
    width = ML_HEADS * ML_PAD
    return pl.pallas_call(
        functools.partial(_mlstm_kernel, seq_len=seq_len),
        grid=(nseq,),
        in_specs=[pl.BlockSpec((seq_len, 4 * width), seq),
                  pl.BlockSpec((seq_len, 128), seq),
                  pl.BlockSpec((1, 128), const),
                  pl.BlockSpec((1, width), const)],
        out_specs=pl.BlockSpec((seq_len, width), seq),
        out_shape=jax.ShapeDtypeStruct((n, width), BF16),
        scratch_shapes=[pltpu.VMEM((seq_len, 128), F32),
                        pltpu.VMEM((nc, 128, CHUNK), F32),
                        pltpu.VMEM((seq_len, width), F32),
                        pltpu.VMEM((2 * ML_HEADS, ML_PAD, ML_PAD), F32),
                        pltpu.VMEM((2 * ML_HEADS, 8, 128), F32),
                        pltpu.VMEM((2, CHUNK, CHUNK), F32)],
        compiler_params=_cparams(("parallel",)),
        name="mlstm",
    )(uc, ug, gb, ng)


def _attend_units(units, lo_b, hi_b, lo_f, hi_f):
    n = units[0][0].shape[0]
    ss = []
    for q, kw, _, bias in units:
        s2 = _dot_nt(jnp.concatenate([q * lo_b, q * hi_b], axis=0), kw)
        ss.append(jnp.concatenate([s2[0:n] + bias, s2[n:2 * n] + bias], axis=0))
    ms = [jnp.max(s, axis=-1, keepdims=True) for s in ss]
    ps = [jnp.exp(s - m) for s, m in zip(ss, ms)]
    ls = [jnp.sum(p, axis=-1, keepdims=True) for p in ps]
    os_ = [_dot(p.astype(BF16), u[2]) / l for p, u, l in zip(ps, units, ls)]
    res = []
    for o2, m, l in zip(os_, ms, ls):
        lse2 = m + jnp.log(l)
        res.append((o2[0:n] * lo_f + o2[n:2 * n] * hi_f, lse2[0:n] * lo_f + lse2[n:2 * n] * hi_f))
    return res


def _dil_kernel(q0_ref, q1_ref, q2_ref, k_ref, v_ref, o_ref,
                qf_ref, kf_ref, vf_ref, kc0_ref, vc0_ref, kc1_ref, vc1_ref, og_ref, lg_ref,
                qc2_ref, kc2_ref, vc2_ref, og2_ref, lg2_ref, band_ref, band2_ref, *, seq_len):
    qr = CHUNK
    pad = DIL_SIDE
    d1, d2 = DIL_CONFIGS[1][1], DIL_CONFIGS[2][1]
    n1, n2 = seq_len // d1, seq_len // d2
    assert n2 == qr and n1 % qr == 0
    qf_ref[0] = q1_ref[...].astype(F32)
    qf_ref[1] = q2_ref[...].astype(F32)
    kf_ref[...] = k_ref[...].astype(F32)
    vf_ref[...] = v_ref[...].astype(F32)
    zeros = jnp.zeros((pad, 128), BF16)
    kc0_ref[0:pad, :] = zeros
    vc0_ref[0:pad, :] = zeros
    kc0_ref[pad + seq_len:2 * pad + seq_len, :] = zeros
    vc0_ref[pad + seq_len:2 * pad + seq_len, :] = zeros
    kc0_ref[pad:pad + seq_len, :] = k_ref[...]
    vc0_ref[pad:pad + seq_len, :] = v_ref[...]
    for r in range(d1):
        kc1_ref[r, 0:pad, :] = zeros
        vc1_ref[r, 0:pad, :] = zeros
        kc1_ref[r, pad + n1:2 * pad + n1, :] = zeros
        vc1_ref[r, pad + n1:2 * pad + n1, :] = zeros
        kc1_ref[r, pad:pad + n1, :] = kf_ref[pl.ds(r, n1, stride=d1), :].astype(BF16)
        vc1_ref[r, pad:pad + n1, :] = vf_ref[pl.ds(r, n1, stride=d1), :].astype(BF16)

    row = lax.broadcasted_iota(jnp.int32, (qr, 2 * qr), 0)
    col = lax.broadcasted_iota(jnp.int32, (qr, 2 * qr), 1)
    band_ref[...] = jnp.where(jnp.abs(col - pad - row) <= DIL_SIDE, 0.0, NEG)
    row2 = lax.broadcasted_iota(jnp.int32, (qr, qr), 0)
    col2 = lax.broadcasted_iota(jnp.int32, (qr, qr), 1)
    band2_ref[...] = jnp.where(jnp.abs(col2 - row2) <= DIL_SIDE, 0.0, NEG)
    lane1 = lax.broadcasted_iota(jnp.int32, (1, 128), 1)
    col1 = lax.broadcasted_iota(jnp.int32, (1, 2 * qr), 1)
    lo_f = jnp.where(lane1 < HEAD_DIM, 1.0, 0.0)
    hi_f = 1.0 - lo_f
    sel = (lo_f.astype(BF16), hi_f.astype(BF16), lo_f, hi_f)

    def window_bias(u0, n):
        inside = (col1 >= pad - u0) & (col1 < n + pad - u0)
        return band_ref[...] + jnp.where(inside, 0.0, NEG)

    def batch0(b4, carry):
        units, starts = [], []
        for i in range(DIL_BATCH):
            u0 = pl.multiple_of((b4 * DIL_BATCH + i) * qr, qr)
            starts.append(u0)
            units.append((q0_ref[pl.ds(u0, qr), :], kc0_ref[pl.ds(u0, 2 * qr), :],
                          vc0_ref[pl.ds(u0, 2 * qr), :], window_bias(u0, seq_len)))
        for u0, (o, lse) in zip(starts, _attend_units(units, *sel)):
            og_ref[0, pl.ds(u0, qr), :] = o
            lg_ref[0, pl.ds(u0, qr), :] = lse
        return carry

    lax.fori_loop(0, seq_len // qr // DIL_BATCH, batch0, 0)

    nb1 = n1 // qr
    assert nb1 == DIL_BATCH

    def batch1(r, carry):
        units, rowsl = [], []
        for i in range(DIL_BATCH):
            u0 = i * qr
            rows = pl.ds(r + u0 * d1, qr, stride=d1)
            rowsl.append(rows)
            units.append((qf_ref[0, rows, :].astype(BF16), kc1_ref[r, u0:u0 + 2 * qr, :],
                          vc1_ref[r, u0:u0 + 2 * qr, :], window_bias(u0, n1)))
        for rows, (o, lse) in zip(rowsl, _attend_units(units, *sel)):
            og_ref[1, rows, :] = o
            lg_ref[1, rows, :] = lse
        return carry

    lax.fori_loop(0, d1, batch1, 0)

    for r in range(d2):
        rows = pl.ds(r, qr, stride=d2)
        qc2_ref[r] = qf_ref[1, rows, :].astype(BF16)
        kc2_ref[r] = kf_ref[rows, :].astype(BF16)
        vc2_ref[r] = vf_ref[rows, :].astype(BF16)

    def batch2(b4, carry):
        rs = [b4 * DIL_BATCH + i for i in range(DIL_BATCH)]
        units = [(qc2_ref[r], kc2_ref[r], vc2_ref[r], band2_ref[...]) for r in rs]
        for r, (o, lse) in zip(rs, _attend_units(units, *sel)):
            og2_ref[r] = o
            lg2_ref[r] = lse
        return carry

    lax.fori_loop(0, d2 // DIL_BATCH, batch2, 0)
    for r in range(d2):
        rows = pl.ds(r, qr, stride=d2)
        og_ref[2, rows, :] = og2_ref[r]
        lg_ref[2, rows, :] = lg2_ref[r]

    def fin(c, carry):
        r0 = pl.multiple_of(c * qr, qr)
        rows = pl.ds(r0, qr)
        l0, l1, l2 = lg_ref[0, rows, :], lg_ref[1, rows, :], lg_ref[2, rows, :]
        m = jnp.maximum(jnp.maximum(l0, l1), l2)
        w0, w1, w2 = jnp.exp(l0 - m), jnp.exp(l1 - m), jnp.exp(l2 - m)
        y = (og_ref[0, rows, :] * w0 + og_ref[1, rows, :] * w1 + og_ref[2, rows, :] * w2)
        o_ref[rows, :] = (y / (w0 + w1 + w2)).astype(o_ref.dtype)
        return carry

    lax.fori_loop(0, seq_len // qr, fin, 0, unroll=2)


def _dil_call(ud, seq_len):
    n = ud.shape[0]
    nseq = n // seq_len
    blk = (seq_len, 128)
    d1, d2 = DIL_CONFIGS[1][1], DIL_CONFIGS[2][1]
    n1 = seq_len // d1

    def col(off):
        return lambda b, p: (b, off + p)

    return pl.pallas_call(
        functools.partial(_dil_kernel, seq_len=seq_len),
        grid=(nseq, 3),
        in_specs=[pl.BlockSpec(blk, col(0)), pl.BlockSpec(blk, col(3)), pl.BlockSpec(blk, col(6)),
                  pl.BlockSpec(blk, col(9)), pl.BlockSpec(blk, col(12))],
        out_specs=pl.BlockSpec(blk, col(0)),
        out_shape=jax.ShapeDtypeStruct((n, 384), BF16),
        scratch_shapes=[pltpu.VMEM((2, seq_len, 128), F32),
                        pltpu.VMEM((seq_len, 128), F32),
                        pltpu.VMEM((seq_len, 128), F32),
                        pltpu.VMEM((seq_len + 2 * DIL_SIDE, 128), BF16),
                        pltpu.VMEM((seq_len + 2 * DIL_SIDE, 128), BF16),
                        pltpu.VMEM((d1, n1 + 2 * DIL_SIDE, 128), BF16),
                        pltpu.VMEM((d1, n1 + 2 * DIL_SIDE, 128), BF16),
                        pltpu.VMEM((3, seq_len, 128), F32),
                        pltpu.VMEM((3, seq_len, 128), F32)]
                       + [pltpu.VMEM((d2, CHUNK, 128), BF16)] * 3
                       + [pltpu.VMEM((d2, CHUNK, 128), F32)] * 2
                       + [pltpu.VMEM((CHUNK, 2 * CHUNK), F32), pltpu.VMEM((CHUNK, CHUNK), F32)],
        compiler_params=_cparams(("parallel", "parallel")),
        name="dil",
    )(ud, ud, ud, ud, ud)


def _out_proj_kernel(ya_ref, yb_ref, yc_ref, yd_ref, *refs, tiles_a):
    if tiles_a is None:
        x_ref = refs[0]
        refs = refs[1:]
    else:
        xa_ref, xb_ref = refs[0:2]
        refs = refs[2:]
    (g1_ref, sc_ref, sh_ref, pg_ref, fg_ref, wa_ref, wb_ref, wc_ref, wd_ref, wrh_ref, wrl_ref, br_ref,
     x1_ref, h2_ref, rt_ref) = refs
    for r0 in range(0, x1_ref.shape[0], OUT_SUB):
        rs = slice(r0, r0 + OUT_SUB)
        if tiles_a is None:
            x_in = x_ref[rs, :]
        else:
            x_in = jnp.where(pl.program_id(0) < tiles_a, xa_ref[rs, :], xb_ref[rs, :])
        y = (_dot(ya_ref[rs, :], wa_ref[...]) + _dot(yb_ref[rs, :], wb_ref[...])
             + _dot(yc_ref[rs, :], wc_ref[...]) + _dot(yd_ref[rs, :], wd_ref[...]))
        ms = jnp.mean(y * y, axis=-1, keepdims=True)
        yn = y * lax.rsqrt(ms + NORM_EPS) * pg_ref[...]
        x1 = x_in + g1_ref[0] * yn
        x1_ref[rs, :] = x1
        ms2 = jnp.mean(x1 * x1, axis=-1, keepdims=True)
        h2 = x1 * lax.rsqrt(ms2 + NORM_EPS) * fg_ref[...]
        h2 = h2 * (1.0 + sc_ref[0]) + sh_ref[0]
        half = h2.shape[1] // 2
        lo = lax.bitcast_convert_type(h2[:, :half].astype(BF16).astype(F32), jnp.uint32)
        hi = lax.bitcast_convert_type(h2[:, half:].astype(BF16).astype(F32), jnp.uint32)
        h2_ref[rs, :] = (lo >> 16) | hi
        h_hi = h2.astype(BF16)
        h_lo = (h2 - h_hi.astype(F32)).astype(BF16)
        logits = (_dot(h_hi, wrh_ref[...]) + _dot(h_lo, wrh_ref[...]) + _dot(h_hi, wrl_ref[...])
                  + br_ref[...])
        lane = lax.broadcasted_iota(jnp.int32, logits.shape, 1)
        vals, idxs = [], []
        for _ in range(TOP_K):
            m = jnp.max(logits, axis=-1, keepdims=True)
            idx = jnp.min(jnp.where(logits == m, lane, 128), axis=-1, keepdims=True)
            vals.append(m)
            idxs.append(idx)
            logits = jnp.where(lane == idx, -3e38, logits)
        es = [jnp.exp(v - vals[0]) for v in vals]
        tot = es[0] + es[1] + es[2] + es[3]
        rt = jnp.zeros(lane.shape, F32)
        for k in range(TOP_K):
            rt = jnp.where(lane == k, idxs[k].astype(F32), rt)
            rt = jnp.where(lane == TOP_K + k, es[k] / tot, rt)
        rt_ref[rs, :] = rt[:, 0:2 * TOP_K]


def _out_proj_call(ya, yb, yc, yd, x, g1, sc, sh, pg, fg, wa, wb, wc, wd, wrh, wrl, br, seq_len, tm=512):
    split = isinstance(x, tuple)
    n, d = ya.shape[0], D_MODEL
    tiles_a = None
    if split:
        tiles_a, x_specs = _split_specs(x[0], x[1], tm)
        x_args = list(x)
    else:
        x_specs = [pl.BlockSpec((tm, d), lambda i: (i, 0))]
        x_args = [x]
    per_seq = seq_len // tm
    row = lambda i: (i, 0)
    seq = lambda i: (i // per_seq, 0, 0)
    const = lambda i: (0, 0)

    def full(a):
        return pl.BlockSpec(a.shape, const)

    return pl.pallas_call(
        functools.partial(_out_proj_kernel, tiles_a=tiles_a),
        grid=(n // tm,),
        in_specs=[pl.BlockSpec((tm, ya.shape[1]), row), pl.BlockSpec((tm, yb.shape[1]), row),
                  pl.BlockSpec((tm, yc.shape[1]), row), pl.BlockSpec((tm, yd.shape[1]), row)]
        + x_specs
        + [pl.BlockSpec((1, 1, d), seq), pl.BlockSpec((1, 1, d), seq),
                  pl.BlockSpec((1, 1, d), seq),
                  full(pg), full(fg), full(wa), full(wb), full(wc), full(wd), full(wrh), full(wrl), full(br)],
        out_specs=[pl.BlockSpec((tm, d), row), pl.BlockSpec((tm, d // 2), row),
                   pl.BlockSpec((tm, 2 * TOP_K), row)],
        out_shape=[jax.ShapeDtypeStruct((n, d), F32), jax.ShapeDtypeStruct((n, d // 2), jnp.uint32),
                   jax.ShapeDtypeStruct((n, 2 * TOP_K), F32)],
        compiler_params=_cparams(("parallel",)),
        name="out_proj",
    )(ya, yb, yc, yd, *x_args, g1, sc, sh, pg, fg, wa, wb, wc, wd, wrh, wrl, br)


def _moe_slots(tile):
    return tile * TOP_K // MOE_ROWS + N_EXPERTS + 2


def _moe_kernel(nb_ref, bf_ref, bs_ref, bv_ref, idx_ref, gate_ref, h_ref,
                wgu_ref, bgu_ref, wdn_ref, bdn_ref, o_ref, xs0_ref, xs1_ref, ys0_ref, ys1_ref):
    t = pl.program_id(0)
    e = pl.program_id(1)
    nb = nb_ref[t]
    rows = MOE_ROWS
    nslot = _moe_slots(h_ref.shape[0])
    half = h_ref.shape[1]

    def gather(dst_ref, blk):
        base = bs_ref[t * nslot + blk]
        for r in range(rows):
            i = idx_ref[0, 0, base + r]
            dst_ref[r:r + 1, :] = h_ref[pl.ds(i, 1), :]

    def scatter(src_ref, blk, nvalid):
        base = bs_ref[t * nslot + blk]
        i0 = idx_ref[0, 0, base]
        for r0 in range(0, rows, MOE_GROUP):
            ks = range(MOE_GROUP)
            ok = [r0 + k < nvalid for k in ks]
            ii = [jnp.where(ok[k], idx_ref[0, 0, base + r0 + k], i0) for k in ks]
            gg = [jnp.where(ok[k], gate_ref[0, 0, base + r0 + k], 0.0) for k in ks]
            cur = [o_ref[pl.ds(ii[k], 1), :] for k in ks]
            for k in reversed(ks):
                o_ref[pl.ds(ii[k], 1), :] = cur[k] + gg[k] * src_ref[r0 + k:r0 + k + 1, :]

    def expert(xs_ref, ys_ref):
        xu = xs_ref[...]
        x_lo = lax.bitcast_convert_type(xu << 16, F32).astype(BF16)
        x_hi = lax.bitcast_convert_type(xu & jnp.uint32(0xFFFF0000), F32).astype(BF16)
        gu = _dot(x_lo, wgu_ref[0, 0:half, :]) + _dot(x_hi, wgu_ref[0, half:2 * half, :]) + bgu_ref[0]
        gate = jnp.minimum(gu[:, :D_EXPERT], SWIGLU_LIMIT)
        up = jnp.clip(gu[:, D_EXPERT:], -SWIGLU_LIMIT, SWIGLU_LIMIT)
        act = (up + 1.0) * gate * _sigmoid(SWIGLU_ALPHA * gate)
        ys_ref[...] = _dot(act.astype(BF16), wdn_ref[0]) + bdn_ref[0]

    @pl.when(e == 0)
    def _():
        o_ref[...] = jnp.zeros_like(o_ref)
        ys1_ref[...] = jnp.zeros_like(ys1_ref)
        gather(xs0_ref, 0)

    bufs = ((xs0_ref, xs1_ref, ys0_ref, ys1_ref), (xs1_ref, xs0_ref, ys1_ref, ys0_ref))

    def block(b, carry):
        prev = jnp.maximum(b - 1, 0)
        nv_prev = jnp.where(b > 0, bv_ref[t * nslot + prev], 0)
        for par, (xs_cur, xs_nxt, ys_cur, ys_prv) in enumerate(bufs):
            @pl.when(b % 2 == par)
            def _(xs_cur=xs_cur, xs_nxt=xs_nxt, ys_cur=ys_cur, ys_prv=ys_prv):
                scatter(ys_prv, prev, nv_prev)
                expert(xs_cur, ys_cur)
                gather(xs_nxt, b + 1)
        return carry

    lax.fori_loop(bf_ref[t * (N_EXPERTS + 1) + e], bf_ref[t * (N_EXPERTS + 1) + e + 1], block, 0)

    for par, (_, _, _, ys_prv) in enumerate(bufs):
        @pl.when((e == N_EXPERTS - 1) & (nb % 2 == par))
        def _(ys_prv=ys_prv):
            scatter(ys_prv, nb - 1, bv_ref[t * nslot + nb - 1])


def _moe_call(nb, bfirst, bs, bv, idx, gates, h2p, wgu, bgu, wdn, bdn, tile):
    n, half = h2p.shape
    d = 2 * half
    ntile = n // tile
    plen = idx.shape[-1]
    wmap = lambda t, e, *_: (e, 0, 0)
    tmap3 = lambda t, e, *_: (t, 0, 0)
    tmap2 = lambda t, e, *_: (t, 0)
    grid_spec = pltpu.PrefetchScalarGridSpec(
        num_scalar_prefetch=4,
        grid=(ntile, N_EXPERTS),
        in_specs=[pl.BlockSpec((1, 1, plen), tmap3, memory_space=pltpu.SMEM),
                  pl.BlockSpec((1, 1, plen), tmap3, memory_space=pltpu.SMEM),
                  pl.BlockSpec((tile, half), tmap2, pipeline_mode=pl.Buffered(1)),
                  pl.BlockSpec((1, d, 2 * D_EXPERT), wmap), pl.BlockSpec((1, 1, 2 * D_EXPERT), wmap),
                  pl.BlockSpec((1, D_EXPERT, d), wmap), pl.BlockSpec((1, 1, d), wmap)],
        out_specs=pl.BlockSpec((tile, d), tmap2, pipeline_mode=pl.Buffered(1)),
        scratch_shapes=[pltpu.VMEM((MOE_ROWS, half), jnp.uint32)] * 2 + [pltpu.VMEM((MOE_ROWS, d), F32)] * 2,
    )
    return pl.pallas_call(
        _moe_kernel,
        grid_spec=grid_spec,
        out_shape=jax.ShapeDtypeStruct((n, d), F32),
        compiler_params=_cparams(("parallel", "arbitrary")),
        name="moe",
    )(nb, bfirst, bs, bv, idx, gates, h2p, wgu, bgu, wdn, bdn)


def _route(rt, tile):
    n = rt.shape[0]
    ntile = n // tile
    npair = tile * TOP_K
    nslot = _moe_slots(tile)
    e = rt[:, :TOP_K].astype(jnp.int32).reshape(ntile, npair)
    g = rt[:, TOP_K:].reshape(ntile, npair)
    tok = jnp.broadcast_to((jnp.arange(npair, dtype=jnp.int32) // TOP_K)[None, :], (ntile, npair))
    _, tok_s, g_s = lax.sort((e, tok, g), dimension=1, is_stable=True, num_keys=1)
    experts = jnp.arange(N_EXPERTS, dtype=jnp.int32)
    counts = jnp.sum(e[:, :, None] == experts, axis=1, dtype=jnp.int32)
    nblocks = (counts + MOE_ROWS - 1) // MOE_ROWS
    bends = jnp.cumsum(nblocks, axis=-1)
    starts = jnp.cumsum(counts, axis=-1) - counts
    nb = bends[:, -1]
    slot = jnp.minimum(jnp.arange(nslot, dtype=jnp.int32)[None, :], nb[:, None] - 1)
    be = jnp.sum(slot[:, :, None] >= bends[:, None, :], axis=-1, dtype=jnp.int32)
    onehot = (be[:, :, None] == experts).astype(jnp.int32)
    pick = lambda v: jnp.sum(onehot * v[:, None, :], axis=-1)
    within = slot - pick(bends - nblocks)
    bs = pick(starts) + within * MOE_ROWS
    bv = jnp.clip(pick(counts) - within * MOE_ROWS, 0, MOE_ROWS)
    padw = ((0, 0), (0, MOE_ROWS))
    idx = jnp.pad(tok_s, padw).reshape(ntile, 1, -1)
    gl = jnp.pad(g_s, padw).reshape(ntile, 1, -1)
    bfirst = jnp.concatenate([jnp.zeros((ntile, 1), jnp.int32), bends], axis=-1)
    flat = lambda v: v.reshape(-1).astype(jnp.int32)
    return flat(nb), flat(bfirst), flat(bs), flat(bv), idx, gl


def _ffn_res_kernel(x_ref, f_ref, g2_ref, pg_ref, oa_ref, ob_ref, *, tiles_a):
    f = f_ref[...]
    ms = jnp.mean(f * f, axis=-1, keepdims=True)
    y = x_ref[...] + g2_ref[0] * (f * lax.rsqrt(ms + NORM_EPS) * pg_ref[...])
    i = pl.program_id(0)

    @pl.when(i < tiles_a)
    def _():
        oa_ref[...] = y

    @pl.when(i >= tiles_a)
    def _():
        ob_ref[...] = y


def _ffn_res_call(x1, f, g2, pg, seq_len, rows_a, tm=1024):
    n, d = x1.shape
    per_seq = seq_len // tm
    tiles_a = rows_a // tm
    row = lambda i: (i, 0)
    return pl.pallas_call(
        functools.partial(_ffn_res_kernel, tiles_a=tiles_a),
        grid=(n // tm,),
        in_specs=[pl.BlockSpec((tm, d), row), pl.BlockSpec((tm, d), row),
                  pl.BlockSpec((1, 1, d), lambda i: (i // per_seq, 0, 0)),
                  pl.BlockSpec((1, d), lambda i: (0, 0))],
        out_specs=[pl.BlockSpec((tm, d), lambda i: (jnp.minimum(i, tiles_a - 1), 0)),
                   pl.BlockSpec((tm, d), lambda i: (jnp.maximum(i - tiles_a, 0), 0))],
        out_shape=[jax.ShapeDtypeStruct((rows_a, d), F32), jax.ShapeDtypeStruct((n - rows_a, d), F32)],
        compiler_params=_cparams(("arbitrary",)),
        name="ffn_res",
    )(x1, f, g2, pg)


def _rope_tables(seq_len):
    half = 8
    inv = 1.0 / (ROPE_THETA ** (jnp.arange(half, dtype=F32) * (2.0 / 16)))
    ang = jnp.arange(seq_len, dtype=F32)[:, None] * inv[None, :]
    cos, sin = jnp.cos(ang), jnp.sin(ang)
    ones = jnp.ones((seq_len, 48), F32)
    zeros8 = jnp.zeros((seq_len, 8), F32)
    zeros48 = jnp.zeros((seq_len, 48), F32)
    rc = jnp.concatenate([cos, cos, ones], axis=1)
    rs1 = jnp.concatenate([zeros8, sin, zeros48], axis=1)
    rs2 = jnp.concatenate([-sin, zeros8, zeros48], axis=1)
    tile2 = lambda a: jnp.concatenate([a, a], axis=1)
    return tile2(rc), tile2(rs1), tile2(rs2)


def _pad_heads(w):
    lead = w.shape[:-1]
    w = w.reshape(lead + (ML_HEADS, ML_DIM))
    w = jnp.pad(w, [(0, 0)] * len(lead) + [(0, 0), (0, ML_PAD - ML_DIM)])
    return w.reshape(lead + (ML_HEADS * ML_PAD,))


def _layout_w_in(w_in):
    d = w_in.shape[0]
    offs = np.cumsum((0,) + IN_SPLITS)
    parts = [w_in[:, offs[i]:offs[i + 1]] for i in range(len(IN_SPLITS))]
    (z, xbc, dt, sq, sk, sv, mq, mk, mv, mo, mi, mf, dq, dk, dv) = parts
    zc = lambda k: jnp.zeros((d, k), w_in.dtype)
    gates = jnp.concatenate([dt, zc(4), mi, mf, zc(128 - 32)], axis=1)
    sq = sq.reshape(d, SWA_Q_HEADS, HEAD_DIM)[:, SWA_HEAD_ORDER, :].reshape(d, -1)
    w = jnp.concatenate([z, xbc, gates, sq, sk, sv,
                         _pad_heads(mq), _pad_heads(mk), _pad_heads(mv), _pad_heads(mo),
                         dq, dk, dv], axis=1)
    assert w.shape[1] == COL_END
    return w.astype(BF16)


def _lane_row(parts, width=128):
    row = jnp.zeros((width,), F32)
    for off, v in parts:
        row = row.at[off:off + v.shape[0]].set(v.astype(F32))
    return row.reshape(1, width)


def _layer(x, pending, mod_l, p, l, tabs, seq_len):
    d = D_MODEL
    nseq = mod_l.shape[0]
    sh1, sc1, g1, sh2, sc2, g2 = [mod_l[:, i * d:(i + 1) * d].reshape(nseq, 1, d) for i in range(6)]
    row = lambda v: v.reshape(1, -1).astype(F32)

    outs = _in_proj_call(x, sc1, sh1, row(p['pre_mix_g'][l]), _layout_w_in(p['w_in'][l]),
                         *tabs, seq_len, ffn=pending)
    if pending is not None:
        x = outs[5]
    ua, ug, ub, uc, ud = outs[:5]
    cw = jnp.pad(p['ssd_conv_w'][l], ((0, 8 - SSD_CONV), (0, 0)))
    ya = _ssd_call(ua, ug, cw, row(p['ssd_conv_b'][l]),
                   _lane_row([(0, p['ssd_a_log'][l].reshape(-1))]),
                   _lane_row([(0, p['ssd_dt_bias'][l].reshape(-1))]),
                   row(jnp.repeat(p['ssd_d'][l], HEAD_DIM)), row(p['ssd_norm_g'][l]), seq_len)
    yb = _swa_call(ub, p['swa_sink'][l].astype(F32), seq_len)
    gb = _lane_row([(LANE_IG, p['mlstm_i_bias'][l].reshape(-1)),
                    (LANE_FG, p['mlstm_f_bias'][l].reshape(-1))])
    yc = _mlstm_call(uc, ug, gb, row(_pad_heads(p['mlstm_norm_g'][l])), seq_len)
    yd = _dil_call(ud, seq_len)

    w_out = p['w_out'][l]
    wa = w_out[0:384].astype(BF16)
    wb = w_out[384:768].reshape(SWA_Q_HEADS, HEAD_DIM, d)[SWA_HEAD_ORDER, :, :].reshape(384, d).astype(BF16)
    wc = jnp.pad(w_out[768:1152].reshape(ML_HEADS, ML_DIM, d),
                 ((0, 0), (0, ML_PAD - ML_DIM), (0, 0))).reshape(ML_HEADS * ML_PAD, d).astype(BF16)
    wd = w_out[1152:1536].astype(BF16)
    wr = jnp.pad(p['w_router'][l], ((0, 0), (0, 128 - N_EXPERTS))).astype(F32)
    wrh = wr.astype(BF16)
    wrl = (wr - wrh.astype(F32)).astype(BF16)
    br = jnp.concatenate([p['b_router'][l].astype(F32), jnp.full((128 - N_EXPERTS,), NEG, F32)]).reshape(1, 128)
    x1, h2, rt = _out_proj_call(ya, yb, yc, yd, x, g1, sc2, sh2, row(p['post_mix_g'][l]),
                                row(p['pre_ffn_g'][l]), wa, wb, wc, wd, wrh, wrl, br, seq_len)

    tile = min(MOE_TILE, x1.shape[0])
    nb, bfirst, bs, bv, idx, gs = _route(rt, tile)
    f = _moe_call(nb, bfirst, bs, bv, idx, gs, h2, p['w_gate_up'][l].astype(BF16),
                  p['b_gate_up'][l].reshape(N_EXPERTS, 1, -1), p['w_down'][l].astype(BF16),
                  p['b_down'][l].reshape(N_EXPERTS, 1, -1), tile)
    return x1, (f, g2, row(p['post_ffn_g'][l]))


def _trunk(xa, xb, c, p):
    nseq_a, seq_len, d = xa.shape
    nseq = nseq_a + xb.shape[0]
    depth = p['w_in'].shape[0]
    mod = _mod_call(c, p['w_mod'], p['b_mod'])
    tabs = _rope_tables(seq_len)
    x = (xa.reshape(-1, d), xb.reshape(-1, d))
    pending = None
    for l in range(depth):
        x, pending = _layer(x, pending, mod[l], p, l, tabs, seq_len)
    ya, yb = _ffn_res_call(x, *pending, seq_len, nseq_a * seq_len)
    return ya.reshape(nseq_a, seq_len, d), yb.reshape(nseq - nseq_a, seq_len, d)


def kernel(x_prompt, x_sample, c_prompt, c_sample, w_mod, b_mod, pre_mix_g, post_mix_g, pre_ffn_g, post_ffn_g, w_in, w_out, ssd_conv_w, ssd_conv_b, ssd_a_log, ssd_dt_bias, ssd_d, ssd_norm_g, swa_sink, mlstm_i_bias, mlstm_f_bias, mlstm_norm_g, w_router, b_router, w_gate_up, b_gate_up, w_down, b_down):
    p = dict(w_mod=w_mod, b_mod=b_mod, pre_mix_g=pre_mix_g, post_mix_g=post_mix_g, pre_ffn_g=pre_ffn_g,
             post_ffn_g=post_ffn_g, w_in=w_in, w_out=w_out, ssd_conv_w=ssd_conv_w, ssd_conv_b=ssd_conv_b,
             ssd_a_log=ssd_a_log, ssd_dt_bias=ssd_dt_bias, ssd_d=ssd_d, ssd_norm_g=ssd_norm_g,
             swa_sink=swa_sink, mlstm_i_bias=mlstm_i_bias, mlstm_f_bias=mlstm_f_bias,
             mlstm_norm_g=mlstm_norm_g, w_router=w_router, b_router=b_router, w_gate_up=w_gate_up,
             b_gate_up=b_gate_up, w_down=w_down, b_down=b_down)
    c = jnp.concatenate([c_prompt, c_sample], axis=0)
    return _trunk(x_prompt, x_sample, c, p)
```
